```python
import math
import jax, jax.numpy as jnp
from jax import lax
import numpy as np

D_MODEL = 1024
BATCH = 8
SEQ = 2048
DEPTH = 2

N_EVEN = (DEPTH + 1) // 2
N_ODD = DEPTH // 2

ATTN_HEADS = 4
ATTN_QK_DIM = 64
ATTN_V_DIM = 2 * ATTN_QK_DIM
ATTN_QK_WIDTH = ATTN_HEADS * 2 * ATTN_QK_DIM
ATTN_WIDTH = ATTN_HEADS * ATTN_V_DIM
ROPE_THETA = 10000.0
Q_BLOCK = 128

S5_WIDTH = D_MODEL - ATTN_WIDTH
S5_GROUP = 16
S5_GROUPS = S5_WIDTH // S5_GROUP
S5_STATE = 64
S5_DT_MIN = 1e-3
S5_DT_MAX = 1e-1

EVEN_IN_WIDTH = 2 * ATTN_QK_WIDTH + ATTN_WIDTH + S5_WIDTH

HGRN_HEADS = 8
HGRN_KEY = D_MODEL // HGRN_HEADS
HGRN_VAL = D_MODEL // HGRN_HEADS
HGRN_CHUNK = 64
ODD_IN_WIDTH = 5 * D_MODEL

D_FF = 4 * D_MODEL
EPS = 1e-6

kernel_name = 'hybrid_diffattn_s5_hgrn2_encoder'


def rmsnorm(x, g):
    xf = x.astype(jnp.float32)
    y = xf * lax.rsqrt(jnp.mean(xf * xf, axis=-1, keepdims=True) + EPS)
    return (y * g.astype(jnp.float32)).astype(x.dtype)


def rope_tables(seq_len, dim, dtype):
    inv = ROPE_THETA ** (-jnp.arange(0, dim, 2, dtype=jnp.float32) / dim)
    ang = jnp.arange(seq_len, dtype=jnp.float32)[:, None] * inv[None, :]
    return jnp.cos(ang).astype(dtype), jnp.sin(ang).astype(dtype)


def apply_rope(t, cos, sin):
    t1, t2 = jnp.split(t, 2, axis=-1)
    return jnp.concatenate([t1 * cos - t2 * sin, t2 * cos + t1 * sin], axis=-1)


def diff_attention(h, lam, subln_g, lambda_init, cos, sin):
    bsz, seq, _ = h.shape
    q, k, v = jnp.split(h, [ATTN_QK_WIDTH, 2 * ATTN_QK_WIDTH], axis=-1)

    def split_qk(t):
        return t.reshape(bsz, seq, ATTN_HEADS, 2, ATTN_QK_DIM).transpose(3, 0, 2, 1, 4)

    q = apply_rope(split_qk(q), cos, sin) * (ATTN_QK_DIM ** -0.5)
    k = apply_rope(split_qk(k), cos, sin)
    v = v.reshape(bsz, seq, ATTN_HEADS, ATTN_V_DIM).transpose(0, 2, 1, 3)
    lam32 = lam.astype(jnp.float32)
    lam_val = (jnp.exp(jnp.sum(lam32[0] * lam32[1])) - jnp.exp(jnp.sum(lam32[2] * lam32[3]))
               + lambda_init)
    n_blk = seq // Q_BLOCK
    q_blocks = q.reshape(2, bsz, ATTN_HEADS, n_blk, Q_BLOCK, ATTN_QK_DIM).transpose(3, 0, 1, 2, 4, 5)

    def attend(qb):
        s = jnp.einsum('cbhqd,cbhkd->cbhqk', qb, k).astype(jnp.float32)
        p = jax.nn.softmax(s, axis=-1)
        w = p[0] - lam_val * p[1]
        return jnp.einsum('bhqk,bhkv->bhqv', w.astype(v.dtype), v)

    o = lax.map(attend, q_blocks)
    o = o.transpose(1, 2, 0, 3, 4).reshape(bsz, ATTN_HEADS, seq, ATTN_V_DIM)
    o = rmsnorm(o, subln_g) * (1.0 - lambda_init)
    return o.transpose(0, 2, 1, 3).reshape(bsz, seq, ATTN_WIDTH)


def _complex_affine_combine(e1, e2):
    a1r, a1i, b1r, b1i = e1
    a2r, a2i, b2r, b2i = e2
    return (a2r * a1r - a2i * a1i,
            a2r * a1i + a2i * a1r,
            a2r * b1r - a2i * b1i + b2r,
            a2r * b1i + a2i * b1r + b2i)


def s5_mixer(u, lam_re, lam_im, log_step, b_re, b_im, c_re, c_im, d_skip, w_glu, b_glu):
    bsz, seq, _ = u.shape
    ug = u.reshape(bsz, seq, S5_GROUPS, S5_GROUP)
    y = u * d_skip
    for direction in range(2):
        lr = jnp.minimum(lam_re[direction], -1e-4)
        li = lam_im[direction]
        dt = jnp.exp(log_step[direction])[:, None]
        mag = jnp.exp(lr * dt)
        ar = mag * jnp.cos(li * dt)
        ai = mag * jnp.sin(li * dt)
        den = lr * lr + li * li
        cr = ((ar - 1.0) * lr + ai * li) / den
        ci = (ai * lr - (ar - 1.0) * li) / den
        bbr = cr[..., None] * b_re[direction] - ci[..., None] * b_im[direction]
        bbi = cr[..., None] * b_im[direction] + ci[..., None] * b_re[direction]
        bur = jnp.einsum('blgn,gpn->blgp', ug, bbr)
        bui = jnp.einsum('blgn,gpn->blgp', ug, bbi)
        elems = (jnp.broadcast_to(ar, bur.shape), jnp.broadcast_to(ai, bur.shape), bur, bui)
        _, _, xr, xi = lax.associative_scan(_complex_affine_combine, elems,
                                            reverse=(direction == 1), axis=1)
        y_dir = (jnp.einsum('blgp,gnp->blgn', xr, c_re[direction])
                 - jnp.einsum('blgp,gnp->blgn', xi, c_im[direction]))
        y = y + y_dir.reshape(bsz, seq, S5_WIDTH)
    y = jax.nn.gelu(y)
    return y * jax.nn.sigmoid(y @ w_glu + b_glu)


def gated_linear_recurrence(q, k, log_f, v):
    bsz, heads, seq, dk = q.shape
    dv = v.shape[-1]
    n = seq // HGRN_CHUNK

    def chunk(t):
        return t.reshape(bsz, heads, n, HGRN_CHUNK, t.shape[-1])

    q, k, log_f, v = chunk(q), chunk(k), chunk(log_f), chunk(v)
    b = jnp.cumsum(log_f, axis=3)
    b_last = b[:, :, :, -1:, :]
    q_dec = q * jnp.exp(b)
    k_inv = k * jnp.exp(-b)
    scores = jnp.einsum('bhnck,bhnsk->bhncs', q_dec, k_inv)
    mask = jnp.tril(jnp.ones((HGRN_CHUNK, HGRN_CHUNK), dtype=bool))
    scores = jnp.where(mask, scores, 0.0)
    o_intra = jnp.einsum('bhncs,bhnsv->bhncv', scores, v)
    chunk_kv = jnp.einsum('bhnck,bhncv->bhnkv', k * jnp.exp(b_last - b), v)
    chunk_decay = jnp.exp(b_last[:, :, :, 0, :])

    def step(state, inp):
        dec, kv = inp
        return dec[..., None] * state + kv, state

    s0 = jnp.zeros((bsz, heads, dk, dv), jnp.float32)
    _, states = lax.scan(step, s0, (jnp.moveaxis(chunk_decay, 2, 0), jnp.moveaxis(chunk_kv, 2, 0)))
    o_inter = jnp.einsum('bhnck,nbhkv->bhncv', q_dec, states)
    return (o_intra + o_inter).reshape(bsz, heads, seq, dv)


def hgrn2_mixer(h, lb, norm_g):
    bsz, seq, _ = h.shape
    q, i, f_fwd, f_bwd, g = jnp.split(h, 5, axis=-1)

    def heads(t):
        return t.astype(jnp.float32).reshape(bsz, seq, HGRN_HEADS, -1).transpose(0, 2, 1, 3)

    lb32 = lb.astype(jnp.float32)
    qh, ih = heads(q), heads(i)
    o = jnp.zeros((bsz, HGRN_HEADS, seq, HGRN_VAL), jnp.float32)
    for f_logit, reverse in ((f_fwd, False), (f_bwd, True)):
        fl = f_logit.astype(jnp.float32)
        f = lb32 + (1.0 - lb32) * jax.nn.sigmoid(fl)
        k = (1.0 - lb32) * jax.nn.sigmoid(-fl)
        args = (qh, heads(k), heads(jnp.log(f)), ih)
        if reverse:
            args = tuple(jnp.flip(a, axis=2) for a in args)
            o = o + jnp.flip(gated_linear_recurrence(*args), axis=2)
        else:
            o = o + gated_linear_recurrence(*args)
    o = o.transpose(0, 2, 1, 3)
    o = rmsnorm(o, norm_g.reshape(HGRN_HEADS, HGRN_VAL)).reshape(bsz, seq, D_MODEL)
    return (o * jax.nn.sigmoid(g.astype(jnp.float32))).astype(h.dtype)


def setup_inputs(seed: int = 0) -> dict:
    key = jax.random.key(seed)
    ks = jax.random.split(key, 24)
    f32 = jnp.float32

    def nrm(k, shape, scale):
        return jax.random.normal(k, shape, f32) * scale

    n_idx = jnp.arange(S5_STATE, dtype=f32)
    s5_shape = (N_EVEN, 2, S5_GROUPS, S5_STATE)
    return {
        'x': nrm(ks[0], (BATCH, SEQ, D_MODEL), 1.0),
        'norm_mix_g': 1.0 + nrm(ks[1], (DEPTH, D_MODEL), 0.02),
        'norm_mlp_g': 1.0 + nrm(ks[2], (DEPTH, D_MODEL), 0.02),
        'final_norm_g': 1.0 + nrm(ks[3], (D_MODEL,), 0.02),
        'w_ff_in': nrm(ks[4], (DEPTH, D_MODEL, D_FF), D_MODEL ** -0.5),
        'w_ff_out': nrm(ks[5], (DEPTH, D_FF, D_MODEL), D_FF ** -0.5),
        'w_in_even': nrm(ks[6], (N_EVEN, D_MODEL, EVEN_IN_WIDTH), D_MODEL ** -0.5),
        'w_out_even': nrm(ks[7], (N_EVEN, ATTN_WIDTH + S5_WIDTH, D_MODEL), (ATTN_WIDTH + S5_WIDTH) ** -0.5),
        'diff_lambda': nrm(ks[8], (N_EVEN, 4, ATTN_QK_DIM), 0.1),
        'diff_subln_g': 1.0 + nrm(ks[9], (N_EVEN, ATTN_V_DIM), 0.02),
        's5_lam_re': -0.5 + nrm(ks[10], s5_shape, 0.01),
        's5_lam_im': math.pi * n_idx + nrm(ks[11], s5_shape, 0.01),
        's5_log_step': jax.random.uniform(ks[12], (N_EVEN, 2, S5_GROUPS), f32,
                                          math.log(S5_DT_MIN), math.log(S5_DT_MAX)),
        's5_b_re': nrm(ks[13], (N_EVEN, 2, S5_GROUPS, S5_STATE, S5_GROUP), (2.0 * S5_GROUP) ** -0.5),
        's5_b_im': nrm(ks[14], (N_EVEN, 2, S5_GROUPS, S5_STATE, S5_GROUP), (2.0 * S5_GROUP) ** -0.5),
        's5_c_re': nrm(ks[15], (N_EVEN, 2, S5_GROUPS, S5_GROUP, S5_STATE), S5_STATE ** -0.5),
        's5_c_im': nrm(ks[16], (N_EVEN, 2, S5_GROUPS, S5_GROUP, S5_STATE), S5_STATE ** -0.5),
        's5_d': nrm(ks[17], (N_EVEN, S5_WIDTH), 1.0),
        's5_w_glu': nrm(ks[18], (N_EVEN, S5_WIDTH, S5_WIDTH), S5_WIDTH ** -0.5),
        's5_b_glu': nrm(ks[19], (N_EVEN, S5_WIDTH), 0.01),
        'w_in_odd': nrm(ks[20], (N_ODD, D_MODEL, ODD_IN_WIDTH), D_MODEL ** -0.5),
        'w_out_odd': nrm(ks[21], (N_ODD, D_MODEL, D_MODEL), D_MODEL ** -0.5),
        'hgrn_norm_g': 1.0 + nrm(ks[22], (N_ODD, D_MODEL), 0.02),
        'hgrn_lb_logits': nrm(ks[23], (DEPTH, D_MODEL), 0.1),
    }


def reference(x, norm_mix_g, norm_mlp_g, final_norm_g, w_ff_in, w_ff_out, w_in_even, w_out_even,
              diff_lambda, diff_subln_g, s5_lam_re, s5_lam_im, s5_log_step, s5_b_re, s5_b_im,
              s5_c_re, s5_c_im, s5_d, s5_w_glu, s5_b_glu, w_in_odd, w_out_odd, hgrn_norm_g,
              hgrn_lb_logits):
    seq = x.shape[1]
    cos, sin = rope_tables(seq, ATTN_QK_DIM, x.dtype)
    lb_soft = jax.nn.softmax(hgrn_lb_logits.astype(jnp.float32), axis=0)
    lb_table = jnp.cumsum(lb_soft, axis=0) - lb_soft[0:1]
    for l in range(DEPTH):
        h = rmsnorm(x, norm_mix_g[l])
        if l % 2 == 0:
            e = l // 2
            proj = h @ w_in_even[e]
            attn_in = proj[..., :2 * ATTN_QK_WIDTH + ATTN_WIDTH]
            u = proj[..., 2 * ATTN_QK_WIDTH + ATTN_WIDTH:]
            lambda_init = 0.8 - 0.6 * math.exp(-0.3 * l)
            a_out = diff_attention(attn_in, diff_lambda[e], diff_subln_g[e], lambda_init, cos, sin)
            b_out = s5_mixer(u, s5_lam_re[e], s5_lam_im[e], s5_log_step[e], s5_b_re[e], s5_b_im[e],
                             s5_c_re[e], s5_c_im[e], s5_d[e], s5_w_glu[e], s5_b_glu[e])
            x = x + jnp.concatenate([a_out, b_out], axis=-1) @ w_out_even[e]
        else:
            o_i = l // 2
            proj = h @ w_in_odd[o_i]
            x = x + hgrn2_mixer(proj, lb_table[l], hgrn_norm_g[o_i]) @ w_out_odd[o_i]
        h = rmsnorm(x, norm_mlp_g[l])
        x = x + jnp.square(jax.nn.relu(h @ w_ff_in[l])) @ w_ff_out[l]
    return rmsnorm(x, final_norm_g)
```

```python
import functools
import math

import jax
import jax.numpy as jnp
from jax import lax
from jax.experimental import pallas as pl
from jax.experimental.pallas import tpu as pltpu

D_MODEL = 1024
DEPTH = 2
ATTN_HEADS = 4
ATTN_QK_DIM = 64
ATTN_V_DIM = 128
ATTN_QK_WIDTH = 512
ATTN_WIDTH = 512
ROPE_THETA = 10000.0
S5_WIDTH = 512
S5_GROUP = 16
S5_GROUPS = 32
S5_STATE = 64
HGRN_HEADS = 8
HGRN_CHUNK = 64
D_FF = 4096
EPS = 1e-6

S5_CHUNK = 16
S5_CW = S5_CHUNK * S5_GROUP
S5_GROUP_BLOCK = 4

TOKEN_TILE = 512
ATTN_Q_TILE = 256
FF_TILE = 1024
VMEM_LIMIT = 56 * 1024 * 1024

BF16 = jnp.bfloat16
F32 = jnp.float32


def _const_spec(shape):
    nd = len(shape)
    return pl.BlockSpec(shape, lambda *_: (0,) * nd, pipeline_mode=pl.Buffered(1))


def _params(n_axes):
    return pltpu.CompilerParams(dimension_semantics=("arbitrary",) * n_axes,
                                vmem_limit_bytes=VMEM_LIMIT)


def _rmsnorm_rows(x, g):
    ms = jnp.mean(x * x, axis=-1, keepdims=True)
    return x * lax.rsqrt(ms + EPS) * g


def _sigmoid(x):
    return 1.0 / (1.0 + jnp.exp(-x))


def _gelu_tanh(x):
    c = math.sqrt(2.0 / math.pi)
    return 0.5 * x * (1.0 + jnp.tanh(c * (x + 0.044715 * (x * x * x))))


def _dot(a, b):
    return jnp.dot(a, b, preferred_element_type=F32)


def _dot_nt(a, b):
    return lax.dot_general(a, b, (((1,), (1,)), ((), ())), preferred_element_type=F32)


def _dot_tn(a, b):
    return lax.dot_general(a, b, (((0,), (0,)), ((), ())), preferred_element_type=F32)


def _pre_even_kernel(x_ref, g_ref, w_ref, cos_ref, sin_ref, q_ref, k_ref, v_ref, u_ref):
    h = _rmsnorm_rows(x_ref[...], g_ref[...]).astype(BF16)
    cos = cos_ref[...]
    sin = sin_ref[...]
    for out_ref, base, scale in ((q_ref, 0, ATTN_QK_DIM ** -0.5), (k_ref, ATTN_QK_WIDTH, 1.0)):
        p = _dot(h, w_ref[:, base:base + ATTN_QK_WIDTH])
        for pair in range(2):
            lo = p[:, 256 * pair:256 * pair + 128]
            hi = p[:, 256 * pair + 128:256 * pair + 256]
            out_ref[:, 256 * pair:256 * pair + 128] = ((lo * cos - hi * sin) * scale).astype(BF16)
            out_ref[:, 256 * pair + 128:256 * pair + 256] = ((hi * cos + lo * sin) * scale).astype(BF16)
    v_ref[...] = _dot(h, w_ref[:, 2 * ATTN_QK_WIDTH:2 * ATTN_QK_WIDTH + ATTN_WIDTH]).astype(BF16)
    u_ref[...] = _dot(h, w_ref[:, 2 * ATTN_QK_WIDTH + ATTN_WIDTH:]).astype(BF16)


def _pre_even(x2, g, w, cos, sin, seq):
    n_tok = x2.shape[0]
    tm = TOKEN_TILE
    n_pos_blocks = seq // tm
    out = jax.ShapeDtypeStruct((n_tok, 512), BF16)
    row_spec = pl.BlockSpec((tm, 512), lambda i: (i, 0))
    return pl.pallas_call(
        _pre_even_kernel,
        grid=(n_tok // tm,),
        in_specs=[pl.BlockSpec((tm, D_MODEL), lambda i: (i, 0)),
                  _const_spec((1, D_MODEL)),
                  _const_spec(w.shape),
                  pl.BlockSpec((tm, 128), lambda i: (i % n_pos_blocks, 0)),
                  pl.BlockSpec((tm, 128), lambda i: (i % n_pos_blocks, 0))],
        out_specs=[row_spec, row_spec, row_spec, row_spec],
        out_shape=[out, out, out, out],
        compiler_params=_params(1),
        name="pre_even",
    )(x2, g, w, cos, sin)


def _attn_kernel(lambda_init, q_ref, k_ref, v_ref, lam_ref, g_ref, o_ref):
    q = q_ref[0]
    k = k_ref[0]
    lam = lam_ref[...]
    lam_val = (jnp.exp(jnp.sum(lam[0:1] * lam[1:2], axis=-1, keepdims=True))
               - jnp.exp(jnp.sum(lam[2:3] * lam[3:4], axis=-1, keepdims=True)) + lambda_init)
    lane_group = (lax.broadcasted_iota(jnp.int32, q.shape, 1) // 32) % 4
    zero = jnp.zeros_like(q)
    for hh in range(2):
        probs = []
        for c in range(2):
            qm = jnp.where(lane_group == 2 * hh + c, q, zero)
            s = _dot_nt(qm, k)
            m = jnp.max(s, axis=-1, keepdims=True)
            e = jnp.exp(s - m)
            l = jnp.sum(e, axis=-1, keepdims=True)
            probs.append((e, 1.0 / l))
        w = probs[0][0] * probs[0][1] - probs[1][0] * (lam_val * probs[1][1])
        o = _dot(w.astype(BF16), v_ref[0, :, 128 * hh:128 * hh + 128])
        o = _rmsnorm_rows(o, g_ref[...]) * (1.0 - lambda_init)
        o_ref[0, :, 128 * hh:128 * hh + 128] = o.astype(o_ref.dtype)


def _attention(q, k, v, lam, subln_g, lambda_init):
    bsz, seq, _ = q.shape
    tq = ATTN_Q_TILE
    kv_spec = pl.BlockSpec((1, seq, 256), lambda b, p, i: (b, 0, p))
    return pl.pallas_call(
        functools.partial(_attn_kernel, lambda_init),
        grid=(bsz, 2, seq // tq),
        in_specs=[pl.BlockSpec((1, tq, 256), lambda b, p, i: (b, i, p)),
                  kv_spec, kv_spec,
                  _const_spec(lam.shape),
                  _const_spec(subln_g.shape)],
        out_specs=pl.BlockSpec((1, tq, 256), lambda b, p, i: (b, i, p)),
        out_shape=jax.ShapeDtypeStruct((bsz, seq, ATTN_WIDTH), BF16),
        compiler_params=_params(3),
        name="diff_attention",
    )(q, k, v, lam, subln_g)


def _s5_kernel(n_chunks, n_batch, u_ref, mi_ref, min_ref, mout_ref, coef_ref, y_ref, s_ref, z_ref):
    gb = u_ref.shape[0]
    for g in range(gb):
        s_ref[g] = _dot(u_ref[g], min_ref[g])
    zero_tile = jnp.zeros((n_batch, 128), F32)
    for g in range(gb):
        z_ref[g, 0:n_batch, 0:128] = zero_tile
        z_ref[g, (n_chunks - 1) * n_batch:n_chunks * n_batch, 128:256] = zero_tile

    def step(c, carry):
        new = []
        cb = n_chunks - 1 - c
        rf = pl.multiple_of(c * n_batch, n_batch)
        rb = pl.multiple_of(cb * n_batch, n_batch)
        for g in range(gb):
            vf, vfs, vb, vbs = carry[4 * g:4 * g + 4]
            coef = coef_ref[g]
            nvf = coef[0:1] * vf + coef[1:2] * vfs + s_ref[g, pl.ds(rf, n_batch), 0:128]
            nvfs = coef[0:1] * vfs + coef[2:3] * vf + s_ref[g, pl.ds(rf, n_batch), 128:256]
            nvb = coef[3:4] * vb + coef[4:5] * vbs + s_ref[g, pl.ds(rb, n_batch), 256:384]
            nvbs = coef[3:4] * vbs + coef[5:6] * vb + s_ref[g, pl.ds(rb, n_batch), 384:512]

            @pl.when(c < n_chunks - 1)
            def _():
                z_ref[g, pl.ds(rf + n_batch, n_batch), 0:128] = nvf
                z_ref[g, pl.ds(rb - n_batch, n_batch), 128:256] = nvb

            new += [nvf, nvfs, nvb, nvbs]
        return tuple(new)

    lax.fori_loop(0, n_chunks, step, tuple(zero_tile for _ in range(4 * gb)))
    for g in range(gb):
        y_ref[g] = _dot(u_ref[g], mi_ref[g]) + _dot(z_ref[g].astype(BF16), mout_ref[g])


def _s5_chunked(u_g, m_intra, m_in, m_out, coef, n_chunks, n_batch):
    n_groups, rows, _ = u_g.shape
    gb = S5_GROUP_BLOCK
    return pl.pallas_call(
        functools.partial(_s5_kernel, n_chunks, n_batch),
        grid=(n_groups // gb,),
        in_specs=[pl.BlockSpec((gb, rows, S5_CW), lambda i: (i, 0, 0)),
                  pl.BlockSpec((gb, S5_CW, S5_CW), lambda i: (i, 0, 0)),
                  pl.BlockSpec((gb, S5_CW, 512), lambda i: (i, 0, 0)),
                  pl.BlockSpec((gb, 256, S5_CW), lambda i: (i, 0, 0)),
                  pl.BlockSpec((gb, 8, 128), lambda i: (i, 0, 0))],
        out_specs=pl.BlockSpec((gb, rows, S5_CW), lambda i: (i, 0, 0)),
        out_shape=jax.ShapeDtypeStruct((n_groups, rows, S5_CW), F32),
        scratch_shapes=[pltpu.VMEM((gb, rows, 512), F32), pltpu.VMEM((gb, rows, 256), F32)],
        compiler_params=_params(1),
        name="s5_chunked",
    )(u_g, m_intra, m_in, m_out, coef)


def _s5_chunk_operators(lam_re, lam_im, log_step, b_re, b_im, c_re, c_im, d_skip):
    hp = lax.Precision.HIGHEST
    t = S5_CHUNK
    lr = jnp.minimum(lam_re, -1e-4)
    li = lam_im
    dt = jnp.exp(log_step)[..., None]
    mag = jnp.exp(lr * dt)
    ar = mag * jnp.cos(li * dt)
    ai = mag * jnp.sin(li * dt)
    den = lr * lr + li * li
    cr = ((ar - 1.0) * lr + ai * li) / den
    ci = (ai * lr - (ar - 1.0) * li) / den
    bbr = cr[..., None] * b_re - ci[..., None] * b_im
    bbi = cr[..., None] * b_im + ci[..., None] * b_re
    taus = jnp.arange(t + 1, dtype=F32)[:, None, None, None]
    pmag = jnp.exp(taus * (lr * dt)[None])
    pr = pmag * jnp.cos(taus * (li * dt)[None])
    pi = pmag * jnp.sin(taus * (li * dt)[None])

    car = pr[:t, :, :, None, :] * c_re[None] - pi[:t, :, :, None, :] * c_im[None]
    cai = pr[:t, :, :, None, :] * c_im[None] + pi[:t, :, :, None, :] * c_re[None]
    taps = (jnp.einsum('tdgnp,dgpm->tdgnm', car, bbr, precision=hp)
            - jnp.einsum('tdgnp,dgpm->tdgnm', cai, bbi, precision=hp))
    i_idx = jnp.arange(t)[:, None]
    j_idx = jnp.arange(t)[None, :]
    lag = j_idx - i_idx
    kf = taps[jnp.clip(lag, 0, t - 1), 0]
    kb = taps[jnp.clip(-lag, 0, t - 1), 1]
    sel = lambda cond, a: jnp.where(cond[:, :, None, None, None], a, 0.0)
    m_intra = sel(lag >= 0, kf) + sel(lag <= 0, kb)
    n_ch = S5_GROUP
    skip = d_skip.reshape(S5_GROUPS, n_ch)[:, :, None] * jnp.eye(n_ch, dtype=F32)[None]
    m_intra = m_intra + sel(lag == 0, jnp.broadcast_to(skip[None, None], m_intra.shape))
    m_intra = m_intra.transpose(2, 0, 4, 1, 3).reshape(S5_GROUPS, S5_CW, S5_CW)

    def in_block(power_r, power_i, d):
        wr = power_r[:, :, :, None] * bbr[d][None] - power_i[:, :, :, None] * bbi[d][None]
        wi = power_r[:, :, :, None] * bbi[d][None] + power_i[:, :, :, None] * bbr[d][None]
        wr = wr.transpose(1, 0, 3, 2).reshape(S5_GROUPS, S5_CW, S5_STATE)
        wi = wi.transpose(1, 0, 3, 2).reshape(S5_GROUPS, S5_CW, S5_STATE)
        return [wr, wi, wi, wr]
    rev = jnp.arange(t - 1, -1, -1)
    m_in = jnp.concatenate(in_block(pr[rev, 0], pi[rev, 0], 0) + in_block(pr[:t, 1], pi[:t, 1], 1), axis=-1)

    def out_block(power_r, power_i, d):
        er = power_r[:, :, None, :] * c_re[d][None] - power_i[:, :, None, :] * c_im[d][None]
        ei = power_r[:, :, None, :] * c_im[d][None] + power_i[:, :, None, :] * c_re[d][None]
        er = er.transpose(1, 3, 0, 2).reshape(S5_GROUPS, S5_STATE, S5_CW)
        ei = ei.transpose(1, 3, 0, 2).reshape(S5_GROUPS, S5_STATE, S5_CW)
        return [er, -ei]
    up = jnp.arange(1, t + 1)
    down = jnp.arange(t, 0, -1)
    m_out = jnp.concatenate(out_block(pr[up, 0], pi[up, 0], 0) + out_block(pr[down, 1], pi[down, 1], 1), axis=1)

    a16r, a16i = pr[t], pi[t]
    rows = []
    for d in range(2):
        rows += [jnp.concatenate([a16r[d], a16r[d]], -1),
                 jnp.concatenate([-a16i[d], a16i[d]], -1),
                 jnp.concatenate([a16i[d], -a16i[d]], -1)]
    rows += [jnp.zeros_like(rows[0])] * 2
    coef = jnp.stack(rows, axis=1)
    return m_intra.astype(BF16), m_in.astype(BF16), m_out.astype(BF16), coef


def _mlp(x1, g_ref, w1_ref, w2_ref):
    h = _rmsnorm_rows(x1, g_ref[...]).astype(BF16)
    acc = x1
    for j in range(D_FF // FF_TILE):
        hid = _dot(h, w1_ref[:, j * FF_TILE:(j + 1) * FF_TILE])
        hid = jnp.square(jnp.maximum(hid, 0.0)).astype(BF16)
        acc = acc + _dot(hid, w2_ref[j * FF_TILE:(j + 1) * FF_TILE, :])
    return acc


def _post_even_kernel(x_ref, a_ref, y_ref, wglu_ref, bglu_ref, wout_ref, g_ref, w1_ref, w2_ref, o_ref):
    y = _gelu_tanh(y_ref[...])
    b_out = y * _sigmoid(_dot(y.astype(BF16), wglu_ref[...]) + bglu_ref[...])
    mix = _dot(a_ref[...], wout_ref[0:ATTN_WIDTH, :]) + _dot(b_out.astype(BF16), wout_ref[ATTN_WIDTH:, :])
    o_ref[...] = _mlp(x_ref[...] + mix, g_ref, w1_ref, w2_ref)


def _post_even(x2, a_out, y_s5, w_glu, b_glu, w_out, g_mlp, w1, w2):
    n_tok = x2.shape[0]
    tm = TOKEN_TILE
    return pl.pallas_call(
        _post_even_kernel,
        grid=(n_tok // tm,),
        in_specs=[pl.BlockSpec((tm, D_MODEL), lambda i: (i, 0)),
                  pl.BlockSpec((tm, ATTN_WIDTH), lambda i: (i, 0)),
                  pl.BlockSpec((tm, S5_WIDTH), lambda i: (i, 0)),
                  _const_spec(w_glu.shape), _const_spec(b_glu.shape), _const_spec(w_out.shape),
                  _const_spec(g_mlp.shape), _const_spec(w1.shape), _const_spec(w2.shape)],
        out_specs=pl.BlockSpec((tm, D_MODEL), lambda i: (i, 0)),
        out_shape=jax.ShapeDtypeStruct((n_tok, D_MODEL), F32),
        compiler_params=_params(1),
        name="post_even",
    )(x2, a_out, y_s5, w_glu, b_glu, w_out, g_mlp, w1, w2)


def _post_odd_kernel(x_ref, m_ref, wout_ref, g_ref, w1_ref, w2_ref, gf_ref, o_ref):
    x1 = x_ref[...] + _dot(m_ref[...], wout_ref[...])
    o_ref[...] = _rmsnorm_rows(_mlp(x1, g_ref, w1_ref, w2_ref), gf_ref[...])


def _post_odd(x2, mixed, w_out, g_mlp, w1, w2, g_final):
    n_tok = x2.shape[0]
    tm = TOKEN_TILE
    return pl.pallas_call(
        _post_odd_kernel,
        grid=(n_tok // tm,),
        in_specs=[pl.BlockSpec((tm, D_MODEL), lambda i: (i, 0)),
                  pl.BlockSpec((tm, D_MODEL), lambda i: (i, 0)),
                  _const_spec(w_out.shape), _const_spec(g_mlp.shape),
                  _const_spec(w1.shape), _const_spec(w2.shape), _const_spec(g_final.shape)],
        out_specs=pl.BlockSpec((tm, D_MODEL), lambda i: (i, 0)),
        out_shape=jax.ShapeDtypeStruct((n_tok, D_MODEL), F32),
        compiler_params=_params(1),
        name="post_odd",
    )(x2, mixed, w_out, g_mlp, w1, w2, g_final)


def _pre_odd_kernel(x_ref, g_ref, w_ref, q_ref, i_ref, ff_ref, fb_ref, gate_ref):
    h = _rmsnorm_rows(x_ref[...], g_ref[...]).astype(BF16)
    for s, out_ref in enumerate((q_ref, i_ref, ff_ref, fb_ref, gate_ref)):
        out_ref[...] = _dot(h, w_ref[:, s * D_MODEL:(s + 1) * D_MODEL]).astype(out_ref.dtype)


def _pre_odd(x2, g, w):
    n_tok = x2.shape[0]
    tm = TOKEN_TILE
    row_spec = pl.BlockSpec((tm, D_MODEL), lambda i: (i, 0))
    lo = jax.ShapeDtypeStruct((n_tok, D_MODEL), BF16)
    hi = jax.ShapeDtypeStruct((n_tok, D_MODEL), F32)
    return pl.pallas_call(
        _pre_odd_kernel,
        grid=(n_tok // tm,),
        in_specs=[row_spec, _const_spec((1, D_MODEL)), _const_spec(w.shape)],
        out_specs=[row_spec] * 5,
        out_shape=[lo, lo, hi, hi, lo],
        compiler_params=_params(1),
        name="pre_odd",
    )(x2, g, w)


def _chunk_cumsum(x, reverse):
    n = x.shape[0]
    pos = lax.broadcasted_iota(jnp.int32, x.shape, 0) % HGRN_CHUNK
    s = 1
    while s < HGRN_CHUNK:
        if reverse:
            shifted = pltpu.roll(x, n - s, 0)
            x = x + jnp.where(pos < HGRN_CHUNK - s, shifted, 0.0)
        else:
            shifted = pltpu.roll(x, s, 0)
            x = x + jnp.where(pos >= s, shifted, 0.0)
        s *= 2
    return x


def _hgrn_kernel(layer, q_ref, v_ref, ff_ref, fb_ref, gate_ref, lbl_ref, ng_ref, o_ref,
                 k_ref, b_ref, acc_ref):
    seq = q_ref.shape[1]
    n_chunks = seq // HGRN_CHUNK
    ch = HGRN_CHUNK
    logits = lbl_ref[...]
    soft = jnp.exp(logits - jnp.max(logits, axis=0, keepdims=True))
    soft = soft / jnp.sum(soft, axis=0, keepdims=True)
    lb = jnp.sum(soft[0:layer + 1], axis=0, keepdims=True) - soft[0:1]

    for d, f_ref in enumerate((ff_ref, fb_ref)):
        fl = f_ref[0]
        f = lb + (1.0 - lb) * _sigmoid(fl)
        k_ref[d] = (1.0 - lb) * _sigmoid(-fl)
        b_ref[d] = _chunk_cumsum(jnp.log(f), reverse=(d == 1))

    row = lax.broadcasted_iota(jnp.int32, (ch, ch), 0)
    col = lax.broadcasted_iota(jnp.int32, (ch, ch), 1)
    masks = (row >= col, row <= col)

    def chunk_step(n, states):
        new_states = []
        for d in range(2):
            c = n if d == 0 else n_chunks - 1 - n
            sl = pl.ds(pl.multiple_of(c * ch, ch), ch)
            b = b_ref[d, sl, :]
            kk = k_ref[d, sl, :]
            b_end = b[ch - 1:ch, :] if d == 0 else b[0:1, :]
            q_dec = (q_ref[0, sl, :].astype(F32) * jnp.exp(b)).astype(BF16)
            k_inv = (kk * jnp.exp(-b)).astype(BF16)
            k_dec = (kk * jnp.exp(b_end - b)).astype(BF16)
            v = v_ref[0, sl, :]
            st = states[d]
            scores = jnp.where(masks[d], _dot_nt(q_dec, k_inv), 0.0).astype(BF16)
            acc_ref[d, sl, :] = _dot(scores, v) + _dot_nt(q_dec, st.astype(BF16))
            new_states.append(jnp.exp(b_end) * st + _dot_tn(v, k_dec))
        return tuple(new_states)

    zero_state = jnp.zeros((128, 128), F32)
    lax.fori_loop(0, n_chunks, chunk_step, (zero_state, zero_state))

    o = _rmsnorm_rows(acc_ref[0] + acc_ref[1], ng_ref[...])
    o_ref[0] = (o * _sigmoid(gate_ref[0].astype(F32))).astype(o_ref.dtype)


def _hgrn(q, v, ff, fb, gate, lb_logits, norm_g, layer):
    bsz, seq, _ = q.shape
    head_spec = pl.BlockSpec((1, seq, 128), lambda b, h: (b, 0, h))
    return pl.pallas_call(
        functools.partial(_hgrn_kernel, layer),
        grid=(bsz, HGRN_HEADS),
        in_specs=[head_spec] * 5 + [pl.BlockSpec((DEPTH, 128), lambda b, h: (0, h)),
                                    pl.BlockSpec((1, 128), lambda b, h: (0, h))],
        out_specs=head_spec,
        out_shape=jax.ShapeDtypeStruct((bsz, seq, D_MODEL), BF16),
        scratch_shapes=[pltpu.VMEM((2, seq, 128), F32)] * 3,
        compiler_params=_params(2),
        name="hgrn2",
    )(q, v, ff, fb, gate, lb_logits, norm_g)


def _rope_pair_tables(seq):
    inv = ROPE_THETA ** (-jnp.arange(0, ATTN_QK_DIM, 2, dtype=F32) / ATTN_QK_DIM)
    ang = jnp.arange(seq, dtype=F32)[:, None] * inv[None, :]
    return jnp.tile(jnp.cos(ang), (1, 4)), jnp.tile(jnp.sin(ang), (1, 4))


def _pair_layout_columns():
    idx = jnp.arange(ATTN_QK_WIDTH).reshape(2, 2, 2, 2, 32)
    return idx.transpose(0, 3, 1, 2, 4).reshape(-1)


def kernel(x, norm_mix_g, norm_mlp_g, final_norm_g, w_ff_in, w_ff_out, w_in_even, w_out_even, diff_lambda, diff_subln_g, s5_lam_re, s5_lam_im, s5_log_step, s5_b_re, s5_b_im, s5_c_re, s5_c_im, s5_d, s5_w_glu, s5_b_glu, w_in_odd, w_out_odd, hgrn_norm_g, hgrn_lb_logits):
    bsz, seq, _ = x.shape
    n_tok = bsz * seq
    x2 = x.reshape(n_tok, D_MODEL)
    cos, sin = _rope_pair_tables(seq)
    perm = _pair_layout_columns()
    n_chunks = seq // S5_CHUNK

    for layer in range(DEPTH):
        g_mix = norm_mix_g[layer].reshape(1, D_MODEL)
        g_mlp = norm_mlp_g[layer].reshape(1, D_MODEL)
        w1 = w_ff_in[layer].astype(BF16)
        w2 = w_ff_out[layer].astype(BF16)
        if layer % 2 == 0:
            e = layer // 2
            w = w_in_even[e]
            w = jnp.concatenate([w[:, :ATTN_QK_WIDTH][:, perm],
                                 w[:, ATTN_QK_WIDTH:2 * ATTN_QK_WIDTH][:, perm],
                                 w[:, 2 * ATTN_QK_WIDTH:]], axis=1).astype(BF16)
            q, k, v, u = _pre_even(x2, g_mix, w, cos, sin, seq)
            lambda_init = 0.8 - 0.6 * math.exp(-0.3 * layer)
            a_out = _attention(q.reshape(bsz, seq, -1), k.reshape(bsz, seq, -1), v.reshape(bsz, seq, -1),
                               diff_lambda[e], diff_subln_g[e].reshape(1, ATTN_V_DIM), lambda_init)
            ops = _s5_chunk_operators(s5_lam_re[e], s5_lam_im[e], s5_log_step[e], s5_b_re[e], s5_b_im[e],
                                      s5_c_re[e], s5_c_im[e], s5_d[e])
            u_g = (u.reshape(bsz, n_chunks, S5_CHUNK, S5_GROUPS, S5_GROUP)
                   .transpose(3, 1, 0, 2, 4).reshape(S5_GROUPS, n_chunks * bsz, S5_CW))
            y_g = _s5_chunked(u_g, *ops, n_chunks, bsz)
            y = (y_g.reshape(S5_GROUPS, n_chunks, bsz, S5_CHUNK, S5_GROUP)
                 .transpose(2, 1, 3, 0, 4).reshape(n_tok, S5_WIDTH))
            x2 = _post_even(x2, a_out.reshape(n_tok, ATTN_WIDTH), y, s5_w_glu[e].astype(BF16),
                            s5_b_glu[e].reshape(1, S5_WIDTH), w_out_even[e].astype(BF16), g_mlp, w1, w2)
        else:
            o_i = layer // 2
            q, v, ff, fb, gate = _pre_odd(x2, g_mix, w_in_odd[o_i].astype(BF16))
            shp = (bsz, seq, D_MODEL)
            mixed = _hgrn(q.reshape(shp), v.reshape(shp), ff.reshape(shp), fb.reshape(shp), gate.reshape(shp),
                          hgrn_lb_logits, hgrn_norm_g[o_i].reshape(1, D_MODEL), layer)
            assert layer == DEPTH - 1, "the final norm is fused into the last (odd) layer's kernel"
            x2 = _post_odd(x2, mixed.reshape(n_tok, D_MODEL), w_out_odd[o_i].astype(BF16), g_mlp, w1, w2,
                           final_norm_g.reshape(1, D_MODEL))
    return x2.reshape(bsz, seq, D_MODEL)
```

```python
import functools
import math

import jax
import jax.numpy as jnp
from jax import lax
from jax.experimental import pallas as pl
from jax.experimental.pallas import tpu as pltpu

D_MODEL = 1024
DEPTH = 2
ATTN_HEADS = 4
ATTN_QK_DIM = 64
ATTN_V_DIM = 128
ATTN_QK_WIDTH = 512
ATTN_WIDTH = 512
ROPE_THETA = 10000.0
S5_WIDTH = 512
S5_GROUP = 16
S5_GROUPS = 32
S5_STATE = 64
HGRN_HEADS = 8
HGRN_CHUNK = 64
D_FF = 4096
EPS = 1e-6

S5_CHUNK = 16
S5_CW = S5_CHUNK * S5_GROUP
S5_PHASES = 4
S5_GROUP_BLOCK = 4

TOKEN_TILE = 512
ATTN_Q_TILE = 256
FF_TILE = 1024
VMEM_LIMIT = 56 * 1024 * 1024

BF16 = jnp.bfloat16
F32 = jnp.float32


def _const_spec(shape):
    nd = len(shape)
    return pl.BlockSpec(shape, lambda *_: (0,) * nd, pipeline_mode=pl.Buffered(1))


def _params(n_axes):
    return pltpu.CompilerParams(dimension_semantics=("arbitrary",) * n_axes,
                                vmem_limit_bytes=VMEM_LIMIT)


def _rmsnorm_rows(x, g):
    ms = jnp.mean(x * x, axis=-1, keepdims=True)
    return x * lax.rsqrt(ms + EPS) * g


def _sigmoid(x):
    return 1.0 / (1.0 + jnp.exp(-x))


def _gelu_tanh(x):
    c = math.sqrt(2.0 / math.pi)
    return 0.5 * x * (1.0 + jnp.tanh(c * (x + 0.044715 * (x * x * x))))


def _dot(a, b):
    return jnp.dot(a, b, preferred_element_type=F32)


def _dot_nt(a, b):
    return lax.dot_general(a, b, (((1,), (1,)), ((), ())), preferred_element_type=F32)


def _dot_tn(a, b):
    return lax.dot_general(a, b, (((0,), (0,)), ((), ())), preferred_element_type=F32)


def _gather_phase_rows(ref, width):
    return jnp.concatenate([ref[0, :, k * width:(k + 1) * width] for k in range(S5_PHASES)], axis=0)


def _scatter_phase_rows(out_ref, width, col, val):
    nc = out_ref.shape[1]
    for ph in range(S5_PHASES):
        out_ref[0, :, ph * width + col:ph * width + col + val.shape[1]] = val[ph * nc:(ph + 1) * nc]


def _phase_spec(n_chunks, width):
    return pl.BlockSpec((1, n_chunks, S5_PHASES * width), lambda b, p: (b, 0, p))


def _channel_major_spec(n_chunks):
    return pl.BlockSpec((S5_GROUPS, S5_PHASES, S5_GROUP, n_chunks), lambda b, p: (0, p, 0, b))


def _pre_even_kernel(x_ref, g_ref, w_ref, wut_ref, cos_ref, sin_ref, q_ref, k_ref, v_ref, ut_ref):
    nc = x_ref.shape[1]
    h = _rmsnorm_rows(_gather_phase_rows(x_ref, D_MODEL), g_ref[...]).astype(BF16)
    cos = jnp.concatenate([cos_ref[:, k * 128:(k + 1) * 128] for k in range(S5_PHASES)], axis=0)
    sin = jnp.concatenate([sin_ref[:, k * 128:(k + 1) * 128] for k in range(S5_PHASES)], axis=0)
    for out_ref, base, scale in ((q_ref, 0, ATTN_QK_DIM ** -0.5), (k_ref, ATTN_QK_WIDTH, 1.0)):
        p = _dot(h, w_ref[:, base:base + ATTN_QK_WIDTH])
        for pair in range(2):
            lo = p[:, 256 * pair:256 * pair + 128]
            hi = p[:, 256 * pair + 128:256 * pair + 256]
            _scatter_phase_rows(out_ref, 512, 256 * pair, ((lo * cos - hi * sin) * scale).astype(BF16))
            _scatter_phase_rows(out_ref, 512, 256 * pair + 128, ((hi * cos + lo * sin) * scale).astype(BF16))
    _scatter_phase_rows(v_ref, 512, 0, _dot(h, w_ref[:, 2 * ATTN_QK_WIDTH:]).astype(BF16))
    ut = _dot_nt(wut_ref[...], h).astype(BF16)
    for ph in range(S5_PHASES):
        ut_ref[:, ph, :, :] = ut[:, ph * nc:(ph + 1) * nc].reshape(S5_GROUPS, S5_GROUP, nc)


def _pre_even(x3, g, w_qkv, wu_t, cos, sin):
    bsz, n_chunks, _ = x3.shape
    qkv = jax.ShapeDtypeStruct((bsz, n_chunks, S5_CHUNK * 512), BF16)
    rope_spec = pl.BlockSpec((n_chunks, S5_PHASES * 128), lambda b, p: (0, p))
    return pl.pallas_call(
        _pre_even_kernel,
        grid=(bsz, S5_CHUNK // S5_PHASES),
        in_specs=[_phase_spec(n_chunks, D_MODEL), _const_spec((1, D_MODEL)), _const_spec(w_qkv.shape),
                  _const_spec(wu_t.shape), rope_spec, rope_spec],
        out_specs=[_phase_spec(n_chunks, 512)] * 3 + [_channel_major_spec(n_chunks)],
        out_shape=[qkv, qkv, qkv,
                   jax.ShapeDtypeStruct((S5_GROUPS, S5_CHUNK, S5_GROUP, bsz * n_chunks), BF16)],
        compiler_params=_params(2),
        name="pre_even",
    )(x3, g, w_qkv, wu_t, cos, sin)


def _attn_kernel(lambda_init, q_ref, k_ref, v_ref, lam_ref, g_ref, o_ref):
    q = q_ref[0]
    k = k_ref[0]
    lam = lam_ref[...]
    lam_val = (jnp.exp(jnp.sum(lam[0:1] * lam[1:2], axis=-1, keepdims=True))
               - jnp.exp(jnp.sum(lam[2:3] * lam[3:4], axis=-1, keepdims=True)) + lambda_init)
    lane_group = (lax.broadcasted_iota(jnp.int32, q.shape, 1) // 32) % 4
    zero = jnp.zeros_like(q)
    for hh in range(2):
        probs = []
        for c in range(2):
            qm = jnp.where(lane_group == 2 * hh + c, q, zero)
            s = _dot_nt(qm, k)
            m = jnp.max(s, axis=-1, keepdims=True)
            e = jnp.exp(s - m)
            l = jnp.sum(e, axis=-1, keepdims=True)
            probs.append((e, 1.0 / l))
        w = probs[0][0] * probs[0][1] - probs[1][0] * (lam_val * probs[1][1])
        o = _dot(w.astype(BF16), v_ref[0, :, 128 * hh:128 * hh + 128])
        o = _rmsnorm_rows(o, g_ref[...]) * (1.0 - lambda_init)
        o_ref[0, :, 128 * hh:128 * hh + 128] = o.astype(o_ref.dtype)


def _attention(q, k, v, lam, subln_g, lambda_init):
    bsz, seq, _ = q.shape
    tq = ATTN_Q_TILE
    kv_spec = pl.BlockSpec((1, seq, 256), lambda b, p, i: (b, 0, p))
    return pl.pallas_call(
        functools.partial(_attn_kernel, lambda_init),
        grid=(bsz, 2, seq // tq),
        in_specs=[pl.BlockSpec((1, tq, 256), lambda b, p, i: (b, i, p)),
                  kv_spec, kv_spec,
                  _const_spec(lam.shape),
                  _const_spec(subln_g.shape)],
        out_specs=pl.BlockSpec((1, tq, 256), lambda b, p, i: (b, i, p)),
        out_shape=jax.ShapeDtypeStruct((bsz, seq, ATTN_WIDTH), BF16),
        compiler_params=_params(3),
        name="diff_attention",
    )(q, k, v, lam, subln_g)


def _s5_kernel(n_chunks, ut_ref, mi_ref, min_ref, mo_ref, coef_ref, y_ref, z_ref):
    n_lanes = ut_ref.shape[2]
    pos = lax.broadcasted_iota(jnp.int32, (S5_STATE, n_lanes), 1) % n_chunks
    n_steps = n_chunks.bit_length() - 1

    def one_group(g, carry):
        ut = ut_ref[g]
        st = _dot(min_ref[g], ut)
        coef = coef_ref[g]
        for d in range(2):
            xr = st[128 * d:128 * d + S5_STATE]
            xi = st[128 * d + S5_STATE:128 * d + 2 * S5_STATE]
            for k in range(n_steps):
                s = 1 << k
                if d == 0:
                    shift, keep = s, pos >= s
                else:
                    shift, keep = n_lanes - s, pos < n_chunks - s
                pr = jnp.where(keep, coef[S5_STATE * d:S5_STATE * (d + 1), k:k + 1], 0.0)
                pi = jnp.where(keep, coef[S5_STATE * d:S5_STATE * (d + 1), 8 + k:9 + k], 0.0)
                sr = pltpu.roll(xr, shift, 1)
                si = pltpu.roll(xi, shift, 1)
                xr, xi = xr + pr * sr - pi * si, xi + pr * si + pi * sr
            if d == 0:
                shift, keep = 1, pos >= 1
            else:
                shift, keep = n_lanes - 1, pos < n_chunks - 1
            z_ref[128 * d:128 * d + S5_STATE] = jnp.where(keep, pltpu.roll(xr, shift, 1), 0.0).astype(BF16)
            z_ref[128 * d + S5_STATE:128 * (d + 1)] = jnp.where(keep, pltpu.roll(xi, shift, 1), 0.0).astype(BF16)
        y_ref[g] = _dot(mi_ref[g], ut) + _dot(mo_ref[g], z_ref[...])
        return carry

    lax.fori_loop(0, ut_ref.shape[0], one_group, 0)


def _s5_chunked(u_t, m_intra_t, m_in_t, m_out_t, coef, n_chunks):
    n_groups, _, n_lanes = u_t.shape
    gb = S5_GROUP_BLOCK
    mat_spec = pl.BlockSpec((gb, S5_CW, S5_CW), lambda i: (i, 0, 0))
    return pl.pallas_call(
        functools.partial(_s5_kernel, n_chunks),
        grid=(n_groups // gb,),
        in_specs=[pl.BlockSpec((gb, S5_CW, n_lanes), lambda i: (i, 0, 0)), mat_spec, mat_spec, mat_spec,
                  pl.BlockSpec((gb, 2 * S5_STATE, 16), lambda i: (i, 0, 0))],
        out_specs=pl.BlockSpec((gb, S5_CW, n_lanes), lambda i: (i, 0, 0)),
        out_shape=jax.ShapeDtypeStruct((n_groups, S5_CW, n_lanes), F32),
        scratch_shapes=[pltpu.VMEM((4 * S5_STATE, n_lanes), BF16)],
        compiler_params=_params(1),
        name="s5_chunked",
    )(u_t, m_intra_t, m_in_t, m_out_t, coef)


def _s5_chunk_operators(lam_re, lam_im, log_step, b_re, b_im, c_re, c_im, d_skip, n_steps):
    hp = lax.Precision.HIGHEST
    t = S5_CHUNK
    lr = jnp.minimum(lam_re, -1e-4)
    li = lam_im
    dt = jnp.exp(log_step)[..., None]
    mag = jnp.exp(lr * dt)
    ar = mag * jnp.cos(li * dt)
    ai = mag * jnp.sin(li * dt)
    den = lr * lr + li * li
    cr = ((ar - 1.0) * lr + ai * li) / den
    ci = (ai * lr - (ar - 1.0) * li) / den
    bbr = cr[..., None] * b_re - ci[..., None] * b_im
    bbi = cr[..., None] * b_im + ci[..., None] * b_re

    def powers(taus):
        taus = taus.astype(F32)
        pmag = jnp.exp(taus * (lr * dt)[..., None])
        ang = taus * (li * dt)[..., None]
        return pmag * jnp.cos(ang), pmag * jnp.sin(ang)

    pr, pi = powers(jnp.arange(t))
    rr = pr[..., None] * bbr[:, :, :, None, :] - pi[..., None] * bbi[:, :, :, None, :]
    ri = pr[..., None] * bbi[:, :, :, None, :] + pi[..., None] * bbr[:, :, :, None, :]
    rf_r = rr[0, :, :, ::-1].reshape(S5_GROUPS, S5_STATE, S5_CW)
    rf_i = ri[0, :, :, ::-1].reshape(S5_GROUPS, S5_STATE, S5_CW)
    rb_r = rr[1].reshape(S5_GROUPS, S5_STATE, S5_CW)
    rb_i = ri[1].reshape(S5_GROUPS, S5_STATE, S5_CW)
    m_in_t = jnp.concatenate([rf_r, rf_i, rb_r, rb_i], axis=1)

    taps_f = (jnp.einsum('gnp,gpx->gnx', c_re[0], rf_r, precision=hp)
              - jnp.einsum('gnp,gpx->gnx', c_im[0], rf_i, precision=hp))
    taps_b = (jnp.einsum('gnp,gpx->gnx', c_re[1], rb_r, precision=hp)
              - jnp.einsum('gnp,gpx->gnx', c_im[1], rb_i, precision=hp))
    skip = d_skip.reshape(S5_GROUPS, S5_GROUP)[:, :, None] * jnp.eye(S5_GROUP, dtype=F32)[None]
    edge = (t - 1) * S5_GROUP
    taps = (jnp.pad(taps_f, ((0, 0), (0, 0), (0, edge))) + jnp.pad(taps_b, ((0, 0), (0, 0), (edge, 0)))
            + jnp.pad(skip, ((0, 0), (0, 0), (edge, edge))))
    m_intra_t = jnp.stack([taps[:, :, (t - 1 - j) * S5_GROUP:(t - 1 - j) * S5_GROUP + S5_CW]
                           for j in range(t)], axis=1).reshape(S5_GROUPS, S5_CW, S5_CW)

    qr, qi = powers(jnp.arange(1, t + 1))
    qr = qr.transpose(0, 1, 3, 2)
    qi = qi.transpose(0, 1, 3, 2)
    er = qr[:, :, :, None, :] * c_re[:, :, None] - qi[:, :, :, None, :] * c_im[:, :, None]
    ei = qr[:, :, :, None, :] * c_im[:, :, None] + qi[:, :, :, None, :] * c_re[:, :, None]
    m_out_t = jnp.concatenate([er[0], -ei[0], er[1, :, ::-1], -ei[1, :, ::-1]],
                              axis=-1).reshape(S5_GROUPS, S5_CW, 4 * S5_STATE)

    ar16, ai16 = qr[:, :, t - 1], qi[:, :, t - 1]
    cols_r, cols_i = [], []
    for _ in range(n_steps):
        cols_r.append(ar16)
        cols_i.append(ai16)
        ar16, ai16 = ar16 * ar16 - ai16 * ai16, 2.0 * ar16 * ai16
    pad = [jnp.zeros_like(ar16)] * (8 - n_steps)
    coef = jnp.stack(cols_r + pad + cols_i + pad, axis=-1)
    coef = coef.transpose(1, 0, 2, 3).reshape(S5_GROUPS, 2 * S5_STATE, 16)
    return m_intra_t.astype(BF16), m_in_t.astype(BF16), m_out_t.astype(BF16), coef


def _mlp(x1, g_ref, w1_ref, w2_ref):
    h = _rmsnorm_rows(x1, g_ref[...]).astype(BF16)
    acc = x1
    for j in range(D_FF // FF_TILE):
        hid = _dot(h, w1_ref[:, j * FF_TILE:(j + 1) * FF_TILE])
        hid = jnp.square(jnp.maximum(hid, 0.0)).astype(BF16)
        acc = acc + _dot(hid, w2_ref[j * FF_TILE:(j + 1) * FF_TILE, :])
    return acc


def _post_even_kernel(x_ref, a_ref, yt_ref, wglut_ref, bglu_ref, wout_ref, g_ref, w1_ref, w2_ref, o_ref):
    nc = x_ref.shape[1]
    yt = jnp.concatenate([yt_ref[:, ph, :, :].reshape(S5_WIDTH, nc) for ph in range(S5_PHASES)], axis=1)
    yt = _gelu_tanh(yt)
    bt = yt * _sigmoid(_dot(wglut_ref[...], yt.astype(BF16)) + bglu_ref[...])
    mix = (_dot(_gather_phase_rows(a_ref, ATTN_WIDTH), wout_ref[0:ATTN_WIDTH, :])
           + _dot_tn(bt.astype(BF16), wout_ref[ATTN_WIDTH:, :]))
    out = _mlp(_gather_phase_rows(x_ref, D_MODEL) + mix, g_ref, w1_ref, w2_ref)
    _scatter_phase_rows(o_ref, D_MODEL, 0, out)


def _post_even(x3, a3, y_t, w_glu_t, b_glu_col, w_out, g_mlp, w1, w2):
    bsz, n_chunks, _ = x3.shape
    return pl.pallas_call(
        _post_even_kernel,
        grid=(bsz, S5_CHUNK // S5_PHASES),
        in_specs=[_phase_spec(n_chunks, D_MODEL), _phase_spec(n_chunks, ATTN_WIDTH),
                  _channel_major_spec(n_chunks),
                  _const_spec(w_glu_t.shape), _const_spec(b_glu_col.shape), _const_spec(w_out.shape),
                  _const_spec(g_mlp.shape), _const_spec(w1.shape), _const_spec(w2.shape)],
        out_specs=_phase_spec(n_chunks, D_MODEL),
        out_shape=jax.ShapeDtypeStruct(x3.shape, F32),
        compiler_params=_params(2),
        name="post_even",
    )(x3, a3, y_t, w_glu_t, b_glu_col, w_out, g_mlp, w1, w2)


def _post_odd_kernel(x_ref, m_ref, wout_ref, g_ref, w1_ref, w2_ref, gf_ref, o_ref):
    x1 = x_ref[...] + _dot(m_ref[...], wout_ref[...])
    o_ref[...] = _rmsnorm_rows(_mlp(x1, g_ref, w1_ref, w2_ref), gf_ref[...])


def _post_odd(x2, mixed, w_out, g_mlp, w1, w2, g_final):
    n_tok = x2.shape[0]
    tm = TOKEN_TILE
    return pl.pallas_call(
        _post_odd_kernel,
        grid=(n_tok // tm,),
        in_specs=[pl.BlockSpec((tm, D_MODEL), lambda i: (i, 0)),
                  pl.BlockSpec((tm, D_MODEL), lambda i: (i, 0)),
                  _const_spec(w_out.shape), _const_spec(g_mlp.shape),
                  _const_spec(w1.shape), _const_spec(w2.shape), _const_spec(g_final.shape)],
        out_specs=pl.BlockSpec((tm, D_MODEL), lambda i: (i, 0)),
        out_shape=jax.ShapeDtypeStruct((n_tok, D_MODEL), F32),
        compiler_params=_params(1),
        name="post_odd",
    )(x2, mixed, w_out, g_mlp, w1, w2, g_final)


def _pre_odd_kernel(x_ref, g_ref, w_ref, q_ref, i_ref, ff_ref, fb_ref, gate_ref):
    h = _rmsnorm_rows(x_ref[...], g_ref[...]).astype(BF16)
    for s, out_ref in enumerate((q_ref, i_ref, ff_ref, fb_ref, gate_ref)):
        out_ref[...] = _dot(h, w_ref[:, s * D_MODEL:(s + 1) * D_MODEL]).astype(out_ref.dtype)


def _pre_odd(x2, g, w):
    n_tok = x2.shape[0]
    tm = TOKEN_TILE
    row_spec = pl.BlockSpec((tm, D_MODEL), lambda i: (i, 0))
    lo = jax.ShapeDtypeStruct((n_tok, D_MODEL), BF16)
    hi = jax.ShapeDtypeStruct((n_tok, D_MODEL), F32)
    return pl.pallas_call(
        _pre_odd_kernel,
        grid=(n_tok // tm,),
        in_specs=[row_spec, _const_spec((1, D_MODEL)), _const_spec(w.shape)],
        out_specs=[row_spec] * 5,
        out_shape=[lo, lo, hi, hi, lo],
        compiler_params=_params(1),
        name="pre_odd",
    )(x2, g, w)


def _chunk_cumsum(x, reverse):
    n = x.shape[0]
    pos = lax.broadcasted_iota(jnp.int32, x.shape, 0) % HGRN_CHUNK
    s = 1
    while s < HGRN_CHUNK:
        if reverse:
            shifted = pltpu.roll(x, n - s, 0)
            x = x + jnp.where(pos < HGRN_CHUNK - s, shifted, 0.0)
        else:
            shifted = pltpu.roll(x, s, 0)
            x = x + jnp.where(pos >= s, shifted, 0.0)
        s *= 2
    return x


def _hgrn_kernel(layer, q_ref, v_ref, ff_ref, fb_ref, gate_ref, lbl_ref, ng_ref, o_ref,
                 k_ref, b_ref, acc_ref):
    seq = q_ref.shape[1]
    n_chunks = seq // HGRN_CHUNK
    ch = HGRN_CHUNK
    logits = lbl_ref[...]
    soft = jnp.exp(logits - jnp.max(logits, axis=0, keepdims=True))
    soft = soft / jnp.sum(soft, axis=0, keepdims=True)
    lb = jnp.sum(soft[0:layer + 1], axis=0, keepdims=True) - soft[0:1]

    for d, f_ref in enumerate((ff_ref, fb_ref)):
        fl = f_ref[0]
        f = lb + (1.0 - lb) * _sigmoid(fl)
        k_ref[d] = (1.0 - lb) * _sigmoid(-fl)
        b_ref[d] = _chunk_cumsum(jnp.log(f), reverse=(d == 1))

    row = lax.broadcasted_iota(jnp.int32, (ch, ch), 0)
    col = lax.broadcasted_iota(jnp.int32, (ch, ch), 1)
    masks = (row >= col, row <= col)

    def chunk_step(n, states):
        new_states = []
        for d in range(2):
            c = n if d == 0 else n_chunks - 1 - n
            sl = pl.ds(pl.multiple_of(c * ch, ch), ch)
            b = b_ref[d, sl, :]
            kk = k_ref[d, sl, :]
            b_end = b[ch - 1:ch, :] if d == 0 else b[0:1, :]
            q_dec = (q_ref[0, sl, :].astype(F32) * jnp.exp(b)).astype(BF16)
            k_inv = (kk * jnp.exp(-b)).astype(BF16)
            k_dec = (kk * jnp.exp(b_end - b)).astype(BF16)
            v = v_ref[0, sl, :]
            st = states[d]
            scores = jnp.where(masks[d], _dot_nt(q_dec, k_inv), 0.0).astype(BF16)
            acc_ref[d, sl, :] = _dot(scores, v) + _dot_nt(q_dec, st.astype(BF16))
            new_states.append(jnp.exp(b_end) * st + _dot_tn(v, k_dec))
        return tuple(new_states)

    zero_state = jnp.zeros((128, 128), F32)
    lax.fori_loop(0, n_chunks, chunk_step, (zero_state, zero_state))

    o = _rmsnorm_rows(acc_ref[0] + acc_ref[1], ng_ref[...])
    o_ref[0] = (o * _sigmoid(gate_ref[0].astype(F32))).astype(o_ref.dtype)


def _hgrn(q, v, ff, fb, gate, lb_logits, norm_g, layer):
    bsz, seq, _ = q.shape
    head_spec = pl.BlockSpec((1, seq, 128), lambda b, h: (b, 0, h))
    return pl.pallas_call(
        functools.partial(_hgrn_kernel, layer),
        grid=(bsz, HGRN_HEADS),
        in_specs=[head_spec] * 5 + [pl.BlockSpec((DEPTH, 128), lambda b, h: (0, h)),
                                    pl.BlockSpec((1, 128), lambda b, h: (0, h))],
        out_specs=head_spec,
        out_shape=jax.ShapeDtypeStruct((bsz, seq, D_MODEL), BF16),
        scratch_shapes=[pltpu.VMEM((2, seq, 128), F32)] * 3,
        compiler_params=_params(2),
        name="hgrn2",
    )(q, v, ff, fb, gate, lb_logits, norm_g)


def _rope_pair_tables(seq):
    inv = ROPE_THETA ** (-jnp.arange(0, ATTN_QK_DIM, 2, dtype=F32) / ATTN_QK_DIM)
    ang = jnp.arange(seq, dtype=F32)[:, None] * inv[None, :]
    return jnp.tile(jnp.cos(ang), (1, 4)), jnp.tile(jnp.sin(ang), (1, 4))


def _pair_layout_columns():
    idx = jnp.arange(ATTN_QK_WIDTH).reshape(2, 2, 2, 2, 32)
    return idx.transpose(0, 3, 1, 2, 4).reshape(-1)


def kernel(x, norm_mix_g, norm_mlp_g, final_norm_g, w_ff_in, w_ff_out, w_in_even, w_out_even, diff_lambda, diff_subln_g, s5_lam_re, s5_lam_im, s5_log_step, s5_b_re, s5_b_im, s5_c_re, s5_c_im, s5_d, s5_w_glu, s5_b_glu, w_in_odd, w_out_odd, hgrn_norm_g, hgrn_lb_logits):
    bsz, seq, _ = x.shape
    n_tok = bsz * seq
    n_chunks = seq // S5_CHUNK
    assert n_chunks & (n_chunks - 1) == 0 and n_chunks % 128 == 0
    x2 = x.reshape(n_tok, D_MODEL)
    cos, sin = _rope_pair_tables(seq)
    cos = cos.reshape(n_chunks, S5_CHUNK * 128)
    sin = sin.reshape(n_chunks, S5_CHUNK * 128)
    perm = _pair_layout_columns()

    for layer in range(DEPTH):
        g_mix = norm_mix_g[layer].reshape(1, D_MODEL)
        g_mlp = norm_mlp_g[layer].reshape(1, D_MODEL)
        w1 = w_ff_in[layer].astype(BF16)
        w2 = w_ff_out[layer].astype(BF16)
        if layer % 2 == 0:
            e = layer // 2
            w = w_in_even[e]
            w_qkv = jnp.concatenate([w[:, :ATTN_QK_WIDTH][:, perm],
                                     w[:, ATTN_QK_WIDTH:2 * ATTN_QK_WIDTH][:, perm],
                                     w[:, 2 * ATTN_QK_WIDTH:2 * ATTN_QK_WIDTH + ATTN_WIDTH]], axis=1).astype(BF16)
            wu_t = w[:, 2 * ATTN_QK_WIDTH + ATTN_WIDTH:].T.astype(BF16)
            x3 = x2.reshape(bsz, n_chunks, S5_CHUNK * D_MODEL)
            q, k, v, u_t = _pre_even(x3, g_mix, w_qkv, wu_t, cos, sin)
            lambda_init = 0.8 - 0.6 * math.exp(-0.3 * layer)
            a_out = _attention(q.reshape(bsz, seq, -1), k.reshape(bsz, seq, -1), v.reshape(bsz, seq, -1),
                               diff_lambda[e], diff_subln_g[e].reshape(1, ATTN_V_DIM), lambda_init)
            ops = _s5_chunk_operators(s5_lam_re[e], s5_lam_im[e], s5_log_step[e], s5_b_re[e], s5_b_im[e],
                                      s5_c_re[e], s5_c_im[e], s5_d[e], n_chunks.bit_length() - 1)
            y_t = _s5_chunked(u_t.reshape(S5_GROUPS, S5_CW, bsz * n_chunks), *ops, n_chunks)
            x3 = _post_even(x3, a_out.reshape(bsz, n_chunks, S5_CHUNK * ATTN_WIDTH),
                            y_t.reshape(S5_GROUPS, S5_CHUNK, S5_GROUP, bsz * n_chunks),
                            s5_w_glu[e].T.astype(BF16), s5_b_glu[e].reshape(S5_WIDTH, 1),
                            w_out_even[e].astype(BF16), g_mlp, w1, w2)
            x2 = x3.reshape(n_tok, D_MODEL)
        else:
            o_i = layer // 2
            q, v, ff, fb, gate = _pre_odd(x2, g_mix, w_in_odd[o_i].astype(BF16))
            shp = (bsz, seq, D_MODEL)
            mixed = _hgrn(q.reshape(shp), v.reshape(shp), ff.reshape(shp), fb.reshape(shp), gate.reshape(shp),
                          hgrn_lb_logits, hgrn_norm_g[o_i].reshape(1, D_MODEL), layer)
            assert layer == DEPTH - 1, "the final norm is fused into the last (odd) layer's kernel"
            x2 = _post_odd(x2, mixed.reshape(n_tok, D_MODEL), w_out_odd[o_i].astype(BF16), g_mlp, w1, w2,
                           final_norm_g.reshape(1, D_MODEL))
    return x2.reshape(bsz, seq, D_MODEL)
```

```python
import functools
import math

import jax
import jax.numpy as jnp
from jax import lax
from jax.experimental import pallas as pl
from jax.experimental.pallas import tpu as pltpu

D_MODEL = 1024
DEPTH = 2
ATTN_HEADS = 4
ATTN_QK_DIM = 64
ATTN_V_DIM = 128
ATTN_QK_WIDTH = 512
ATTN_WIDTH = 512
ROPE_THETA = 10000.0
S5_WIDTH = 512
S5_GROUP = 16
S5_GROUPS = 32
S5_STATE = 64
HGRN_HEADS = 8
HGRN_CHUNK = 64
D_FF = 4096
EPS = 1e-6

S5_CHUNK = 16
S5_CW = S5_CHUNK * S5_GROUP
S5_GROUP_BLOCK = 4

TOKEN_TILE = 512
ATTN_Q_TILE = 256
FF_TILE = 1024
VMEM_LIMIT = 56 * 1024 * 1024

BF16 = jnp.bfloat16
F32 = jnp.float32


def _const_spec(shape):
    nd = len(shape)
    return pl.BlockSpec(shape, lambda *_: (0,) * nd, pipeline_mode=pl.Buffered(1))


def _params(n_axes):
    return pltpu.CompilerParams(dimension_semantics=("arbitrary",) * n_axes,
                                vmem_limit_bytes=VMEM_LIMIT)


def _rmsnorm_rows(x, g):
    ms = jnp.mean(x * x, axis=-1, keepdims=True)
    return x * lax.rsqrt(ms + EPS) * g


def _sigmoid(x):
    return 1.0 / (1.0 + jnp.exp(-x))


def _gelu_tanh(x):
    c = math.sqrt(2.0 / math.pi)
    return 0.5 * x * (1.0 + jnp.tanh(c * (x + 0.044715 * (x * x * x))))


def _dot(a, b):
    return jnp.dot(a, b, preferred_element_type=F32)


def _dot_nt(a, b):
    return lax.dot_general(a, b, (((1,), (1,)), ((), ())), preferred_element_type=F32)


def _dot_tn(a, b):
    return lax.dot_general(a, b, (((0,), (0,)), ((), ())), preferred_element_type=F32)


def _pre_even_kernel(x_ref, g_ref, w_ref, cos_ref, sin_ref, q_ref, k_ref, v_ref):
    h = _rmsnorm_rows(x_ref[...], g_ref[...]).astype(BF16)
    cos = cos_ref[...]
    sin = sin_ref[...]
    for out_ref, base, scale in ((q_ref, 0, ATTN_QK_DIM ** -0.5), (k_ref, ATTN_QK_WIDTH, 1.0)):
        p = _dot(h, w_ref[:, base:base + ATTN_QK_WIDTH])
        for pair in range(2):
            lo = p[:, 256 * pair:256 * pair + 128]
            hi = p[:, 256 * pair + 128:256 * pair + 256]
            out_ref[:, 256 * pair:256 * pair + 128] = ((lo * cos - hi * sin) * scale).astype(BF16)
            out_ref[:, 256 * pair + 128:256 * pair + 256] = ((hi * cos + lo * sin) * scale).astype(BF16)
    v_ref[...] = _dot(h, w_ref[:, 2 * ATTN_QK_WIDTH:]).astype(BF16)


def _pre_even(x2, g, w_qkv, cos, sin, seq):
    n_tok = x2.shape[0]
    tm = TOKEN_TILE
    n_pos_blocks = seq // tm
    out = jax.ShapeDtypeStruct((n_tok, 512), BF16)
    row_spec = pl.BlockSpec((tm, 512), lambda i: (i, 0))
    rope_spec = pl.BlockSpec((tm, 128), lambda i: (i % n_pos_blocks, 0))
    return pl.pallas_call(
        _pre_even_kernel,
        grid=(n_tok // tm,),
        in_specs=[pl.BlockSpec((tm, D_MODEL), lambda i: (i, 0)), _const_spec((1, D_MODEL)),
                  _const_spec(w_qkv.shape), rope_spec, rope_spec],
        out_specs=[row_spec] * 3,
        out_shape=[out] * 3,
        compiler_params=_params(1),
        name="pre_even",
    )(x2, g, w_qkv, cos, sin)


def _s5_in_kernel(x_ref, g_ref, wu_ref, ut_ref, u_scr):
    nc = x_ref.shape[0] // S5_CHUNK
    groups_per_slab = 128 // S5_GROUP
    h = _rmsnorm_rows(x_ref[...], g_ref[...]).astype(BF16)
    u = _dot(h, wu_ref[...])
    for j in range(S5_WIDTH // 128):
        u_scr[j] = u[:, 128 * j:128 * (j + 1)]
    for j in range(S5_WIDTH // 128):
        for ph in range(S5_CHUNK):
            t = u_scr[j, pl.ds(ph, nc, stride=S5_CHUNK), :]
            ut_ref[groups_per_slab * j:groups_per_slab * (j + 1), ph, :, :] = (
                t.T.astype(BF16).reshape(groups_per_slab, S5_GROUP, nc))


def _s5_in(x2, g, wu, bsz, seq):
    nc = seq // S5_CHUNK
    return pl.pallas_call(
        _s5_in_kernel,
        grid=(bsz,),
        in_specs=[pl.BlockSpec((seq, D_MODEL), lambda b: (b, 0)), _const_spec((1, D_MODEL)),
                  _const_spec(wu.shape)],
        out_specs=pl.BlockSpec((S5_GROUPS, S5_CHUNK, S5_GROUP, nc), lambda b: (0, 0, 0, b)),
        out_shape=jax.ShapeDtypeStruct((S5_GROUPS, S5_CHUNK, S5_GROUP, bsz * nc), BF16),
        scratch_shapes=[pltpu.VMEM((S5_WIDTH // 128, seq, 128), F32)],
        compiler_params=_params(1),
        name="s5_in",
    )(x2, g, wu)


def _s5_out_kernel(yt_ref, wglut_ref, bglu_ref, o_ref, b_scr):
    nc = yt_ref.shape[3]
    yt = jnp.concatenate([yt_ref[:, ph, :, :].reshape(S5_WIDTH, nc) for ph in range(S5_CHUNK)], axis=1)
    yt = _gelu_tanh(yt)
    bt = yt * _sigmoid(_dot(wglut_ref[...], yt.astype(BF16)) + bglu_ref[...])
    for j in range(S5_WIDTH // 128):
        for ph in range(S5_CHUNK):
            b_scr[j, pl.ds(ph, nc, stride=S5_CHUNK), :] = bt[128 * j:128 * (j + 1), ph * nc:(ph + 1) * nc].T
    for j in range(S5_WIDTH // 128):
        o_ref[:, 128 * j:128 * (j + 1)] = b_scr[j]


def _s5_out(y_t, w_glu_t, b_glu_col, bsz, seq):
    nc = seq // S5_CHUNK
    return pl.pallas_call(
        _s5_out_kernel,
        grid=(bsz,),
        in_specs=[pl.BlockSpec((S5_GROUPS, S5_CHUNK, S5_GROUP, nc), lambda b: (0, 0, 0, b)),
                  _const_spec(w_glu_t.shape), _const_spec(b_glu_col.shape)],
        out_specs=pl.BlockSpec((seq, S5_WIDTH), lambda b: (b, 0)),
        out_shape=jax.ShapeDtypeStruct((bsz * seq, S5_WIDTH), F32),
        scratch_shapes=[pltpu.VMEM((S5_WIDTH // 128, seq, 128), F32)],
        compiler_params=_params(1),
        name="s5_out",
    )(y_t, w_glu_t, b_glu_col)


def _attn_kernel(lambda_init, q_ref, k_ref, v_ref, lam_ref, g_ref, o_ref):
    q = q_ref[0]
    k = k_ref[0]
    lam = lam_ref[...]
    lam_val = (jnp.exp(jnp.sum(lam[0:1] * lam[1:2], axis=-1, keepdims=True))
               - jnp.exp(jnp.sum(lam[2:3] * lam[3:4], axis=-1, keepdims=True)) + lambda_init)
    lane_group = (lax.broadcasted_iota(jnp.int32, q.shape, 1) // 32) % 4
    zero = jnp.zeros_like(q)
    for hh in range(2):
        probs = []
        for c in range(2):
            qm = jnp.where(lane_group == 2 * hh + c, q, zero)
            s = _dot_nt(qm, k)
            m = jnp.max(s, axis=-1, keepdims=True)
            e = jnp.exp(s - m)
            l = jnp.sum(e, axis=-1, keepdims=True)
            probs.append((e, 1.0 / l))
        w = probs[0][0] * probs[0][1] - probs[1][0] * (lam_val * probs[1][1])
        o = _dot(w.astype(BF16), v_ref[0, :, 128 * hh:128 * hh + 128])
        o = _rmsnorm_rows(o, g_ref[...]) * (1.0 - lambda_init)
        o_ref[0, :, 128 * hh:128 * hh + 128] = o.astype(o_ref.dtype)


def _attention(q, k, v, lam, subln_g, lambda_init):
    bsz, seq, _ = q.shape
    tq = ATTN_Q_TILE
    kv_spec = pl.BlockSpec((1, seq, 256), lambda b, p, i: (b, 0, p))
    return pl.pallas_call(
        functools.partial(_attn_kernel, lambda_init),
        grid=(bsz, 2, seq // tq),
        in_specs=[pl.BlockSpec((1, tq, 256), lambda b, p, i: (b, i, p)),
                  kv_spec, kv_spec,
                  _const_spec(lam.shape),
                  _const_spec(subln_g.shape)],
        out_specs=pl.BlockSpec((1, tq, 256), lambda b, p, i: (b, i, p)),
        out_shape=jax.ShapeDtypeStruct((bsz, seq, ATTN_WIDTH), BF16),
        compiler_params=_params(3),
        name="diff_attention",
    )(q, k, v, lam, subln_g)


def _s5_kernel(n_chunks, ut_ref, mi_ref, min_ref, mo_ref, coef_ref, y_ref, z_ref):
    n_lanes = ut_ref.shape[2]
    pos = lax.broadcasted_iota(jnp.int32, (S5_STATE, n_lanes), 1) % n_chunks
    n_steps = n_chunks.bit_length() - 1

    def one_group(g, carry):
        ut = ut_ref[g]
        st = _dot(min_ref[g], ut)
        coef = coef_ref[g]
        for d in range(2):
            xr = st[128 * d:128 * d + S5_STATE]
            xi = st[128 * d + S5_STATE:128 * d + 2 * S5_STATE]
            for k in range(n_steps):
                s = 1 << k
                if d == 0:
                    shift, keep = s, pos >= s
                else:
                    shift, keep = n_lanes - s, pos < n_chunks - s
                pr = jnp.where(keep, coef[S5_STATE * d:S5_STATE * (d + 1), k:k + 1], 0.0)
                pi = jnp.where(keep, coef[S5_STATE * d:S5_STATE * (d + 1), 8 + k:9 + k], 0.0)
                sr = pltpu.roll(xr, shift, 1)
                si = pltpu.roll(xi, shift, 1)
                xr, xi = xr + pr * sr - pi * si, xi + pr * si + pi * sr
            if d == 0:
                shift, keep = 1, pos >= 1
            else:
                shift, keep = n_lanes - 1, pos < n_chunks - 1
            z_ref[128 * d:128 * d + S5_STATE] = jnp.where(keep, pltpu.roll(xr, shift, 1), 0.0).astype(BF16)
            z_ref[128 * d + S5_STATE:128 * (d + 1)] = jnp.where(keep, pltpu.roll(xi, shift, 1), 0.0).astype(BF16)
        y_ref[g] = _dot(mi_ref[g], ut) + _dot(mo_ref[g], z_ref[...])
        return carry

    lax.fori_loop(0, ut_ref.shape[0], one_group, 0)


def _s5_chunked(u_t, m_intra_t, m_in_t, m_out_t, coef, n_chunks):
    n_groups, _, n_lanes = u_t.shape
    gb = S5_GROUP_BLOCK
    mat_spec = pl.BlockSpec((gb, S5_CW, S5_CW), lambda i: (i, 0, 0))
    return pl.pallas_call(
        functools.partial(_s5_kernel, n_chunks),
        grid=(n_groups // gb,),
        in_specs=[pl.BlockSpec((gb, S5_CW, n_lanes), lambda i: (i, 0, 0)), mat_spec, mat_spec, mat_spec,
                  pl.BlockSpec((gb, 2 * S5_STATE, 16), lambda i: (i, 0, 0))],
        out_specs=pl.BlockSpec((gb, S5_CW, n_lanes), lambda i: (i, 0, 0)),
        out_shape=jax.ShapeDtypeStruct((n_groups, S5_CW, n_lanes), F32),
        scratch_shapes=[pltpu.VMEM((4 * S5_STATE, n_lanes), BF16)],
        compiler_params=_params(1),
        name="s5_chunked",
    )(u_t, m_intra_t, m_in_t, m_out_t, coef)


def _s5_chunk_operators(lam_re, lam_im, log_step, b_re, b_im, c_re, c_im, d_skip, n_steps):
    hp = lax.Precision.HIGHEST
    t = S5_CHUNK
    lr = jnp.minimum(lam_re, -1e-4)
    li = lam_im
    dt = jnp.exp(log_step)[..., None]
    mag = jnp.exp(lr * dt)
    ar = mag * jnp.cos(li * dt)
    ai = mag * jnp.sin(li * dt)
    den = lr * lr + li * li
    cr = ((ar - 1.0) * lr + ai * li) / den
    ci = (ai * lr - (ar - 1.0) * li) / den
    bbr = cr[..., None] * b_re - ci[..., None] * b_im
    bbi = cr[..., None] * b_im + ci[..., None] * b_re

    def powers(taus):
        taus = taus.astype(F32)
        pmag = jnp.exp(taus * (lr * dt)[..., None])
        ang = taus * (li * dt)[..., None]
        return pmag * jnp.cos(ang), pmag * jnp.sin(ang)

    pr, pi = powers(jnp.arange(t))
    rr = pr[..., None] * bbr[:, :, :, None, :] - pi[..., None] * bbi[:, :, :, None, :]
    ri = pr[..., None] * bbi[:, :, :, None, :] + pi[..., None] * bbr[:, :, :, None, :]
    rf_r = rr[0, :, :, ::-1].reshape(S5_GROUPS, S5_STATE, S5_CW)
    rf_i = ri[0, :, :, ::-1].reshape(S5_GROUPS, S5_STATE, S5_CW)
    rb_r = rr[1].reshape(S5_GROUPS, S5_STATE, S5_CW)
    rb_i = ri[1].reshape(S5_GROUPS, S5_STATE, S5_CW)
    m_in_t = jnp.concatenate([rf_r, rf_i, rb_r, rb_i], axis=1)

    taps_f = (jnp.einsum('gnp,gpx->gnx', c_re[0], rf_r, precision=hp)
              - jnp.einsum('gnp,gpx->gnx', c_im[0], rf_i, precision=hp))
    taps_b = (jnp.einsum('gnp,gpx->gnx', c_re[1], rb_r, precision=hp)
              - jnp.einsum('gnp,gpx->gnx', c_im[1], rb_i, precision=hp))
    skip = d_skip.reshape(S5_GROUPS, S5_GROUP)[:, :, None] * jnp.eye(S5_GROUP, dtype=F32)[None]
    edge = (t - 1) * S5_GROUP
    taps = (jnp.pad(taps_f, ((0, 0), (0, 0), (0, edge))) + jnp.pad(taps_b, ((0, 0), (0, 0), (edge, 0)))
            + jnp.pad(skip, ((0, 0), (0, 0), (edge, edge))))
    m_intra_t = jnp.stack([taps[:, :, (t - 1 - j) * S5_GROUP:(t - 1 - j) * S5_GROUP + S5_CW]
                           for j in range(t)], axis=1).reshape(S5_GROUPS, S5_CW, S5_CW)

    qr, qi = powers(jnp.arange(1, t + 1))
    qr = qr.transpose(0, 1, 3, 2)
    qi = qi.transpose(0, 1, 3, 2)
    er = qr[:, :, :, None, :] * c_re[:, :, None] - qi[:, :, :, None, :] * c_im[:, :, None]
    ei = qr[:, :, :, None, :] * c_im[:, :, None] + qi[:, :, :, None, :] * c_re[:, :, None]
    m_out_t = jnp.concatenate([er[0], -ei[0], er[1, :, ::-1], -ei[1, :, ::-1]],
                              axis=-1).reshape(S5_GROUPS, S5_CW, 4 * S5_STATE)

    ar16, ai16 = qr[:, :, t - 1], qi[:, :, t - 1]
    cols_r, cols_i = [], []
    for _ in range(n_steps):
        cols_r.append(ar16)
        cols_i.append(ai16)
        ar16, ai16 = ar16 * ar16 - ai16 * ai16, 2.0 * ar16 * ai16
    pad = [jnp.zeros_like(ar16)] * (8 - n_steps)
    coef = jnp.stack(cols_r + pad + cols_i + pad, axis=-1)
    coef = coef.transpose(1, 0, 2, 3).reshape(S5_GROUPS, 2 * S5_STATE, 16)
    return m_intra_t.astype(BF16), m_in_t.astype(BF16), m_out_t.astype(BF16), coef


def _mlp(x1, g_ref, w1_ref, w2_ref):
    h = _rmsnorm_rows(x1, g_ref[...]).astype(BF16)
    acc = x1
    for j in range(D_FF // FF_TILE):
        hid = _dot(h, w1_ref[:, j * FF_TILE:(j + 1) * FF_TILE])
        hid = jnp.square(jnp.maximum(hid, 0.0)).astype(BF16)
        acc = acc + _dot(hid, w2_ref[j * FF_TILE:(j + 1) * FF_TILE, :])
    return acc


def _post_even_kernel(x_ref, a_ref, b_ref, wout_ref, g_ref, w1_ref, w2_ref, o_ref):
    mix = (_dot(a_ref[...], wout_ref[0:ATTN_WIDTH, :])
           + _dot(b_ref[...].astype(BF16), wout_ref[ATTN_WIDTH:, :]))
    o_ref[...] = _mlp(x_ref[...] + mix, g_ref, w1_ref, w2_ref)


def _post_even(x2, a_out, b_out, w_out, g_mlp, w1, w2):
    n_tok = x2.shape[0]
    tm = TOKEN_TILE
    return pl.pallas_call(
        _post_even_kernel,
        grid=(n_tok // tm,),
        in_specs=[pl.BlockSpec((tm, D_MODEL), lambda i: (i, 0)),
                  pl.BlockSpec((tm, ATTN_WIDTH), lambda i: (i, 0)),
                  pl.BlockSpec((tm, S5_WIDTH), lambda i: (i, 0)),
                  _const_spec(w_out.shape), _const_spec(g_mlp.shape), _const_spec(w1.shape),
                  _const_spec(w2.shape)],
        out_specs=pl.BlockSpec((tm, D_MODEL), lambda i: (i, 0)),
        out_shape=jax.ShapeDtypeStruct((n_tok, D_MODEL), F32),
        compiler_params=_params(1),
        name="post_even",
    )(x2, a_out, b_out, w_out, g_mlp, w1, w2)


def _post_odd_kernel(x_ref, m_ref, wout_ref, g_ref, w1_ref, w2_ref, gf_ref, o_ref):
    x1 = x_ref[...] + _dot(m_ref[...], wout_ref[...])
    o_ref[...] = _rmsnorm_rows(_mlp(x1, g_ref, w1_ref, w2_ref), gf_ref[...])


def _post_odd(x2, mixed, w_out, g_mlp, w1, w2, g_final):
    n_tok = x2.shape[0]
    tm = TOKEN_TILE
    return pl.pallas_call(
        _post_odd_kernel,
        grid=(n_tok // tm,),
        in_specs=[pl.BlockSpec((tm, D_MODEL), lambda i: (i, 0)),
                  pl.BlockSpec((tm, D_MODEL), lambda i: (i, 0)),
                  _const_spec(w_out.shape), _const_spec(g_mlp.shape),
                  _const_spec(w1.shape), _const_spec(w2.shape), _const_spec(g_final.shape)],
        out_specs=pl.BlockSpec((tm, D_MODEL), lambda i: (i, 0)),
        out_shape=jax.ShapeDtypeStruct((n_tok, D_MODEL), F32),
        compiler_params=_params(1),
        name="post_odd",
    )(x2, mixed, w_out, g_mlp, w1, w2, g_final)


def _pre_odd_kernel(x_ref, g_ref, w_ref, q_ref, i_ref, ff_ref, fb_ref, gate_ref):
    h = _rmsnorm_rows(x_ref[...], g_ref[...]).astype(BF16)
    for s, out_ref in enumerate((q_ref, i_ref, ff_ref, fb_ref, gate_ref)):
        out_ref[...] = _dot(h, w_ref[:, s * D_MODEL:(s + 1) * D_MODEL]).astype(out_ref.dtype)


def _pre_odd(x2, g, w):
    n_tok = x2.shape[0]
    tm = TOKEN_TILE
    row_spec = pl.BlockSpec((tm, D_MODEL), lambda i: (i, 0))
    lo = jax.ShapeDtypeStruct((n_tok, D_MODEL), BF16)
    hi = jax.ShapeDtypeStruct((n_tok, D_MODEL), F32)
    return pl.pallas_call(
        _pre_odd_kernel,
        grid=(n_tok // tm,),
        in_specs=[row_spec, _const_spec((1, D_MODEL)), _const_spec(w.shape)],
        out_specs=[row_spec] * 5,
        out_shape=[lo, lo, hi, hi, lo],
        compiler_params=_params(1),
        name="pre_odd",
    )(x2, g, w)


def _chunk_cumsum(x, reverse):
    n = x.shape[0]
    pos = lax.broadcasted_iota(jnp.int32, x.shape, 0) % HGRN_CHUNK
    s = 1
    while s < HGRN_CHUNK:
        if reverse:
            shifted = pltpu.roll(x, n - s, 0)
            x = x + jnp.where(pos < HGRN_CHUNK - s, shifted, 0.0)
        else:
            shifted = pltpu.roll(x, s, 0)
            x = x + jnp.where(pos >= s, shifted, 0.0)
        s *= 2
    return x


def _hgrn_kernel(layer, q_ref, v_ref, ff_ref, fb_ref, gate_ref, lbl_ref, ng_ref, o_ref,
                 k_ref, b_ref, acc_ref):
    seq = q_ref.shape[1]
    n_chunks = seq // HGRN_CHUNK
    ch = HGRN_CHUNK
    logits = lbl_ref[...]
    soft = jnp.exp(logits - jnp.max(logits, axis=0, keepdims=True))
    soft = soft / jnp.sum(soft, axis=0, keepdims=True)
    lb = jnp.sum(soft[0:layer + 1], axis=0, keepdims=True) - soft[0:1]

    for d, f_ref in enumerate((ff_ref, fb_ref)):
        fl = f_ref[0]
        f = lb + (1.0 - lb) * _sigmoid(fl)
        k_ref[d] = (1.0 - lb) * _sigmoid(-fl)
        b_ref[d] = _chunk_cumsum(jnp.log(f), reverse=(d == 1))

    row = lax.broadcasted_iota(jnp.int32, (ch, ch), 0)
    col = lax.broadcasted_iota(jnp.int32, (ch, ch), 1)
    masks = (row >= col, row <= col)

    def chunk_step(n, states):
        new_states = []
        for d in range(2):
            c = n if d == 0 else n_chunks - 1 - n
            sl = pl.ds(pl.multiple_of(c * ch, ch), ch)
            b = b_ref[d, sl, :]
            kk = k_ref[d, sl, :]
            b_end = b[ch - 1:ch, :] if d == 0 else b[0:1, :]
            q_dec = (q_ref[0, sl, :].astype(F32) * jnp.exp(b)).astype(BF16)
            k_inv = (kk * jnp.exp(-b)).astype(BF16)
            k_dec = (kk * jnp.exp(b_end - b)).astype(BF16)
            v = v_ref[0, sl, :]
            st = states[d]
            scores = jnp.where(masks[d], _dot_nt(q_dec, k_inv), 0.0).astype(BF16)
            acc_ref[d, sl, :] = _dot(scores, v) + _dot_nt(q_dec, st.astype(BF16))
            new_states.append(jnp.exp(b_end) * st + _dot_tn(v, k_dec))
        return tuple(new_states)

    zero_state = jnp.zeros((128, 128), F32)
    lax.fori_loop(0, n_chunks, chunk_step, (zero_state, zero_state))

    o = _rmsnorm_rows(acc_ref[0] + acc_ref[1], ng_ref[...])
    o_ref[0] = (o * _sigmoid(gate_ref[0].astype(F32))).astype(o_ref.dtype)


def _hgrn(q, v, ff, fb, gate, lb_logits, norm_g, layer):
    bsz, seq, _ = q.shape
    head_spec = pl.BlockSpec((1, seq, 128), lambda b, h: (b, 0, h))
    return pl.pallas_call(
        functools.partial(_hgrn_kernel, layer),
        grid=(bsz, HGRN_HEADS),
        in_specs=[head_spec] * 5 + [pl.BlockSpec((DEPTH, 128), lambda b, h: (0, h)),
                                    pl.BlockSpec((1, 128), lambda b, h: (0, h))],
        out_specs=head_spec,
        out_shape=jax.ShapeDtypeStruct((bsz, seq, D_MODEL), BF16),
        scratch_shapes=[pltpu.VMEM((2, seq, 128), F32)] * 3,
        compiler_params=_params(2),
        name="hgrn2",
    )(q, v, ff, fb, gate, lb_logits, norm_g)


def _rope_pair_tables(seq):
    inv = ROPE_THETA ** (-jnp.arange(0, ATTN_QK_DIM, 2, dtype=F32) / ATTN_QK_DIM)
    ang = jnp.arange(seq, dtype=F32)[:, None] * inv[None, :]
    return jnp.tile(jnp.cos(ang), (1, 4)), jnp.tile(jnp.sin(ang), (1, 4))


def _pair_layout_columns():
    idx = jnp.arange(ATTN_QK_WIDTH).reshape(2, 2, 2, 2, 32)
    return idx.transpose(0, 3, 1, 2, 4).reshape(-1)


def kernel(x, norm_mix_g, norm_mlp_g, final_norm_g, w_ff_in, w_ff_out, w_in_even, w_out_even, diff_lambda, diff_subln_g, s5_lam_re, s5_lam_im, s5_log_step, s5_b_re, s5_b_im, s5_c_re, s5_c_im, s5_d, s5_w_glu, s5_b_glu, w_in_odd, w_out_odd, hgrn_norm_g, hgrn_lb_logits):
    bsz, seq, _ = x.shape
    n_tok = bsz * seq
    n_chunks = seq // S5_CHUNK
    assert n_chunks & (n_chunks - 1) == 0 and n_chunks % 128 == 0
    x2 = x.reshape(n_tok, D_MODEL)
    cos, sin = _rope_pair_tables(seq)
    perm = _pair_layout_columns()

    for layer in range(DEPTH):
        g_mix = norm_mix_g[layer].reshape(1, D_MODEL)
        g_mlp = norm_mlp_g[layer].reshape(1, D_MODEL)
        w1 = w_ff_in[layer].astype(BF16)
        w2 = w_ff_out[layer].astype(BF16)
        if layer % 2 == 0:
            e = layer // 2
            w = w_in_even[e]
            w_qkv = jnp.concatenate([w[:, :ATTN_QK_WIDTH][:, perm],
                                     w[:, ATTN_QK_WIDTH:2 * ATTN_QK_WIDTH][:, perm],
                                     w[:, 2 * ATTN_QK_WIDTH:2 * ATTN_QK_WIDTH + ATTN_WIDTH]], axis=1).astype(BF16)
            wu = w[:, 2 * ATTN_QK_WIDTH + ATTN_WIDTH:].astype(BF16)
            q, k, v = _pre_even(x2, g_mix, w_qkv, cos, sin, seq)
            u_t = _s5_in(x2, g_mix, wu, bsz, seq)
            lambda_init = 0.8 - 0.6 * math.exp(-0.3 * layer)
            a_out = _attention(q.reshape(bsz, seq, -1), k.reshape(bsz, seq, -1), v.reshape(bsz, seq, -1),
                               diff_lambda[e], diff_subln_g[e].reshape(1, ATTN_V_DIM), lambda_init)
            ops = _s5_chunk_operators(s5_lam_re[e], s5_lam_im[e], s5_log_step[e], s5_b_re[e], s5_b_im[e],
                                      s5_c_re[e], s5_c_im[e], s5_d[e], n_chunks.bit_length() - 1)
            y_t = _s5_chunked(u_t.reshape(S5_GROUPS, S5_CW, bsz * n_chunks), *ops, n_chunks)
            b_out = _s5_out(y_t.reshape(S5_GROUPS, S5_CHUNK, S5_GROUP, bsz * n_chunks),
                            s5_w_glu[e].T.astype(BF16), s5_b_glu[e].reshape(S5_WIDTH, 1), bsz, seq)
            x2 = _post_even(x2, a_out.reshape(n_tok, ATTN_WIDTH), b_out, w_out_even[e].astype(BF16),
                            g_mlp, w1, w2)
        else:
            o_i = layer // 2
            q, v, ff, fb, gate = _pre_odd(x2, g_mix, w_in_odd[o_i].astype(BF16))
            shp = (bsz, seq, D_MODEL)
            mixed = _hgrn(q.reshape(shp), v.reshape(shp), ff.reshape(shp), fb.reshape(shp), gate.reshape(shp),
                          hgrn_lb_logits, hgrn_norm_g[o_i].reshape(1, D_MODEL), layer)
            assert layer == DEPTH - 1, "the final norm is fused into the last (odd) layer's kernel"
            x2 = _post_odd(x2, mixed.reshape(n_tok, D_MODEL), w_out_odd[o_i].astype(BF16), g_mlp, w1, w2,
                           final_norm_g.reshape(1, D_MODEL))
    return x2.reshape(bsz, seq, D_MODEL)
```

```python
import functools
import math

import jax
import jax.numpy as jnp
from jax import lax
from jax.experimental import pallas as pl
from jax.experimental.pallas import tpu as pltpu

D_MODEL = 1024
DEPTH = 2
ATTN_HEADS = 4
ATTN_QK_DIM = 64
ATTN_V_DIM = 128
ATTN_QK_WIDTH = 512
ATTN_WIDTH = 512
ROPE_THETA = 10000.0
S5_WIDTH = 512
S5_GROUP = 16
S5_GROUPS = 32
S5_STATE = 64
HGRN_HEADS = 8
HGRN_CHUNK = 64
HGRN_BLOCK = 256
D_FF = 4096
EPS = 1e-6

S5_CHUNK = 16
S5_CW = S5_CHUNK * S5_GROUP
S5_GROUP_BLOCK = 4

TOKEN_TILE = 512
ATTN_Q_TILE = 256
FF_TILE = 1024
VMEM_LIMIT = 56 * 1024 * 1024

BF16 = jnp.bfloat16
F32 = jnp.float32


def _const_spec(shape):
    nd = len(shape)
    return pl.BlockSpec(shape, lambda *_: (0,) * nd, pipeline_mode=pl.Buffered(1))


def _params(n_axes):
    return pltpu.CompilerParams(dimension_semantics=("arbitrary",) * n_axes,
                                vmem_limit_bytes=VMEM_LIMIT)


def _rmsnorm_rows(x, g):
    ms = jnp.mean(x * x, axis=-1, keepdims=True)
    return x * lax.rsqrt(ms + EPS) * g


def _sigmoid(x):
    return 1.0 / (1.0 + jnp.exp(-x))


def _gelu_tanh(x):
    c = math.sqrt(2.0 / math.pi)
    return 0.5 * x * (1.0 + jnp.tanh(c * (x + 0.044715 * (x * x * x))))


def _dot(a, b):
    return jnp.dot(a, b, preferred_element_type=F32)


def _dot_nt(a, b):
    return lax.dot_general(a, b, (((1,), (1,)), ((), ())), preferred_element_type=F32)


def _dot_tn(a, b):
    return lax.dot_general(a, b, (((0,), (0,)), ((), ())), preferred_element_type=F32)


def _pre_even_kernel(x_ref, g_ref, w_ref, cos_ref, sin_ref, q_ref, k_ref, v_ref):
    h = _rmsnorm_rows(x_ref[...], g_ref[...]).astype(BF16)
    cos = cos_ref[...]
    sin = sin_ref[...]
    for out_ref, base, scale in ((q_ref, 0, ATTN_QK_DIM ** -0.5), (k_ref, ATTN_QK_WIDTH, 1.0)):
        p = _dot(h, w_ref[:, base:base + ATTN_QK_WIDTH])
        for pair in range(2):
            lo = p[:, 256 * pair:256 * pair + 128]
            hi = p[:, 256 * pair + 128:256 * pair + 256]
            out_ref[:, 256 * pair:256 * pair + 128] = ((lo * cos - hi * sin) * scale).astype(BF16)
            out_ref[:, 256 * pair + 128:256 * pair + 256] = ((hi * cos + lo * sin) * scale).astype(BF16)
    v_ref[...] = _dot(h, w_ref[:, 2 * ATTN_QK_WIDTH:]).astype(BF16)


def _pre_even(x2, g, w_qkv, cos, sin, seq):
    n_tok = x2.shape[0]
    tm = TOKEN_TILE
    n_pos_blocks = seq // tm
    out = jax.ShapeDtypeStruct((n_tok, 512), BF16)
    row_spec = pl.BlockSpec((tm, 512), lambda i: (i, 0))
    rope_spec = pl.BlockSpec((tm, 128), lambda i: (i % n_pos_blocks, 0))
    return pl.pallas_call(
        _pre_even_kernel,
        grid=(n_tok // tm,),
        in_specs=[pl.BlockSpec((tm, D_MODEL), lambda i: (i, 0)), _const_spec((1, D_MODEL)),
                  _const_spec(w_qkv.shape), rope_spec, rope_spec],
        out_specs=[row_spec] * 3,
        out_shape=[out] * 3,
        compiler_params=_params(1),
        name="pre_even",
    )(x2, g, w_qkv, cos, sin)


def _s5_in_kernel(x_ref, g_ref, wu_ref, ut_ref, u_scr):
    nc = x_ref.shape[0] // S5_CHUNK
    groups_per_slab = 128 // S5_GROUP
    h = _rmsnorm_rows(x_ref[...], g_ref[...]).astype(BF16)
    u = _dot(h, wu_ref[...])
    for j in range(S5_WIDTH // 128):
        u_scr[j] = u[:, 128 * j:128 * (j + 1)]
    for j in range(S5_WIDTH // 128):
        for ph in range(S5_CHUNK):
            t = u_scr[j, pl.ds(ph, nc, stride=S5_CHUNK), :]
            ut_ref[groups_per_slab * j:groups_per_slab * (j + 1), ph, :, :] = (
                t.T.astype(BF16).reshape(groups_per_slab, S5_GROUP, nc))


def _s5_in(x2, g, wu, bsz, seq):
    nc = seq // S5_CHUNK
    return pl.pallas_call(
        _s5_in_kernel,
        grid=(bsz,),
        in_specs=[pl.BlockSpec((seq, D_MODEL), lambda b: (b, 0)), _const_spec((1, D_MODEL)),
                  _const_spec(wu.shape)],
        out_specs=pl.BlockSpec((S5_GROUPS, S5_CHUNK, S5_GROUP, nc), lambda b: (0, 0, 0, b)),
        out_shape=jax.ShapeDtypeStruct((S5_GROUPS, S5_CHUNK, S5_GROUP, bsz * nc), BF16),
        scratch_shapes=[pltpu.VMEM((S5_WIDTH // 128, seq, 128), F32)],
        compiler_params=_params(1),
        name="s5_in",
    )(x2, g, wu)


def _s5_out_kernel(yt_ref, wglut_ref, bglu_ref, o_ref, b_scr):
    nc = yt_ref.shape[3]
    yt = jnp.concatenate([yt_ref[:, ph, :, :].reshape(S5_WIDTH, nc) for ph in range(S5_CHUNK)], axis=1)
    yt = _gelu_tanh(yt)
    bt = yt * _sigmoid(_dot(wglut_ref[...], yt.astype(BF16)) + bglu_ref[...])
    for j in range(S5_WIDTH // 128):
        for ph in range(S5_CHUNK):
            b_scr[j, pl.ds(ph, nc, stride=S5_CHUNK), :] = bt[128 * j:128 * (j + 1), ph * nc:(ph + 1) * nc].T
    for j in range(S5_WIDTH // 128):
        o_ref[:, 128 * j:128 * (j + 1)] = b_scr[j]


def _s5_out(y_t, w_glu_t, b_glu_col, bsz, seq):
    nc = seq // S5_CHUNK
    return pl.pallas_call(
        _s5_out_kernel,
        grid=(bsz,),
        in_specs=[pl.BlockSpec((S5_GROUPS, S5_CHUNK, S5_GROUP, nc), lambda b: (0, 0, 0, b)),
                  _const_spec(w_glu_t.shape), _const_spec(b_glu_col.shape)],
        out_specs=pl.BlockSpec((seq, S5_WIDTH), lambda b: (b, 0)),
        out_shape=jax.ShapeDtypeStruct((bsz * seq, S5_WIDTH), F32),
        scratch_shapes=[pltpu.VMEM((S5_WIDTH // 128, seq, 128), F32)],
        compiler_params=_params(1),
        name="s5_out",
    )(y_t, w_glu_t, b_glu_col)


def _attn_kernel(lambda_init, q_ref, k_ref, v_ref, lam_ref, g_ref, o_ref):
    q = q_ref[0]
    k = k_ref[0]
    lam = lam_ref[...]
    lam_val = (jnp.exp(jnp.sum(lam[0:1] * lam[1:2], axis=-1, keepdims=True))
               - jnp.exp(jnp.sum(lam[2:3] * lam[3:4], axis=-1, keepdims=True)) + lambda_init)
    lane_group = (lax.broadcasted_iota(jnp.int32, q.shape, 1) // 32) % 4
    zero = jnp.zeros_like(q)
    for hh in range(2):
        probs = []
        for c in range(2):
            qm = jnp.where(lane_group == 2 * hh + c, q, zero)
            s = _dot_nt(qm, k)
            m = jnp.max(s, axis=-1, keepdims=True)
            e = jnp.exp(s - m)
            l = jnp.sum(e, axis=-1, keepdims=True)
            probs.append((e, 1.0 / l))
        w = probs[0][0] * probs[0][1] - probs[1][0] * (lam_val * probs[1][1])
        o = _dot(w.astype(BF16), v_ref[0, :, 128 * hh:128 * hh + 128])
        o = _rmsnorm_rows(o, g_ref[...]) * (1.0 - lambda_init)
        o_ref[0, :, 128 * hh:128 * hh + 128] = o.astype(o_ref.dtype)


def _attention(q, k, v, lam, subln_g, lambda_init):
    bsz, seq, _ = q.shape
    tq = ATTN_Q_TILE
    kv_spec = pl.BlockSpec((1, seq, 256), lambda b, p, i: (b, 0, p))
    return pl.pallas_call(
        functools.partial(_attn_kernel, lambda_init),
        grid=(bsz, 2, seq // tq),
        in_specs=[pl.BlockSpec((1, tq, 256), lambda b, p, i: (b, i, p)),
                  kv_spec, kv_spec,
                  _const_spec(lam.shape),
                  _const_spec(subln_g.shape)],
        out_specs=pl.BlockSpec((1, tq, 256), lambda b, p, i: (b, i, p)),
        out_shape=jax.ShapeDtypeStruct((bsz, seq, ATTN_WIDTH), BF16),
        compiler_params=_params(3),
        name="diff_attention",
    )(q, k, v, lam, subln_g)


def _s5_kernel(n_chunks, ut_ref, mi_ref, min_ref, mo_ref, coef_ref, y_ref, z_ref):
    n_lanes = ut_ref.shape[2]
    pos = lax.broadcasted_iota(jnp.int32, (S5_STATE, n_lanes), 1) % n_chunks
    n_steps = n_chunks.bit_length() - 1

    def one_group(g, carry):
        ut = ut_ref[g]
        st = _dot(min_ref[g], ut)
        coef = coef_ref[g]
        for d in range(2):
            xr = st[128 * d:128 * d + S5_STATE]
            xi = st[128 * d + S5_STATE:128 * d + 2 * S5_STATE]
            for k in range(n_steps):
                s = 1 << k
                if d == 0:
                    shift, keep = s, pos >= s
                else:
                    shift, keep = n_lanes - s, pos < n_chunks - s
                pr = jnp.where(keep, coef[S5_STATE * d:S5_STATE * (d + 1), k:k + 1], 0.0)
                pi = jnp.where(keep, coef[S5_STATE * d:S5_STATE * (d + 1), 8 + k:9 + k], 0.0)
                sr = pltpu.roll(xr, shift, 1)
                si = pltpu.roll(xi, shift, 1)
                xr, xi = xr + pr * sr - pi * si, xi + pr * si + pi * sr
            if d == 0:
                shift, keep = 1, pos >= 1
            else:
                shift, keep = n_lanes - 1, pos < n_chunks - 1
            z_ref[128 * d:128 * d + S5_STATE] = jnp.where(keep, pltpu.roll(xr, shift, 1), 0.0).astype(BF16)
            z_ref[128 * d + S5_STATE:128 * (d + 1)] = jnp.where(keep, pltpu.roll(xi, shift, 1), 0.0).astype(BF16)
        y_ref[g] = _dot(mi_ref[g], ut) + _dot(mo_ref[g], z_ref[...])
        return carry

    lax.fori_loop(0, ut_ref.shape[0], one_group, 0)


def _s5_chunked(u_t, m_intra_t, m_in_t, m_out_t, coef, n_chunks):
    n_groups, _, n_lanes = u_t.shape
    gb = S5_GROUP_BLOCK
    mat_spec = pl.BlockSpec((gb, S5_CW, S5_CW), lambda i: (i, 0, 0))
    return pl.pallas_call(
        functools.partial(_s5_kernel, n_chunks),
        grid=(n_groups // gb,),
        in_specs=[pl.BlockSpec((gb, S5_CW, n_lanes), lambda i: (i, 0, 0)), mat_spec, mat_spec, mat_spec,
                  pl.BlockSpec((gb, 2 * S5_STATE, 16), lambda i: (i, 0, 0))],
        out_specs=pl.BlockSpec((gb, S5_CW, n_lanes), lambda i: (i, 0, 0)),
        out_shape=jax.ShapeDtypeStruct((n_groups, S5_CW, n_lanes), F32),
        scratch_shapes=[pltpu.VMEM((4 * S5_STATE, n_lanes), BF16)],
        compiler_params=_params(1),
        name="s5_chunked",
    )(u_t, m_intra_t, m_in_t, m_out_t, coef)


def _s5_chunk_operators(lam_re, lam_im, log_step, b_re, b_im, c_re, c_im, d_skip, n_steps):
    hp = lax.Precision.HIGHEST
    t = S5_CHUNK
    lr = jnp.minimum(lam_re, -1e-4)
    li = lam_im
    dt = jnp.exp(log_step)[..., None]
    mag = jnp.exp(lr * dt)
    ar = mag * jnp.cos(li * dt)
    ai = mag * jnp.sin(li * dt)
    den = lr * lr + li * li
    cr = ((ar - 1.0) * lr + ai * li) / den
    ci = (ai * lr - (ar - 1.0) * li) / den
    bbr = cr[..., None] * b_re - ci[..., None] * b_im
    bbi = cr[..., None] * b_im + ci[..., None] * b_re

    def powers(taus):
        taus = taus.astype(F32)
        pmag = jnp.exp(taus * (lr * dt)[..., None])
        ang = taus * (li * dt)[..., None]
        return pmag * jnp.cos(ang), pmag * jnp.sin(ang)

    pr, pi = powers(jnp.arange(t))
    rr = pr[..., None] * bbr[:, :, :, None, :] - pi[..., None] * bbi[:, :, :, None, :]
    ri = pr[..., None] * bbi[:, :, :, None, :] + pi[..., None] * bbr[:, :, :, None, :]
    rf_r = rr[0, :, :, ::-1].reshape(S5_GROUPS, S5_STATE, S5_CW)
    rf_i = ri[0, :, :, ::-1].reshape(S5_GROUPS, S5_STATE, S5_CW)
    rb_r = rr[1].reshape(S5_GROUPS, S5_STATE, S5_CW)
    rb_i = ri[1].reshape(S5_GROUPS, S5_STATE, S5_CW)
    m_in_t = jnp.concatenate([rf_r, rf_i, rb_r, rb_i], axis=1)

    taps_f = (jnp.einsum('gnp,gpx->gnx', c_re[0], rf_r, precision=hp)
              - jnp.einsum('gnp,gpx->gnx', c_im[0], rf_i, precision=hp))
    taps_b = (jnp.einsum('gnp,gpx->gnx', c_re[1], rb_r, precision=hp)
              - jnp.einsum('gnp,gpx->gnx', c_im[1], rb_i, precision=hp))
    skip = d_skip.reshape(S5_GROUPS, S5_GROUP)[:, :, None] * jnp.eye(S5_GROUP, dtype=F32)[None]
    edge = (t - 1) * S5_GROUP
    taps = (jnp.pad(taps_f, ((0, 0), (0, 0), (0, edge))) + jnp.pad(taps_b, ((0, 0), (0, 0), (edge, 0)))
            + jnp.pad(skip, ((0, 0), (0, 0), (edge, edge))))
    m_intra_t = jnp.stack([taps[:, :, (t - 1 - j) * S5_GROUP:(t - 1 - j) * S5_GROUP + S5_CW]
                           for j in range(t)], axis=1).reshape(S5_GROUPS, S5_CW, S5_CW)

    qr, qi = powers(jnp.arange(1, t + 1))
    qr = qr.transpose(0, 1, 3, 2)
    qi = qi.transpose(0, 1, 3, 2)
    er = qr[:, :, :, None, :] * c_re[:, :, None] - qi[:, :, :, None, :] * c_im[:, :, None]
    ei = qr[:, :, :, None, :] * c_im[:, :, None] + qi[:, :, :, None, :] * c_re[:, :, None]
    m_out_t = jnp.concatenate([er[0], -ei[0], er[1, :, ::-1], -ei[1, :, ::-1]],
                              axis=-1).reshape(S5_GROUPS, S5_CW, 4 * S5_STATE)

    ar16, ai16 = qr[:, :, t - 1], qi[:, :, t - 1]
    cols_r, cols_i = [], []
    for _ in range(n_steps):
        cols_r.append(ar16)
        cols_i.append(ai16)
        ar16, ai16 = ar16 * ar16 - ai16 * ai16, 2.0 * ar16 * ai16
    pad = [jnp.zeros_like(ar16)] * (8 - n_steps)
    coef = jnp.stack(cols_r + pad + cols_i + pad, axis=-1)
    coef = coef.transpose(1, 0, 2, 3).reshape(S5_GROUPS, 2 * S5_STATE, 16)
    return m_intra_t.astype(BF16), m_in_t.astype(BF16), m_out_t.astype(BF16), coef


def _mlp(x1, g_ref, w1_ref, w2_ref):
    h = _rmsnorm_rows(x1, g_ref[...]).astype(BF16)
    acc = x1
    for j in range(D_FF // FF_TILE):
        hid = _dot(h, w1_ref[:, j * FF_TILE:(j + 1) * FF_TILE])
        hid = jnp.square(jnp.maximum(hid, 0.0)).astype(BF16)
        acc = acc + _dot(hid, w2_ref[j * FF_TILE:(j + 1) * FF_TILE, :])
    return acc


def _post_even_kernel(x_ref, a_ref, b_ref, wout_ref, g_ref, w1_ref, w2_ref, o_ref):
    mix = (_dot(a_ref[...], wout_ref[0:ATTN_WIDTH, :])
           + _dot(b_ref[...].astype(BF16), wout_ref[ATTN_WIDTH:, :]))
    o_ref[...] = _mlp(x_ref[...] + mix, g_ref, w1_ref, w2_ref)


def _post_even(x2, a_out, b_out, w_out, g_mlp, w1, w2):
    n_tok = x2.shape[0]
    tm = TOKEN_TILE
    return pl.pallas_call(
        _post_even_kernel,
        grid=(n_tok // tm,),
        in_specs=[pl.BlockSpec((tm, D_MODEL), lambda i: (i, 0)),
                  pl.BlockSpec((tm, ATTN_WIDTH), lambda i: (i, 0)),
                  pl.BlockSpec((tm, S5_WIDTH), lambda i: (i, 0)),
                  _const_spec(w_out.shape), _const_spec(g_mlp.shape), _const_spec(w1.shape),
                  _const_spec(w2.shape)],
        out_specs=pl.BlockSpec((tm, D_MODEL), lambda i: (i, 0)),
        out_shape=jax.ShapeDtypeStruct((n_tok, D_MODEL), F32),
        compiler_params=_params(1),
        name="post_even",
    )(x2, a_out, b_out, w_out, g_mlp, w1, w2)


def _post_odd_kernel(x_ref, m_ref, wout_ref, g_ref, w1_ref, w2_ref, gf_ref, o_ref):
    x1 = x_ref[...] + _dot(m_ref[...], wout_ref[...])
    o_ref[...] = _rmsnorm_rows(_mlp(x1, g_ref, w1_ref, w2_ref), gf_ref[...])


def _post_odd(x2, mixed, w_out, g_mlp, w1, w2, g_final):
    n_tok = x2.shape[0]
    tm = TOKEN_TILE
    return pl.pallas_call(
        _post_odd_kernel,
        grid=(n_tok // tm,),
        in_specs=[pl.BlockSpec((tm, D_MODEL), lambda i: (i, 0)),
                  pl.BlockSpec((tm, D_MODEL), lambda i: (i, 0)),
                  _const_spec(w_out.shape), _const_spec(g_mlp.shape),
                  _const_spec(w1.shape), _const_spec(w2.shape), _const_spec(g_final.shape)],
        out_specs=pl.BlockSpec((tm, D_MODEL), lambda i: (i, 0)),
        out_shape=jax.ShapeDtypeStruct((n_tok, D_MODEL), F32),
        compiler_params=_params(1),
        name="post_odd",
    )(x2, mixed, w_out, g_mlp, w1, w2, g_final)


def _pre_odd_kernel(x_ref, g_ref, w_ref, q_ref, i_ref, ff_ref, fb_ref, gate_ref):
    h = _rmsnorm_rows(x_ref[...], g_ref[...]).astype(BF16)
    for s, out_ref in enumerate((q_ref, i_ref, ff_ref, fb_ref, gate_ref)):
        out_ref[...] = _dot(h, w_ref[:, s * D_MODEL:(s + 1) * D_MODEL]).astype(out_ref.dtype)


def _pre_odd(x2, g, w):
    n_tok = x2.shape[0]
    tm = TOKEN_TILE
    row_spec = pl.BlockSpec((tm, D_MODEL), lambda i: (i, 0))
    lo = jax.ShapeDtypeStruct((n_tok, D_MODEL), BF16)
    hi = jax.ShapeDtypeStruct((n_tok, D_MODEL), F32)
    return pl.pallas_call(
        _pre_odd_kernel,
        grid=(n_tok // tm,),
        in_specs=[row_spec, _const_spec((1, D_MODEL)), _const_spec(w.shape)],
        out_specs=[row_spec] * 5,
        out_shape=[lo, lo, hi, hi, lo],
        compiler_params=_params(1),
        name="pre_odd",
    )(x2, g, w)


def _chunk_cumprod(x, reverse):
    n = x.shape[0]
    pos = lax.broadcasted_iota(jnp.int32, x.shape, 0) % HGRN_CHUNK
    s = 1
    while s < HGRN_CHUNK:
        if reverse:
            shifted = pltpu.roll(x, n - s, 0)
            x = x * jnp.where(pos < HGRN_CHUNK - s, shifted, 1.0)
        else:
            shifted = pltpu.roll(x, s, 0)
            x = x * jnp.where(pos >= s, shifted, 1.0)
        s *= 2
    return x


def _hgrn_kernel(layer, q_ref, v_ref, ff_ref, fb_ref, gate_ref, lbl_ref, ng_ref, o_ref,
                 qd_ref, ki_ref, kd_ref, eb_ref, kv_ref, st_ref, acc_ref):
    seq = q_ref.shape[1]
    ch = HGRN_CHUNK
    blk = HGRN_BLOCK
    n_chunks = seq // ch
    per_blk = blk // ch
    logits = lbl_ref[...]
    soft = jnp.exp(logits - jnp.max(logits, axis=0, keepdims=True))
    soft = soft / jnp.sum(soft, axis=0, keepdims=True)
    lb = jnp.sum(soft[0:layer + 1], axis=0, keepdims=True) - soft[0:1]

    c2 = 0.5 * (1.0 - lb)
    c1 = lb + c2
    qf = q_ref[0].astype(F32)
    for d, f_ref in enumerate((ff_ref, fb_ref)):
        ct = c2 * jnp.tanh(0.5 * f_ref[0])
        eb = _chunk_cumprod(c1 + ct, reverse=(d == 1))
        eb_ref[d] = eb
        k_inv = (c2 - ct) / eb
        eb3 = eb.reshape(n_chunks, ch, 128)
        e_end = eb3[:, ch - 1:ch, :] if d == 0 else eb3[:, 0:1, :]
        qd_ref[d] = (qf * eb).astype(BF16)
        ki_ref[d] = k_inv.astype(BF16)
        kd_ref[d] = (k_inv.reshape(n_chunks, ch, 128) * e_end).reshape(seq, 128).astype(BF16)

    row = lax.broadcasted_iota(jnp.int32, (blk, blk), 0)
    col = lax.broadcasted_iota(jnp.int32, (blk, blk), 1)
    same_chunk = (row // ch) == (col // ch)

    def intra(j, carry):
        r0 = pl.multiple_of(j * blk, blk)
        sl = pl.ds(r0, blk)
        v = v_ref[0, sl, :]
        for d in range(2):
            causal = (row >= col) if d == 0 else (row <= col)
            s = _dot_nt(qd_ref[d, sl, :], ki_ref[d, sl, :])
            p = jnp.where(same_chunk, jnp.where(causal, s, 0.0), 0.0).astype(BF16)
            acc_ref[d, sl, :] = _dot(p, v)
            for cc in range(per_blk):
                rows = pl.ds(r0 + cc * ch, ch)
                kv_ref[d, j * per_blk + cc] = _dot_tn(v_ref[0, rows, :], kd_ref[d, rows, :])
        return carry

    lax.fori_loop(0, seq // blk, intra, 0)

    def scan(n, states):
        new_states = []
        for d in range(2):
            c = n if d == 0 else n_chunks - 1 - n
            st_ref[d, c] = states[d].astype(BF16)
            end_row = c * ch + (ch - 1 if d == 0 else 0)
            new_states.append(eb_ref[d, pl.ds(end_row, 1), :] * states[d] + kv_ref[d, c])
        return tuple(new_states)

    zero_state = jnp.zeros((128, 128), F32)
    lax.fori_loop(0, n_chunks, scan, (zero_state, zero_state))

    def inter(c, carry):
        rows = pl.ds(pl.multiple_of(c * ch, ch), ch)
        for d in range(2):
            acc_ref[d, rows, :] += _dot_nt(qd_ref[d, rows, :], st_ref[d, c])
        return carry

    lax.fori_loop(0, n_chunks, inter, 0, unroll=4)

    o = _rmsnorm_rows(acc_ref[0] + acc_ref[1], ng_ref[...])
    gate = 0.5 + 0.5 * jnp.tanh(0.5 * gate_ref[0].astype(F32))
    o_ref[0] = (o * gate).astype(o_ref.dtype)


def _hgrn(q, v, ff, fb, gate, lb_logits, norm_g, layer):
    bsz, seq, _ = q.shape
    n_chunks = seq // HGRN_CHUNK
    head_spec = pl.BlockSpec((1, seq, 128), lambda b, h: (b, 0, h))
    return pl.pallas_call(
        functools.partial(_hgrn_kernel, layer),
        grid=(bsz, HGRN_HEADS),
        in_specs=[head_spec] * 5 + [pl.BlockSpec((DEPTH, 128), lambda b, h: (0, h)),
                                    pl.BlockSpec((1, 128), lambda b, h: (0, h))],
        out_specs=head_spec,
        out_shape=jax.ShapeDtypeStruct((bsz, seq, D_MODEL), BF16),
        scratch_shapes=[pltpu.VMEM((2, seq, 128), BF16)] * 3
        + [pltpu.VMEM((2, seq, 128), F32),
           pltpu.VMEM((2, n_chunks, 128, 128), F32),
           pltpu.VMEM((2, n_chunks, 128, 128), BF16),
           pltpu.VMEM((2, seq, 128), F32)],
        compiler_params=_params(2),
        name="hgrn2",
    )(q, v, ff, fb, gate, lb_logits, norm_g)


def _rope_pair_tables(seq):
    inv = ROPE_THETA ** (-jnp.arange(0, ATTN_QK_DIM, 2, dtype=F32) / ATTN_QK_DIM)
    ang = jnp.arange(seq, dtype=F32)[:, None] * inv[None, :]
    return jnp.tile(jnp.cos(ang), (1, 4)), jnp.tile(jnp.sin(ang), (1, 4))


def _pair_layout_columns():
    idx = jnp.arange(ATTN_QK_WIDTH).reshape(2, 2, 2, 2, 32)
    return idx.transpose(0, 3, 1, 2, 4).reshape(-1)


def kernel(x, norm_mix_g, norm_mlp_g, final_norm_g, w_ff_in, w_ff_out, w_in_even, w_out_even, diff_lambda, diff_subln_g, s5_lam_re, s5_lam_im, s5_log_step, s5_b_re, s5_b_im, s5_c_re, s5_c_im, s5_d, s5_w_glu, s5_b_glu, w_in_odd, w_out_odd, hgrn_norm_g, hgrn_lb_logits):
    bsz, seq, _ = x.shape
    n_tok = bsz * seq
    n_chunks = seq // S5_CHUNK
    assert n_chunks & (n_chunks - 1) == 0 and n_chunks % 128 == 0
    x2 = x.reshape(n_tok, D_MODEL)
    cos, sin = _rope_pair_tables(seq)
    perm = _pair_layout_columns()

    for layer in range(DEPTH):
        g_mix = norm_mix_g[layer].reshape(1, D_MODEL)
        g_mlp = norm_mlp_g[layer].reshape(1, D_MODEL)
        w1 = w_ff_in[layer].astype(BF16)
        w2 = w_ff_out[layer].astype(BF16)
        if layer % 2 == 0:
            e = layer // 2
            w = w_in_even[e]
            w_qkv = jnp.concatenate([w[:, :ATTN_QK_WIDTH][:, perm],
                                     w[:, ATTN_QK_WIDTH:2 * ATTN_QK_WIDTH][:, perm],
                                     w[:, 2 * ATTN_QK_WIDTH:2 * ATTN_QK_WIDTH + ATTN_WIDTH]], axis=1).astype(BF16)
            wu = w[:, 2 * ATTN_QK_WIDTH + ATTN_WIDTH:].astype(BF16)
            q, k, v = _pre_even(x2, g_mix, w_qkv, cos, sin, seq)
            u_t = _s5_in(x2, g_mix, wu, bsz, seq)
            lambda_init = 0.8 - 0.6 * math.exp(-0.3 * layer)
            a_out = _attention(q.reshape(bsz, seq, -1), k.reshape(bsz, seq, -1), v.reshape(bsz, seq, -1),
                               diff_lambda[e], diff_subln_g[e].reshape(1, ATTN_V_DIM), lambda_init)
            ops = _s5_chunk_operators(s5_lam_re[e], s5_lam_im[e], s5_log_step[e], s5_b_re[e], s5_b_im[e],
                                      s5_c_re[e], s5_c_im[e], s5_d[e], n_chunks.bit_length() - 1)
            y_t = _s5_chunked(u_t.reshape(S5_GROUPS, S5_CW, bsz * n_chunks), *ops, n_chunks)
            b_out = _s5_out(y_t.reshape(S5_GROUPS, S5_CHUNK, S5_GROUP, bsz * n_chunks),
                            s5_w_glu[e].T.astype(BF16), s5_b_glu[e].reshape(S5_WIDTH, 1), bsz, seq)
            x2 = _post_even(x2, a_out.reshape(n_tok, ATTN_WIDTH), b_out, w_out_even[e].astype(BF16),
                            g_mlp, w1, w2)
        else:
            o_i = layer // 2
            q, v, ff, fb, gate = _pre_odd(x2, g_mix, w_in_odd[o_i].astype(BF16))
            shp = (bsz, seq, D_MODEL)
            mixed = _hgrn(q.reshape(shp), v.reshape(shp), ff.reshape(shp), fb.reshape(shp), gate.reshape(shp),
                          hgrn_lb_logits, hgrn_norm_g[o_i].reshape(1, D_MODEL), layer)
            assert layer == DEPTH - 1, "the final norm is fused into the last (odd) layer's kernel"
            x2 = _post_odd(x2, mixed.reshape(n_tok, D_MODEL), w_out_odd[o_i].astype(BF16), g_mlp, w1, w2,
                           final_norm_g.reshape(1, D_MODEL))
    return x2.reshape(bsz, seq, D_MODEL)
```

```python
import functools
import math

import jax
import jax.numpy as jnp
from jax import lax
from jax.experimental import pallas as pl
from jax.experimental.pallas import tpu as pltpu

D_MODEL = 1024
DEPTH = 2
ATTN_HEADS = 4
ATTN_QK_DIM = 64
ATTN_V_DIM = 128
ATTN_QK_WIDTH = 512
ATTN_WIDTH = 512
ROPE_THETA = 10000.0
S5_WIDTH = 512
S5_GROUP = 16
S5_GROUPS = 32
S5_STATE = 64
HGRN_HEADS = 8
HGRN_CHUNK = 64
HGRN_BLOCK = 256
D_FF = 4096
EPS = 1e-6

S5_CHUNK = 16
S5_CW = S5_CHUNK * S5_GROUP
S5_GROUP_BLOCK = 4

TOKEN_TILE = 512
ATTN_Q_TILE = 256
FF_TILE = 1024
VMEM_LIMIT = 56 * 1024 * 1024

BF16 = jnp.bfloat16
F32 = jnp.float32


def _const_spec(shape):
    nd = len(shape)
    return pl.BlockSpec(shape, lambda *_: (0,) * nd, pipeline_mode=pl.Buffered(1))


def _params(n_axes):
    return pltpu.CompilerParams(dimension_semantics=("arbitrary",) * n_axes,
                                vmem_limit_bytes=VMEM_LIMIT)


def _rmsnorm_rows(x, g):
    ms = jnp.mean(x * x, axis=-1, keepdims=True)
    return x * lax.rsqrt(ms + EPS) * g


def _sigmoid(x):
    return 1.0 / (1.0 + jnp.exp(-x))


def _gelu_tanh(x):
    c = math.sqrt(2.0 / math.pi)
    return 0.5 * x * (1.0 + jnp.tanh(c * (x + 0.044715 * (x * x * x))))


def _dot(a, b):
    return jnp.dot(a, b, preferred_element_type=F32)


def _dot_nt(a, b):
    return lax.dot_general(a, b, (((1,), (1,)), ((), ())), preferred_element_type=F32)


def _dot_tn(a, b):
    return lax.dot_general(a, b, (((0,), (0,)), ((), ())), preferred_element_type=F32)


def _pre_even_kernel(x_ref, g_ref, w_ref, cos_ref, sin_ref, q_ref, k_ref, v_ref):
    h = _rmsnorm_rows(x_ref[...], g_ref[...]).astype(BF16)
    cos = cos_ref[...]
    sin = sin_ref[...]
    for out_ref, base, scale in ((q_ref, 0, ATTN_QK_DIM ** -0.5), (k_ref, ATTN_QK_WIDTH, 1.0)):
        p = _dot(h, w_ref[:, base:base + ATTN_QK_WIDTH])
        for pair in range(2):
            lo = p[:, 256 * pair:256 * pair + 128]
            hi = p[:, 256 * pair + 128:256 * pair + 256]
            out_ref[:, 256 * pair:256 * pair + 128] = ((lo * cos - hi * sin) * scale).astype(BF16)
            out_ref[:, 256 * pair + 128:256 * pair + 256] = ((hi * cos + lo * sin) * scale).astype(BF16)
    v_ref[...] = _dot(h, w_ref[:, 2 * ATTN_QK_WIDTH:]).astype(BF16)


def _pre_even(x2, g, w_qkv, cos, sin, seq):
    n_tok = x2.shape[0]
    tm = TOKEN_TILE
    n_pos_blocks = seq // tm
    out = jax.ShapeDtypeStruct((n_tok, 512), BF16)
    row_spec = pl.BlockSpec((tm, 512), lambda i: (i, 0))
    rope_spec = pl.BlockSpec((tm, 128), lambda i: (i % n_pos_blocks, 0))
    return pl.pallas_call(
        _pre_even_kernel,
        grid=(n_tok // tm,),
        in_specs=[pl.BlockSpec((tm, D_MODEL), lambda i: (i, 0)), _const_spec((1, D_MODEL)),
                  _const_spec(w_qkv.shape), rope_spec, rope_spec],
        out_specs=[row_spec] * 3,
        out_shape=[out] * 3,
        compiler_params=_params(1),
        name="pre_even",
    )(x2, g, w_qkv, cos, sin)


def _s5_in_kernel(x_ref, g_ref, wu_ref, ut_ref, u_scr):
    nc = x_ref.shape[0] // S5_CHUNK
    groups_per_slab = 128 // S5_GROUP
    h = _rmsnorm_rows(x_ref[...], g_ref[...]).astype(BF16)
    u = _dot(h, wu_ref[...])
    for j in range(S5_WIDTH // 128):
        u_scr[j] = u[:, 128 * j:128 * (j + 1)]
    for j in range(S5_WIDTH // 128):
        for ph in range(S5_CHUNK):
            t = u_scr[j, pl.ds(ph, nc, stride=S5_CHUNK), :]
            ut_ref[groups_per_slab * j:groups_per_slab * (j + 1), ph, :, :] = (
                t.T.astype(BF16).reshape(groups_per_slab, S5_GROUP, nc))


def _s5_in(x2, g, wu, bsz, seq):
    nc = seq // S5_CHUNK
    return pl.pallas_call(
        _s5_in_kernel,
        grid=(bsz,),
        in_specs=[pl.BlockSpec((seq, D_MODEL), lambda b: (b, 0)), _const_spec((1, D_MODEL)),
                  _const_spec(wu.shape)],
        out_specs=pl.BlockSpec((S5_GROUPS, S5_CHUNK, S5_GROUP, nc), lambda b: (0, 0, 0, b)),
        out_shape=jax.ShapeDtypeStruct((S5_GROUPS, S5_CHUNK, S5_GROUP, bsz * nc), BF16),
        scratch_shapes=[pltpu.VMEM((S5_WIDTH // 128, seq, 128), F32)],
        compiler_params=_params(1),
        name="s5_in",
    )(x2, g, wu)


def _s5_out_kernel(yt_ref, wglut_ref, bglu_ref, o_ref, b_scr):
    nc = yt_ref.shape[3]
    yt = jnp.concatenate([yt_ref[:, ph, :, :].reshape(S5_WIDTH, nc) for ph in range(S5_CHUNK)], axis=1)
    yt = _gelu_tanh(yt)
    bt = yt * _sigmoid(_dot(wglut_ref[...], yt.astype(BF16)) + bglu_ref[...])
    for j in range(S5_WIDTH // 128):
        for ph in range(S5_CHUNK):
            b_scr[j, pl.ds(ph, nc, stride=S5_CHUNK), :] = bt[128 * j:128 * (j + 1), ph * nc:(ph + 1) * nc].T
    for j in range(S5_WIDTH // 128):
        o_ref[:, 128 * j:128 * (j + 1)] = b_scr[j]


def _s5_out(y_t, w_glu_t, b_glu_col, bsz, seq):
    nc = seq // S5_CHUNK
    return pl.pallas_call(
        _s5_out_kernel,
        grid=(bsz,),
        in_specs=[pl.BlockSpec((S5_GROUPS, S5_CHUNK, S5_GROUP, nc), lambda b: (0, 0, 0, b)),
                  _const_spec(w_glu_t.shape), _const_spec(b_glu_col.shape)],
        out_specs=pl.BlockSpec((seq, S5_WIDTH), lambda b: (b, 0)),
        out_shape=jax.ShapeDtypeStruct((bsz * seq, S5_WIDTH), F32),
        scratch_shapes=[pltpu.VMEM((S5_WIDTH // 128, seq, 128), F32)],
        compiler_params=_params(1),
        name="s5_out",
    )(y_t, w_glu_t, b_glu_col)


def _attn_kernel(lambda_init, q_ref, k_ref, v_ref, lam_ref, g_ref, o_ref):
    q = q_ref[0]
    k = k_ref[0]
    lam = lam_ref[...]
    lam_val = (jnp.exp(jnp.sum(lam[0:1] * lam[1:2], axis=-1, keepdims=True))
               - jnp.exp(jnp.sum(lam[2:3] * lam[3:4], axis=-1, keepdims=True)) + lambda_init)
    lane_group = (lax.broadcasted_iota(jnp.int32, q.shape, 1) // 32) % 4
    zero = jnp.zeros_like(q)
    for hh in range(2):
        probs = []
        for c in range(2):
            qm = jnp.where(lane_group == 2 * hh + c, q, zero)
            s = _dot_nt(qm, k)
            m = jnp.max(s, axis=-1, keepdims=True)
            e = jnp.exp(s - m)
            l = jnp.sum(e, axis=-1, keepdims=True)
            probs.append((e, 1.0 / l))
        w = probs[0][0] * probs[0][1] - probs[1][0] * (lam_val * probs[1][1])
        o = _dot(w.astype(BF16), v_ref[0, :, 128 * hh:128 * hh + 128])
        o = _rmsnorm_rows(o, g_ref[...]) * (1.0 - lambda_init)
        o_ref[0, :, 128 * hh:128 * hh + 128] = o.astype(o_ref.dtype)


def _attention(q, k, v, lam, subln_g, lambda_init):
    bsz, seq, _ = q.shape
    tq = ATTN_Q_TILE
    kv_spec = pl.BlockSpec((1, seq, 256), lambda b, p, i: (b, 0, p))
    return pl.pallas_call(
        functools.partial(_attn_kernel, lambda_init),
        grid=(bsz, 2, seq // tq),
        in_specs=[pl.BlockSpec((1, tq, 256), lambda b, p, i: (b, i, p)),
                  kv_spec, kv_spec,
                  _const_spec(lam.shape),
                  _const_spec(subln_g.shape)],
        out_specs=pl.BlockSpec((1, tq, 256), lambda b, p, i: (b, i, p)),
        out_shape=jax.ShapeDtypeStruct((bsz, seq, ATTN_WIDTH), BF16),
        compiler_params=_params(3),
        name="diff_attention",
    )(q, k, v, lam, subln_g)


def _s5_kernel(n_chunks, ut_ref, mi_ref, min_ref, mo_ref, coef_ref, y_ref, z_ref):
    n_lanes = ut_ref.shape[2]
    pos = lax.broadcasted_iota(jnp.int32, (S5_STATE, n_lanes), 1) % n_chunks
    n_steps = n_chunks.bit_length() - 1

    def one_group(g, carry):
        ut = ut_ref[g]
        st = _dot(min_ref[g], ut)
        coef = coef_ref[g]
        for d in range(2):
            xr = st[128 * d:128 * d + S5_STATE]
            xi = st[128 * d + S5_STATE:128 * d + 2 * S5_STATE]
            for k in range(n_steps):
                s = 1 << k
                if d == 0:
                    shift, keep = s, pos >= s
                else:
                    shift, keep = n_lanes - s, pos < n_chunks - s
                pr = jnp.where(keep, coef[S5_STATE * d:S5_STATE * (d + 1), k:k + 1], 0.0)
                pi = jnp.where(keep, coef[S5_STATE * d:S5_STATE * (d + 1), 8 + k:9 + k], 0.0)
                sr = pltpu.roll(xr, shift, 1)
                si = pltpu.roll(xi, shift, 1)
                xr, xi = xr + pr * sr - pi * si, xi + pr * si + pi * sr
            if d == 0:
                shift, keep = 1, pos >= 1
            else:
                shift, keep = n_lanes - 1, pos < n_chunks - 1
            z_ref[128 * d:128 * d + S5_STATE] = jnp.where(keep, pltpu.roll(xr, shift, 1), 0.0).astype(BF16)
            z_ref[128 * d + S5_STATE:128 * (d + 1)] = jnp.where(keep, pltpu.roll(xi, shift, 1), 0.0).astype(BF16)
        y_ref[g] = _dot(mi_ref[g], ut) + _dot(mo_ref[g], z_ref[...])
        return carry

    lax.fori_loop(0, ut_ref.shape[0], one_group, 0)


def _s5_chunked(u_t, m_intra_t, m_in_t, m_out_t, coef, n_chunks):
    n_groups, _, n_lanes = u_t.shape
    gb = S5_GROUP_BLOCK
    mat_spec = pl.BlockSpec((gb, S5_CW, S5_CW), lambda i: (i, 0, 0))
    return pl.pallas_call(
        functools.partial(_s5_kernel, n_chunks),
        grid=(n_groups // gb,),
        in_specs=[pl.BlockSpec((gb, S5_CW, n_lanes), lambda i: (i, 0, 0)), mat_spec, mat_spec, mat_spec,
                  pl.BlockSpec((gb, 2 * S5_STATE, 16), lambda i: (i, 0, 0))],
        out_specs=pl.BlockSpec((gb, S5_CW, n_lanes), lambda i: (i, 0, 0)),
        out_shape=jax.ShapeDtypeStruct((n_groups, S5_CW, n_lanes), F32),
        scratch_shapes=[pltpu.VMEM((4 * S5_STATE, n_lanes), BF16)],
        compiler_params=_params(1),
        name="s5_chunked",
    )(u_t, m_intra_t, m_in_t, m_out_t, coef)


def _s5_chunk_operators(lam_re, lam_im, log_step, b_re, b_im, c_re, c_im, d_skip, n_steps):
    hp = lax.Precision.HIGHEST
    t = S5_CHUNK
    lr = jnp.minimum(lam_re, -1e-4)
    li = lam_im
    dt = jnp.exp(log_step)[..., None]
    mag = jnp.exp(lr * dt)
    ar = mag * jnp.cos(li * dt)
    ai = mag * jnp.sin(li * dt)
    den = lr * lr + li * li
    cr = ((ar - 1.0) * lr + ai * li) / den
    ci = (ai * lr - (ar - 1.0) * li) / den
    bbr = cr[..., None] * b_re - ci[..., None] * b_im
    bbi = cr[..., None] * b_im + ci[..., None] * b_re

    def powers(taus):
        taus = taus.astype(F32)
        pmag = jnp.exp(taus * (lr * dt)[..., None])
        ang = taus * (li * dt)[..., None]
        return pmag * jnp.cos(ang), pmag * jnp.sin(ang)

    pr, pi = powers(jnp.arange(t))
    rr = pr[..., None] * bbr[:, :, :, None, :] - pi[..., None] * bbi[:, :, :, None, :]
    ri = pr[..., None] * bbi[:, :, :, None, :] + pi[..., None] * bbr[:, :, :, None, :]
    rf_r = rr[0, :, :, ::-1].reshape(S5_GROUPS, S5_STATE, S5_CW)
    rf_i = ri[0, :, :, ::-1].reshape(S5_GROUPS, S5_STATE, S5_CW)
    rb_r = rr[1].reshape(S5_GROUPS, S5_STATE, S5_CW)
    rb_i = ri[1].reshape(S5_GROUPS, S5_STATE, S5_CW)
    m_in_t = jnp.concatenate([rf_r, rf_i, rb_r, rb_i], axis=1)

    taps_f = (jnp.einsum('gnp,gpx->gnx', c_re[0], rf_r, precision=hp)
              - jnp.einsum('gnp,gpx->gnx', c_im[0], rf_i, precision=hp))
    taps_b = (jnp.einsum('gnp,gpx->gnx', c_re[1], rb_r, precision=hp)
              - jnp.einsum('gnp,gpx->gnx', c_im[1], rb_i, precision=hp))
    skip = d_skip.reshape(S5_GROUPS, S5_GROUP)[:, :, None] * jnp.eye(S5_GROUP, dtype=F32)[None]
    edge = (t - 1) * S5_GROUP
    taps = (jnp.pad(taps_f, ((0, 0), (0, 0), (0, edge))) + jnp.pad(taps_b, ((0, 0), (0, 0), (edge, 0)))
            + jnp.pad(skip, ((0, 0), (0, 0), (edge, edge))))
    m_intra_t = jnp.stack([taps[:, :, (t - 1 - j) * S5_GROUP:(t - 1 - j) * S5_GROUP + S5_CW]
                           for j in range(t)], axis=1).reshape(S5_GROUPS, S5_CW, S5_CW)

    qr, qi = powers(jnp.arange(1, t + 1))
    qr = qr.transpose(0, 1, 3, 2)
    qi = qi.transpose(0, 1, 3, 2)
    er = qr[:, :, :, None, :] * c_re[:, :, None] - qi[:, :, :, None, :] * c_im[:, :, None]
    ei = qr[:, :, :, None, :] * c_im[:, :, None] + qi[:, :, :, None, :] * c_re[:, :, None]
    m_out_t = jnp.concatenate([er[0], -ei[0], er[1, :, ::-1], -ei[1, :, ::-1]],
                              axis=-1).reshape(S5_GROUPS, S5_CW, 4 * S5_STATE)

    ar16, ai16 = qr[:, :, t - 1], qi[:, :, t - 1]
    cols_r, cols_i = [], []
    for _ in range(n_steps):
        cols_r.append(ar16)
        cols_i.append(ai16)
        ar16, ai16 = ar16 * ar16 - ai16 * ai16, 2.0 * ar16 * ai16
    pad = [jnp.zeros_like(ar16)] * (8 - n_steps)
    coef = jnp.stack(cols_r + pad + cols_i + pad, axis=-1)
    coef = coef.transpose(1, 0, 2, 3).reshape(S5_GROUPS, 2 * S5_STATE, 16)
    return m_intra_t.astype(BF16), m_in_t.astype(BF16), m_out_t.astype(BF16), coef


def _mlp(x1, g_ref, w1_ref, w2_ref):
    h = _rmsnorm_rows(x1, g_ref[...]).astype(BF16)
    acc = x1
    for j in range(D_FF // FF_TILE):
        hid = _dot(h, w1_ref[:, j * FF_TILE:(j + 1) * FF_TILE])
        hid = jnp.square(jnp.maximum(hid, 0.0)).astype(BF16)
        acc = acc + _dot(hid, w2_ref[j * FF_TILE:(j + 1) * FF_TILE, :])
    return acc


def _post_even_kernel(x_ref, a_ref, b_ref, wout_ref, g_ref, w1_ref, w2_ref, o_ref):
    mix = (_dot(a_ref[...], wout_ref[0:ATTN_WIDTH, :])
           + _dot(b_ref[...].astype(BF16), wout_ref[ATTN_WIDTH:, :]))
    o_ref[...] = _mlp(x_ref[...] + mix, g_ref, w1_ref, w2_ref)


def _post_even(x2, a_out, b_out, w_out, g_mlp, w1, w2):
    n_tok = x2.shape[0]
    tm = TOKEN_TILE
    return pl.pallas_call(
        _post_even_kernel,
        grid=(n_tok // tm,),
        in_specs=[pl.BlockSpec((tm, D_MODEL), lambda i: (i, 0)),
                  pl.BlockSpec((tm, ATTN_WIDTH), lambda i: (i, 0)),
                  pl.BlockSpec((tm, S5_WIDTH), lambda i: (i, 0)),
                  _const_spec(w_out.shape), _const_spec(g_mlp.shape), _const_spec(w1.shape),
                  _const_spec(w2.shape)],
        out_specs=pl.BlockSpec((tm, D_MODEL), lambda i: (i, 0)),
        out_shape=jax.ShapeDtypeStruct((n_tok, D_MODEL), F32),
        compiler_params=_params(1),
        name="post_even",
    )(x2, a_out, b_out, w_out, g_mlp, w1, w2)


def _post_odd_kernel(x_ref, m_ref, wout_ref, g_ref, w1_ref, w2_ref, gf_ref, o_ref):
    x1 = x_ref[...] + _dot(m_ref[...], wout_ref[...])
    o_ref[...] = _rmsnorm_rows(_mlp(x1, g_ref, w1_ref, w2_ref), gf_ref[...])


def _post_odd(x2, mixed, w_out, g_mlp, w1, w2, g_final):
    n_tok = x2.shape[0]
    tm = TOKEN_TILE
    return pl.pallas_call(
        _post_odd_kernel,
        grid=(n_tok // tm,),
        in_specs=[pl.BlockSpec((tm, D_MODEL), lambda i: (i, 0)),
                  pl.BlockSpec((tm, D_MODEL), lambda i: (i, 0)),
                  _const_spec(w_out.shape), _const_spec(g_mlp.shape),
                  _const_spec(w1.shape), _const_spec(w2.shape), _const_spec(g_final.shape)],
        out_specs=pl.BlockSpec((tm, D_MODEL), lambda i: (i, 0)),
        out_shape=jax.ShapeDtypeStruct((n_tok, D_MODEL), F32),
        compiler_params=_params(1),
        name="post_odd",
    )(x2, mixed, w_out, g_mlp, w1, w2, g_final)


def _pre_odd_kernel(x_ref, g_ref, w_ref, q_ref, i_ref, ff_ref, fb_ref, gate_ref):
    h = _rmsnorm_rows(x_ref[...], g_ref[...]).astype(BF16)
    for s, out_ref in enumerate((q_ref, i_ref, ff_ref, fb_ref, gate_ref)):
        out_ref[...] = _dot(h, w_ref[:, s * D_MODEL:(s + 1) * D_MODEL]).astype(out_ref.dtype)


def _pre_odd(x2, g, w):
    n_tok = x2.shape[0]
    tm = TOKEN_TILE
    row_spec = pl.BlockSpec((tm, D_MODEL), lambda i: (i, 0))
    lo = jax.ShapeDtypeStruct((n_tok, D_MODEL), BF16)
    hi = jax.ShapeDtypeStruct((n_tok, D_MODEL), F32)
    return pl.pallas_call(
        _pre_odd_kernel,
        grid=(n_tok // tm,),
        in_specs=[row_spec, _const_spec((1, D_MODEL)), _const_spec(w.shape)],
        out_specs=[row_spec] * 5,
        out_shape=[lo, lo, hi, hi, lo],
        compiler_params=_params(1),
        name="pre_odd",
    )(x2, g, w)


def _chunk_cumprod(x, reverse):
    n = x.shape[0]
    pos = lax.broadcasted_iota(jnp.int32, x.shape, 0) % HGRN_CHUNK
    s = 1
    while s < HGRN_CHUNK:
        if reverse:
            shifted = pltpu.roll(x, n - s, 0)
            x = x * jnp.where(pos < HGRN_CHUNK - s, shifted, 1.0)
        else:
            shifted = pltpu.roll(x, s, 0)
            x = x * jnp.where(pos >= s, shifted, 1.0)
        s *= 2
    return x


def _hgrn_kernel(layer, q_ref, v_ref, ff_ref, fb_ref, gate_ref, lbl_ref, ng_ref, o_ref,
                 qd_ref, dec_ref, kv_ref, st_ref, acc_ref):
    seq = q_ref.shape[1]
    ch = HGRN_CHUNK
    blk = HGRN_BLOCK
    n_chunks = seq // ch
    per_blk = blk // ch
    logits = lbl_ref[...]
    soft = jnp.exp(logits - jnp.max(logits, axis=0, keepdims=True))
    soft = soft / jnp.sum(soft, axis=0, keepdims=True)
    lb = jnp.sum(soft[0:layer + 1], axis=0, keepdims=True) - soft[0:1]

    c2 = 0.5 * (1.0 - lb)
    c1 = lb + c2
    row = lax.broadcasted_iota(jnp.int32, (blk, blk), 0)
    col = lax.broadcasted_iota(jnp.int32, (blk, blk), 1)
    same_chunk = (row // ch) == (col // ch)
    row_chunk = lax.broadcasted_iota(jnp.int32, (blk, 128), 0) // ch

    def intra(j, carry):
        sl = pl.ds(pl.multiple_of(j * blk, blk), blk)
        v = v_ref[0, sl, :]
        vt = v.astype(F32).T.astype(BF16)
        qf = q_ref[0, sl, :].astype(F32)
        for d, f_ref in enumerate((ff_ref, fb_ref)):
            ct = c2 * jnp.tanh(0.5 * f_ref[0, sl, :])
            eb = _chunk_cumprod(c1 + ct, reverse=(d == 1))
            k_inv = (c2 - ct) / eb
            eb3 = eb.reshape(per_blk, ch, 128)
            e_end = eb3[:, ch - 1:ch, :] if d == 0 else eb3[:, 0:1, :]
            dec_ref[d, pl.ds(j * per_blk, per_blk)] = e_end
            q_dec = (qf * eb).astype(BF16)
            qd_ref[d, sl, :] = q_dec
            k_dec = (k_inv.reshape(per_blk, ch, 128) * e_end).reshape(blk, 128).astype(BF16)

            causal = (row >= col) if d == 0 else (row <= col)
            s = _dot_nt(q_dec, k_inv.astype(BF16))
            p = jnp.where(same_chunk, jnp.where(causal, s, 0.0), 0.0).astype(BF16)
            acc_ref[d, sl, :] = _dot(p, v)
            rhs = jnp.concatenate([jnp.where(row_chunk == cc, k_dec, jnp.zeros_like(k_dec))
                                   for cc in range(per_blk)], axis=1)
            kvs = _dot(vt, rhs)
            for cc in range(per_blk):
                kv_ref[d, j * per_blk + cc] = kvs[:, 128 * cc:128 * (cc + 1)]
        return carry

    lax.fori_loop(0, seq // blk, intra, 0, unroll=4)

    def scan(n, states):
        new_states = []
        for d in range(2):
            c = n if d == 0 else n_chunks - 1 - n
            st_ref[d, c] = states[d].astype(BF16)
            new_states.append(dec_ref[d, c] * states[d] + kv_ref[d, c])
        return tuple(new_states)

    zero_state = jnp.zeros((128, 128), F32)
    lax.fori_loop(0, n_chunks, scan, (zero_state, zero_state))

    for c in range(n_chunks):
        rows = slice(c * ch, (c + 1) * ch)
        for d in range(2):
            acc_ref[d, rows, :] += _dot_nt(qd_ref[d, rows, :], st_ref[d, c])

    o = _rmsnorm_rows(acc_ref[0] + acc_ref[1], ng_ref[...])
    gate = 0.5 + 0.5 * jnp.tanh(0.5 * gate_ref[0].astype(F32))
    o_ref[0] = (o * gate).astype(o_ref.dtype)


def _hgrn(q, v, ff, fb, gate, lb_logits, norm_g, layer):
    bsz, seq, _ = q.shape
    n_chunks = seq // HGRN_CHUNK
    head_spec = pl.BlockSpec((1, seq, 128), lambda b, h: (b, 0, h))
    return pl.pallas_call(
        functools.partial(_hgrn_kernel, layer),
        grid=(bsz, HGRN_HEADS),
        in_specs=[head_spec] * 5 + [pl.BlockSpec((DEPTH, 128), lambda b, h: (0, h)),
                                    pl.BlockSpec((1, 128), lambda b, h: (0, h))],
        out_specs=head_spec,
        out_shape=jax.ShapeDtypeStruct((bsz, seq, D_MODEL), BF16),
        scratch_shapes=[
           pltpu.VMEM((2, seq, 128), BF16),
           pltpu.VMEM((2, n_chunks, 1, 128), F32),
           pltpu.VMEM((2, n_chunks, 128, 128), F32),
           pltpu.VMEM((2, n_chunks, 128, 128), BF16),
           pltpu.VMEM((2, seq, 128), F32)],
        compiler_params=_params(2),
        name="hgrn2",
    )(q, v, ff, fb, gate, lb_logits, norm_g)


def _rope_pair_tables(seq):
    inv = ROPE_THETA ** (-jnp.arange(0, ATTN_QK_DIM, 2, dtype=F32) / ATTN_QK_DIM)
    ang = jnp.arange(seq, dtype=F32)[:, None] * inv[None, :]
    return jnp.tile(jnp.cos(ang), (1, 4)), jnp.tile(jnp.sin(ang), (1, 4))


def _pair_layout_columns():
    idx = jnp.arange(ATTN_QK_WIDTH).reshape(2, 2, 2, 2, 32)
    return idx.transpose(0, 3, 1, 2, 4).reshape(-1)


def kernel(x, norm_mix_g, norm_mlp_g, final_norm_g, w_ff_in, w_ff_out, w_in_even, w_out_even, diff_lambda, diff_subln_g, s5_lam_re, s5_lam_im, s5_log_step, s5_b_re, s5_b_im, s5_c_re, s5_c_im, s5_d, s5_w_glu, s5_b_glu, w_in_odd, w_out_odd, hgrn_norm_g, hgrn_lb_logits):
    bsz, seq, _ = x.shape
    n_tok = bsz * seq
    n_chunks = seq // S5_CHUNK
    assert n_chunks & (n_chunks - 1) == 0 and n_chunks % 128 == 0
    x2 = x.reshape(n_tok, D_MODEL)
    cos, sin = _rope_pair_tables(seq)
    perm = _pair_layout_columns()

    for layer in range(DEPTH):
        g_mix = norm_mix_g[layer].reshape(1, D_MODEL)
        g_mlp = norm_mlp_g[layer].reshape(1, D_MODEL)
        w1 = w_ff_in[layer].astype(BF16)
        w2 = w_ff_out[layer].astype(BF16)
        if layer % 2 == 0:
            e = layer // 2
            w = w_in_even[e]
            w_qkv = jnp.concatenate([w[:, :ATTN_QK_WIDTH][:, perm],
                                     w[:, ATTN_QK_WIDTH:2 * ATTN_QK_WIDTH][:, perm],
                                     w[:, 2 * ATTN_QK_WIDTH:2 * ATTN_QK_WIDTH + ATTN_WIDTH]], axis=1).astype(BF16)
            wu = w[:, 2 * ATTN_QK_WIDTH + ATTN_WIDTH:].astype(BF16)
            q, k, v = _pre_even(x2, g_mix, w_qkv, cos, sin, seq)
            u_t = _s5_in(x2, g_mix, wu, bsz, seq)
            lambda_init = 0.8 - 0.6 * math.exp(-0.3 * layer)
            a_out = _attention(q.reshape(bsz, seq, -1), k.reshape(bsz, seq, -1), v.reshape(bsz, seq, -1),
                               diff_lambda[e], diff_subln_g[e].reshape(1, ATTN_V_DIM), lambda_init)
            ops = _s5_chunk_operators(s5_lam_re[e], s5_lam_im[e], s5_log_step[e], s5_b_re[e], s5_b_im[e],
                                      s5_c_re[e], s5_c_im[e], s5_d[e], n_chunks.bit_length() - 1)
            y_t = _s5_chunked(u_t.reshape(S5_GROUPS, S5_CW, bsz * n_chunks), *ops, n_chunks)
            b_out = _s5_out(y_t.reshape(S5_GROUPS, S5_CHUNK, S5_GROUP, bsz * n_chunks),
                            s5_w_glu[e].T.astype(BF16), s5_b_glu[e].reshape(S5_WIDTH, 1), bsz, seq)
            x2 = _post_even(x2, a_out.reshape(n_tok, ATTN_WIDTH), b_out, w_out_even[e].astype(BF16),
                            g_mlp, w1, w2)
        else:
            o_i = layer // 2
            q, v, ff, fb, gate = _pre_odd(x2, g_mix, w_in_odd[o_i].astype(BF16))
            shp = (bsz, seq, D_MODEL)
            mixed = _hgrn(q.reshape(shp), v.reshape(shp), ff.reshape(shp), fb.reshape(shp), gate.reshape(shp),
                          hgrn_lb_logits, hgrn_norm_g[o_i].reshape(1, D_MODEL), layer)
            assert layer == DEPTH - 1, "the final norm is fused into the last (odd) layer's kernel"
            x2 = _post_odd(x2, mixed.reshape(n_tok, D_MODEL), w_out_odd[o_i].astype(BF16), g_mlp, w1, w2,
                           final_norm_g.reshape(1, D_MODEL))
    return x2.reshape(bsz, seq, D_MODEL)
```

```python
import functools
import math

import jax
import jax.numpy as jnp
from jax import lax
from jax.experimental import pallas as pl
from jax.experimental.pallas import tpu as pltpu

D_MODEL = 1024
DEPTH = 2
ATTN_HEADS = 4
ATTN_QK_DIM = 64
ATTN_V_DIM = 128
ATTN_QK_WIDTH = 512
ATTN_WIDTH = 512
ROPE_THETA = 10000.0
S5_WIDTH = 512
S5_GROUP = 16
S5_GROUPS = 32
S5_STATE = 64
HGRN_HEADS = 8
HGRN_CHUNK = 64
HGRN_BLOCK = 256
D_FF = 4096
EPS = 1e-6

S5_CHUNK = 16
S5_CW = S5_CHUNK * S5_GROUP
S5_GROUP_BLOCK = 4

TOKEN_TILE = 512
ATTN_Q_TILE = 256
FF_TILE = 1024
VMEM_LIMIT = 56 * 1024 * 1024

BF16 = jnp.bfloat16
F32 = jnp.float32


def _const_spec(shape):
    nd = len(shape)
    return pl.BlockSpec(shape, lambda *_: (0,) * nd, pipeline_mode=pl.Buffered(1))


def _params(n_axes):
    return pltpu.CompilerParams(dimension_semantics=("arbitrary",) * n_axes,
                                vmem_limit_bytes=VMEM_LIMIT)


def _rmsnorm_rows(x, g):
    ms = jnp.mean(x * x, axis=-1, keepdims=True)
    return x * lax.rsqrt(ms + EPS) * g


def _sigmoid(x):
    return 1.0 / (1.0 + jnp.exp(-x))


def _gelu_tanh(x):
    c = math.sqrt(2.0 / math.pi)
    return 0.5 * x * (1.0 + jnp.tanh(c * (x + 0.044715 * (x * x * x))))


def _dot(a, b):
    return jnp.dot(a, b, preferred_element_type=F32)


def _dot_nt(a, b):
    return lax.dot_general(a, b, (((1,), (1,)), ((), ())), preferred_element_type=F32)


def _dot_tn(a, b):
    return lax.dot_general(a, b, (((0,), (0,)), ((), ())), preferred_element_type=F32)


def _pre_even_kernel(x_ref, g_ref, w_ref, cos_ref, sin_ref, q_ref, k_ref, v_ref):
    h = _rmsnorm_rows(x_ref[...], g_ref[...]).astype(BF16)
    cos = cos_ref[...]
    sin = sin_ref[...]
    for out_ref, base, scale in ((q_ref, 0, ATTN_QK_DIM ** -0.5 * math.log2(math.e)),
                                 (k_ref, ATTN_QK_WIDTH, 1.0)):
        p = _dot(h, w_ref[:, base:base + ATTN_QK_WIDTH])
        for pair in range(2):
            lo = p[:, 256 * pair:256 * pair + 128]
            hi = p[:, 256 * pair + 128:256 * pair + 256]
            out_ref[:, 256 * pair:256 * pair + 128] = ((lo * cos - hi * sin) * scale).astype(BF16)
            out_ref[:, 256 * pair + 128:256 * pair + 256] = ((hi * cos + lo * sin) * scale).astype(BF16)
    v_ref[...] = _dot(h, w_ref[:, 2 * ATTN_QK_WIDTH:]).astype(BF16)


def _pre_even(x2, g, w_qkv, cos, sin, seq):
    n_tok = x2.shape[0]
    tm = TOKEN_TILE
    n_pos_blocks = seq // tm
    out = jax.ShapeDtypeStruct((n_tok, 512), BF16)
    row_spec = pl.BlockSpec((tm, 512), lambda i: (i, 0))
    rope_spec = pl.BlockSpec((tm, 128), lambda i: (i % n_pos_blocks, 0))
    return pl.pallas_call(
        _pre_even_kernel,
        grid=(n_tok // tm,),
        in_specs=[pl.BlockSpec((tm, D_MODEL), lambda i: (i, 0)), _const_spec((1, D_MODEL)),
                  _const_spec(w_qkv.shape), rope_spec, rope_spec],
        out_specs=[row_spec] * 3,
        out_shape=[out] * 3,
        compiler_params=_params(1),
        name="pre_even",
    )(x2, g, w_qkv, cos, sin)


def _s5_in_kernel(x_ref, g_ref, wu_ref, ut_ref, u_scr):
    nc = x_ref.shape[0] // S5_CHUNK
    groups_per_slab = 128 // S5_GROUP
    h = _rmsnorm_rows(x_ref[...], g_ref[...]).astype(BF16)
    u = _dot(h, wu_ref[...])
    for j in range(S5_WIDTH // 128):
        u_scr[j] = u[:, 128 * j:128 * (j + 1)]
    for j in range(S5_WIDTH // 128):
        for ph in range(S5_CHUNK):
            t = u_scr[j, pl.ds(ph, nc, stride=S5_CHUNK), :]
            ut_ref[groups_per_slab * j:groups_per_slab * (j + 1), ph, :, :] = (
                t.T.astype(BF16).reshape(groups_per_slab, S5_GROUP, nc))


def _s5_in(x2, g, wu, bsz, seq):
    nc = seq // S5_CHUNK
    return pl.pallas_call(
        _s5_in_kernel,
        grid=(bsz,),
        in_specs=[pl.BlockSpec((seq, D_MODEL), lambda b: (b, 0)), _const_spec((1, D_MODEL)),
                  _const_spec(wu.shape)],
        out_specs=pl.BlockSpec((S5_GROUPS, S5_CHUNK, S5_GROUP, nc), lambda b: (0, 0, 0, b)),
        out_shape=jax.ShapeDtypeStruct((S5_GROUPS, S5_CHUNK, S5_GROUP, bsz * nc), BF16),
        scratch_shapes=[pltpu.VMEM((S5_WIDTH // 128, seq, 128), F32)],
        compiler_params=_params(1),
        name="s5_in",
    )(x2, g, wu)


def _s5_out_kernel(yt_ref, wglut_ref, bglu_ref, o_ref, b_scr):
    nc = yt_ref.shape[3]
    yt = jnp.concatenate([yt_ref[:, ph, :, :].reshape(S5_WIDTH, nc) for ph in range(S5_CHUNK)], axis=1)
    yt = _gelu_tanh(yt)
    bt = yt * _sigmoid(_dot(wglut_ref[...], yt.astype(BF16)) + bglu_ref[...])
    for j in range(S5_WIDTH // 128):
        for ph in range(S5_CHUNK):
            b_scr[j, pl.ds(ph, nc, stride=S5_CHUNK), :] = bt[128 * j:128 * (j + 1), ph * nc:(ph + 1) * nc].T
    for j in range(S5_WIDTH // 128):
        o_ref[:, 128 * j:128 * (j + 1)] = b_scr[j]


def _s5_out(y_t, w_glu_t, b_glu_col, bsz, seq):
    nc = seq // S5_CHUNK
    return pl.pallas_call(
        _s5_out_kernel,
        grid=(bsz,),
        in_specs=[pl.BlockSpec((S5_GROUPS, S5_CHUNK, S5_GROUP, nc), lambda b: (0, 0, 0, b)),
                  _const_spec(w_glu_t.shape), _const_spec(b_glu_col.shape)],
        out_specs=pl.BlockSpec((seq, S5_WIDTH), lambda b: (b, 0)),
        out_shape=jax.ShapeDtypeStruct((bsz * seq, S5_WIDTH), F32),
        scratch_shapes=[pltpu.VMEM((S5_WIDTH // 128, seq, 128), F32)],
        compiler_params=_params(1),
        name="s5_out",
    )(y_t, w_glu_t, b_glu_col)


def _attn_kernel(lambda_init, q_ref, k_ref, v_ref, lam_ref, g_ref, o_ref):
    q = q_ref[0]
    k = k_ref[0]
    lam = lam_ref[...]
    lam_val = (jnp.exp(jnp.sum(lam[0:1] * lam[1:2], axis=-1, keepdims=True))
               - jnp.exp(jnp.sum(lam[2:3] * lam[3:4], axis=-1, keepdims=True)) + lambda_init)
    tq = q.shape[0]
    lane_group = (lax.broadcasted_iota(jnp.int32, q.shape, 1) // 32) % 4
    zero = jnp.zeros_like(q)
    q_all = jnp.concatenate([jnp.where(lane_group == hc, q, zero) for hc in range(4)], axis=0)
    s = _dot_nt(q_all, k)
    e = jnp.exp2(s - jnp.max(s, axis=-1, keepdims=True)).astype(BF16)
    ones = jnp.ones((k.shape[0], ATTN_V_DIM), BF16)
    for hh in range(2):
        v_ext = jnp.concatenate([v_ref[0, :, 128 * hh:128 * hh + 128], ones], axis=1)
        r = _dot(e[2 * hh * tq:2 * (hh + 1) * tq], v_ext)
        o = (r[0:tq, 0:ATTN_V_DIM] / r[0:tq, ATTN_V_DIM:]
             - lam_val * (r[tq:2 * tq, 0:ATTN_V_DIM] / r[tq:2 * tq, ATTN_V_DIM:]))
        o = _rmsnorm_rows(o, g_ref[...]) * (1.0 - lambda_init)
        o_ref[0, :, 128 * hh:128 * hh + 128] = o.astype(o_ref.dtype)


def _attention(q, k, v, lam, subln_g, lambda_init):
    bsz, seq, _ = q.shape
    tq = ATTN_Q_TILE
    kv_spec = pl.BlockSpec((1, seq, 256), lambda b, p, i: (b, 0, p))
    return pl.pallas_call(
        functools.partial(_attn_kernel, lambda_init),
        grid=(bsz, 2, seq // tq),
        in_specs=[pl.BlockSpec((1, tq, 256), lambda b, p, i: (b, i, p)),
                  kv_spec, kv_spec,
                  _const_spec(lam.shape),
                  _const_spec(subln_g.shape)],
        out_specs=pl.BlockSpec((1, tq, 256), lambda b, p, i: (b, i, p)),
        out_shape=jax.ShapeDtypeStruct((bsz, seq, ATTN_WIDTH), BF16),
        compiler_params=_params(3),
        name="diff_attention",
    )(q, k, v, lam, subln_g)


def _s5_kernel(n_chunks, ut_ref, mi_ref, min_ref, mo_ref, coef_ref, y_ref, z_ref):
    n_lanes = ut_ref.shape[2]
    pos = lax.broadcasted_iota(jnp.int32, (S5_STATE, n_lanes), 1) % n_chunks
    n_steps = n_chunks.bit_length() - 1

    def one_group(g, carry):
        ut = ut_ref[g]
        st = _dot(min_ref[g], ut)
        coef = coef_ref[g]
        for d in range(2):
            xr = st[128 * d:128 * d + S5_STATE]
            xi = st[128 * d + S5_STATE:128 * d + 2 * S5_STATE]
            for k in range(n_steps):
                s = 1 << k
                if d == 0:
                    shift, keep = s, pos >= s
                else:
                    shift, keep = n_lanes - s, pos < n_chunks - s
                pr = jnp.where(keep, coef[S5_STATE * d:S5_STATE * (d + 1), k:k + 1], 0.0)
                pi = jnp.where(keep, coef[S5_STATE * d:S5_STATE * (d + 1), 8 + k:9 + k], 0.0)
                sr = pltpu.roll(xr, shift, 1)
                si = pltpu.roll(xi, shift, 1)
                xr, xi = xr + pr * sr - pi * si, xi + pr * si + pi * sr
            if d == 0:
                shift, keep = 1, pos >= 1
            else:
                shift, keep = n_lanes - 1, pos < n_chunks - 1
            z_ref[128 * d:128 * d + S5_STATE] = jnp.where(keep, pltpu.roll(xr, shift, 1), 0.0).astype(BF16)
            z_ref[128 * d + S5_STATE:128 * (d + 1)] = jnp.where(keep, pltpu.roll(xi, shift, 1), 0.0).astype(BF16)
        y_ref[g] = _dot(mi_ref[g], ut) + _dot(mo_ref[g], z_ref[...])
        return carry

    lax.fori_loop(0, ut_ref.shape[0], one_group, 0)


def _s5_chunked(u_t, m_intra_t, m_in_t, m_out_t, coef, n_chunks):
    n_groups, _, n_lanes = u_t.shape
    gb = S5_GROUP_BLOCK
    mat_spec = pl.BlockSpec((gb, S5_CW, S5_CW), lambda i: (i, 0, 0))
    return pl.pallas_call(
        functools.partial(_s5_kernel, n_chunks),
        grid=(n_groups // gb,),
        in_specs=[pl.BlockSpec((gb, S5_CW, n_lanes), lambda i: (i, 0, 0)), mat_spec, mat_spec, mat_spec,
                  pl.BlockSpec((gb, 2 * S5_STATE, 16), lambda i: (i, 0, 0))],
        out_specs=pl.BlockSpec((gb, S5_CW, n_lanes), lambda i: (i, 0, 0)),
        out_shape=jax.ShapeDtypeStruct((n_groups, S5_CW, n_lanes), F32),
        scratch_shapes=[pltpu.VMEM((4 * S5_STATE, n_lanes), BF16)],
        compiler_params=_params(1),
        name="s5_chunked",
    )(u_t, m_intra_t, m_in_t, m_out_t, coef)


def _s5_chunk_operators(lam_re, lam_im, log_step, b_re, b_im, c_re, c_im, d_skip, n_steps):
    hp = lax.Precision.HIGHEST
    t = S5_CHUNK
    lr = jnp.minimum(lam_re, -1e-4)
    li = lam_im
    dt = jnp.exp(log_step)[..., None]
    mag = jnp.exp(lr * dt)
    ar = mag * jnp.cos(li * dt)
    ai = mag * jnp.sin(li * dt)
    den = lr * lr + li * li
    cr = ((ar - 1.0) * lr + ai * li) / den
    ci = (ai * lr - (ar - 1.0) * li) / den
    bbr = cr[..., None] * b_re - ci[..., None] * b_im
    bbi = cr[..., None] * b_im + ci[..., None] * b_re

    def powers(taus):
        taus = taus.astype(F32)
        pmag = jnp.exp(taus * (lr * dt)[..., None])
        ang = taus * (li * dt)[..., None]
        return pmag * jnp.cos(ang), pmag * jnp.sin(ang)

    pr, pi = powers(jnp.arange(t))
    rr = pr[..., None] * bbr[:, :, :, None, :] - pi[..., None] * bbi[:, :, :, None, :]
    ri = pr[..., None] * bbi[:, :, :, None, :] + pi[..., None] * bbr[:, :, :, None, :]
    rf_r = rr[0, :, :, ::-1].reshape(S5_GROUPS, S5_STATE, S5_CW)
    rf_i = ri[0, :, :, ::-1].reshape(S5_GROUPS, S5_STATE, S5_CW)
    rb_r = rr[1].reshape(S5_GROUPS, S5_STATE, S5_CW)
    rb_i = ri[1].reshape(S5_GROUPS, S5_STATE, S5_CW)
    m_in_t = jnp.concatenate([rf_r, rf_i, rb_r, rb_i], axis=1)

    taps_f = (jnp.einsum('gnp,gpx->gnx', c_re[0], rf_r, precision=hp)
              - jnp.einsum('gnp,gpx->gnx', c_im[0], rf_i, precision=hp))
    taps_b = (jnp.einsum('gnp,gpx->gnx', c_re[1], rb_r, precision=hp)
              - jnp.einsum('gnp,gpx->gnx', c_im[1], rb_i, precision=hp))
    skip = d_skip.reshape(S5_GROUPS, S5_GROUP)[:, :, None] * jnp.eye(S5_GROUP, dtype=F32)[None]
    edge = (t - 1) * S5_GROUP
    taps = (jnp.pad(taps_f, ((0, 0), (0, 0), (0, edge))) + jnp.pad(taps_b, ((0, 0), (0, 0), (edge, 0)))
            + jnp.pad(skip, ((0, 0), (0, 0), (edge, edge))))
    m_intra_t = jnp.stack([taps[:, :, (t - 1 - j) * S5_GROUP:(t - 1 - j) * S5_GROUP + S5_CW]
                           for j in range(t)], axis=1).reshape(S5_GROUPS, S5_CW, S5_CW)

    qr, qi = powers(jnp.arange(1, t + 1))
    qr = qr.transpose(0, 1, 3, 2)
    qi = qi.transpose(0, 1, 3, 2)
    er = qr[:, :, :, None, :] * c_re[:, :, None] - qi[:, :, :, None, :] * c_im[:, :, None]
    ei = qr[:, :, :, None, :] * c_im[:, :, None] + qi[:, :, :, None, :] * c_re[:, :, None]
    m_out_t = jnp.concatenate([er[0], -ei[0], er[1, :, ::-1], -ei[1, :, ::-1]],
                              axis=-1).reshape(S5_GROUPS, S5_CW, 4 * S5_STATE)

    ar16, ai16 = qr[:, :, t - 1], qi[:, :, t - 1]
    cols_r, cols_i = [], []
    for _ in range(n_steps):
        cols_r.append(ar16)
        cols_i.append(ai16)
        ar16, ai16 = ar16 * ar16 - ai16 * ai16, 2.0 * ar16 * ai16
    pad = [jnp.zeros_like(ar16)] * (8 - n_steps)
    coef = jnp.stack(cols_r + pad + cols_i + pad, axis=-1)
    coef = coef.transpose(1, 0, 2, 3).reshape(S5_GROUPS, 2 * S5_STATE, 16)
    return m_intra_t.astype(BF16), m_in_t.astype(BF16), m_out_t.astype(BF16), coef


def _mlp(x1, g_ref, w1_ref, w2_ref):
    h = _rmsnorm_rows(x1, g_ref[...]).astype(BF16)
    acc = x1
    for j in range(D_FF // FF_TILE):
        hid = _dot(h, w1_ref[:, j * FF_TILE:(j + 1) * FF_TILE])
        hid = jnp.square(jnp.maximum(hid, 0.0)).astype(BF16)
        acc = acc + _dot(hid, w2_ref[j * FF_TILE:(j + 1) * FF_TILE, :])
    return acc


def _post_even_kernel(x_ref, a_ref, b_ref, wout_ref, g_ref, w1_ref, w2_ref, o_ref):
    mix = (_dot(a_ref[...], wout_ref[0:ATTN_WIDTH, :])
           + _dot(b_ref[...].astype(BF16), wout_ref[ATTN_WIDTH:, :]))
    o_ref[...] = _mlp(x_ref[...] + mix, g_ref, w1_ref, w2_ref)


def _post_even(x2, a_out, b_out, w_out, g_mlp, w1, w2):
    n_tok = x2.shape[0]
    tm = TOKEN_TILE
    return pl.pallas_call(
        _post_even_kernel,
        grid=(n_tok // tm,),
        in_specs=[pl.BlockSpec((tm, D_MODEL), lambda i: (i, 0)),
                  pl.BlockSpec((tm, ATTN_WIDTH), lambda i: (i, 0)),
                  pl.BlockSpec((tm, S5_WIDTH), lambda i: (i, 0)),
                  _const_spec(w_out.shape), _const_spec(g_mlp.shape), _const_spec(w1.shape),
                  _const_spec(w2.shape)],
        out_specs=pl.BlockSpec((tm, D_MODEL), lambda i: (i, 0)),
        out_shape=jax.ShapeDtypeStruct((n_tok, D_MODEL), F32),
        compiler_params=_params(1),
        name="post_even",
    )(x2, a_out, b_out, w_out, g_mlp, w1, w2)


def _post_odd_kernel(x_ref, m_ref, wout_ref, g_ref, w1_ref, w2_ref, gf_ref, o_ref):
    x1 = x_ref[...] + _dot(m_ref[...], wout_ref[...])
    o_ref[...] = _rmsnorm_rows(_mlp(x1, g_ref, w1_ref, w2_ref), gf_ref[...])


def _post_odd(x2, mixed, w_out, g_mlp, w1, w2, g_final):
    n_tok = x2.shape[0]
    tm = TOKEN_TILE
    return pl.pallas_call(
        _post_odd_kernel,
        grid=(n_tok // tm,),
        in_specs=[pl.BlockSpec((tm, D_MODEL), lambda i: (i, 0)),
                  pl.BlockSpec((tm, D_MODEL), lambda i: (i, 0)),
                  _const_spec(w_out.shape), _const_spec(g_mlp.shape),
                  _const_spec(w1.shape), _const_spec(w2.shape), _const_spec(g_final.shape)],
        out_specs=pl.BlockSpec((tm, D_MODEL), lambda i: (i, 0)),
        out_shape=jax.ShapeDtypeStruct((n_tok, D_MODEL), F32),
        compiler_params=_params(1),
        name="post_odd",
    )(x2, mixed, w_out, g_mlp, w1, w2, g_final)


def _pre_odd_kernel(x_ref, g_ref, w_ref, q_ref, i_ref, ff_ref, fb_ref, gate_ref):
    h = _rmsnorm_rows(x_ref[...], g_ref[...]).astype(BF16)
    for s, out_ref in enumerate((q_ref, i_ref, ff_ref, fb_ref, gate_ref)):
        out_ref[...] = _dot(h, w_ref[:, s * D_MODEL:(s + 1) * D_MODEL]).astype(out_ref.dtype)


def _pre_odd(x2, g, w):
    n_tok = x2.shape[0]
    tm = TOKEN_TILE
    row_spec = pl.BlockSpec((tm, D_MODEL), lambda i: (i, 0))
    lo = jax.ShapeDtypeStruct((n_tok, D_MODEL), BF16)
    hi = jax.ShapeDtypeStruct((n_tok, D_MODEL), F32)
    return pl.pallas_call(
        _pre_odd_kernel,
        grid=(n_tok // tm,),
        in_specs=[row_spec, _const_spec((1, D_MODEL)), _const_spec(w.shape)],
        out_specs=[row_spec] * 5,
        out_shape=[lo, lo, hi, hi, lo],
        compiler_params=_params(1),
        name="pre_odd",
    )(x2, g, w)


def _chunk_cumprod(x, reverse):
    n = x.shape[0]
    pos = lax.broadcasted_iota(jnp.int32, x.shape, 0) % HGRN_CHUNK
    s = 1
    while s < HGRN_CHUNK:
        if reverse:
            shifted = pltpu.roll(x, n - s, 0)
            x = x * jnp.where(pos < HGRN_CHUNK - s, shifted, 1.0)
        else:
            shifted = pltpu.roll(x, s, 0)
            x = x * jnp.where(pos >= s, shifted, 1.0)
        s *= 2
    return x


def _hgrn_kernel(layer, q_ref, v_ref, ff_ref, fb_ref, gate_ref, lbl_ref, ng_ref, o_ref,
                 qd_ref, dec_ref, kv_ref, st_ref, acc_ref):
    seq = q_ref.shape[1]
    ch = HGRN_CHUNK
    blk = HGRN_BLOCK
    n_chunks = seq // ch
    per_blk = blk // ch
    logits = lbl_ref[...]
    soft = jnp.exp(logits - jnp.max(logits, axis=0, keepdims=True))
    soft = soft / jnp.sum(soft, axis=0, keepdims=True)
    lb = jnp.sum(soft[0:layer + 1], axis=0, keepdims=True) - soft[0:1]

    c2 = 0.5 * (1.0 - lb)
    c1 = lb + c2
    row = lax.broadcasted_iota(jnp.int32, (blk, blk), 0)
    col = lax.broadcasted_iota(jnp.int32, (blk, blk), 1)
    same_chunk = (row // ch) == (col // ch)
    row_chunk = lax.broadcasted_iota(jnp.int32, (blk, 128), 0) // ch

    def intra(j, carry):
        sl = pl.ds(pl.multiple_of(j * blk, blk), blk)
        v = v_ref[0, sl, :]
        vt = v.astype(F32).T.astype(BF16)
        qf = q_ref[0, sl, :].astype(F32)
        for d, f_ref in enumerate((ff_ref, fb_ref)):
            ct = c2 * jnp.tanh(0.5 * f_ref[0, sl, :])
            eb = _chunk_cumprod(c1 + ct, reverse=(d == 1))
            k_inv = (c2 - ct) / eb
            eb3 = eb.reshape(per_blk, ch, 128)
            e_end = eb3[:, ch - 1:ch, :] if d == 0 else eb3[:, 0:1, :]
            dec_ref[d, pl.ds(j * per_blk, per_blk)] = e_end
            q_dec = (qf * eb).astype(BF16)
            qd_ref[d, sl, :] = q_dec
            k_dec = (k_inv.reshape(per_blk, ch, 128) * e_end).reshape(blk, 128).astype(BF16)

            causal = (row >= col) if d == 0 else (row <= col)
            s = _dot_nt(q_dec, k_inv.astype(BF16))
            p = jnp.where(same_chunk, jnp.where(causal, s, 0.0), 0.0).astype(BF16)
            acc_ref[d, sl, :] = _dot(p, v)
            rhs = jnp.concatenate([jnp.where(row_chunk == cc, k_dec, jnp.zeros_like(k_dec))
                                   for cc in range(per_blk)], axis=1)
            kvs = _dot(vt, rhs)
            for cc in range(per_blk):
                kv_ref[d, j * per_blk + cc] = kvs[:, 128 * cc:128 * (cc + 1)]
        return carry

    lax.fori_loop(0, seq // blk, intra, 0, unroll=4)

    def scan(n, states):
        new_states = []
        for d in range(2):
            c = n if d == 0 else n_chunks - 1 - n
            st_ref[d, c] = states[d].astype(BF16)
            new_states.append(dec_ref[d, c] * states[d] + kv_ref[d, c])
        return tuple(new_states)

    zero_state = jnp.zeros((128, 128), F32)
    lax.fori_loop(0, n_chunks, scan, (zero_state, zero_state))

    for c in range(n_chunks):
        rows = slice(c * ch, (c + 1) * ch)
        for d in range(2):
            acc_ref[d, rows, :] += _dot_nt(qd_ref[d, rows, :], st_ref[d, c])

    o = _rmsnorm_rows(acc_ref[0] + acc_ref[1], ng_ref[...])
    gate = 0.5 + 0.5 * jnp.tanh(0.5 * gate_ref[0].astype(F32))
    o_ref[0] = (o * gate).astype(o_ref.dtype)


def _hgrn(q, v, ff, fb, gate, lb_logits, norm_g, layer):
    bsz, seq, _ = q.shape
    n_chunks = seq // HGRN_CHUNK
    head_spec = pl.BlockSpec((1, seq, 128), lambda b, h: (b, 0, h))
    return pl.pallas_call(
        functools.partial(_hgrn_kernel, layer),
        grid=(bsz, HGRN_HEADS),
        in_specs=[head_spec] * 5 + [pl.BlockSpec((DEPTH, 128), lambda b, h: (0, h)),
                                    pl.BlockSpec((1, 128), lambda b, h: (0, h))],
        out_specs=head_spec,
        out_shape=jax.ShapeDtypeStruct((bsz, seq, D_MODEL), BF16),
        scratch_shapes=[
           pltpu.VMEM((2, seq, 128), BF16),
           pltpu.VMEM((2, n_chunks, 1, 128), F32),
           pltpu.VMEM((2, n_chunks, 128, 128), F32),
           pltpu.VMEM((2, n_chunks, 128, 128), BF16),
           pltpu.VMEM((2, seq, 128), F32)],
        compiler_params=_params(2),
        name="hgrn2",
    )(q, v, ff, fb, gate, lb_logits, norm_g)


def _rope_pair_tables(seq):
    inv = ROPE_THETA ** (-jnp.arange(0, ATTN_QK_DIM, 2, dtype=F32) / ATTN_QK_DIM)
    ang = jnp.arange(seq, dtype=F32)[:, None] * inv[None, :]
    return jnp.tile(jnp.cos(ang), (1, 4)), jnp.tile(jnp.sin(ang), (1, 4))


def _pair_layout_columns():
    idx = jnp.arange(ATTN_QK_WIDTH).reshape(2, 2, 2, 2, 32)
    return idx.transpose(0, 3, 1, 2, 4).reshape(-1)


def kernel(x, norm_mix_g, norm_mlp_g, final_norm_g, w_ff_in, w_ff_out, w_in_even, w_out_even, diff_lambda, diff_subln_g, s5_lam_re, s5_lam_im, s5_log_step, s5_b_re, s5_b_im, s5_c_re, s5_c_im, s5_d, s5_w_glu, s5_b_glu, w_in_odd, w_out_odd, hgrn_norm_g, hgrn_lb_logits):
    bsz, seq, _ = x.shape
    n_tok = bsz * seq
    n_chunks = seq // S5_CHUNK
    assert n_chunks & (n_chunks - 1) == 0 and n_chunks % 128 == 0
    x2 = x.reshape(n_tok, D_MODEL)
    cos, sin = _rope_pair_tables(seq)
    perm = _pair_layout_columns()

    for layer in range(DEPTH):
        g_mix = norm_mix_g[layer].reshape(1, D_MODEL)
        g_mlp = norm_mlp_g[layer].reshape(1, D_MODEL)
        w1 = w_ff_in[layer].astype(BF16)
        w2 = w_ff_out[layer].astype(BF16)
        if layer % 2 == 0:
            e = layer // 2
            w = w_in_even[e]
            w_qkv = jnp.concatenate([w[:, :ATTN_QK_WIDTH][:, perm],
                                     w[:, ATTN_QK_WIDTH:2 * ATTN_QK_WIDTH][:, perm],
                                     w[:, 2 * ATTN_QK_WIDTH:2 * ATTN_QK_WIDTH + ATTN_WIDTH]], axis=1).astype(BF16)
            wu = w[:, 2 * ATTN_QK_WIDTH + ATTN_WIDTH:].astype(BF16)
            q, k, v = _pre_even(x2, g_mix, w_qkv, cos, sin, seq)
            u_t = _s5_in(x2, g_mix, wu, bsz, seq)
            lambda_init = 0.8 - 0.6 * math.exp(-0.3 * layer)
            a_out = _attention(q.reshape(bsz, seq, -1), k.reshape(bsz, seq, -1), v.reshape(bsz, seq, -1),
                               diff_lambda[e], diff_subln_g[e].reshape(1, ATTN_V_DIM), lambda_init)
            ops = _s5_chunk_operators(s5_lam_re[e], s5_lam_im[e], s5_log_step[e], s5_b_re[e], s5_b_im[e],
                                      s5_c_re[e], s5_c_im[e], s5_d[e], n_chunks.bit_length() - 1)
            y_t = _s5_chunked(u_t.reshape(S5_GROUPS, S5_CW, bsz * n_chunks), *ops, n_chunks)
            b_out = _s5_out(y_t.reshape(S5_GROUPS, S5_CHUNK, S5_GROUP, bsz * n_chunks),
                            s5_w_glu[e].T.astype(BF16), s5_b_glu[e].reshape(S5_WIDTH, 1), bsz, seq)
            x2 = _post_even(x2, a_out.reshape(n_tok, ATTN_WIDTH), b_out, w_out_even[e].astype(BF16),
                            g_mlp, w1, w2)
        else:
            o_i = layer // 2
            q, v, ff, fb, gate = _pre_odd(x2, g_mix, w_in_odd[o_i].astype(BF16))
            shp = (bsz, seq, D_MODEL)
            mixed = _hgrn(q.reshape(shp), v.reshape(shp), ff.reshape(shp), fb.reshape(shp), gate.reshape(shp),
                          hgrn_lb_logits, hgrn_norm_g[o_i].reshape(1, D_MODEL), layer)
            assert layer == DEPTH - 1, "the final norm is fused into the last (odd) layer's kernel"
            x2 = _post_odd(x2, mixed.reshape(n_tok, D_MODEL), w_out_odd[o_i].astype(BF16), g_mlp, w1, w2,
                           final_norm_g.reshape(1, D_MODEL))
    return x2.reshape(bsz, seq, D_MODEL)
```

```python
import functools
import math

import jax
import jax.numpy as jnp
from jax import lax
from jax.experimental import pallas as pl
from jax.experimental.pallas import tpu as pltpu

D_MODEL = 1024
DEPTH = 2
ATTN_HEADS = 4
ATTN_QK_DIM = 64
ATTN_V_DIM = 128
ATTN_QK_WIDTH = 512
ATTN_WIDTH = 512
ROPE_THETA = 10000.0
S5_WIDTH = 512
S5_GROUP = 16
S5_GROUPS = 32
S5_STATE = 64
HGRN_HEADS = 8
HGRN_CHUNK = 64
HGRN_BLOCK = 256
D_FF = 4096
EPS = 1e-6

S5_CHUNK = 16
S5_CW = S5_CHUNK * S5_GROUP
S5_GROUP_BLOCK = 4

TOKEN_TILE = 512
ATTN_Q_TILE = 256
FF_TILE = 1024
VMEM_LIMIT = 56 * 1024 * 1024

BF16 = jnp.bfloat16
F32 = jnp.float32


def _const_spec(shape):
    nd = len(shape)
    return pl.BlockSpec(shape, lambda *_: (0,) * nd, pipeline_mode=pl.Buffered(1))


def _params(n_axes):
    return pltpu.CompilerParams(dimension_semantics=("arbitrary",) * n_axes,
                                vmem_limit_bytes=VMEM_LIMIT)


def _rmsnorm_rows(x, g):
    ms = jnp.mean(x * x, axis=-1, keepdims=True)
    return x * lax.rsqrt(ms + EPS) * g


def _sigmoid(x):
    return 1.0 / (1.0 + jnp.exp(-x))


def _gelu_tanh(x):
    c = math.sqrt(2.0 / math.pi)
    return 0.5 * x * (1.0 + jnp.tanh(c * (x + 0.044715 * (x * x * x))))


def _dot(a, b):
    return jnp.dot(a, b, preferred_element_type=F32)


def _dot_nt(a, b):
    return lax.dot_general(a, b, (((1,), (1,)), ((), ())), preferred_element_type=F32)


def _dot_tn(a, b):
    return lax.dot_general(a, b, (((0,), (0,)), ((), ())), preferred_element_type=F32)


def _pre_even_kernel(x_ref, g_ref, w_ref, cos_ref, sin_ref, q_ref, k_ref, v_ref):
    h = _rmsnorm_rows(x_ref[...], g_ref[...]).astype(BF16)
    cos = cos_ref[...]
    sin = sin_ref[...]
    for out_ref, base, scale in ((q_ref, 0, ATTN_QK_DIM ** -0.5 * math.log2(math.e)),
                                 (k_ref, ATTN_QK_WIDTH, 1.0)):
        p = _dot(h, w_ref[:, base:base + ATTN_QK_WIDTH])
        for pair in range(2):
            lo = p[:, 256 * pair:256 * pair + 128]
            hi = p[:, 256 * pair + 128:256 * pair + 256]
            out_ref[:, 256 * pair:256 * pair + 128] = ((lo * cos - hi * sin) * scale).astype(BF16)
            out_ref[:, 256 * pair + 128:256 * pair + 256] = ((hi * cos + lo * sin) * scale).astype(BF16)
    v_ref[...] = _dot(h, w_ref[:, 2 * ATTN_QK_WIDTH:]).astype(BF16)


def _pre_even(x2, g, w_qkv, cos, sin, seq):
    n_tok = x2.shape[0]
    tm = TOKEN_TILE
    n_pos_blocks = seq // tm
    out = jax.ShapeDtypeStruct((n_tok, 512), BF16)
    row_spec = pl.BlockSpec((tm, 512), lambda i: (i, 0))
    rope_spec = pl.BlockSpec((tm, 128), lambda i: (i % n_pos_blocks, 0))
    return pl.pallas_call(
        _pre_even_kernel,
        grid=(n_tok // tm,),
        in_specs=[pl.BlockSpec((tm, D_MODEL), lambda i: (i, 0)), _const_spec((1, D_MODEL)),
                  _const_spec(w_qkv.shape), rope_spec, rope_spec],
        out_specs=[row_spec] * 3,
        out_shape=[out] * 3,
        compiler_params=_params(1),
        name="pre_even",
    )(x2, g, w_qkv, cos, sin)


def _s5_in_kernel(x_ref, g_ref, wu_ref, ut_ref, u_scr):
    nc = x_ref.shape[0] // S5_CHUNK
    groups_per_slab = 128 // S5_GROUP
    h = _rmsnorm_rows(x_ref[...], g_ref[...]).astype(BF16)
    u = _dot(h, wu_ref[...])
    for j in range(S5_WIDTH // 128):
        u_scr[j] = u[:, 128 * j:128 * (j + 1)]
    for j in range(S5_WIDTH // 128):
        for ph in range(S5_CHUNK):
            t = u_scr[j, pl.ds(ph, nc, stride=S5_CHUNK), :]
            ut_ref[groups_per_slab * j:groups_per_slab * (j + 1), ph, :, :] = (
                t.T.astype(BF16).reshape(groups_per_slab, S5_GROUP, nc))


def _s5_in(x2, g, wu, bsz, seq):
    nc = seq // S5_CHUNK
    return pl.pallas_call(
        _s5_in_kernel,
        grid=(bsz,),
        in_specs=[pl.BlockSpec((seq, D_MODEL), lambda b: (b, 0)), _const_spec((1, D_MODEL)),
                  _const_spec(wu.shape)],
        out_specs=pl.BlockSpec((S5_GROUPS, S5_CHUNK, S5_GROUP, nc), lambda b: (0, 0, 0, b)),
        out_shape=jax.ShapeDtypeStruct((S5_GROUPS, S5_CHUNK, S5_GROUP, bsz * nc), BF16),
        scratch_shapes=[pltpu.VMEM((S5_WIDTH // 128, seq, 128), F32)],
        compiler_params=_params(1),
        name="s5_in",
    )(x2, g, wu)


def _s5_out_kernel(yt_ref, wglut_ref, bglu_ref, o_ref, b_scr):
    nc = yt_ref.shape[3]
    yt = jnp.concatenate([yt_ref[:, ph, :, :].reshape(S5_WIDTH, nc) for ph in range(S5_CHUNK)], axis=1)
    yt = _gelu_tanh(yt)
    bt = yt * _sigmoid(_dot(wglut_ref[...], yt.astype(BF16)) + bglu_ref[...])
    for j in range(S5_WIDTH // 128):
        for ph in range(S5_CHUNK):
            b_scr[j, pl.ds(ph, nc, stride=S5_CHUNK), :] = bt[128 * j:128 * (j + 1), ph * nc:(ph + 1) * nc].T
    for j in range(S5_WIDTH // 128):
        o_ref[:, 128 * j:128 * (j + 1)] = b_scr[j]


def _s5_out(y_t, w_glu_t, b_glu_col, bsz, seq):
    nc = seq // S5_CHUNK
    return pl.pallas_call(
        _s5_out_kernel,
        grid=(bsz,),
        in_specs=[pl.BlockSpec((S5_GROUPS, S5_CHUNK, S5_GROUP, nc), lambda b: (0, 0, 0, b)),
                  _const_spec(w_glu_t.shape), _const_spec(b_glu_col.shape)],
        out_specs=pl.BlockSpec((seq, S5_WIDTH), lambda b: (b, 0)),
        out_shape=jax.ShapeDtypeStruct((bsz * seq, S5_WIDTH), F32),
        scratch_shapes=[pltpu.VMEM((S5_WIDTH // 128, seq, 128), F32)],
        compiler_params=_params(1),
        name="s5_out",
    )(y_t, w_glu_t, b_glu_col)


def _attn_kernel(lambda_init, q_ref, k_ref, v_ref, lam_ref, g_ref, o_ref):
    q = q_ref[0]
    k = k_ref[0]
    lam = lam_ref[...]
    lam_val = (jnp.exp(jnp.sum(lam[0:1] * lam[1:2], axis=-1, keepdims=True))
               - jnp.exp(jnp.sum(lam[2:3] * lam[3:4], axis=-1, keepdims=True)) + lambda_init)
    tq = q.shape[0]
    lane_group = (lax.broadcasted_iota(jnp.int32, q.shape, 1) // 32) % 4
    zero = jnp.zeros_like(q)
    q_all = jnp.concatenate([jnp.where(lane_group == hc, q, zero) for hc in range(4)], axis=0)
    s = _dot_nt(q_all, k)
    e = jnp.exp2(s - jnp.max(s, axis=-1, keepdims=True)).astype(BF16)
    ones = jnp.ones((k.shape[0], ATTN_V_DIM), BF16)
    for hh in range(2):
        v_ext = jnp.concatenate([v_ref[0, :, 128 * hh:128 * hh + 128], ones], axis=1)
        r = _dot(e[2 * hh * tq:2 * (hh + 1) * tq], v_ext)
        o = (r[0:tq, 0:ATTN_V_DIM] / r[0:tq, ATTN_V_DIM:]
             - lam_val * (r[tq:2 * tq, 0:ATTN_V_DIM] / r[tq:2 * tq, ATTN_V_DIM:]))
        o = _rmsnorm_rows(o, g_ref[...]) * (1.0 - lambda_init)
        o_ref[0, :, 128 * hh:128 * hh + 128] = o.astype(o_ref.dtype)


def _attention(q, k, v, lam, subln_g, lambda_init):
    bsz, seq, _ = q.shape
    tq = ATTN_Q_TILE
    kv_spec = pl.BlockSpec((1, seq, 256), lambda b, p, i: (b, 0, p))
    return pl.pallas_call(
        functools.partial(_attn_kernel, lambda_init),
        grid=(bsz, 2, seq // tq),
        in_specs=[pl.BlockSpec((1, tq, 256), lambda b, p, i: (b, i, p)),
                  kv_spec, kv_spec,
                  _const_spec(lam.shape),
                  _const_spec(subln_g.shape)],
        out_specs=pl.BlockSpec((1, tq, 256), lambda b, p, i: (b, i, p)),
        out_shape=jax.ShapeDtypeStruct((bsz, seq, ATTN_WIDTH), BF16),
        compiler_params=_params(3),
        name="diff_attention",
    )(q, k, v, lam, subln_g)


def _s5_kernel(n_chunks, ut_ref, mi_ref, min_ref, mo_ref, coef_ref, y_ref, z_ref):
    n_lanes = ut_ref.shape[2]
    pos = lax.broadcasted_iota(jnp.int32, (S5_STATE, n_lanes), 1) % n_chunks
    n_steps = n_chunks.bit_length() - 1

    def one_group(g, carry):
        ut = ut_ref[g]
        st = _dot(min_ref[g], ut)
        coef = coef_ref[g]
        for d in range(2):
            xr = st[128 * d:128 * d + S5_STATE]
            xi = st[128 * d + S5_STATE:128 * d + 2 * S5_STATE]
            for k in range(n_steps):
                s = 1 << k
                if d == 0:
                    shift, keep = s, pos >= s
                else:
                    shift, keep = n_lanes - s, pos < n_chunks - s
                pr = jnp.where(keep, coef[S5_STATE * d:S5_STATE * (d + 1), k:k + 1], 0.0)
                pi = jnp.where(keep, coef[S5_STATE * d:S5_STATE * (d + 1), 8 + k:9 + k], 0.0)
                sr = pltpu.roll(xr, shift, 1)
                si = pltpu.roll(xi, shift, 1)
                xr, xi = xr + pr * sr - pi * si, xi + pr * si + pi * sr
            if d == 0:
                shift, keep = 1, pos >= 1
            else:
                shift, keep = n_lanes - 1, pos < n_chunks - 1
            z_ref[128 * d:128 * d + S5_STATE] = jnp.where(keep, pltpu.roll(xr, shift, 1), 0.0).astype(BF16)
            z_ref[128 * d + S5_STATE:128 * (d + 1)] = jnp.where(keep, pltpu.roll(xi, shift, 1), 0.0).astype(BF16)
        y_ref[g] = _dot(mi_ref[g], ut) + _dot(mo_ref[g], z_ref[...])
        return carry

    lax.fori_loop(0, ut_ref.shape[0], one_group, 0)


def _s5_chunked(u_t, m_intra_t, m_in_t, m_out_t, coef, n_chunks):
    n_groups, _, n_lanes = u_t.shape
    gb = S5_GROUP_BLOCK
    mat_spec = pl.BlockSpec((gb, S5_CW, S5_CW), lambda i: (i, 0, 0))
    return pl.pallas_call(
        functools.partial(_s5_kernel, n_chunks),
        grid=(n_groups // gb,),
        in_specs=[pl.BlockSpec((gb, S5_CW, n_lanes), lambda i: (i, 0, 0)), mat_spec, mat_spec, mat_spec,
                  pl.BlockSpec((gb, 2 * S5_STATE, 16), lambda i: (i, 0, 0))],
        out_specs=pl.BlockSpec((gb, S5_CW, n_lanes), lambda i: (i, 0, 0)),
        out_shape=jax.ShapeDtypeStruct((n_groups, S5_CW, n_lanes), F32),
        scratch_shapes=[pltpu.VMEM((4 * S5_STATE, n_lanes), BF16)],
        compiler_params=_params(1),
        name="s5_chunked",
    )(u_t, m_intra_t, m_in_t, m_out_t, coef)


def _s5_chunk_operators(lam_re, lam_im, log_step, b_re, b_im, c_re, c_im, d_skip, n_steps):
    hp = lax.Precision.HIGHEST
    t = S5_CHUNK
    lr = jnp.minimum(lam_re, -1e-4)
    li = lam_im
    dt = jnp.exp(log_step)[..., None]
    mag = jnp.exp(lr * dt)
    ar = mag * jnp.cos(li * dt)
    ai = mag * jnp.sin(li * dt)
    den = lr * lr + li * li
    cr = ((ar - 1.0) * lr + ai * li) / den
    ci = (ai * lr - (ar - 1.0) * li) / den
    bbr = cr[..., None] * b_re - ci[..., None] * b_im
    bbi = cr[..., None] * b_im + ci[..., None] * b_re

    def powers(taus):
        taus = taus.astype(F32)
        pmag = jnp.exp(taus * (lr * dt)[..., None])
        ang = taus * (li * dt)[..., None]
        return pmag * jnp.cos(ang), pmag * jnp.sin(ang)

    pr, pi = powers(jnp.arange(t))
    rr = pr[..., None] * bbr[:, :, :, None, :] - pi[..., None] * bbi[:, :, :, None, :]
    ri = pr[..., None] * bbi[:, :, :, None, :] + pi[..., None] * bbr[:, :, :, None, :]
    rf_r = rr[0, :, :, ::-1].reshape(S5_GROUPS, S5_STATE, S5_CW)
    rf_i = ri[0, :, :, ::-1].reshape(S5_GROUPS, S5_STATE, S5_CW)
    rb_r = rr[1].reshape(S5_GROUPS, S5_STATE, S5_CW)
    rb_i = ri[1].reshape(S5_GROUPS, S5_STATE, S5_CW)
    m_in_t = jnp.concatenate([rf_r, rf_i, rb_r, rb_i], axis=1)

    taps_f = (jnp.einsum('gnp,gpx->gnx', c_re[0], rf_r, precision=hp)
              - jnp.einsum('gnp,gpx->gnx', c_im[0], rf_i, precision=hp))
    taps_b = (jnp.einsum('gnp,gpx->gnx', c_re[1], rb_r, precision=hp)
              - jnp.einsum('gnp,gpx->gnx', c_im[1], rb_i, precision=hp))
    skip = d_skip.reshape(S5_GROUPS, S5_GROUP)[:, :, None] * jnp.eye(S5_GROUP, dtype=F32)[None]
    edge = (t - 1) * S5_GROUP
    taps = (jnp.pad(taps_f, ((0, 0), (0, 0), (0, edge))) + jnp.pad(taps_b, ((0, 0), (0, 0), (edge, 0)))
            + jnp.pad(skip, ((0, 0), (0, 0), (edge, edge))))
    m_intra_t = jnp.stack([taps[:, :, (t - 1 - j) * S5_GROUP:(t - 1 - j) * S5_GROUP + S5_CW]
                           for j in range(t)], axis=1).reshape(S5_GROUPS, S5_CW, S5_CW)

    qr, qi = powers(jnp.arange(1, t + 1))
    qr = qr.transpose(0, 1, 3, 2)
    qi = qi.transpose(0, 1, 3, 2)
    er = qr[:, :, :, None, :] * c_re[:, :, None] - qi[:, :, :, None, :] * c_im[:, :, None]
    ei = qr[:, :, :, None, :] * c_im[:, :, None] + qi[:, :, :, None, :] * c_re[:, :, None]
    m_out_t = jnp.concatenate([er[0], -ei[0], er[1, :, ::-1], -ei[1, :, ::-1]],
                              axis=-1).reshape(S5_GROUPS, S5_CW, 4 * S5_STATE)

    ar16, ai16 = qr[:, :, t - 1], qi[:, :, t - 1]
    cols_r, cols_i = [], []
    for _ in range(n_steps):
        cols_r.append(ar16)
        cols_i.append(ai16)
        ar16, ai16 = ar16 * ar16 - ai16 * ai16, 2.0 * ar16 * ai16
    pad = [jnp.zeros_like(ar16)] * (8 - n_steps)
    coef = jnp.stack(cols_r + pad + cols_i + pad, axis=-1)
    coef = coef.transpose(1, 0, 2, 3).reshape(S5_GROUPS, 2 * S5_STATE, 16)
    return m_intra_t.astype(BF16), m_in_t.astype(BF16), m_out_t.astype(BF16), coef


def _mlp(x1, g_ref, w1_ref, w2_ref):
    h = _rmsnorm_rows(x1, g_ref[...]).astype(BF16)
    acc = x1
    for j in range(D_FF // FF_TILE):
        hid = _dot(h, w1_ref[:, j * FF_TILE:(j + 1) * FF_TILE])
        hid = jnp.square(jnp.maximum(hid, 0.0)).astype(BF16)
        acc = acc + _dot(hid, w2_ref[j * FF_TILE:(j + 1) * FF_TILE, :])
    return acc


def _post_even_kernel(x_ref, a_ref, b_ref, wout_ref, g_ref, w1_ref, w2_ref, o_ref):
    mix = (_dot(a_ref[...], wout_ref[0:ATTN_WIDTH, :])
           + _dot(b_ref[...].astype(BF16), wout_ref[ATTN_WIDTH:, :]))
    o_ref[...] = _mlp(x_ref[...] + mix, g_ref, w1_ref, w2_ref)


def _post_even(x2, a_out, b_out, w_out, g_mlp, w1, w2):
    n_tok = x2.shape[0]
    tm = TOKEN_TILE
    return pl.pallas_call(
        _post_even_kernel,
        grid=(n_tok // tm,),
        in_specs=[pl.BlockSpec((tm, D_MODEL), lambda i: (i, 0)),
                  pl.BlockSpec((tm, ATTN_WIDTH), lambda i: (i, 0)),
                  pl.BlockSpec((tm, S5_WIDTH), lambda i: (i, 0)),
                  _const_spec(w_out.shape), _const_spec(g_mlp.shape), _const_spec(w1.shape),
                  _const_spec(w2.shape)],
        out_specs=pl.BlockSpec((tm, D_MODEL), lambda i: (i, 0)),
        out_shape=jax.ShapeDtypeStruct((n_tok, D_MODEL), F32),
        compiler_params=_params(1),
        name="post_even",
    )(x2, a_out, b_out, w_out, g_mlp, w1, w2)


def _post_odd_kernel(x_ref, m_ref, wout_ref, g_ref, w1_ref, w2_ref, gf_ref, o_ref):
    x1 = x_ref[...] + _dot(m_ref[...], wout_ref[...])
    o_ref[...] = _rmsnorm_rows(_mlp(x1, g_ref, w1_ref, w2_ref), gf_ref[...])


def _post_odd(x2, mixed, w_out, g_mlp, w1, w2, g_final):
    n_tok = x2.shape[0]
    tm = TOKEN_TILE
    return pl.pallas_call(
        _post_odd_kernel,
        grid=(n_tok // tm,),
        in_specs=[pl.BlockSpec((tm, D_MODEL), lambda i: (i, 0)),
                  pl.BlockSpec((tm, D_MODEL), lambda i: (i, 0)),
                  _const_spec(w_out.shape), _const_spec(g_mlp.shape),
                  _const_spec(w1.shape), _const_spec(w2.shape), _const_spec(g_final.shape)],
        out_specs=pl.BlockSpec((tm, D_MODEL), lambda i: (i, 0)),
        out_shape=jax.ShapeDtypeStruct((n_tok, D_MODEL), F32),
        compiler_params=_params(1),
        name="post_odd",
    )(x2, mixed, w_out, g_mlp, w1, w2, g_final)


def _pre_odd_kernel(x_ref, g_ref, w_ref, q_ref, i_ref, ff_ref, fb_ref, gate_ref):
    h = _rmsnorm_rows(x_ref[...], g_ref[...]).astype(BF16)
    for s, out_ref in enumerate((q_ref, i_ref, ff_ref, fb_ref, gate_ref)):
        out_ref[...] = _dot(h, w_ref[:, s * D_MODEL:(s + 1) * D_MODEL]).astype(out_ref.dtype)


def _pre_odd(x2, g, w):
    n_tok = x2.shape[0]
    tm = TOKEN_TILE
    row_spec = pl.BlockSpec((tm, D_MODEL), lambda i: (i, 0))
    lo = jax.ShapeDtypeStruct((n_tok, D_MODEL), BF16)
    hi = jax.ShapeDtypeStruct((n_tok, D_MODEL), F32)
    return pl.pallas_call(
        _pre_odd_kernel,
        grid=(n_tok // tm,),
        in_specs=[row_spec, _const_spec((1, D_MODEL)), _const_spec(w.shape)],
        out_specs=[row_spec] * 5,
        out_shape=[lo, lo, hi, hi, lo],
        compiler_params=_params(1),
        name="pre_odd",
    )(x2, g, w)


def _chunk_cumprod(x, reverse):
    n = x.shape[0]
    pos = lax.broadcasted_iota(jnp.int32, x.shape, 0) % HGRN_CHUNK
    s = 1
    while s < HGRN_CHUNK:
        if reverse:
            shifted = pltpu.roll(x, n - s, 0)
            x = x * jnp.where(pos < HGRN_CHUNK - s, shifted, 1.0)
        else:
            shifted = pltpu.roll(x, s, 0)
            x = x * jnp.where(pos >= s, shifted, 1.0)
        s *= 2
    return x


def _hgrn_kernel(layer, q_ref, v_ref, ff_ref, fb_ref, gate_ref, lbl_ref, ng_ref, o_ref,
                 qd_ref, dec_ref, kv_ref, acc_ref):
    seq = q_ref.shape[1]
    ch = HGRN_CHUNK
    blk = HGRN_BLOCK
    n_chunks = seq // ch
    per_blk = blk // ch
    logits = lbl_ref[...]
    soft = jnp.exp(logits - jnp.max(logits, axis=0, keepdims=True))
    soft = soft / jnp.sum(soft, axis=0, keepdims=True)
    lb = jnp.sum(soft[0:layer + 1], axis=0, keepdims=True) - soft[0:1]

    c2 = 0.5 * (1.0 - lb)
    c1 = lb + c2
    row = lax.broadcasted_iota(jnp.int32, (blk, blk), 0)
    col = lax.broadcasted_iota(jnp.int32, (blk, blk), 1)
    u32 = lambda a: a.astype(jnp.uint32)
    keep = (u32(row - col) <= u32(row % ch), u32(col - row) <= u32(ch - 1 - row % ch))
    row_chunk = lax.broadcasted_iota(jnp.int32, (blk, 128), 0) // ch

    def intra(j, carry):
        sl = pl.ds(pl.multiple_of(j * blk, blk), blk)
        v = v_ref[0, sl, :]
        vt = v.astype(F32).T.astype(BF16)
        qf = q_ref[0, sl, :].astype(F32)
        for d, f_ref in enumerate((ff_ref, fb_ref)):
            ct = c2 * jnp.tanh(0.5 * f_ref[0, sl, :])
            eb = _chunk_cumprod(c1 + ct, reverse=(d == 1))
            k_inv = (c2 - ct) / eb
            eb3 = eb.reshape(per_blk, ch, 128)
            e_end = eb3[:, ch - 1:ch, :] if d == 0 else eb3[:, 0:1, :]
            dec_ref[d, pl.ds(j * per_blk, per_blk)] = e_end
            q_dec = (qf * eb).astype(BF16)
            qd_ref[d, sl, :] = q_dec
            k_dec = (k_inv.reshape(per_blk, ch, 128) * e_end).reshape(blk, 128).astype(BF16)

            s = _dot_nt(q_dec, k_inv.astype(BF16))
            p = jnp.where(keep[d], s, 0.0).astype(BF16)
            acc_ref[d, sl, :] = _dot(p, v)
            rhs = jnp.concatenate([jnp.where(row_chunk == cc, k_dec, jnp.zeros_like(k_dec))
                                   for cc in range(per_blk)], axis=1)
            kvs = _dot(vt, rhs)
            for cc in range(per_blk):
                kv_ref[d, j * per_blk + cc] = kvs[:, 128 * cc:128 * (cc + 1)]
        return carry

    lax.fori_loop(0, seq // blk, intra, 0, unroll=4)

    def finish(rows):
        o = _rmsnorm_rows(acc_ref[0, rows, :] + acc_ref[1, rows, :], ng_ref[...])
        gate = 0.5 + 0.5 * jnp.tanh(0.5 * gate_ref[0, rows, :].astype(F32))
        o_ref[0, rows, :] = (o * gate).astype(o_ref.dtype)

    states = [jnp.zeros((128, 128), F32)] * 2
    for n in range(n_chunks):
        for d in range(2):
            c = n if d == 0 else n_chunks - 1 - n
            rows = slice(c * ch, (c + 1) * ch)
            acc_ref[d, rows, :] += _dot_nt(qd_ref[d, rows, :], states[d].astype(BF16))
            states[d] = dec_ref[d, c] * states[d] + kv_ref[d, c]
        if 2 * n >= n_chunks:
            finish(slice(n * ch, (n + 1) * ch))
            finish(slice((n_chunks - 1 - n) * ch, (n_chunks - n) * ch))


def _hgrn(q, v, ff, fb, gate, lb_logits, norm_g, layer):
    bsz, seq, _ = q.shape
    n_chunks = seq // HGRN_CHUNK
    head_spec = pl.BlockSpec((1, seq, 128), lambda b, h: (b, 0, h))
    return pl.pallas_call(
        functools.partial(_hgrn_kernel, layer),
        grid=(bsz, HGRN_HEADS),
        in_specs=[head_spec] * 5 + [pl.BlockSpec((DEPTH, 128), lambda b, h: (0, h)),
                                    pl.BlockSpec((1, 128), lambda b, h: (0, h))],
        out_specs=head_spec,
        out_shape=jax.ShapeDtypeStruct((bsz, seq, D_MODEL), BF16),
        scratch_shapes=[
           pltpu.VMEM((2, seq, 128), BF16),
           pltpu.VMEM((2, n_chunks, 1, 128), F32),
           pltpu.VMEM((2, n_chunks, 128, 128), F32),
           pltpu.VMEM((2, seq, 128), F32)],
        compiler_params=_params(2),
        name="hgrn2",
    )(q, v, ff, fb, gate, lb_logits, norm_g)


def _rope_pair_tables(seq):
    inv = ROPE_THETA ** (-jnp.arange(0, ATTN_QK_DIM, 2, dtype=F32) / ATTN_QK_DIM)
    ang = jnp.arange(seq, dtype=F32)[:, None] * inv[None, :]
    return jnp.tile(jnp.cos(ang), (1, 4)), jnp.tile(jnp.sin(ang), (1, 4))


def _pair_layout_columns():
    idx = jnp.arange(ATTN_QK_WIDTH).reshape(2, 2, 2, 2, 32)
    return idx.transpose(0, 3, 1, 2, 4).reshape(-1)


def kernel(x, norm_mix_g, norm_mlp_g, final_norm_g, w_ff_in, w_ff_out, w_in_even, w_out_even, diff_lambda, diff_subln_g, s5_lam_re, s5_lam_im, s5_log_step, s5_b_re, s5_b_im, s5_c_re, s5_c_im, s5_d, s5_w_glu, s5_b_glu, w_in_odd, w_out_odd, hgrn_norm_g, hgrn_lb_logits):
    bsz, seq, _ = x.shape
    n_tok = bsz * seq
    n_chunks = seq // S5_CHUNK
    assert n_chunks & (n_chunks - 1) == 0 and n_chunks % 128 == 0
    x2 = x.reshape(n_tok, D_MODEL)
    cos, sin = _rope_pair_tables(seq)
    perm = _pair_layout_columns()

    for layer in range(DEPTH):
        g_mix = norm_mix_g[layer].reshape(1, D_MODEL)
        g_mlp = norm_mlp_g[layer].reshape(1, D_MODEL)
        w1 = w_ff_in[layer].astype(BF16)
        w2 = w_ff_out[layer].astype(BF16)
        if layer % 2 == 0:
            e = layer // 2
            w = w_in_even[e]
            w_qkv = jnp.concatenate([w[:, :ATTN_QK_WIDTH][:, perm],
                                     w[:, ATTN_QK_WIDTH:2 * ATTN_QK_WIDTH][:, perm],
                                     w[:, 2 * ATTN_QK_WIDTH:2 * ATTN_QK_WIDTH + ATTN_WIDTH]], axis=1).astype(BF16)
            wu = w[:, 2 * ATTN_QK_WIDTH + ATTN_WIDTH:].astype(BF16)
            q, k, v = _pre_even(x2, g_mix, w_qkv, cos, sin, seq)
            u_t = _s5_in(x2, g_mix, wu, bsz, seq)
            lambda_init = 0.8 - 0.6 * math.exp(-0.3 * layer)
            a_out = _attention(q.reshape(bsz, seq, -1), k.reshape(bsz, seq, -1), v.reshape(bsz, seq, -1),
                               diff_lambda[e], diff_subln_g[e].reshape(1, ATTN_V_DIM), lambda_init)
            ops = _s5_chunk_operators(s5_lam_re[e], s5_lam_im[e], s5_log_step[e], s5_b_re[e], s5_b_im[e],
                                      s5_c_re[e], s5_c_im[e], s5_d[e], n_chunks.bit_length() - 1)
            y_t = _s5_chunked(u_t.reshape(S5_GROUPS, S5_CW, bsz * n_chunks), *ops, n_chunks)
            b_out = _s5_out(y_t.reshape(S5_GROUPS, S5_CHUNK, S5_GROUP, bsz * n_chunks),
                            s5_w_glu[e].T.astype(BF16), s5_b_glu[e].reshape(S5_WIDTH, 1), bsz, seq)
            x2 = _post_even(x2, a_out.reshape(n_tok, ATTN_WIDTH), b_out, w_out_even[e].astype(BF16),
                            g_mlp, w1, w2)
        else:
            o_i = layer // 2
            q, v, ff, fb, gate = _pre_odd(x2, g_mix, w_in_odd[o_i].astype(BF16))
            shp = (bsz, seq, D_MODEL)
            mixed = _hgrn(q.reshape(shp), v.reshape(shp), ff.reshape(shp), fb.reshape(shp), gate.reshape(shp),
                          hgrn_lb_logits, hgrn_norm_g[o_i].reshape(1, D_MODEL), layer)
            assert layer == DEPTH - 1, "the final norm is fused into the last (odd) layer's kernel"
            x2 = _post_odd(x2, mixed.reshape(n_tok, D_MODEL), w_out_odd[o_i].astype(BF16), g_mlp, w1, w2,
                           final_norm_g.reshape(1, D_MODEL))
    return x2.reshape(bsz, seq, D_MODEL)
```

```python
import functools
import math

import jax
import jax.numpy as jnp
from jax import lax
from jax.experimental import pallas as pl
from jax.experimental.pallas import tpu as pltpu

D_MODEL = 1024
DEPTH = 2
ATTN_HEADS = 4
ATTN_QK_DIM = 64
ATTN_V_DIM = 128
ATTN_QK_WIDTH = 512
ATTN_WIDTH = 512
ROPE_THETA = 10000.0
S5_WIDTH = 512
S5_GROUP = 16
S5_GROUPS = 32
S5_STATE = 64
HGRN_HEADS = 8
HGRN_CHUNK = 64
HGRN_BLOCK = 256
D_FF = 4096
EPS = 1e-6

S5_CHUNK = 16
S5_CW = S5_CHUNK * S5_GROUP

TOKEN_TILE = 512
ATTN_Q_TILE = 256
FF_TILE = 1024
VMEM_LIMIT = 56 * 1024 * 1024

BF16 = jnp.bfloat16
F32 = jnp.float32


def _const_spec(shape):
    nd = len(shape)
    return pl.BlockSpec(shape, lambda *_: (0,) * nd, pipeline_mode=pl.Buffered(1))


def _params(n_axes):
    return pltpu.CompilerParams(dimension_semantics=("arbitrary",) * n_axes,
                                vmem_limit_bytes=VMEM_LIMIT)


def _rmsnorm_rows(x, g):
    ms = jnp.mean(x * x, axis=-1, keepdims=True)
    return x * lax.rsqrt(ms + EPS) * g


def _sigmoid(x):
    return 1.0 / (1.0 + jnp.exp(-x))


def _gelu_tanh(x):
    c = math.sqrt(2.0 / math.pi)
    return 0.5 * x * (1.0 + jnp.tanh(c * (x + 0.044715 * (x * x * x))))


def _dot(a, b):
    return jnp.dot(a, b, preferred_element_type=F32)


def _dot_nt(a, b):
    return lax.dot_general(a, b, (((1,), (1,)), ((), ())), preferred_element_type=F32)


def _dot_tn(a, b):
    return lax.dot_general(a, b, (((0,), (0,)), ((), ())), preferred_element_type=F32)


def _pre_even_kernel(x_ref, g_ref, w_ref, cos_ref, sin_ref, q_ref, k_ref, v_ref):
    h = _rmsnorm_rows(x_ref[...], g_ref[...]).astype(BF16)
    cos = cos_ref[...]
    sin = sin_ref[...]
    for out_ref, base, scale in ((q_ref, 0, ATTN_QK_DIM ** -0.5 * math.log2(math.e)),
                                 (k_ref, ATTN_QK_WIDTH, 1.0)):
        p = _dot(h, w_ref[:, base:base + ATTN_QK_WIDTH])
        for pair in range(2):
            lo = p[:, 256 * pair:256 * pair + 128]
            hi = p[:, 256 * pair + 128:256 * pair + 256]
            out_ref[:, 256 * pair:256 * pair + 128] = ((lo * cos - hi * sin) * scale).astype(BF16)
            out_ref[:, 256 * pair + 128:256 * pair + 256] = ((hi * cos + lo * sin) * scale).astype(BF16)
    v_ref[...] = _dot(h, w_ref[:, 2 * ATTN_QK_WIDTH:]).astype(BF16)


def _pre_even(x2, g, w_qkv, cos, sin, seq):
    n_tok = x2.shape[0]
    tm = TOKEN_TILE
    n_pos_blocks = seq // tm
    out = jax.ShapeDtypeStruct((n_tok, 512), BF16)
    row_spec = pl.BlockSpec((tm, 512), lambda i: (i, 0))
    rope_spec = pl.BlockSpec((tm, 128), lambda i: (i % n_pos_blocks, 0))
    return pl.pallas_call(
        _pre_even_kernel,
        grid=(n_tok // tm,),
        in_specs=[pl.BlockSpec((tm, D_MODEL), lambda i: (i, 0)), _const_spec((1, D_MODEL)),
                  _const_spec(w_qkv.shape), rope_spec, rope_spec],
        out_specs=[row_spec] * 3,
        out_shape=[out] * 3,
        compiler_params=_params(1),
        name="pre_even",
    )(x2, g, w_qkv, cos, sin)


def _s5_in_kernel(x_ref, g_ref, wu_ref, ut_ref, u_scr):
    nc = x_ref.shape[0] // S5_CHUNK
    groups_per_slab = 128 // S5_GROUP
    h = _rmsnorm_rows(x_ref[...], g_ref[...]).astype(BF16)
    u = _dot(h, wu_ref[...])
    for j in range(S5_WIDTH // 128):
        u_scr[j] = u[:, 128 * j:128 * (j + 1)]
    for j in range(S5_WIDTH // 128):
        for ph in range(S5_CHUNK):
            t = u_scr[j, pl.ds(ph, nc, stride=S5_CHUNK), :]
            ut_ref[groups_per_slab * j:groups_per_slab * (j + 1), ph, :, :] = (
                t.T.astype(BF16).reshape(groups_per_slab, S5_GROUP, nc))


def _s5_in(x2, g, wu, bsz, seq):
    nc = seq // S5_CHUNK
    return pl.pallas_call(
        _s5_in_kernel,
        grid=(bsz,),
        in_specs=[pl.BlockSpec((seq, D_MODEL), lambda b: (b, 0)), _const_spec((1, D_MODEL)),
                  _const_spec(wu.shape)],
        out_specs=pl.BlockSpec((S5_GROUPS, S5_CHUNK, S5_GROUP, nc), lambda b: (0, 0, 0, b)),
        out_shape=jax.ShapeDtypeStruct((S5_GROUPS, S5_CHUNK, S5_GROUP, bsz * nc), BF16),
        scratch_shapes=[pltpu.VMEM((S5_WIDTH // 128, seq, 128), F32)],
        compiler_params=_params(1),
        name="s5_in",
    )(x2, g, wu)


def _s5_out_kernel(yt_ref, wglut_ref, bglu_ref, o_ref, b_scr):
    nc = yt_ref.shape[3]
    yt = jnp.concatenate([yt_ref[:, ph, :, :].reshape(S5_WIDTH, nc) for ph in range(S5_CHUNK)], axis=1)
    yt = _gelu_tanh(yt)
    bt = yt * _sigmoid(_dot(wglut_ref[...], yt.astype(BF16)) + bglu_ref[...])
    for j in range(S5_WIDTH // 128):
        for ph in range(S5_CHUNK):
            b_scr[j, pl.ds(ph, nc, stride=S5_CHUNK), :] = bt[128 * j:128 * (j + 1), ph * nc:(ph + 1) * nc].T
    for j in range(S5_WIDTH // 128):
        o_ref[:, 128 * j:128 * (j + 1)] = b_scr[j]


def _s5_out(y_t, w_glu_t, b_glu_col, bsz, seq):
    nc = seq // S5_CHUNK
    return pl.pallas_call(
        _s5_out_kernel,
        grid=(bsz,),
        in_specs=[pl.BlockSpec((S5_GROUPS, S5_CHUNK, S5_GROUP, nc), lambda b: (0, 0, 0, b)),
                  _const_spec(w_glu_t.shape), _const_spec(b_glu_col.shape)],
        out_specs=pl.BlockSpec((seq, S5_WIDTH), lambda b: (b, 0)),
        out_shape=jax.ShapeDtypeStruct((bsz * seq, S5_WIDTH), F32),
        scratch_shapes=[pltpu.VMEM((S5_WIDTH // 128, seq, 128), F32)],
        compiler_params=_params(1),
        name="s5_out",
    )(y_t, w_glu_t, b_glu_col)


def _attn_kernel(lambda_init, q_ref, k_ref, v_ref, lam_ref, g_ref, o_ref):
    q = q_ref[0]
    k = k_ref[0]
    lam = lam_ref[...]
    lam_val = (jnp.exp(jnp.sum(lam[0:1] * lam[1:2], axis=-1, keepdims=True))
               - jnp.exp(jnp.sum(lam[2:3] * lam[3:4], axis=-1, keepdims=True)) + lambda_init)
    tq = q.shape[0]
    lane_group = (lax.broadcasted_iota(jnp.int32, q.shape, 1) // 32) % 4
    zero = jnp.zeros_like(q)
    q_all = jnp.concatenate([jnp.where(lane_group == hc, q, zero) for hc in range(4)], axis=0)
    s = _dot_nt(q_all, k)
    e = jnp.exp2(s - jnp.max(s, axis=-1, keepdims=True)).astype(BF16)
    ones = jnp.ones((k.shape[0], ATTN_V_DIM), BF16)
    for hh in range(2):
        v_ext = jnp.concatenate([v_ref[0, :, 128 * hh:128 * hh + 128], ones], axis=1)
        r = _dot(e[2 * hh * tq:2 * (hh + 1) * tq], v_ext)
        o = (r[0:tq, 0:ATTN_V_DIM] / r[0:tq, ATTN_V_DIM:]
             - lam_val * (r[tq:2 * tq, 0:ATTN_V_DIM] / r[tq:2 * tq, ATTN_V_DIM:]))
        o = _rmsnorm_rows(o, g_ref[...]) * (1.0 - lambda_init)
        o_ref[0, :, 128 * hh:128 * hh + 128] = o.astype(o_ref.dtype)


def _attention(q, k, v, lam, subln_g, lambda_init):
    bsz, seq, _ = q.shape
    tq = ATTN_Q_TILE
    kv_spec = pl.BlockSpec((1, seq, 256), lambda b, p, i: (b, 0, p))
    return pl.pallas_call(
        functools.partial(_attn_kernel, lambda_init),
        grid=(bsz, 2, seq // tq),
        in_specs=[pl.BlockSpec((1, tq, 256), lambda b, p, i: (b, i, p)),
                  kv_spec, kv_spec,
                  _const_spec(lam.shape),
                  _const_spec(subln_g.shape)],
        out_specs=pl.BlockSpec((1, tq, 256), lambda b, p, i: (b, i, p)),
        out_shape=jax.ShapeDtypeStruct((bsz, seq, ATTN_WIDTH), BF16),
        compiler_params=_params(3),
        name="diff_attention",
    )(q, k, v, lam, subln_g)


def _s5_kernel(n_chunks, ut_ref, mi_ref, min_ref, mo_ref, ctab_ref, y_ref):
    n_rows = ut_ref.shape[2]
    n_steps = n_chunks.bit_length() - 1
    ut2 = ut_ref[...].reshape(2 * S5_CW, n_rows)
    st = _dot_tn(ut2, min_ref[0])
    ctab = ctab_ref[0]
    pos = lax.broadcasted_iota(jnp.int32, (n_rows, 128), 0) % n_chunks
    zs = []
    for d in range(2):
        xr = st[:, 256 * d:256 * d + 128]
        xi = st[:, 256 * d + 128:256 * d + 256]
        for k in range(n_steps):
            s = 1 << k
            if d == 0:
                shift, keep = s, pos >= s
            else:
                shift, keep = n_rows - s, pos < n_chunks - s
            pr = ctab[4 * k + 2 * d:4 * k + 2 * d + 1]
            pi = ctab[4 * k + 2 * d + 1:4 * k + 2 * d + 2]
            sr = jnp.where(keep, pltpu.roll(xr, shift, 0), 0.0)
            si = jnp.where(keep, pltpu.roll(xi, shift, 0), 0.0)
            xr, xi = xr + pr * sr - pi * si, xi + pr * si + pi * sr
        if d == 0:
            shift, keep = 1, pos >= 1
        else:
            shift, keep = n_rows - 1, pos < n_chunks - 1
        zs += [jnp.where(keep, pltpu.roll(xr, shift, 0), 0.0), jnp.where(keep, pltpu.roll(xi, shift, 0), 0.0)]
    z = jnp.concatenate(zs, axis=1).astype(BF16)
    for g in range(2):
        y_ref[g] = _dot(mi_ref[g], ut_ref[g]) + _dot_nt(mo_ref[g], z)


def _s5_chunked(u_t, m_intra_t, m_in_pair, m_out_ext, ctab, n_chunks):
    n_groups, _, n_lanes = u_t.shape
    return pl.pallas_call(
        functools.partial(_s5_kernel, n_chunks),
        grid=(n_groups // 2,),
        in_specs=[pl.BlockSpec((2, S5_CW, n_lanes), lambda i: (i, 0, 0)),
                  pl.BlockSpec((2, S5_CW, S5_CW), lambda i: (i, 0, 0)),
                  pl.BlockSpec((1, 2 * S5_CW, 512), lambda i: (i, 0, 0)),
                  pl.BlockSpec((2, S5_CW, 512), lambda i: (i, 0, 0)),
                  pl.BlockSpec((1, 32, 128), lambda i: (i, 0, 0))],
        out_specs=pl.BlockSpec((2, S5_CW, n_lanes), lambda i: (i, 0, 0)),
        out_shape=jax.ShapeDtypeStruct((n_groups, S5_CW, n_lanes), F32),
        compiler_params=_params(1),
        name="s5_chunked",
    )(u_t, m_intra_t, m_in_pair, m_out_ext, ctab)


def _s5_chunk_operators(lam_re, lam_im, log_step, b_re, b_im, c_re, c_im, d_skip, n_steps):
    hp = lax.Precision.HIGHEST
    t = S5_CHUNK
    g2 = S5_GROUPS // 2
    lr = jnp.minimum(lam_re, -1e-4)
    li = lam_im
    dt = jnp.exp(log_step)[..., None]
    mag = jnp.exp(lr * dt)
    ar = mag * jnp.cos(li * dt)
    ai = mag * jnp.sin(li * dt)
    den = lr * lr + li * li
    cr = ((ar - 1.0) * lr + ai * li) / den
    ci = (ai * lr - (ar - 1.0) * li) / den
    bbr = (cr[..., None] * b_re - ci[..., None] * b_im).transpose(0, 1, 3, 2)
    bbi = (cr[..., None] * b_im + ci[..., None] * b_re).transpose(0, 1, 3, 2)

    def powers(taus):
        taus = taus.astype(F32)[:, None, :, None]
        pmag = jnp.exp(taus * (lr * dt)[:, :, None, :])
        ang = taus * (li * dt)[:, :, None, :]
        return pmag * jnp.cos(ang), pmag * jnp.sin(ang)

    slot = jnp.arange(2 * t - 1)
    lag = jnp.stack([jnp.maximum(t - 1 - slot, 0), jnp.maximum(slot - (t - 1), 0)])
    live = jnp.stack([slot <= t - 1, slot >= t - 1]).astype(F32)[:, None, :, None]
    pr, pi = powers(lag)
    pr, pi = pr * live, pi * live
    rr = pr[:, :, :, None, :] * bbr[:, :, None] - pi[:, :, :, None, :] * bbi[:, :, None]
    ri = pr[:, :, :, None, :] * bbi[:, :, None] + pi[:, :, :, None, :] * bbr[:, :, None]

    taps = jnp.einsum('dgnk,dgsmk->gnsm', jnp.concatenate([c_re, -c_im], axis=-1),
                      jnp.concatenate([rr, ri], axis=-1), precision=hp)
    eye = jnp.eye(S5_GROUP, dtype=F32)
    at_lag0 = (slot == t - 1).astype(F32)
    taps = taps + (d_skip.reshape(S5_GROUPS, S5_GROUP, 1, 1) * eye[None, :, None, :]
                   * at_lag0[None, None, :, None])
    taps = taps.reshape(S5_GROUPS, S5_GROUP, (2 * t - 1) * S5_GROUP)
    m_intra_t = jnp.stack([taps[:, :, (t - 1 - j) * S5_GROUP:(t - 1 - j) * S5_GROUP + S5_CW]
                           for j in range(t)], axis=1).reshape(S5_GROUPS, S5_CW, S5_CW)

    m_in = jnp.stack([rr[0, :, :t], ri[0, :, :t], rr[1, :, t - 1:], ri[1, :, t - 1:]], axis=3)
    pair_eye = jnp.eye(2, dtype=F32)
    m_in_pair = (m_in.reshape(g2, 2, S5_CW, 4, 1, S5_STATE) * pair_eye[None, :, None, None, :, None]
                 ).reshape(g2, 2 * S5_CW, 512)

    up = jnp.arange(1, t + 1)
    qr, qi = powers(jnp.stack([up, t + 1 - up]))
    er = qr[:, :, :, None, :] * c_re[:, :, None] - qi[:, :, :, None, :] * c_im[:, :, None]
    ei = qr[:, :, :, None, :] * c_im[:, :, None] + qi[:, :, :, None, :] * c_re[:, :, None]
    m_out = jnp.stack([er[0], -ei[0], er[1], -ei[1]], axis=3)
    in_pair = jnp.tile(pair_eye, (g2, 1))
    m_out_ext = (m_out.reshape(S5_GROUPS, S5_CW, 4, 1, S5_STATE) * in_pair[:, None, None, :, None]
                 ).reshape(S5_GROUPS, S5_CW, 512)

    ar16, ai16 = qr[0, :, t - 1], qi[0, :, t - 1]
    br16, bi16 = qr[1, :, 0], qi[1, :, 0]
    rows = []
    for _ in range(n_steps):
        rows += [ar16, ai16, br16, bi16]
        ar16, ai16 = ar16 * ar16 - ai16 * ai16, 2.0 * ar16 * ai16
        br16, bi16 = br16 * br16 - bi16 * bi16, 2.0 * br16 * bi16
    rows += [jnp.zeros_like(ar16)] * (32 - len(rows))
    ctab = jnp.stack(rows, axis=1).reshape(g2, 2, 32, S5_STATE).transpose(0, 2, 1, 3).reshape(g2, 32, 128)
    return m_intra_t.astype(BF16), m_in_pair.astype(BF16), m_out_ext.astype(BF16), ctab


def _mlp(x1, g_ref, w1_ref, w2_ref):
    h = _rmsnorm_rows(x1, g_ref[...]).astype(BF16)
    acc = x1
    for j in range(D_FF // FF_TILE):
        hid = _dot(h, w1_ref[:, j * FF_TILE:(j + 1) * FF_TILE])
        hid = jnp.square(jnp.maximum(hid, 0.0)).astype(BF16)
        acc = acc + _dot(hid, w2_ref[j * FF_TILE:(j + 1) * FF_TILE, :])
    return acc


def _post_even_kernel(x_ref, a_ref, b_ref, wout_ref, g_ref, w1_ref, w2_ref, o_ref):
    mix = (_dot(a_ref[...], wout_ref[0:ATTN_WIDTH, :])
           + _dot(b_ref[...].astype(BF16), wout_ref[ATTN_WIDTH:, :]))
    o_ref[...] = _mlp(x_ref[...] + mix, g_ref, w1_ref, w2_ref)


def _post_even(x2, a_out, b_out, w_out, g_mlp, w1, w2):
    n_tok = x2.shape[0]
    tm = TOKEN_TILE
    return pl.pallas_call(
        _post_even_kernel,
        grid=(n_tok // tm,),
        in_specs=[pl.BlockSpec((tm, D_MODEL), lambda i: (i, 0)),
                  pl.BlockSpec((tm, ATTN_WIDTH), lambda i: (i, 0)),
                  pl.BlockSpec((tm, S5_WIDTH), lambda i: (i, 0)),
                  _const_spec(w_out.shape), _const_spec(g_mlp.shape), _const_spec(w1.shape),
                  _const_spec(w2.shape)],
        out_specs=pl.BlockSpec((tm, D_MODEL), lambda i: (i, 0)),
        out_shape=jax.ShapeDtypeStruct((n_tok, D_MODEL), F32),
        compiler_params=_params(1),
        name="post_even",
    )(x2, a_out, b_out, w_out, g_mlp, w1, w2)


def _post_odd_kernel(x_ref, m_ref, wout_ref, g_ref, w1_ref, w2_ref, gf_ref, o_ref):
    x1 = x_ref[...] + _dot(m_ref[...], wout_ref[...])
    o_ref[...] = _rmsnorm_rows(_mlp(x1, g_ref, w1_ref, w2_ref), gf_ref[...])


def _post_odd(x2, mixed, w_out, g_mlp, w1, w2, g_final):
    n_tok = x2.shape[0]
    tm = TOKEN_TILE
    return pl.pallas_call(
        _post_odd_kernel,
        grid=(n_tok // tm,),
        in_specs=[pl.BlockSpec((tm, D_MODEL), lambda i: (i, 0)),
                  pl.BlockSpec((tm, D_MODEL), lambda i: (i, 0)),
                  _const_spec(w_out.shape), _const_spec(g_mlp.shape),
                  _const_spec(w1.shape), _const_spec(w2.shape), _const_spec(g_final.shape)],
        out_specs=pl.BlockSpec((tm, D_MODEL), lambda i: (i, 0)),
        out_shape=jax.ShapeDtypeStruct((n_tok, D_MODEL), F32),
        compiler_params=_params(1),
        name="post_odd",
    )(x2, mixed, w_out, g_mlp, w1, w2, g_final)


def _pre_odd_kernel(x_ref, g_ref, w_ref, q_ref, i_ref, ff_ref, fb_ref, gate_ref):
    h = _rmsnorm_rows(x_ref[...], g_ref[...]).astype(BF16)
    for s, out_ref in enumerate((q_ref, i_ref, ff_ref, fb_ref, gate_ref)):
        out_ref[...] = _dot(h, w_ref[:, s * D_MODEL:(s + 1) * D_MODEL]).astype(out_ref.dtype)


def _pre_odd(x2, g, w):
    n_tok = x2.shape[0]
    tm = TOKEN_TILE
    row_spec = pl.BlockSpec((tm, D_MODEL), lambda i: (i, 0))
    lo = jax.ShapeDtypeStruct((n_tok, D_MODEL), BF16)
    hi = jax.ShapeDtypeStruct((n_tok, D_MODEL), F32)
    return pl.pallas_call(
        _pre_odd_kernel,
        grid=(n_tok // tm,),
        in_specs=[row_spec, _const_spec((1, D_MODEL)), _const_spec(w.shape)],
        out_specs=[row_spec] * 5,
        out_shape=[lo, lo, hi, hi, lo],
        compiler_params=_params(1),
        name="pre_odd",
    )(x2, g, w)


def _chunk_cumprod(x, reverse):
    n = x.shape[0]
    pos = lax.broadcasted_iota(jnp.int32, x.shape, 0) % HGRN_CHUNK
    s = 1
    while s < HGRN_CHUNK:
        if reverse:
            shifted = pltpu.roll(x, n - s, 0)
            x = x * jnp.where(pos < HGRN_CHUNK - s, shifted, 1.0)
        else:
            shifted = pltpu.roll(x, s, 0)
            x = x * jnp.where(pos >= s, shifted, 1.0)
        s *= 2
    return x


def _hgrn_kernel(layer, q_ref, v_ref, ff_ref, fb_ref, gate_ref, lbl_ref, ng_ref, o_ref,
                 qd_ref, dec_ref, kv_ref, acc_ref):
    seq = q_ref.shape[1]
    ch = HGRN_CHUNK
    blk = HGRN_BLOCK
    n_chunks = seq // ch
    per_blk = blk // ch
    logits = lbl_ref[...]
    soft = jnp.exp(logits - jnp.max(logits, axis=0, keepdims=True))
    soft = soft / jnp.sum(soft, axis=0, keepdims=True)
    lb = jnp.sum(soft[0:layer + 1], axis=0, keepdims=True) - soft[0:1]

    c2 = 0.5 * (1.0 - lb)
    c1 = lb + c2
    row = lax.broadcasted_iota(jnp.int32, (blk, blk), 0)
    col = lax.broadcasted_iota(jnp.int32, (blk, blk), 1)
    u32 = lambda a: a.astype(jnp.uint32)
    keep = (u32(row - col) <= u32(row % ch), u32(col - row) <= u32(ch - 1 - row % ch))
    row_chunk = lax.broadcasted_iota(jnp.int32, (blk, 128), 0) // ch

    def intra(j, carry):
        sl = pl.ds(pl.multiple_of(j * blk, blk), blk)
        v = v_ref[0, sl, :]
        vt = v.astype(F32).T.astype(BF16)
        qf = q_ref[0, sl, :].astype(F32)
        for d, f_ref in enumerate((ff_ref, fb_ref)):
            ct = c2 * jnp.tanh(0.5 * f_ref[0, sl, :])
            eb = _chunk_cumprod(c1 + ct, reverse=(d == 1))
            k_inv = (c2 - ct) / eb
            eb3 = eb.reshape(per_blk, ch, 128)
            e_end = eb3[:, ch - 1:ch, :] if d == 0 else eb3[:, 0:1, :]
            dec_ref[d, pl.ds(j * per_blk, per_blk)] = e_end
            q_dec = (qf * eb).astype(BF16)
            qd_ref[d, sl, :] = q_dec
            k_dec = (k_inv.reshape(per_blk, ch, 128) * e_end).reshape(blk, 128).astype(BF16)

            s = _dot_nt(q_dec, k_inv.astype(BF16))
            p = jnp.where(keep[d], s, 0.0).astype(BF16)
            acc_ref[d, sl, :] = _dot(p, v)
            rhs = jnp.concatenate([jnp.where(row_chunk == cc, k_dec, jnp.zeros_like(k_dec))
                                   for cc in range(per_blk)], axis=1)
            kvs = _dot(vt, rhs)
            for cc in range(per_blk):
                kv_ref[d, j * per_blk + cc] = kvs[:, 128 * cc:128 * (cc + 1)]
        return carry

    lax.fori_loop(0, seq // blk, intra, 0, unroll=4)

    def finish(rows):
        o = _rmsnorm_rows(acc_ref[0, rows, :] + acc_ref[1, rows, :], ng_ref[...])
        gate = 0.5 + 0.5 * jnp.tanh(0.5 * gate_ref[0, rows, :].astype(F32))
        o_ref[0, rows, :] = (o * gate).astype(o_ref.dtype)

    states = [jnp.zeros((128, 128), F32)] * 2
    for n in range(n_chunks):
        for d in range(2):
            c = n if d == 0 else n_chunks - 1 - n
            rows = slice(c * ch, (c + 1) * ch)
            acc_ref[d, rows, :] += _dot_nt(qd_ref[d, rows, :], states[d].astype(BF16))
            states[d] = dec_ref[d, c] * states[d] + kv_ref[d, c]
        if 2 * n >= n_chunks:
            finish(slice(n * ch, (n + 1) * ch))
            finish(slice((n_chunks - 1 - n) * ch, (n_chunks - n) * ch))


def _hgrn(q, v, ff, fb, gate, lb_logits, norm_g, layer):
    bsz, seq, _ = q.shape
    n_chunks = seq // HGRN_CHUNK
    head_spec = pl.BlockSpec((1, seq, 128), lambda b, h: (b, 0, h))
    return pl.pallas_call(
        functools.partial(_hgrn_kernel, layer),
        grid=(bsz, HGRN_HEADS),
        in_specs=[head_spec] * 5 + [pl.BlockSpec((DEPTH, 128), lambda b, h: (0, h)),
                                    pl.BlockSpec((1, 128), lambda b, h: (0, h))],
        out_specs=head_spec,
        out_shape=jax.ShapeDtypeStruct((bsz, seq, D_MODEL), BF16),
        scratch_shapes=[
           pltpu.VMEM((2, seq, 128), BF16),
           pltpu.VMEM((2, n_chunks, 1, 128), F32),
           pltpu.VMEM((2, n_chunks, 128, 128), F32),
           pltpu.VMEM((2, seq, 128), F32)],
        compiler_params=_params(2),
        name="hgrn2",
    )(q, v, ff, fb, gate, lb_logits, norm_g)


def _rope_pair_tables(seq):
    inv = ROPE_THETA ** (-jnp.arange(0, ATTN_QK_DIM, 2, dtype=F32) / ATTN_QK_DIM)
    ang = jnp.arange(seq, dtype=F32)[:, None] * inv[None, :]
    return jnp.tile(jnp.cos(ang), (1, 4)), jnp.tile(jnp.sin(ang), (1, 4))


def _pair_layout_columns():
    idx = jnp.arange(ATTN_QK_WIDTH).reshape(2, 2, 2, 2, 32)
    return idx.transpose(0, 3, 1, 2, 4).reshape(-1)


def kernel(x, norm_mix_g, norm_mlp_g, final_norm_g, w_ff_in, w_ff_out, w_in_even, w_out_even, diff_lambda, diff_subln_g, s5_lam_re, s5_lam_im, s5_log_step, s5_b_re, s5_b_im, s5_c_re, s5_c_im, s5_d, s5_w_glu, s5_b_glu, w_in_odd, w_out_odd, hgrn_norm_g, hgrn_lb_logits):
    bsz, seq, _ = x.shape
    n_tok = bsz * seq
    n_chunks = seq // S5_CHUNK
    assert n_chunks & (n_chunks - 1) == 0 and n_chunks % 128 == 0
    x2 = x.reshape(n_tok, D_MODEL)
    cos, sin = _rope_pair_tables(seq)
    perm = _pair_layout_columns()

    for layer in range(DEPTH):
        g_mix = norm_mix_g[layer].reshape(1, D_MODEL)
        g_mlp = norm_mlp_g[layer].reshape(1, D_MODEL)
        w1 = w_ff_in[layer].astype(BF16)
        w2 = w_ff_out[layer].astype(BF16)
        if layer % 2 == 0:
            e = layer // 2
            w = w_in_even[e]
            w_qkv = jnp.concatenate([w[:, :ATTN_QK_WIDTH][:, perm],
                                     w[:, ATTN_QK_WIDTH:2 * ATTN_QK_WIDTH][:, perm],
                                     w[:, 2 * ATTN_QK_WIDTH:2 * ATTN_QK_WIDTH + ATTN_WIDTH]], axis=1).astype(BF16)
            wu = w[:, 2 * ATTN_QK_WIDTH + ATTN_WIDTH:].astype(BF16)
            q, k, v = _pre_even(x2, g_mix, w_qkv, cos, sin, seq)
            u_t = _s5_in(x2, g_mix, wu, bsz, seq)
            lambda_init = 0.8 - 0.6 * math.exp(-0.3 * layer)
            a_out = _attention(q.reshape(bsz, seq, -1), k.reshape(bsz, seq, -1), v.reshape(bsz, seq, -1),
                               diff_lambda[e], diff_subln_g[e].reshape(1, ATTN_V_DIM), lambda_init)
            ops = _s5_chunk_operators(s5_lam_re[e], s5_lam_im[e], s5_log_step[e], s5_b_re[e], s5_b_im[e],
                                      s5_c_re[e], s5_c_im[e], s5_d[e], n_chunks.bit_length() - 1)
            y_t = _s5_chunked(u_t.reshape(S5_GROUPS, S5_CW, bsz * n_chunks), *ops, n_chunks)
            b_out = _s5_out(y_t.reshape(S5_GROUPS, S5_CHUNK, S5_GROUP, bsz * n_chunks),
                            s5_w_glu[e].T.astype(BF16), s5_b_glu[e].reshape(S5_WIDTH, 1), bsz, seq)
            x2 = _post_even(x2, a_out.reshape(n_tok, ATTN_WIDTH), b_out, w_out_even[e].astype(BF16),
                            g_mlp, w1, w2)
        else:
            o_i = layer // 2
            q, v, ff, fb, gate = _pre_odd(x2, g_mix, w_in_odd[o_i].astype(BF16))
            shp = (bsz, seq, D_MODEL)
            mixed = _hgrn(q.reshape(shp), v.reshape(shp), ff.reshape(shp), fb.reshape(shp), gate.reshape(shp),
                          hgrn_lb_logits, hgrn_norm_g[o_i].reshape(1, D_MODEL), layer)
            assert layer == DEPTH - 1, "the final norm is fused into the last (odd) layer's kernel"
            x2 = _post_odd(x2, mixed.reshape(n_tok, D_MODEL), w_out_odd[o_i].astype(BF16), g_mlp, w1, w2,
                           final_norm_g.reshape(1, D_MODEL))
    return x2.reshape(bsz, seq, D_MODEL)
```

```python
import functools
import math

import jax
import jax.numpy as jnp
from jax import lax
from jax.experimental import pallas as pl
from jax.experimental.pallas import tpu as pltpu

D_MODEL = 1024
DEPTH = 2
ATTN_HEADS = 4
ATTN_QK_DIM = 64
ATTN_V_DIM = 128
ATTN_QK_WIDTH = 512
ATTN_WIDTH = 512
ROPE_THETA = 10000.0
S5_WIDTH = 512
S5_GROUP = 16
S5_GROUPS = 32
S5_STATE = 64
HGRN_HEADS = 8
HGRN_CHUNK = 64
HGRN_BLOCK = 256
D_FF = 4096
EPS = 1e-6

S5_CHUNK = 16
S5_CW = S5_CHUNK * S5_GROUP

TOKEN_TILE = 512
ATTN_Q_TILE = 512
ATTN_Q_SUB = 256
FF_TILE = 1024
VMEM_LIMIT = 56 * 1024 * 1024

BF16 = jnp.bfloat16
F32 = jnp.float32


def _const_spec(shape):
    nd = len(shape)
    return pl.BlockSpec(shape, lambda *_: (0,) * nd, pipeline_mode=pl.Buffered(1))


def _params(n_axes):
    return pltpu.CompilerParams(dimension_semantics=("arbitrary",) * n_axes,
                                vmem_limit_bytes=VMEM_LIMIT)


def _rmsnorm_rows(x, g):
    ms = jnp.mean(x * x, axis=-1, keepdims=True)
    return x * lax.rsqrt(ms + EPS) * g


def _sigmoid(x):
    return 1.0 / (1.0 + jnp.exp(-x))


def _gelu_tanh(x):
    c = math.sqrt(2.0 / math.pi)
    return 0.5 * x * (1.0 + jnp.tanh(c * (x + 0.044715 * (x * x * x))))


def _dot(a, b):
    return jnp.dot(a, b, preferred_element_type=F32)


def _dot_nt(a, b):
    return lax.dot_general(a, b, (((1,), (1,)), ((), ())), preferred_element_type=F32)


def _dot_tn(a, b):
    return lax.dot_general(a, b, (((0,), (0,)), ((), ())), preferred_element_type=F32)


def _pre_even_kernel(x_ref, g_ref, w_ref, cos_ref, sin_ref, q_ref, k_ref, v_ref):
    h = _rmsnorm_rows(x_ref[...], g_ref[...]).astype(BF16)
    cos = cos_ref[...]
    sin = sin_ref[...]
    for out_ref, base, scale in ((q_ref, 0, ATTN_QK_DIM ** -0.5 * math.log2(math.e)),
                                 (k_ref, ATTN_QK_WIDTH, 1.0)):
        p = _dot(h, w_ref[:, base:base + ATTN_QK_WIDTH])
        for pair in range(2):
            lo = p[:, 256 * pair:256 * pair + 128]
            hi = p[:, 256 * pair + 128:256 * pair + 256]
            out_ref[:, 256 * pair:256 * pair + 128] = ((lo * cos - hi * sin) * scale).astype(BF16)
            out_ref[:, 256 * pair + 128:256 * pair + 256] = ((hi * cos + lo * sin) * scale).astype(BF16)
    v_ref[...] = _dot(h, w_ref[:, 2 * ATTN_QK_WIDTH:]).astype(BF16)


def _pre_even(x2, g, w_qkv, cos, sin, seq):
    n_tok = x2.shape[0]
    tm = TOKEN_TILE
    n_pos_blocks = seq // tm
    out = jax.ShapeDtypeStruct((n_tok, 512), BF16)
    row_spec = pl.BlockSpec((tm, 512), lambda i: (i, 0))
    rope_spec = pl.BlockSpec((tm, 128), lambda i: (i % n_pos_blocks, 0))
    return pl.pallas_call(
        _pre_even_kernel,
        grid=(n_tok // tm,),
        in_specs=[pl.BlockSpec((tm, D_MODEL), lambda i: (i, 0)), _const_spec((1, D_MODEL)),
                  _const_spec(w_qkv.shape), rope_spec, rope_spec],
        out_specs=[row_spec] * 3,
        out_shape=[out] * 3,
        compiler_params=_params(1),
        name="pre_even",
    )(x2, g, w_qkv, cos, sin)


def _s5_in_kernel(x_ref, g_ref, wu_ref, ut_ref, u_scr):
    nc = x_ref.shape[0] // S5_CHUNK
    groups_per_slab = 128 // S5_GROUP
    h = _rmsnorm_rows(x_ref[...], g_ref[...]).astype(BF16)
    u = _dot(h, wu_ref[...])
    for j in range(S5_WIDTH // 128):
        u_scr[j] = u[:, 128 * j:128 * (j + 1)]
    for j in range(S5_WIDTH // 128):
        for ph in range(S5_CHUNK):
            t = u_scr[j, pl.ds(ph, nc, stride=S5_CHUNK), :]
            ut_ref[groups_per_slab * j:groups_per_slab * (j + 1), ph, :, :] = (
                t.T.astype(BF16).reshape(groups_per_slab, S5_GROUP, nc))


def _s5_in(x2, g, wu, bsz, seq):
    nc = seq // S5_CHUNK
    return pl.pallas_call(
        _s5_in_kernel,
        grid=(bsz,),
        in_specs=[pl.BlockSpec((seq, D_MODEL), lambda b: (b, 0)), _const_spec((1, D_MODEL)),
                  _const_spec(wu.shape)],
        out_specs=pl.BlockSpec((S5_GROUPS, S5_CHUNK, S5_GROUP, nc), lambda b: (0, 0, 0, b)),
        out_shape=jax.ShapeDtypeStruct((S5_GROUPS, S5_CHUNK, S5_GROUP, bsz * nc), BF16),
        scratch_shapes=[pltpu.VMEM((S5_WIDTH // 128, seq, 128), F32)],
        compiler_params=_params(1),
        name="s5_in",
    )(x2, g, wu)


def _s5_out_kernel(yt_ref, wglut_ref, bglu_ref, o_ref, b_scr):
    nc = yt_ref.shape[3]
    yt = jnp.concatenate([yt_ref[:, ph, :, :].reshape(S5_WIDTH, nc) for ph in range(S5_CHUNK)], axis=1)
    yt = _gelu_tanh(yt)
    bt = yt * _sigmoid(_dot(wglut_ref[...], yt.astype(BF16)) + bglu_ref[...])
    for j in range(S5_WIDTH // 128):
        for ph in range(S5_CHUNK):
            b_scr[j, pl.ds(ph, nc, stride=S5_CHUNK), :] = bt[128 * j:128 * (j + 1), ph * nc:(ph + 1) * nc].T
    for j in range(S5_WIDTH // 128):
        o_ref[:, 128 * j:128 * (j + 1)] = b_scr[j]


def _s5_out(y_t, w_glu_t, b_glu_col, bsz, seq):
    nc = seq // S5_CHUNK
    return pl.pallas_call(
        _s5_out_kernel,
        grid=(bsz,),
        in_specs=[pl.BlockSpec((S5_GROUPS, S5_CHUNK, S5_GROUP, nc), lambda b: (0, 0, 0, b)),
                  _const_spec(w_glu_t.shape), _const_spec(b_glu_col.shape)],
        out_specs=pl.BlockSpec((seq, S5_WIDTH), lambda b: (b, 0)),
        out_shape=jax.ShapeDtypeStruct((bsz * seq, S5_WIDTH), F32),
        scratch_shapes=[pltpu.VMEM((S5_WIDTH // 128, seq, 128), F32)],
        compiler_params=_params(1),
        name="s5_out",
    )(y_t, w_glu_t, b_glu_col)


def _attn_kernel(lambda_init, q_ref, k_ref, v_ref, lam_ref, g_ref, o_ref):
    k = k_ref[0]
    lam = lam_ref[...]
    lam_val = (jnp.exp(jnp.sum(lam[0:1] * lam[1:2], axis=-1, keepdims=True))
               - jnp.exp(jnp.sum(lam[2:3] * lam[3:4], axis=-1, keepdims=True)) + lambda_init)
    tq = ATTN_Q_SUB
    lane_group = (lax.broadcasted_iota(jnp.int32, (tq, 256), 1) // 32) % 4
    ones = jnp.ones((k.shape[0], ATTN_V_DIM), BF16)
    v_ext = [jnp.concatenate([v_ref[0, :, 128 * hh:128 * hh + 128], ones], axis=1) for hh in range(2)]
    for t in range(q_ref.shape[1] // tq):
        q = q_ref[0, t * tq:(t + 1) * tq, :]
        zero = jnp.zeros_like(q)
        q_all = jnp.concatenate([jnp.where(lane_group == hc, q, zero) for hc in range(4)], axis=0)
        s = _dot_nt(q_all, k)
        e = jnp.exp2(s - jnp.max(s, axis=-1, keepdims=True)).astype(BF16)
        for hh in range(2):
            r = _dot(e[2 * hh * tq:2 * (hh + 1) * tq], v_ext[hh])
            o = (r[0:tq, 0:ATTN_V_DIM] / r[0:tq, ATTN_V_DIM:]
                 - lam_val * (r[tq:2 * tq, 0:ATTN_V_DIM] / r[tq:2 * tq, ATTN_V_DIM:]))
            o = _rmsnorm_rows(o, g_ref[...]) * (1.0 - lambda_init)
            o_ref[0, t * tq:(t + 1) * tq, 128 * hh:128 * hh + 128] = o.astype(o_ref.dtype)


def _attention(q, k, v, lam, subln_g, lambda_init):
    bsz, seq, _ = q.shape
    tq = ATTN_Q_TILE
    kv_spec = pl.BlockSpec((1, seq, 256), lambda b, p, i: (b, 0, p))
    return pl.pallas_call(
        functools.partial(_attn_kernel, lambda_init),
        grid=(bsz, 2, seq // tq),
        in_specs=[pl.BlockSpec((1, tq, 256), lambda b, p, i: (b, i, p)),
                  kv_spec, kv_spec,
                  _const_spec(lam.shape),
                  _const_spec(subln_g.shape)],
        out_specs=pl.BlockSpec((1, tq, 256), lambda b, p, i: (b, i, p)),
        out_shape=jax.ShapeDtypeStruct((bsz, seq, ATTN_WIDTH), BF16),
        compiler_params=_params(3),
        name="diff_attention",
    )(q, k, v, lam, subln_g)


def _s5_kernel(n_chunks, ut_ref, mi_ref, min_ref, mo_ref, ctab_ref, y_ref):
    n_rows = ut_ref.shape[2]
    n_steps = n_chunks.bit_length() - 1
    ut2 = ut_ref[...].reshape(2 * S5_CW, n_rows)
    st = _dot_tn(ut2, min_ref[0])
    ctab = ctab_ref[0]
    pos = lax.broadcasted_iota(jnp.int32, (n_rows, 128), 0) % n_chunks
    zs = []
    for d in range(2):
        xr = st[:, 256 * d:256 * d + 128]
        xi = st[:, 256 * d + 128:256 * d + 256]
        for k in range(n_steps):
            s = 1 << k
            if d == 0:
                shift, keep = s, pos >= s
            else:
                shift, keep = n_rows - s, pos < n_chunks - s
            pr = ctab[4 * k + 2 * d:4 * k + 2 * d + 1]
            pi = ctab[4 * k + 2 * d + 1:4 * k + 2 * d + 2]
            sr = jnp.where(keep, pltpu.roll(xr, shift, 0), 0.0)
            si = jnp.where(keep, pltpu.roll(xi, shift, 0), 0.0)
            xr, xi = xr + pr * sr - pi * si, xi + pr * si + pi * sr
        if d == 0:
            shift, keep = 1, pos >= 1
        else:
            shift, keep = n_rows - 1, pos < n_chunks - 1
        zs += [jnp.where(keep, pltpu.roll(xr, shift, 0), 0.0), jnp.where(keep, pltpu.roll(xi, shift, 0), 0.0)]
    z = jnp.concatenate(zs, axis=1).astype(BF16)
    for g in range(2):
        y_ref[g] = _dot(mi_ref[g], ut_ref[g]) + _dot_nt(mo_ref[g], z)


def _s5_chunked(u_t, m_intra_t, m_in_pair, m_out_ext, ctab, n_chunks):
    n_groups, _, n_lanes = u_t.shape
    return pl.pallas_call(
        functools.partial(_s5_kernel, n_chunks),
        grid=(n_groups // 2,),
        in_specs=[pl.BlockSpec((2, S5_CW, n_lanes), lambda i: (i, 0, 0)),
                  pl.BlockSpec((2, S5_CW, S5_CW), lambda i: (i, 0, 0)),
                  pl.BlockSpec((1, 2 * S5_CW, 512), lambda i: (i, 0, 0)),
                  pl.BlockSpec((2, S5_CW, 512), lambda i: (i, 0, 0)),
                  pl.BlockSpec((1, 32, 128), lambda i: (i, 0, 0))],
        out_specs=pl.BlockSpec((2, S5_CW, n_lanes), lambda i: (i, 0, 0)),
        out_shape=jax.ShapeDtypeStruct((n_groups, S5_CW, n_lanes), F32),
        compiler_params=_params(1),
        name="s5_chunked",
    )(u_t, m_intra_t, m_in_pair, m_out_ext, ctab)


def _dot_hp(a, b):
    return jnp.dot(a, b, precision=lax.Precision.HIGHEST, preferred_element_type=F32)


def _discretise(lr, li, log_step):
    lr = jnp.minimum(lr, -1e-4)
    dt = jnp.exp(log_step)
    mag = jnp.exp(lr * dt)
    ar = mag * jnp.cos(li * dt)
    ai = mag * jnp.sin(li * dt)
    den = lr * lr + li * li
    cr = ((ar - 1.0) * lr + ai * li) / den
    ci = (ai * lr - (ar - 1.0) * li) / den
    return lr * dt, li * dt, cr, ci


def _s5_ops_kernel(n_steps, colp_ref, rowp_ref, bcol_ref, crow_ref, dcol_ref,
                   mi_ref, min_ref, mo_ref, ctab_ref):
    t = S5_CHUNK
    n = S5_GROUP
    f32 = lambda m: jnp.where(m, 1.0, 0.0)
    i16 = lambda shape, dim: lax.broadcasted_iota(jnp.int32, shape, dim)
    rep_lanes = f32(i16((t, S5_CW), 1) // n == i16((t, S5_CW), 0))
    tile_lanes = f32(i16((n, S5_CW), 1) % n == i16((n, S5_CW), 0))
    rep_rows = f32(i16((S5_CW, t), 0) // n == i16((S5_CW, t), 1))
    tile_rows = f32(i16((S5_CW, n), 0) % n == i16((S5_CW, n), 1))
    row_group = i16((2 * S5_STATE, S5_CW), 0) // S5_STATE
    lane_group = i16((1, 2 * S5_STATE), 1) // S5_STATE
    lane16 = i16((1, t), 1).astype(F32)
    row16 = i16((t, 1), 0).astype(F32)
    colp = colp_ref[0]
    rowp = rowp_ref[0]

    taps = [[None, None], [None, None]]
    min_t_rows = []
    mo_kinds = []
    for d in range(2):
        lmag, ang, cr, ci = _discretise(colp[:, 3 * d:3 * d + 1], colp[:, 3 * d + 1:3 * d + 2],
                                        colp[:, 3 * d + 2:3 * d + 3])
        b_re, b_im = bcol_ref[0, 2 * d], bcol_ref[0, 2 * d + 1]
        bbr = _dot_hp(cr * b_re - ci * b_im, tile_lanes)
        bbi = _dot_hp(cr * b_im + ci * b_re, tile_lanes)
        lag = (t - 1.0) - lane16 if d == 0 else lane16
        pmag = jnp.exp(lmag * lag)
        pr = _dot_hp(pmag * jnp.cos(ang * lag), rep_lanes)
        pi = _dot_hp(pmag * jnp.sin(ang * lag), rep_lanes)
        rr = pr * bbr - pi * bbi
        ri = pr * bbi + pi * bbr
        for part in (rr, ri):
            min_t_rows.append(jnp.concatenate([jnp.where(row_group == gs, part, 0.0) for gs in range(2)], axis=1))
        c_re, c_im = crow_ref[0, 2 * d], crow_ref[0, 2 * d + 1]
        for gs in range(2):
            own = lane_group == gs
            taps[gs][d] = (_dot_hp(jnp.where(own, c_re, 0.0), rr) - _dot_hp(jnp.where(own, c_im, 0.0), ri))

        lmag_r, ang_r, _, _ = _discretise(rowp[3 * d:3 * d + 1], rowp[3 * d + 1:3 * d + 2],
                                          rowp[3 * d + 2:3 * d + 3])
        tau = row16 + 1.0 if d == 0 else float(t) - row16
        qmag = jnp.exp(tau * lmag_r)
        qr16 = qmag * jnp.cos(tau * ang_r)
        qi16 = qmag * jnp.sin(tau * ang_r)
        qr = _dot_hp(rep_rows, qr16)
        qi = _dot_hp(rep_rows, qi16)
        cre = _dot_hp(tile_rows, c_re)
        cim = _dot_hp(tile_rows, c_im)
        mo_kinds += [qr * cre - qi * cim, -(qr * cim + qi * cre)]

        far = t - 1 if d == 0 else 0
        pw_r, pw_i = qr16[far:far + 1], qi16[far:far + 1]
        for k in range(n_steps):
            ctab_ref[0, 4 * k + 2 * d:4 * k + 2 * d + 1, :] = pw_r
            ctab_ref[0, 4 * k + 2 * d + 1:4 * k + 2 * d + 2, :] = pw_i
            pw_r, pw_i = pw_r * pw_r - pw_i * pw_i, 2.0 * pw_r * pw_i
    ctab_ref[0, 4 * n_steps:, :] = jnp.zeros((32 - 4 * n_steps, 2 * S5_STATE), F32)

    min_ref[0] = jnp.concatenate(min_t_rows, axis=0).T.astype(BF16)
    lane_group_wide = i16((1, 2 * S5_STATE), 1) // S5_STATE
    for gs in range(2):
        mo_ref[gs] = jnp.concatenate([jnp.where(lane_group_wide == gs, kind, 0.0) for kind in mo_kinds],
                                     axis=1).astype(BF16)
        zeros = jnp.zeros((n, S5_CW), F32)
        edge = (t - 1) * n
        lane = i16((n, 2 * S5_CW), 1)
        skip = jnp.where(lane == edge + i16((n, 2 * S5_CW), 0), dcol_ref[0, gs], 0.0)
        kk = (jnp.concatenate([taps[gs][0], zeros], axis=1)
              + pltpu.roll(jnp.concatenate([taps[gs][1], zeros], axis=1), edge, 1) + skip)
        strips = []
        for j in range(t):
            shift = (t - 1 - j) * n
            moved = kk if shift == 0 else pltpu.roll(kk, 2 * S5_CW - shift, 1)
            strips.append(moved[:, 0:S5_CW])
        mi_ref[gs] = jnp.concatenate(strips, axis=0).astype(BF16)


def _s5_operators(lam_re, lam_im, log_step, b_re, b_im, c_re, c_im, d_skip, n_steps):
    g2 = S5_GROUPS // 2
    zero = jnp.zeros_like(lam_re[0])
    step = [jnp.broadcast_to(log_step[d][:, None], lam_re[d].shape) for d in range(2)]
    params = jnp.stack([lam_re[0], lam_im[0], step[0], lam_re[1], lam_im[1], step[1], zero, zero])
    colp = params.transpose(1, 2, 0).reshape(g2, 2 * S5_STATE, 8)
    rowp = params.reshape(8, g2, 2 * S5_STATE).transpose(1, 0, 2)
    bcol = (jnp.stack([b_re[0], b_im[0], b_re[1], b_im[1]])
            .reshape(4, g2, 2 * S5_STATE, S5_GROUP).transpose(1, 0, 2, 3))
    crow = (jnp.stack([c_re[0], c_im[0], c_re[1], c_im[1]])
            .reshape(4, g2, 2, S5_GROUP, S5_STATE).transpose(1, 0, 3, 2, 4).reshape(g2, 4, S5_GROUP, 2 * S5_STATE))
    dcol = d_skip.reshape(g2, 2, S5_GROUP, 1)
    pair = lambda *shape: pl.BlockSpec((1,) + shape, lambda i: (i,) + (0,) * len(shape))
    return pl.pallas_call(
        functools.partial(_s5_ops_kernel, n_steps),
        grid=(g2,),
        in_specs=[pair(2 * S5_STATE, 8), pair(8, 2 * S5_STATE), pair(4, 2 * S5_STATE, S5_GROUP),
                  pair(4, S5_GROUP, 2 * S5_STATE), pair(2, S5_GROUP, 1)],
        out_specs=[pl.BlockSpec((2, S5_CW, S5_CW), lambda i: (i, 0, 0)), pair(2 * S5_CW, 512),
                   pl.BlockSpec((2, S5_CW, 512), lambda i: (i, 0, 0)), pair(32, 2 * S5_STATE)],
        out_shape=[jax.ShapeDtypeStruct((S5_GROUPS, S5_CW, S5_CW), BF16),
                   jax.ShapeDtypeStruct((g2, 2 * S5_CW, 512), BF16),
                   jax.ShapeDtypeStruct((S5_GROUPS, S5_CW, 512), BF16),
                   jax.ShapeDtypeStruct((g2, 32, 2 * S5_STATE), F32)],
        compiler_params=_params(1),
        name="s5_operators",
    )(colp, rowp, bcol, crow, dcol)


def _mlp(x1, g_ref, w1_ref, w2_ref):
    h = _rmsnorm_rows(x1, g_ref[...]).astype(BF16)
    acc = x1
    for j in range(D_FF // FF_TILE):
        hid = _dot(h, w1_ref[:, j * FF_TILE:(j + 1) * FF_TILE])
        hid = jnp.square(jnp.maximum(hid, 0.0)).astype(BF16)
        acc = acc + _dot(hid, w2_ref[j * FF_TILE:(j + 1) * FF_TILE, :])
    return acc


def _post_even_kernel(x_ref, a_ref, b_ref, wout_ref, g_ref, w1_ref, w2_ref, o_ref):
    mix = (_dot(a_ref[...], wout_ref[0:ATTN_WIDTH, :])
           + _dot(b_ref[...].astype(BF16), wout_ref[ATTN_WIDTH:, :]))
    o_ref[...] = _mlp(x_ref[...] + mix, g_ref, w1_ref, w2_ref)


def _post_even(x2, a_out, b_out, w_out, g_mlp, w1, w2):
    n_tok = x2.shape[0]
    tm = TOKEN_TILE
    return pl.pallas_call(
        _post_even_kernel,
        grid=(n_tok // tm,),
        in_specs=[pl.BlockSpec((tm, D_MODEL), lambda i: (i, 0)),
                  pl.BlockSpec((tm, ATTN_WIDTH), lambda i: (i, 0)),
                  pl.BlockSpec((tm, S5_WIDTH), lambda i: (i, 0)),
                  _const_spec(w_out.shape), _const_spec(g_mlp.shape), _const_spec(w1.shape),
                  _const_spec(w2.shape)],
        out_specs=pl.BlockSpec((tm, D_MODEL), lambda i: (i, 0)),
        out_shape=jax.ShapeDtypeStruct((n_tok, D_MODEL), F32),
        compiler_params=_params(1),
        name="post_even",
    )(x2, a_out, b_out, w_out, g_mlp, w1, w2)


def _post_odd_kernel(x_ref, m_ref, wout_ref, g_ref, w1_ref, w2_ref, gf_ref, o_ref):
    x1 = x_ref[...] + _dot(m_ref[...], wout_ref[...])
    o_ref[...] = _rmsnorm_rows(_mlp(x1, g_ref, w1_ref, w2_ref), gf_ref[...])


def _post_odd(x2, mixed, w_out, g_mlp, w1, w2, g_final):
    n_tok = x2.shape[0]
    tm = TOKEN_TILE
    return pl.pallas_call(
        _post_odd_kernel,
        grid=(n_tok // tm,),
        in_specs=[pl.BlockSpec((tm, D_MODEL), lambda i: (i, 0)),
                  pl.BlockSpec((tm, D_MODEL), lambda i: (i, 0)),
                  _const_spec(w_out.shape), _const_spec(g_mlp.shape),
                  _const_spec(w1.shape), _const_spec(w2.shape), _const_spec(g_final.shape)],
        out_specs=pl.BlockSpec((tm, D_MODEL), lambda i: (i, 0)),
        out_shape=jax.ShapeDtypeStruct((n_tok, D_MODEL), F32),
        compiler_params=_params(1),
        name="post_odd",
    )(x2, mixed, w_out, g_mlp, w1, w2, g_final)


def _pre_odd_kernel(x_ref, g_ref, w_ref, q_ref, i_ref, ff_ref, fb_ref, gate_ref):
    h = _rmsnorm_rows(x_ref[...], g_ref[...]).astype(BF16)
    for s, out_ref in enumerate((q_ref, i_ref, ff_ref, fb_ref, gate_ref)):
        out_ref[...] = _dot(h, w_ref[:, s * D_MODEL:(s + 1) * D_MODEL]).astype(out_ref.dtype)


def _pre_odd(x2, g, w):
    n_tok = x2.shape[0]
    tm = TOKEN_TILE
    row_spec = pl.BlockSpec((tm, D_MODEL), lambda i: (i, 0))
    lo = jax.ShapeDtypeStruct((n_tok, D_MODEL), BF16)
    hi = jax.ShapeDtypeStruct((n_tok, D_MODEL), F32)
    return pl.pallas_call(
        _pre_odd_kernel,
        grid=(n_tok // tm,),
        in_specs=[row_spec, _const_spec((1, D_MODEL)), _const_spec(w.shape)],
        out_specs=[row_spec] * 5,
        out_shape=[lo, lo, hi, hi, lo],
        compiler_params=_params(1),
        name="pre_odd",
    )(x2, g, w)


def _chunk_cumprod(x, reverse):
    n = x.shape[0]
    pos = lax.broadcasted_iota(jnp.int32, x.shape, 0) % HGRN_CHUNK
    s = 1
    while s < HGRN_CHUNK:
        if reverse:
            shifted = pltpu.roll(x, n - s, 0)
            x = x * jnp.where(pos < HGRN_CHUNK - s, shifted, 1.0)
        else:
            shifted = pltpu.roll(x, s, 0)
            x = x * jnp.where(pos >= s, shifted, 1.0)
        s *= 2
    return x


def _hgrn_kernel(layer, q_ref, v_ref, ff_ref, fb_ref, gate_ref, lbl_ref, ng_ref, o_ref,
                 qd_ref, dec_ref, kv_ref, acc_ref):
    seq = q_ref.shape[1]
    ch = HGRN_CHUNK
    blk = HGRN_BLOCK
    n_chunks = seq // ch
    per_blk = blk // ch
    logits = lbl_ref[...]
    soft = jnp.exp(logits - jnp.max(logits, axis=0, keepdims=True))
    soft = soft / jnp.sum(soft, axis=0, keepdims=True)
    lb = jnp.sum(soft[0:layer + 1], axis=0, keepdims=True) - soft[0:1]

    c2 = 0.5 * (1.0 - lb)
    c1 = lb + c2
    row = lax.broadcasted_iota(jnp.int32, (blk, blk), 0)
    col = lax.broadcasted_iota(jnp.int32, (blk, blk), 1)
    u32 = lambda a: a.astype(jnp.uint32)
    keep = (u32(row - col) <= u32(row % ch), u32(col - row) <= u32(ch - 1 - row % ch))
    row_chunk = lax.broadcasted_iota(jnp.int32, (blk, 128), 0) // ch

    def intra(j, carry):
        sl = pl.ds(pl.multiple_of(j * blk, blk), blk)
        v = v_ref[0, sl, :]
        vt = v.astype(F32).T.astype(BF16)
        qf = q_ref[0, sl, :].astype(F32)
        for d, f_ref in enumerate((ff_ref, fb_ref)):
            ct = c2 * jnp.tanh(0.5 * f_ref[0, sl, :])
            eb = _chunk_cumprod(c1 + ct, reverse=(d == 1))
            k_inv = (c2 - ct) / eb
            eb3 = eb.reshape(per_blk, ch, 128)
            e_end = eb3[:, ch - 1:ch, :] if d == 0 else eb3[:, 0:1, :]
            dec_ref[d, pl.ds(j * per_blk, per_blk)] = e_end
            q_dec = (qf * eb).astype(BF16)
            qd_ref[d, sl, :] = q_dec
            k_dec = (k_inv.reshape(per_blk, ch, 128) * e_end).reshape(blk, 128).astype(BF16)

            s = _dot_nt(q_dec, k_inv.astype(BF16))
            p = jnp.where(keep[d], s, 0.0).astype(BF16)
            acc_ref[d, sl, :] = _dot(p, v)
            rhs = jnp.concatenate([jnp.where(row_chunk == cc, k_dec, jnp.zeros_like(k_dec))
                                   for cc in range(per_blk)], axis=1)
            kvs = _dot(vt, rhs)
            for cc in range(per_blk):
                kv_ref[d, j * per_blk + cc] = kvs[:, 128 * cc:128 * (cc + 1)]
        return carry

    lax.fori_loop(0, seq // blk, intra, 0, unroll=4)

    def finish(rows):
        o = _rmsnorm_rows(acc_ref[0, rows, :] + acc_ref[1, rows, :], ng_ref[...])
        gate = 0.5 + 0.5 * jnp.tanh(0.5 * gate_ref[0, rows, :].astype(F32))
        o_ref[0, rows, :] = (o * gate).astype(o_ref.dtype)

    states = [jnp.zeros((128, 128), F32)] * 2
    for n in range(n_chunks):
        for d in range(2):
            c = n if d == 0 else n_chunks - 1 - n
            rows = slice(c * ch, (c + 1) * ch)
            acc_ref[d, rows, :] += _dot_nt(qd_ref[d, rows, :], states[d].astype(BF16))
            states[d] = dec_ref[d, c] * states[d] + kv_ref[d, c]
        if 2 * n >= n_chunks:
            finish(slice(n * ch, (n + 1) * ch))
            finish(slice((n_chunks - 1 - n) * ch, (n_chunks - n) * ch))


def _hgrn(q, v, ff, fb, gate, lb_logits, norm_g, layer):
    bsz, seq, _ = q.shape
    n_chunks = seq // HGRN_CHUNK
    head_spec = pl.BlockSpec((1, seq, 128), lambda b, h: (b, 0, h))
    return pl.pallas_call(
        functools.partial(_hgrn_kernel, layer),
        grid=(bsz, HGRN_HEADS),
        in_specs=[head_spec] * 5 + [pl.BlockSpec((DEPTH, 128), lambda b, h: (0, h)),
                                    pl.BlockSpec((1, 128), lambda b, h: (0, h))],
        out_specs=head_spec,
        out_shape=jax.ShapeDtypeStruct((bsz, seq, D_MODEL), BF16),
        scratch_shapes=[
           pltpu.VMEM((2, seq, 128), BF16),
           pltpu.VMEM((2, n_chunks, 1, 128), F32),
           pltpu.VMEM((2, n_chunks, 128, 128), F32),
           pltpu.VMEM((2, seq, 128), F32)],
        compiler_params=_params(2),
        name="hgrn2",
    )(q, v, ff, fb, gate, lb_logits, norm_g)


def _rope_pair_tables(seq):
    inv = ROPE_THETA ** (-jnp.arange(0, ATTN_QK_DIM, 2, dtype=F32) / ATTN_QK_DIM)
    ang = jnp.arange(seq, dtype=F32)[:, None] * inv[None, :]
    return jnp.tile(jnp.cos(ang), (1, 4)), jnp.tile(jnp.sin(ang), (1, 4))


def _pair_layout_columns():
    idx = jnp.arange(ATTN_QK_WIDTH).reshape(2, 2, 2, 2, 32)
    return idx.transpose(0, 3, 1, 2, 4).reshape(-1)


def kernel(x, norm_mix_g, norm_mlp_g, final_norm_g, w_ff_in, w_ff_out, w_in_even, w_out_even, diff_lambda, diff_subln_g, s5_lam_re, s5_lam_im, s5_log_step, s5_b_re, s5_b_im, s5_c_re, s5_c_im, s5_d, s5_w_glu, s5_b_glu, w_in_odd, w_out_odd, hgrn_norm_g, hgrn_lb_logits):
    bsz, seq, _ = x.shape
    n_tok = bsz * seq
    n_chunks = seq // S5_CHUNK
    assert n_chunks & (n_chunks - 1) == 0 and n_chunks % 128 == 0
    x2 = x.reshape(n_tok, D_MODEL)
    cos, sin = _rope_pair_tables(seq)
    perm = _pair_layout_columns()

    for layer in range(DEPTH):
        g_mix = norm_mix_g[layer].reshape(1, D_MODEL)
        g_mlp = norm_mlp_g[layer].reshape(1, D_MODEL)
        w1 = w_ff_in[layer].astype(BF16)
        w2 = w_ff_out[layer].astype(BF16)
        if layer % 2 == 0:
            e = layer // 2
            w = w_in_even[e]
            w_qkv = jnp.concatenate([w[:, :ATTN_QK_WIDTH][:, perm],
                                     w[:, ATTN_QK_WIDTH:2 * ATTN_QK_WIDTH][:, perm],
                                     w[:, 2 * ATTN_QK_WIDTH:2 * ATTN_QK_WIDTH + ATTN_WIDTH]], axis=1).astype(BF16)
            wu = w[:, 2 * ATTN_QK_WIDTH + ATTN_WIDTH:].astype(BF16)
            q, k, v = _pre_even(x2, g_mix, w_qkv, cos, sin, seq)
            u_t = _s5_in(x2, g_mix, wu, bsz, seq)
            lambda_init = 0.8 - 0.6 * math.exp(-0.3 * layer)
            a_out = _attention(q.reshape(bsz, seq, -1), k.reshape(bsz, seq, -1), v.reshape(bsz, seq, -1),
                               diff_lambda[e], diff_subln_g[e].reshape(1, ATTN_V_DIM), lambda_init)
            ops = _s5_operators(s5_lam_re[e], s5_lam_im[e], s5_log_step[e], s5_b_re[e], s5_b_im[e],
                                s5_c_re[e], s5_c_im[e], s5_d[e], n_chunks.bit_length() - 1)
            y_t = _s5_chunked(u_t.reshape(S5_GROUPS, S5_CW, bsz * n_chunks), *ops, n_chunks)
            b_out = _s5_out(y_t.reshape(S5_GROUPS, S5_CHUNK, S5_GROUP, bsz * n_chunks),
                            s5_w_glu[e].T.astype(BF16), s5_b_glu[e].reshape(S5_WIDTH, 1), bsz, seq)
            x2 = _post_even(x2, a_out.reshape(n_tok, ATTN_WIDTH), b_out, w_out_even[e].astype(BF16),
                            g_mlp, w1, w2)
        else:
            o_i = layer // 2
            q, v, ff, fb, gate = _pre_odd(x2, g_mix, w_in_odd[o_i].astype(BF16))
            shp = (bsz, seq, D_MODEL)
            mixed = _hgrn(q.reshape(shp), v.reshape(shp), ff.reshape(shp), fb.reshape(shp), gate.reshape(shp),
                          hgrn_lb_logits, hgrn_norm_g[o_i].reshape(1, D_MODEL), layer)
            assert layer == DEPTH - 1, "the final norm is fused into the last (odd) layer's kernel"
            x2 = _post_odd(x2, mixed.reshape(n_tok, D_MODEL), w_out_odd[o_i].astype(BF16), g_mlp, w1, w2,
                           final_norm_g.reshape(1, D_MODEL))
    return x2.reshape(bsz, seq, D_MODEL)
```

```python
import functools
import math

import jax
import jax.numpy as jnp
from jax import lax
from jax.experimental import pallas as pl
from jax.experimental.pallas import tpu as pltpu

D_MODEL = 1024
DEPTH = 2
ATTN_HEADS = 4
ATTN_QK_DIM = 64
ATTN_V_DIM = 128
ATTN_QK_WIDTH = 512
ATTN_WIDTH = 512
ROPE_THETA = 10000.0
S5_WIDTH = 512
S5_GROUP = 16
S5_GROUPS = 32
S5_STATE = 64
HGRN_HEADS = 8
HGRN_CHUNK = 64
HGRN_BLOCK = 256
D_FF = 4096
EPS = 1e-6

S5_CHUNK = 16
S5_CW = S5_CHUNK * S5_GROUP

TOKEN_TILE = 512
ATTN_Q_TILE = 512
ATTN_Q_SUB = 512
FF_TILE = 1024
VMEM_LIMIT = 56 * 1024 * 1024

BF16 = jnp.bfloat16
F32 = jnp.float32


def _const_spec(shape):
    nd = len(shape)
    return pl.BlockSpec(shape, lambda *_: (0,) * nd, pipeline_mode=pl.Buffered(1))


def _params(n_axes):
    return pltpu.CompilerParams(dimension_semantics=("arbitrary",) * n_axes,
                                vmem_limit_bytes=VMEM_LIMIT)


def _rmsnorm_rows(x, g):
    ms = jnp.mean(x * x, axis=-1, keepdims=True)
    return x * lax.rsqrt(ms + EPS) * g


def _gelu_tanh(x):
    c = math.sqrt(2.0 / math.pi)
    return 0.5 * x * (1.0 + jnp.tanh(c * (x + 0.044715 * (x * x * x))))


def _dot(a, b):
    return jnp.dot(a, b, preferred_element_type=F32)


def _dot_nt(a, b):
    return lax.dot_general(a, b, (((1,), (1,)), ((), ())), preferred_element_type=F32)


def _dot_tn(a, b):
    return lax.dot_general(a, b, (((0,), (0,)), ((), ())), preferred_element_type=F32)


def _pre_even_kernel(x_ref, g_ref, w_ref, cos_ref, sin_ref, q_ref, k_ref, v_ref):
    h = _rmsnorm_rows(x_ref[...], g_ref[...]).astype(BF16)
    cos = cos_ref[...]
    sin = sin_ref[...]
    for out_ref, base, scale in ((q_ref, 0, ATTN_QK_DIM ** -0.5 * math.log2(math.e)),
                                 (k_ref, ATTN_QK_WIDTH, 1.0)):
        p = _dot(h, w_ref[:, base:base + ATTN_QK_WIDTH])
        for pair in range(2):
            lo = p[:, 256 * pair:256 * pair + 128]
            hi = p[:, 256 * pair + 128:256 * pair + 256]
            out_ref[:, 256 * pair:256 * pair + 128] = ((lo * cos - hi * sin) * scale).astype(BF16)
            out_ref[:, 256 * pair + 128:256 * pair + 256] = ((hi * cos + lo * sin) * scale).astype(BF16)
    v_ref[...] = _dot(h, w_ref[:, 2 * ATTN_QK_WIDTH:]).astype(BF16)


def _pre_even(x2, g, w_qkv, cos, sin, seq):
    n_tok = x2.shape[0]
    tm = TOKEN_TILE
    n_pos_blocks = seq // tm
    out = jax.ShapeDtypeStruct((n_tok, 512), BF16)
    row_spec = pl.BlockSpec((tm, 512), lambda i: (i, 0))
    rope_spec = pl.BlockSpec((tm, 128), lambda i: (i % n_pos_blocks, 0))
    return pl.pallas_call(
        _pre_even_kernel,
        grid=(n_tok // tm,),
        in_specs=[pl.BlockSpec((tm, D_MODEL), lambda i: (i, 0)), _const_spec((1, D_MODEL)),
                  _const_spec(w_qkv.shape), rope_spec, rope_spec],
        out_specs=[row_spec] * 3,
        out_shape=[out] * 3,
        compiler_params=_params(1),
        name="pre_even",
    )(x2, g, w_qkv, cos, sin)


def _s5_in_kernel(x_ref, g_ref, wu_ref, ut_ref, u_scr):
    nc = x_ref.shape[0] // S5_CHUNK
    groups_per_slab = 128 // S5_GROUP
    h = _rmsnorm_rows(x_ref[...], g_ref[...]).astype(BF16)
    u = _dot(h, wu_ref[...])
    for j in range(S5_WIDTH // 128):
        u_scr[j] = u[:, 128 * j:128 * (j + 1)]
    for j in range(S5_WIDTH // 128):
        for ph in range(S5_CHUNK):
            t = u_scr[j, pl.ds(ph, nc, stride=S5_CHUNK), :]
            ut_ref[groups_per_slab * j:groups_per_slab * (j + 1), ph, :, :] = (
                t.T.astype(BF16).reshape(groups_per_slab, S5_GROUP, nc))


def _s5_in(x2, g, wu, bsz, seq):
    nc = seq // S5_CHUNK
    return pl.pallas_call(
        _s5_in_kernel,
        grid=(bsz,),
        in_specs=[pl.BlockSpec((seq, D_MODEL), lambda b: (b, 0)), _const_spec((1, D_MODEL)),
                  _const_spec(wu.shape)],
        out_specs=pl.BlockSpec((S5_GROUPS, S5_CHUNK, S5_GROUP, nc), lambda b: (0, 0, 0, b)),
        out_shape=jax.ShapeDtypeStruct((S5_GROUPS, S5_CHUNK, S5_GROUP, bsz * nc), BF16),
        scratch_shapes=[pltpu.VMEM((S5_WIDTH // 128, seq, 128), F32)],
        compiler_params=_params(1),
        name="s5_in",
    )(x2, g, wu)


def _s5_out_kernel(yt_ref, wglut_ref, bglu_ref, o_ref, b_scr):
    nc = yt_ref.shape[3]
    yt = jnp.concatenate([yt_ref[:, ph, :, :].reshape(S5_WIDTH, nc) for ph in range(S5_CHUNK)], axis=1)
    yt = _gelu_tanh(yt)
    z = _dot(wglut_ref[...], yt.astype(BF16)) + bglu_ref[...]
    bt = yt * (0.5 + 0.5 * jnp.tanh(0.5 * z))
    for j in range(S5_WIDTH // 128):
        for ph in range(S5_CHUNK):
            b_scr[j, pl.ds(ph, nc, stride=S5_CHUNK), :] = bt[128 * j:128 * (j + 1), ph * nc:(ph + 1) * nc].T
    for j in range(S5_WIDTH // 128):
        o_ref[:, 128 * j:128 * (j + 1)] = b_scr[j]


def _s5_out(y_t, w_glu_t, b_glu_col, bsz, seq):
    nc = seq // S5_CHUNK
    return pl.pallas_call(
        _s5_out_kernel,
        grid=(bsz,),
        in_specs=[pl.BlockSpec((S5_GROUPS, S5_CHUNK, S5_GROUP, nc), lambda b: (0, 0, 0, b)),
                  _const_spec(w_glu_t.shape), _const_spec(b_glu_col.shape)],
        out_specs=pl.BlockSpec((seq, S5_WIDTH), lambda b: (b, 0)),
        out_shape=jax.ShapeDtypeStruct((bsz * seq, S5_WIDTH), F32),
        scratch_shapes=[pltpu.VMEM((S5_WIDTH // 128, seq, 128), F32)],
        compiler_params=_params(1),
        name="s5_out",
    )(y_t, w_glu_t, b_glu_col)


def _attn_kernel(lambda_init, n_weights, q_ref, k_ref, v_ref, lam_ref, g_ref, *refs):
    o_ref = refs[n_weights]
    for w_ref, w16_ref in zip(refs[:n_weights], refs[n_weights + 1:]):
        w16_ref[...] = w_ref[...].astype(BF16)
    k = k_ref[0]
    lam = lam_ref[...]
    lam_val = (jnp.exp(jnp.sum(lam[0:1] * lam[1:2], axis=-1, keepdims=True))
               - jnp.exp(jnp.sum(lam[2:3] * lam[3:4], axis=-1, keepdims=True)) + lambda_init)
    tq = ATTN_Q_SUB
    lane_group = (lax.broadcasted_iota(jnp.int32, (tq, 256), 1) // 32) % 4
    ones = jnp.ones((k.shape[0], ATTN_V_DIM), BF16)
    v_ext = [jnp.concatenate([v_ref[0, :, 128 * hh:128 * hh + 128], ones], axis=1) for hh in range(2)]
    for t in range(q_ref.shape[1] // tq):
        q = q_ref[0, t * tq:(t + 1) * tq, :]
        zero = jnp.zeros_like(q)
        q_all = jnp.concatenate([jnp.where(lane_group == hc, q, zero) for hc in range(4)], axis=0)
        s = _dot_nt(q_all, k)
        e = jnp.exp2(s - jnp.max(s, axis=-1, keepdims=True)).astype(BF16)
        for hh in range(2):
            r = _dot(e[2 * hh * tq:2 * (hh + 1) * tq], v_ext[hh])
            o = (r[0:tq, 0:ATTN_V_DIM] / r[0:tq, ATTN_V_DIM:]
                 - lam_val * (r[tq:2 * tq, 0:ATTN_V_DIM] / r[tq:2 * tq, ATTN_V_DIM:]))
            o = _rmsnorm_rows(o, g_ref[...]) * (1.0 - lambda_init)
            o_ref[0, t * tq:(t + 1) * tq, 128 * hh:128 * hh + 128] = o.astype(o_ref.dtype)


def _attention(q, k, v, lam, subln_g, lambda_init, later_weights):
    bsz, seq, _ = q.shape
    tq = ATTN_Q_TILE
    n_q = seq // tq
    n_steps = bsz * 2 * n_q
    kv_spec = pl.BlockSpec((1, seq, 256), lambda b, p, i: (b, 0, p))
    slab_specs = [pl.BlockSpec((w.shape[0] // n_steps, w.shape[1]), lambda b, p, i: ((b * 2 + p) * n_q + i, 0))
                  for w in later_weights]
    outs = pl.pallas_call(
        functools.partial(_attn_kernel, lambda_init, len(later_weights)),
        grid=(bsz, 2, n_q),
        in_specs=[pl.BlockSpec((1, tq, 256), lambda b, p, i: (b, i, p)),
                  kv_spec, kv_spec,
                  _const_spec(lam.shape),
                  _const_spec(subln_g.shape)] + slab_specs,
        out_specs=[pl.BlockSpec((1, tq, 256), lambda b, p, i: (b, i, p))] + slab_specs,
        out_shape=[jax.ShapeDtypeStruct((bsz, seq, ATTN_WIDTH), BF16)]
        + [jax.ShapeDtypeStruct(w.shape, BF16) for w in later_weights],
        compiler_params=_params(3),
        name="diff_attention",
    )(q, k, v, lam, subln_g, *later_weights)
    return outs[0], outs[1:]


def _s5_kernel(n_chunks, ut_ref, mi_ref, min_ref, mo_ref, ctab_ref, y_ref):
    n_rows = ut_ref.shape[2]
    n_steps = n_chunks.bit_length() - 1
    ut2 = ut_ref[...].reshape(2 * S5_CW, n_rows)
    st = _dot_tn(ut2, min_ref[0])
    ctab = ctab_ref[0]
    pos = lax.broadcasted_iota(jnp.int32, (n_rows, 128), 0) % n_chunks
    zs = []
    for d in range(2):
        xr = st[:, 256 * d:256 * d + 128]
        xi = st[:, 256 * d + 128:256 * d + 256]
        for k in range(n_steps):
            s = 1 << k
            if d == 0:
                shift, keep = s, pos >= s
            else:
                shift, keep = n_rows - s, pos < n_chunks - s
            pr = ctab[4 * k + 2 * d:4 * k + 2 * d + 1]
            pi = ctab[4 * k + 2 * d + 1:4 * k + 2 * d + 2]
            sr = jnp.where(keep, pltpu.roll(xr, shift, 0), 0.0)
            si = jnp.where(keep, pltpu.roll(xi, shift, 0), 0.0)
            xr, xi = xr + pr * sr - pi * si, xi + pr * si + pi * sr
        if d == 0:
            shift, keep = 1, pos >= 1
        else:
            shift, keep = n_rows - 1, pos < n_chunks - 1
        zs += [jnp.where(keep, pltpu.roll(xr, shift, 0), 0.0), jnp.where(keep, pltpu.roll(xi, shift, 0), 0.0)]
    z = jnp.concatenate(zs, axis=1).astype(BF16)
    for g in range(2):
        y_ref[g] = _dot(mi_ref[g], ut_ref[g]) + _dot_nt(mo_ref[g], z)


def _s5_chunked(u_t, m_intra_t, m_in_pair, m_out_ext, ctab, n_chunks):
    n_groups, _, n_lanes = u_t.shape
    return pl.pallas_call(
        functools.partial(_s5_kernel, n_chunks),
        grid=(n_groups // 2,),
        in_specs=[pl.BlockSpec((2, S5_CW, n_lanes), lambda i: (i, 0, 0)),
                  pl.BlockSpec((2, S5_CW, S5_CW), lambda i: (i, 0, 0)),
                  pl.BlockSpec((1, 2 * S5_CW, 512), lambda i: (i, 0, 0)),
                  pl.BlockSpec((2, S5_CW, 512), lambda i: (i, 0, 0)),
                  pl.BlockSpec((1, 32, 128), lambda i: (i, 0, 0))],
        out_specs=pl.BlockSpec((2, S5_CW, n_lanes), lambda i: (i, 0, 0)),
        out_shape=jax.ShapeDtypeStruct((n_groups, S5_CW, n_lanes), F32),
        compiler_params=_params(1),
        name="s5_chunked",
    )(u_t, m_intra_t, m_in_pair, m_out_ext, ctab)


def _dot_hp(a, b):
    return jnp.dot(a, b, precision=lax.Precision.HIGHEST, preferred_element_type=F32)


def _discretise(lr, li, log_step):
    lr = jnp.minimum(lr, -1e-4)
    dt = jnp.exp(log_step)
    mag = jnp.exp(lr * dt)
    ar = mag * jnp.cos(li * dt)
    ai = mag * jnp.sin(li * dt)
    den = lr * lr + li * li
    cr = ((ar - 1.0) * lr + ai * li) / den
    ci = (ai * lr - (ar - 1.0) * li) / den
    return lr * dt, li * dt, cr, ci


def _s5_ops_kernel(n_steps, colp_ref, rowp_ref, bcol_ref, crow_ref, dcol_ref,
                   mi_ref, min_ref, mo_ref, ctab_ref):
    t = S5_CHUNK
    n = S5_GROUP
    f32 = lambda m: jnp.where(m, 1.0, 0.0)
    i16 = lambda shape, dim: lax.broadcasted_iota(jnp.int32, shape, dim)
    rep_lanes = f32(i16((t, S5_CW), 1) // n == i16((t, S5_CW), 0))
    tile_lanes = f32(i16((n, S5_CW), 1) % n == i16((n, S5_CW), 0))
    rep_rows = f32(i16((S5_CW, t), 0) // n == i16((S5_CW, t), 1))
    tile_rows = f32(i16((S5_CW, n), 0) % n == i16((S5_CW, n), 1))
    row_group = i16((2 * S5_STATE, S5_CW), 0) // S5_STATE
    lane_group = i16((1, 2 * S5_STATE), 1) // S5_STATE
    lane16 = i16((1, t), 1).astype(F32)
    row16 = i16((t, 1), 0).astype(F32)
    colp = colp_ref[0]
    rowp = rowp_ref[0]

    taps = [[None, None], [None, None]]
    min_t_rows = []
    mo_kinds = []
    for d in range(2):
        lmag, ang, cr, ci = _discretise(colp[:, 3 * d:3 * d + 1], colp[:, 3 * d + 1:3 * d + 2],
                                        colp[:, 3 * d + 2:3 * d + 3])
        b_re, b_im = bcol_ref[0, 2 * d], bcol_ref[0, 2 * d + 1]
        bbr = _dot_hp(cr * b_re - ci * b_im, tile_lanes)
        bbi = _dot_hp(cr * b_im + ci * b_re, tile_lanes)
        lag = (t - 1.0) - lane16 if d == 0 else lane16
        pmag = jnp.exp(lmag * lag)
        pr = _dot_hp(pmag * jnp.cos(ang * lag), rep_lanes)
        pi = _dot_hp(pmag * jnp.sin(ang * lag), rep_lanes)
        rr = pr * bbr - pi * bbi
        ri = pr * bbi + pi * bbr
        for part in (rr, ri):
            min_t_rows.append(jnp.concatenate([jnp.where(row_group == gs, part, 0.0) for gs in range(2)], axis=1))
        c_re, c_im = crow_ref[0, 2 * d], crow_ref[0, 2 * d + 1]
        for gs in range(2):
            own = lane_group == gs
            taps[gs][d] = (_dot_hp(jnp.where(own, c_re, 0.0), rr) - _dot_hp(jnp.where(own, c_im, 0.0), ri))

        lmag_r, ang_r, _, _ = _discretise(rowp[3 * d:3 * d + 1], rowp[3 * d + 1:3 * d + 2],
                                          rowp[3 * d + 2:3 * d + 3])
        tau = row16 + 1.0 if d == 0 else float(t) - row16
        qmag = jnp.exp(tau * lmag_r)
        qr16 = qmag * jnp.cos(tau * ang_r)
        qi16 = qmag * jnp.sin(tau * ang_r)
        qr = _dot_hp(rep_rows, qr16)
        qi = _dot_hp(rep_rows, qi16)
        cre = _dot_hp(tile_rows, c_re)
        cim = _dot_hp(tile_rows, c_im)
        mo_kinds += [qr * cre - qi * cim, -(qr * cim + qi * cre)]

        far = t - 1 if d == 0 else 0
        pw_r, pw_i = qr16[far:far + 1], qi16[far:far + 1]
        for k in range(n_steps):
            ctab_ref[0, 4 * k + 2 * d:4 * k + 2 * d + 1, :] = pw_r
            ctab_ref[0, 4 * k + 2 * d + 1:4 * k + 2 * d + 2, :] = pw_i
            pw_r, pw_i = pw_r * pw_r - pw_i * pw_i, 2.0 * pw_r * pw_i
    ctab_ref[0, 4 * n_steps:, :] = jnp.zeros((32 - 4 * n_steps, 2 * S5_STATE), F32)

    min_ref[0] = jnp.concatenate(min_t_rows, axis=0).T.astype(BF16)
    lane_group_wide = i16((1, 2 * S5_STATE), 1) // S5_STATE
    for gs in range(2):
        mo_ref[gs] = jnp.concatenate([jnp.where(lane_group_wide == gs, kind, 0.0) for kind in mo_kinds],
                                     axis=1).astype(BF16)
        zeros = jnp.zeros((n, S5_CW), F32)
        edge = (t - 1) * n
        lane = i16((n, 2 * S5_CW), 1)
        skip = jnp.where(lane == edge + i16((n, 2 * S5_CW), 0), dcol_ref[0, gs], 0.0)
        kk = (jnp.concatenate([taps[gs][0], zeros], axis=1)
              + pltpu.roll(jnp.concatenate([taps[gs][1], zeros], axis=1), edge, 1) + skip)
        strips = []
        for j in range(t):
            shift = (t - 1 - j) * n
            moved = kk if shift == 0 else pltpu.roll(kk, 2 * S5_CW - shift, 1)
            strips.append(moved[:, 0:S5_CW])
        mi_ref[gs] = jnp.concatenate(strips, axis=0).astype(BF16)


def _s5_operators(lam_re, lam_im, log_step, b_re, b_im, c_re, c_im, d_skip, n_steps):
    g2 = S5_GROUPS // 2
    zero = jnp.zeros_like(lam_re[0])
    step = [jnp.broadcast_to(log_step[d][:, None], lam_re[d].shape) for d in range(2)]
    params = jnp.stack([lam_re[0], lam_im[0], step[0], lam_re[1], lam_im[1], step[1], zero, zero])
    colp = params.transpose(1, 2, 0).reshape(g2, 2 * S5_STATE, 8)
    rowp = params.reshape(8, g2, 2 * S5_STATE).transpose(1, 0, 2)
    bcol = (jnp.stack([b_re[0], b_im[0], b_re[1], b_im[1]])
            .reshape(4, g2, 2 * S5_STATE, S5_GROUP).transpose(1, 0, 2, 3))
    crow = (jnp.stack([c_re[0], c_im[0], c_re[1], c_im[1]])
            .reshape(4, g2, 2, S5_GROUP, S5_STATE).transpose(1, 0, 3, 2, 4).reshape(g2, 4, S5_GROUP, 2 * S5_STATE))
    dcol = d_skip.reshape(g2, 2, S5_GROUP, 1)
    pair = lambda *shape: pl.BlockSpec((1,) + shape, lambda i: (i,) + (0,) * len(shape))
    return pl.pallas_call(
        functools.partial(_s5_ops_kernel, n_steps),
        grid=(g2,),
        in_specs=[pair(2 * S5_STATE, 8), pair(8, 2 * S5_STATE), pair(4, 2 * S5_STATE, S5_GROUP),
                  pair(4, S5_GROUP, 2 * S5_STATE), pair(2, S5_GROUP, 1)],
        out_specs=[pl.BlockSpec((2, S5_CW, S5_CW), lambda i: (i, 0, 0)), pair(2 * S5_CW, 512),
                   pl.BlockSpec((2, S5_CW, 512), lambda i: (i, 0, 0)), pair(32, 2 * S5_STATE)],
        out_shape=[jax.ShapeDtypeStruct((S5_GROUPS, S5_CW, S5_CW), BF16),
                   jax.ShapeDtypeStruct((g2, 2 * S5_CW, 512), BF16),
                   jax.ShapeDtypeStruct((S5_GROUPS, S5_CW, 512), BF16),
                   jax.ShapeDtypeStruct((g2, 32, 2 * S5_STATE), F32)],
        compiler_params=_params(1),
        name="s5_operators",
    )(colp, rowp, bcol, crow, dcol)


def _mlp(x1, g_ref, w1_ref, w2_ref):
    h = _rmsnorm_rows(x1, g_ref[...]).astype(BF16)
    acc = x1
    for j in range(D_FF // FF_TILE):
        hid = _dot(h, w1_ref[:, j * FF_TILE:(j + 1) * FF_TILE])
        hid = jnp.square(jnp.maximum(hid, 0.0)).astype(BF16)
        acc = acc + _dot(hid, w2_ref[j * FF_TILE:(j + 1) * FF_TILE, :])
    return acc


def _post_even_kernel(x_ref, a_ref, b_ref, wout_ref, g_ref, w1_ref, w2_ref, o_ref):
    mix = (_dot(a_ref[...], wout_ref[0:ATTN_WIDTH, :])
           + _dot(b_ref[...].astype(BF16), wout_ref[ATTN_WIDTH:, :]))
    o_ref[...] = _mlp(x_ref[...] + mix, g_ref, w1_ref, w2_ref)


def _post_even(x2, a_out, b_out, w_out, g_mlp, w1, w2):
    n_tok = x2.shape[0]
    tm = TOKEN_TILE
    return pl.pallas_call(
        _post_even_kernel,
        grid=(n_tok // tm,),
        in_specs=[pl.BlockSpec((tm, D_MODEL), lambda i: (i, 0)),
                  pl.BlockSpec((tm, ATTN_WIDTH), lambda i: (i, 0)),
                  pl.BlockSpec((tm, S5_WIDTH), lambda i: (i, 0)),
                  _const_spec(w_out.shape), _const_spec(g_mlp.shape), _const_spec(w1.shape),
                  _const_spec(w2.shape)],
        out_specs=pl.BlockSpec((tm, D_MODEL), lambda i: (i, 0)),
        out_shape=jax.ShapeDtypeStruct((n_tok, D_MODEL), F32),
        compiler_params=_params(1),
        name="post_even",
    )(x2, a_out, b_out, w_out, g_mlp, w1, w2)


def _post_odd_kernel(x_ref, m_ref, wout_ref, g_ref, w1_ref, w2_ref, gf_ref, o_ref):
    x1 = x_ref[...] + _dot(m_ref[...], wout_ref[...])
    o_ref[...] = _rmsnorm_rows(_mlp(x1, g_ref, w1_ref, w2_ref), gf_ref[...])


def _post_odd(x2, mixed, w_out, g_mlp, w1, w2, g_final):
    n_tok = x2.shape[0]
    tm = TOKEN_TILE
    return pl.pallas_call(
        _post_odd_kernel,
        grid=(n_tok // tm,),
        in_specs=[pl.BlockSpec((tm, D_MODEL), lambda i: (i, 0)),
                  pl.BlockSpec((tm, D_MODEL), lambda i: (i, 0)),
                  _const_spec(w_out.shape), _const_spec(g_mlp.shape),
                  _const_spec(w1.shape), _const_spec(w2.shape), _const_spec(g_final.shape)],
        out_specs=pl.BlockSpec((tm, D_MODEL), lambda i: (i, 0)),
        out_shape=jax.ShapeDtypeStruct((n_tok, D_MODEL), F32),
        compiler_params=_params(1),
        name="post_odd",
    )(x2, mixed, w_out, g_mlp, w1, w2, g_final)


def _pre_odd_kernel(x_ref, g_ref, w_ref, q_ref, i_ref, ff_ref, fb_ref, gate_ref):
    h = _rmsnorm_rows(x_ref[...], g_ref[...]).astype(BF16)
    for s, out_ref in enumerate((q_ref, i_ref, ff_ref, fb_ref, gate_ref)):
        out_ref[...] = _dot(h, w_ref[:, s * D_MODEL:(s + 1) * D_MODEL]).astype(out_ref.dtype)


def _pre_odd(x2, g, w):
    n_tok = x2.shape[0]
    tm = TOKEN_TILE
    row_spec = pl.BlockSpec((tm, D_MODEL), lambda i: (i, 0))
    lo = jax.ShapeDtypeStruct((n_tok, D_MODEL), BF16)
    hi = jax.ShapeDtypeStruct((n_tok, D_MODEL), F32)
    return pl.pallas_call(
        _pre_odd_kernel,
        grid=(n_tok // tm,),
        in_specs=[row_spec, _const_spec((1, D_MODEL)), _const_spec(w.shape)],
        out_specs=[row_spec] * 5,
        out_shape=[lo, lo, hi, hi, lo],
        compiler_params=_params(1),
        name="pre_odd",
    )(x2, g, w)


def _chunk_cumprod(x, reverse):
    n = x.shape[0]
    pos = lax.broadcasted_iota(jnp.int32, x.shape, 0) % HGRN_CHUNK
    s = 1
    while s < HGRN_CHUNK:
        if reverse:
            shifted = pltpu.roll(x, n - s, 0)
            x = x * jnp.where(pos < HGRN_CHUNK - s, shifted, 1.0)
        else:
            shifted = pltpu.roll(x, s, 0)
            x = x * jnp.where(pos >= s, shifted, 1.0)
        s *= 2
    return x


def _hgrn_kernel(layer, q_ref, v_ref, ff_ref, fb_ref, gate_ref, lbl_ref, ng_ref, o_ref,
                 qd_ref, dec_ref, kv_ref, acc_ref):
    seq = q_ref.shape[1]
    ch = HGRN_CHUNK
    blk = HGRN_BLOCK
    n_chunks = seq // ch
    per_blk = blk // ch
    logits = lbl_ref[...]
    soft = jnp.exp(logits - jnp.max(logits, axis=0, keepdims=True))
    soft = soft / jnp.sum(soft, axis=0, keepdims=True)
    lb = jnp.sum(soft[0:layer + 1], axis=0, keepdims=True) - soft[0:1]

    c2 = 0.5 * (1.0 - lb)
    c1 = lb + c2
    row = lax.broadcasted_iota(jnp.int32, (blk, blk), 0)
    col = lax.broadcasted_iota(jnp.int32, (blk, blk), 1)
    u32 = lambda a: a.astype(jnp.uint32)
    keep = (u32(row - col) <= u32(row % ch), u32(col - row) <= u32(ch - 1 - row % ch))
    row_chunk = lax.broadcasted_iota(jnp.int32, (blk, 128), 0) // ch

    def intra(j, carry):
        sl = pl.ds(pl.multiple_of(j * blk, blk), blk)
        v = v_ref[0, sl, :]
        vt = v.astype(F32).T.astype(BF16)
        qf = q_ref[0, sl, :].astype(F32)
        for d, f_ref in enumerate((ff_ref, fb_ref)):
            ct = c2 * jnp.tanh(f_ref[0, sl, :])
            eb = _chunk_cumprod(c1 + ct, reverse=(d == 1))
            k_inv = (c2 - ct) / eb
            eb3 = eb.reshape(per_blk, ch, 128)
            e_end = eb3[:, ch - 1:ch, :] if d == 0 else eb3[:, 0:1, :]
            dec_ref[d, pl.ds(j * per_blk, per_blk)] = e_end
            q_dec = (qf * eb).astype(BF16)
            qd_ref[d, sl, :] = q_dec
            k_dec = (k_inv.reshape(per_blk, ch, 128) * e_end).reshape(blk, 128).astype(BF16)

            s = _dot_nt(q_dec, k_inv.astype(BF16))
            p = jnp.where(keep[d], s, 0.0).astype(BF16)
            acc_ref[d, sl, :] = _dot(p, v)
            rhs = jnp.concatenate([jnp.where(row_chunk == cc, k_dec, jnp.zeros_like(k_dec))
                                   for cc in range(per_blk)], axis=1)
            kvs = _dot(vt, rhs)
            for cc in range(per_blk):
                kv_ref[d, j * per_blk + cc] = kvs[:, 128 * cc:128 * (cc + 1)]
        return carry

    lax.fori_loop(0, seq // blk, intra, 0, unroll=4)

    def finish(rows):
        o = _rmsnorm_rows(acc_ref[0, rows, :] + acc_ref[1, rows, :], ng_ref[...])
        gate = 0.5 + 0.5 * jnp.tanh(gate_ref[0, rows, :].astype(F32))
        o_ref[0, rows, :] = (o * gate).astype(o_ref.dtype)

    states = [jnp.zeros((128, 128), F32)] * 2
    for n in range(n_chunks):
        for d in range(2):
            c = n if d == 0 else n_chunks - 1 - n
            rows = slice(c * ch, (c + 1) * ch)
            acc_ref[d, rows, :] += _dot_nt(qd_ref[d, rows, :], states[d].astype(BF16))
            states[d] = dec_ref[d, c] * states[d] + kv_ref[d, c]
        if 2 * n >= n_chunks:
            finish(slice(n * ch, (n + 1) * ch))
            finish(slice((n_chunks - 1 - n) * ch, (n_chunks - n) * ch))


def _hgrn(q, v, ff, fb, gate, lb_logits, norm_g, layer):
    bsz, seq, _ = q.shape
    n_chunks = seq // HGRN_CHUNK
    head_spec = pl.BlockSpec((1, seq, 128), lambda b, h: (b, 0, h))
    return pl.pallas_call(
        functools.partial(_hgrn_kernel, layer),
        grid=(bsz, HGRN_HEADS),
        in_specs=[head_spec] * 5 + [pl.BlockSpec((DEPTH, 128), lambda b, h: (0, h)),
                                    pl.BlockSpec((1, 128), lambda b, h: (0, h))],
        out_specs=head_spec,
        out_shape=jax.ShapeDtypeStruct((bsz, seq, D_MODEL), BF16),
        scratch_shapes=[
           pltpu.VMEM((2, seq, 128), BF16),
           pltpu.VMEM((2, n_chunks, 1, 128), F32),
           pltpu.VMEM((2, n_chunks, 128, 128), F32),
           pltpu.VMEM((2, seq, 128), F32)],
        compiler_params=_params(2),
        name="hgrn2",
    )(q, v, ff, fb, gate, lb_logits, norm_g)


def _rope_pair_tables(seq):
    inv = ROPE_THETA ** (-jnp.arange(0, ATTN_QK_DIM, 2, dtype=F32) / ATTN_QK_DIM)
    ang = jnp.arange(seq, dtype=F32)[:, None] * inv[None, :]
    return jnp.tile(jnp.cos(ang), (1, 4)), jnp.tile(jnp.sin(ang), (1, 4))


def _to_pair_layout(w):
    rows = w.shape[0]
    return w.reshape(rows, 2, 2, 2, 2, 32).transpose(0, 1, 4, 2, 3, 5).reshape(rows, ATTN_QK_WIDTH)


def kernel(x, norm_mix_g, norm_mlp_g, final_norm_g, w_ff_in, w_ff_out, w_in_even, w_out_even, diff_lambda, diff_subln_g, s5_lam_re, s5_lam_im, s5_log_step, s5_b_re, s5_b_im, s5_c_re, s5_c_im, s5_d, s5_w_glu, s5_b_glu, w_in_odd, w_out_odd, hgrn_norm_g, hgrn_lb_logits):
    bsz, seq, _ = x.shape
    n_tok = bsz * seq
    n_chunks = seq // S5_CHUNK
    assert n_chunks & (n_chunks - 1) == 0 and n_chunks % 128 == 0
    x2 = x.reshape(n_tok, D_MODEL)
    cos, sin = _rope_pair_tables(seq)

    assert DEPTH == 2, "an even (attention + S5) layer followed by an odd (HGRN2) layer"
    for layer in range(DEPTH):
        g_mix = norm_mix_g[layer].reshape(1, D_MODEL)
        g_mlp = norm_mlp_g[layer].reshape(1, D_MODEL)
        if layer % 2 == 0:
            e = layer // 2
            w = w_in_even[e]
            w_qkv = jnp.concatenate([_to_pair_layout(w[:, :ATTN_QK_WIDTH]),
                                     _to_pair_layout(w[:, ATTN_QK_WIDTH:2 * ATTN_QK_WIDTH]),
                                     w[:, 2 * ATTN_QK_WIDTH:2 * ATTN_QK_WIDTH + ATTN_WIDTH]], axis=1).astype(BF16)
            wu = w[:, 2 * ATTN_QK_WIDTH + ATTN_WIDTH:].astype(BF16)
            q, k, v = _pre_even(x2, g_mix, w_qkv, cos, sin, seq)
            u_t = _s5_in(x2, g_mix, wu, bsz, seq)
            lambda_init = 0.8 - 0.6 * math.exp(-0.3 * layer)
            a_out, w16 = _attention(q.reshape(bsz, seq, -1), k.reshape(bsz, seq, -1), v.reshape(bsz, seq, -1),
                                    diff_lambda[e], diff_subln_g[e].reshape(1, ATTN_V_DIM), lambda_init,
                                    [w_out_even[e], w_ff_in[layer], w_ff_out[layer],
                                     w_out_odd[e], w_ff_in[layer + 1], w_ff_out[layer + 1]])
            w_out_e, w1, w2, w_out_o, w1_next, w2_next = w16
            ops = _s5_operators(s5_lam_re[e], s5_lam_im[e], s5_log_step[e], s5_b_re[e], s5_b_im[e],
                                s5_c_re[e], s5_c_im[e], s5_d[e], n_chunks.bit_length() - 1)
            y_t = _s5_chunked(u_t.reshape(S5_GROUPS, S5_CW, bsz * n_chunks), *ops, n_chunks)
            b_out = _s5_out(y_t.reshape(S5_GROUPS, S5_CHUNK, S5_GROUP, bsz * n_chunks),
                            s5_w_glu[e].T.astype(BF16), s5_b_glu[e].reshape(S5_WIDTH, 1), bsz, seq)
            x2 = _post_even(x2, a_out.reshape(n_tok, ATTN_WIDTH), b_out, w_out_e, g_mlp, w1, w2)
        else:
            o_i = layer // 2
            w = w_in_odd[o_i]
            w = jnp.concatenate([w[:, :2 * D_MODEL], 0.5 * w[:, 2 * D_MODEL:]], axis=1).astype(BF16)
            q, v, ff, fb, gate = _pre_odd(x2, g_mix, w)
            shp = (bsz, seq, D_MODEL)
            mixed = _hgrn(q.reshape(shp), v.reshape(shp), ff.reshape(shp), fb.reshape(shp), gate.reshape(shp),
                          hgrn_lb_logits, hgrn_norm_g[o_i].reshape(1, D_MODEL), layer)
            x2 = _post_odd(x2, mixed.reshape(n_tok, D_MODEL), w_out_o, g_mlp, w1_next, w2_next,
                           final_norm_g.reshape(1, D_MODEL))
    return x2.reshape(bsz, seq, D_MODEL)
```

```python
import functools
import math

import jax
import jax.numpy as jnp
from jax import lax
from jax.experimental import pallas as pl
from jax.experimental.pallas import tpu as pltpu

D_MODEL = 1024
DEPTH = 2
ATTN_HEADS = 4
ATTN_QK_DIM = 64
ATTN_V_DIM = 128
ATTN_QK_WIDTH = 512
ATTN_WIDTH = 512
ROPE_THETA = 10000.0
S5_WIDTH = 512
S5_GROUP = 16
S5_GROUPS = 32
S5_STATE = 64
HGRN_HEADS = 8
HGRN_CHUNK = 64
HGRN_BLOCK = 256
D_FF = 4096
EPS = 1e-6

S5_CHUNK = 16
S5_CW = S5_CHUNK * S5_GROUP

TOKEN_TILE = 512
ATTN_Q_TILE = 512
ATTN_Q_SUB = 512
FF_TILE = 1024
VMEM_LIMIT = 56 * 1024 * 1024

BF16 = jnp.bfloat16
F32 = jnp.float32


def _const_spec(shape):
    nd = len(shape)
    return pl.BlockSpec(shape, lambda *_: (0,) * nd, pipeline_mode=pl.Buffered(1))


def _params(n_axes):
    return pltpu.CompilerParams(dimension_semantics=("arbitrary",) * n_axes,
                                vmem_limit_bytes=VMEM_LIMIT)


def _rmsnorm_rows(x, g):
    ms = jnp.mean(x * x, axis=-1, keepdims=True)
    return x * lax.rsqrt(ms + EPS) * g


def _gelu_tanh(x):
    c = math.sqrt(2.0 / math.pi)
    return 0.5 * x * (1.0 + jnp.tanh(c * (x + 0.044715 * (x * x * x))))


def _dot(a, b):
    return jnp.dot(a, b, preferred_element_type=F32)


def _dot_nt(a, b):
    return lax.dot_general(a, b, (((1,), (1,)), ((), ())), preferred_element_type=F32)


def _dot_tn(a, b):
    return lax.dot_general(a, b, (((0,), (0,)), ((), ())), preferred_element_type=F32)


def _pre_even_kernel(x_ref, g_ref, w_ref, cos_ref, sin_ref, q_ref, k_ref, v_ref):
    h = _rmsnorm_rows(x_ref[...], g_ref[...]).astype(BF16)
    cos = cos_ref[...]
    sin = sin_ref[...]
    for out_ref, base, scale in ((q_ref, 0, ATTN_QK_DIM ** -0.5 * math.log2(math.e)),
                                 (k_ref, ATTN_QK_WIDTH, 1.0)):
        p = _dot(h, w_ref[:, base:base + ATTN_QK_WIDTH])
        for pair in range(2):
            lo = p[:, 256 * pair:256 * pair + 128]
            hi = p[:, 256 * pair + 128:256 * pair + 256]
            out_ref[:, 256 * pair:256 * pair + 128] = ((lo * cos - hi * sin) * scale).astype(BF16)
            out_ref[:, 256 * pair + 128:256 * pair + 256] = ((hi * cos + lo * sin) * scale).astype(BF16)
    v_ref[...] = _dot(h, w_ref[:, 2 * ATTN_QK_WIDTH:]).astype(BF16)


def _pre_even(x2, g, w_qkv, cos, sin, seq):
    n_tok = x2.shape[0]
    tm = TOKEN_TILE
    n_pos_blocks = seq // tm
    out = jax.ShapeDtypeStruct((n_tok, 512), BF16)
    row_spec = pl.BlockSpec((tm, 512), lambda i: (i, 0))
    rope_spec = pl.BlockSpec((tm, 128), lambda i: (i % n_pos_blocks, 0))
    return pl.pallas_call(
        _pre_even_kernel,
        grid=(n_tok // tm,),
        in_specs=[pl.BlockSpec((tm, D_MODEL), lambda i: (i, 0)), _const_spec((1, D_MODEL)),
                  _const_spec(w_qkv.shape), rope_spec, rope_spec],
        out_specs=[row_spec] * 3,
        out_shape=[out] * 3,
        compiler_params=_params(1),
        name="pre_even",
    )(x2, g, w_qkv, cos, sin)


def _s5_in_kernel(x_ref, g_ref, wu_ref, ut_ref, u_scr):
    nc = x_ref.shape[0] // S5_CHUNK
    groups_per_slab = 128 // S5_GROUP
    h = _rmsnorm_rows(x_ref[...], g_ref[...]).astype(BF16)
    u = _dot(h, wu_ref[...])
    for j in range(S5_WIDTH // 128):
        u_scr[j] = u[:, 128 * j:128 * (j + 1)]
    for j in range(S5_WIDTH // 128):
        for ph in range(S5_CHUNK):
            t = u_scr[j, pl.ds(ph, nc, stride=S5_CHUNK), :]
            ut_ref[groups_per_slab * j:groups_per_slab * (j + 1), ph, :, :] = (
                t.T.astype(BF16).reshape(groups_per_slab, S5_GROUP, nc))


def _s5_in(x2, g, wu, bsz, seq):
    nc = seq // S5_CHUNK
    return pl.pallas_call(
        _s5_in_kernel,
        grid=(bsz,),
        in_specs=[pl.BlockSpec((seq, D_MODEL), lambda b: (b, 0)), _const_spec((1, D_MODEL)),
                  _const_spec(wu.shape)],
        out_specs=pl.BlockSpec((S5_GROUPS, S5_CHUNK, S5_GROUP, nc), lambda b: (0, 0, 0, b)),
        out_shape=jax.ShapeDtypeStruct((S5_GROUPS, S5_CHUNK, S5_GROUP, bsz * nc), BF16),
        scratch_shapes=[pltpu.VMEM((S5_WIDTH // 128, seq, 128), F32)],
        compiler_params=_params(1),
        name="s5_in",
    )(x2, g, wu)


def _s5_out_kernel(yt_ref, wglut_ref, bglu_ref, o_ref, b_scr):
    nc = yt_ref.shape[3]
    yt = jnp.concatenate([yt_ref[:, ph, :, :].reshape(S5_WIDTH, nc) for ph in range(S5_CHUNK)], axis=1)
    yt = _gelu_tanh(yt)
    z = _dot(wglut_ref[...], yt.astype(BF16)) + bglu_ref[...]
    bt = yt * (0.5 + 0.5 * jnp.tanh(0.5 * z))
    for j in range(S5_WIDTH // 128):
        for ph in range(S5_CHUNK):
            b_scr[j, pl.ds(ph, nc, stride=S5_CHUNK), :] = bt[128 * j:128 * (j + 1), ph * nc:(ph + 1) * nc].T
    for j in range(S5_WIDTH // 128):
        o_ref[:, 128 * j:128 * (j + 1)] = b_scr[j]


def _s5_out(y_t, w_glu_t, b_glu_col, bsz, seq):
    nc = seq // S5_CHUNK
    return pl.pallas_call(
        _s5_out_kernel,
        grid=(bsz,),
        in_specs=[pl.BlockSpec((S5_GROUPS, S5_CHUNK, S5_GROUP, nc), lambda b: (0, 0, 0, b)),
                  _const_spec(w_glu_t.shape), _const_spec(b_glu_col.shape)],
        out_specs=pl.BlockSpec((seq, S5_WIDTH), lambda b: (b, 0)),
        out_shape=jax.ShapeDtypeStruct((bsz * seq, S5_WIDTH), F32),
        scratch_shapes=[pltpu.VMEM((S5_WIDTH // 128, seq, 128), F32)],
        compiler_params=_params(1),
        name="s5_out",
    )(y_t, w_glu_t, b_glu_col)


def _attn_kernel(lambda_init, halve_from, q_ref, k_ref, v_ref, lam_ref, g_ref, *refs):
    n_weights = len(halve_from)
    o_ref = refs[n_weights]
    for w_ref, w16_ref, col0 in zip(refs[:n_weights], refs[n_weights + 1:], halve_from):
        w = w_ref[...]
        if col0 is not None:
            w = w * jnp.where(lax.broadcasted_iota(jnp.int32, w.shape, 1) >= col0, 0.5, 1.0)
        w16_ref[...] = w.astype(BF16)
    k = k_ref[0]
    lam = lam_ref[...]
    lam_val = (jnp.exp(jnp.sum(lam[0:1] * lam[1:2], axis=-1, keepdims=True))
               - jnp.exp(jnp.sum(lam[2:3] * lam[3:4], axis=-1, keepdims=True)) + lambda_init)
    tq = ATTN_Q_SUB
    lane_group = (lax.broadcasted_iota(jnp.int32, (tq, 256), 1) // 32) % 4
    ones = jnp.ones((k.shape[0], ATTN_V_DIM), BF16)
    v_ext = [jnp.concatenate([v_ref[0, :, 128 * hh:128 * hh + 128], ones], axis=1) for hh in range(2)]
    for t in range(q_ref.shape[1] // tq):
        q = q_ref[0, t * tq:(t + 1) * tq, :]
        zero = jnp.zeros_like(q)
        q_all = jnp.concatenate([jnp.where(lane_group == hc, q, zero) for hc in range(4)], axis=0)
        s = _dot_nt(q_all, k)
        e = jnp.exp2(s - jnp.max(s, axis=-1, keepdims=True)).astype(BF16)
        for hh in range(2):
            r = _dot(e[2 * hh * tq:2 * (hh + 1) * tq], v_ext[hh])
            o = (r[0:tq, 0:ATTN_V_DIM] / r[0:tq, ATTN_V_DIM:]
                 - lam_val * (r[tq:2 * tq, 0:ATTN_V_DIM] / r[tq:2 * tq, ATTN_V_DIM:]))
            o = _rmsnorm_rows(o, g_ref[...]) * (1.0 - lambda_init)
            o_ref[0, t * tq:(t + 1) * tq, 128 * hh:128 * hh + 128] = o.astype(o_ref.dtype)


def _attention(q, k, v, lam, subln_g, lambda_init, later_weights, halve_from):
    bsz, seq, _ = q.shape
    tq = ATTN_Q_TILE
    n_q = seq // tq
    n_steps = bsz * 2 * n_q
    kv_spec = pl.BlockSpec((1, seq, 256), lambda b, p, i: (b, 0, p))
    slab_specs = [pl.BlockSpec((w.shape[0] // n_steps, w.shape[1]), lambda b, p, i: ((b * 2 + p) * n_q + i, 0))
                  for w in later_weights]
    outs = pl.pallas_call(
        functools.partial(_attn_kernel, lambda_init, tuple(halve_from)),
        grid=(bsz, 2, n_q),
        in_specs=[pl.BlockSpec((1, tq, 256), lambda b, p, i: (b, i, p)),
                  kv_spec, kv_spec,
                  _const_spec(lam.shape),
                  _const_spec(subln_g.shape)] + slab_specs,
        out_specs=[pl.BlockSpec((1, tq, 256), lambda b, p, i: (b, i, p))] + slab_specs,
        out_shape=[jax.ShapeDtypeStruct((bsz, seq, ATTN_WIDTH), BF16)]
        + [jax.ShapeDtypeStruct(w.shape, BF16) for w in later_weights],
        compiler_params=_params(3),
        name="diff_attention",
    )(q, k, v, lam, subln_g, *later_weights)
    return outs[0], outs[1:]


def _s5_kernel(n_chunks, ut_ref, mi_ref, min_ref, mo_ref, ctab_ref, y_ref):
    n_rows = ut_ref.shape[2]
    n_steps = n_chunks.bit_length() - 1
    ut2 = ut_ref[...].reshape(2 * S5_CW, n_rows)
    st = _dot_tn(ut2, min_ref[0])
    ctab = ctab_ref[0]
    pos = lax.broadcasted_iota(jnp.int32, (n_rows, 128), 0) % n_chunks
    zs = []
    for d in range(2):
        xr = st[:, 256 * d:256 * d + 128]
        xi = st[:, 256 * d + 128:256 * d + 256]
        for k in range(n_steps):
            s = 1 << k
            if d == 0:
                shift, keep = s, pos >= s
            else:
                shift, keep = n_rows - s, pos < n_chunks - s
            pr = ctab[4 * k + 2 * d:4 * k + 2 * d + 1]
            pi = ctab[4 * k + 2 * d + 1:4 * k + 2 * d + 2]
            sr = jnp.where(keep, pltpu.roll(xr, shift, 0), 0.0)
            si = jnp.where(keep, pltpu.roll(xi, shift, 0), 0.0)
            xr, xi = xr + pr * sr - pi * si, xi + pr * si + pi * sr
        if d == 0:
            shift, keep = 1, pos >= 1
        else:
            shift, keep = n_rows - 1, pos < n_chunks - 1
        zs += [jnp.where(keep, pltpu.roll(xr, shift, 0), 0.0), jnp.where(keep, pltpu.roll(xi, shift, 0), 0.0)]
    z = jnp.concatenate(zs, axis=1).astype(BF16)
    for g in range(2):
        y_ref[g] = _dot(mi_ref[g], ut_ref[g]) + _dot_nt(mo_ref[g], z)


def _s5_chunked(u_t, m_intra_t, m_in_pair, m_out_ext, ctab, n_chunks):
    n_groups, _, n_lanes = u_t.shape
    return pl.pallas_call(
        functools.partial(_s5_kernel, n_chunks),
        grid=(n_groups // 2,),
        in_specs=[pl.BlockSpec((2, S5_CW, n_lanes), lambda i: (i, 0, 0)),
                  pl.BlockSpec((2, S5_CW, S5_CW), lambda i: (i, 0, 0)),
                  pl.BlockSpec((1, 2 * S5_CW, 512), lambda i: (i, 0, 0)),
                  pl.BlockSpec((2, S5_CW, 512), lambda i: (i, 0, 0)),
                  pl.BlockSpec((1, 32, 128), lambda i: (i, 0, 0))],
        out_specs=pl.BlockSpec((2, S5_CW, n_lanes), lambda i: (i, 0, 0)),
        out_shape=jax.ShapeDtypeStruct((n_groups, S5_CW, n_lanes), F32),
        compiler_params=_params(1),
        name="s5_chunked",
    )(u_t, m_intra_t, m_in_pair, m_out_ext, ctab)


def _dot_hp(a, b):
    return jnp.dot(a, b, precision=lax.Precision.HIGHEST, preferred_element_type=F32)


def _bf16_terms(x):
    hi = x.astype(BF16)
    r = x - hi.astype(F32)
    mid = r.astype(BF16)
    return hi, mid, (r - mid.astype(F32)).astype(BF16)


def _spread(values, onehot):
    return sum(_dot(term, onehot) for term in _bf16_terms(values))


def _spread_rows(onehot, values):
    return sum(_dot(onehot, term) for term in _bf16_terms(values))


def _discretise(lr, li, log_step):
    lr = jnp.minimum(lr, -1e-4)
    dt = jnp.exp(log_step)
    mag = jnp.exp(lr * dt)
    ar = mag * jnp.cos(li * dt)
    ai = mag * jnp.sin(li * dt)
    den = lr * lr + li * li
    cr = ((ar - 1.0) * lr + ai * li) / den
    ci = (ai * lr - (ar - 1.0) * li) / den
    return lr * dt, li * dt, cr, ci


def _s5_ops_kernel(n_steps, colp_ref, rowp_ref, bcol_ref, crow_ref, dcol_ref,
                   mi_ref, min_ref, mo_ref, ctab_ref):
    t = S5_CHUNK
    n = S5_GROUP
    onehot = lambda m: jnp.where(m, 1.0, 0.0).astype(BF16)
    i16 = lambda shape, dim: lax.broadcasted_iota(jnp.int32, shape, dim)
    rep_lanes = onehot(i16((t, S5_CW), 1) // n == i16((t, S5_CW), 0))
    tile_lanes = onehot(i16((n, S5_CW), 1) % n == i16((n, S5_CW), 0))
    rep_rows = onehot(i16((S5_CW, t), 0) // n == i16((S5_CW, t), 1))
    tile_rows = onehot(i16((S5_CW, n), 0) % n == i16((S5_CW, n), 1))
    row_group = i16((2 * S5_STATE, S5_CW), 0) // S5_STATE
    lane_group = i16((1, 2 * S5_STATE), 1) // S5_STATE
    lane16 = i16((1, t), 1).astype(F32)
    row16 = i16((t, 1), 0).astype(F32)
    colp = colp_ref[0]
    rowp = rowp_ref[0]

    taps = [[None, None], [None, None]]
    min_t_rows = []
    mo_kinds = []
    for d in range(2):
        lmag, ang, cr, ci = _discretise(colp[:, 3 * d:3 * d + 1], colp[:, 3 * d + 1:3 * d + 2],
                                        colp[:, 3 * d + 2:3 * d + 3])
        b_re, b_im = bcol_ref[0, 2 * d], bcol_ref[0, 2 * d + 1]
        bbr = _spread(cr * b_re - ci * b_im, tile_lanes)
        bbi = _spread(cr * b_im + ci * b_re, tile_lanes)
        lag = (t - 1.0) - lane16 if d == 0 else lane16
        pmag = jnp.exp(lmag * lag)
        pr = _spread(pmag * jnp.cos(ang * lag), rep_lanes)
        pi = _spread(pmag * jnp.sin(ang * lag), rep_lanes)
        rr = pr * bbr - pi * bbi
        ri = pr * bbi + pi * bbr
        for part in (rr, ri):
            min_t_rows.append(jnp.concatenate([jnp.where(row_group == gs, part, 0.0) for gs in range(2)], axis=1))
        c_re, c_im = crow_ref[0, 2 * d], crow_ref[0, 2 * d + 1]
        for gs in range(2):
            own = lane_group == gs
            taps[gs][d] = (_dot_hp(jnp.where(own, c_re, 0.0), rr) - _dot_hp(jnp.where(own, c_im, 0.0), ri))

        lmag_r, ang_r, _, _ = _discretise(rowp[3 * d:3 * d + 1], rowp[3 * d + 1:3 * d + 2],
                                          rowp[3 * d + 2:3 * d + 3])
        tau = row16 + 1.0 if d == 0 else float(t) - row16
        qmag = jnp.exp(tau * lmag_r)
        qr16 = qmag * jnp.cos(tau * ang_r)
        qi16 = qmag * jnp.sin(tau * ang_r)
        qr = _spread_rows(rep_rows, qr16)
        qi = _spread_rows(rep_rows, qi16)
        cre = _spread_rows(tile_rows, c_re)
        cim = _spread_rows(tile_rows, c_im)
        mo_kinds += [qr * cre - qi * cim, -(qr * cim + qi * cre)]

        far = t - 1 if d == 0 else 0
        pw_r, pw_i = qr16[far:far + 1], qi16[far:far + 1]
        for k in range(n_steps):
            ctab_ref[0, 4 * k + 2 * d:4 * k + 2 * d + 1, :] = pw_r
            ctab_ref[0, 4 * k + 2 * d + 1:4 * k + 2 * d + 2, :] = pw_i
            pw_r, pw_i = pw_r * pw_r - pw_i * pw_i, 2.0 * pw_r * pw_i
    ctab_ref[0, 4 * n_steps:, :] = jnp.zeros((32 - 4 * n_steps, 2 * S5_STATE), F32)

    min_ref[0] = jnp.concatenate(min_t_rows, axis=0).T.astype(BF16)
    lane_group_wide = i16((1, 2 * S5_STATE), 1) // S5_STATE
    for gs in range(2):
        mo_ref[gs] = jnp.concatenate([jnp.where(lane_group_wide == gs, kind, 0.0) for kind in mo_kinds],
                                     axis=1).astype(BF16)
        zeros = jnp.zeros((n, S5_CW), F32)
        edge = (t - 1) * n
        lane = i16((n, 2 * S5_CW), 1)
        skip = jnp.where(lane == edge + i16((n, 2 * S5_CW), 0), dcol_ref[0, gs], 0.0)
        kk = (jnp.concatenate([taps[gs][0], zeros], axis=1)
              + pltpu.roll(jnp.concatenate([taps[gs][1], zeros], axis=1), edge, 1) + skip)
        strips = []
        for j in range(t):
            shift = (t - 1 - j) * n
            moved = kk if shift == 0 else pltpu.roll(kk, 2 * S5_CW - shift, 1)
            strips.append(moved[:, 0:S5_CW])
        mi_ref[gs] = jnp.concatenate(strips, axis=0).astype(BF16)


def _s5_operators(lam_re, lam_im, log_step, b_re, b_im, c_re, c_im, d_skip, n_steps):
    g2 = S5_GROUPS // 2
    zero = jnp.zeros_like(lam_re[0])
    step = [jnp.broadcast_to(log_step[d][:, None], lam_re[d].shape) for d in range(2)]
    params = jnp.stack([lam_re[0], lam_im[0], step[0], lam_re[1], lam_im[1], step[1], zero, zero])
    colp = params.transpose(1, 2, 0).reshape(g2, 2 * S5_STATE, 8)
    rowp = params.reshape(8, g2, 2 * S5_STATE).transpose(1, 0, 2)
    bcol = (jnp.stack([b_re[0], b_im[0], b_re[1], b_im[1]])
            .reshape(4, g2, 2 * S5_STATE, S5_GROUP).transpose(1, 0, 2, 3))
    crow = (jnp.stack([c_re[0], c_im[0], c_re[1], c_im[1]])
            .reshape(4, g2, 2, S5_GROUP, S5_STATE).transpose(1, 0, 3, 2, 4).reshape(g2, 4, S5_GROUP, 2 * S5_STATE))
    dcol = d_skip.reshape(g2, 2, S5_GROUP, 1)
    pair = lambda *shape: pl.BlockSpec((1,) + shape, lambda i: (i,) + (0,) * len(shape))
    return pl.pallas_call(
        functools.partial(_s5_ops_kernel, n_steps),
        grid=(g2,),
        in_specs=[pair(2 * S5_STATE, 8), pair(8, 2 * S5_STATE), pair(4, 2 * S5_STATE, S5_GROUP),
                  pair(4, S5_GROUP, 2 * S5_STATE), pair(2, S5_GROUP, 1)],
        out_specs=[pl.BlockSpec((2, S5_CW, S5_CW), lambda i: (i, 0, 0)), pair(2 * S5_CW, 512),
                   pl.BlockSpec((2, S5_CW, 512), lambda i: (i, 0, 0)), pair(32, 2 * S5_STATE)],
        out_shape=[jax.ShapeDtypeStruct((S5_GROUPS, S5_CW, S5_CW), BF16),
                   jax.ShapeDtypeStruct((g2, 2 * S5_CW, 512), BF16),
                   jax.ShapeDtypeStruct((S5_GROUPS, S5_CW, 512), BF16),
                   jax.ShapeDtypeStruct((g2, 32, 2 * S5_STATE), F32)],
        compiler_params=_params(1),
        name="s5_operators",
    )(colp, rowp, bcol, crow, dcol)


def _mlp(x1, g_ref, w1_ref, w2_ref):
    h = _rmsnorm_rows(x1, g_ref[...]).astype(BF16)
    acc = x1
    for j in range(D_FF // FF_TILE):
        hid = _dot(h, w1_ref[:, j * FF_TILE:(j + 1) * FF_TILE])
        hid = jnp.square(jnp.maximum(hid, 0.0)).astype(BF16)
        acc = acc + _dot(hid, w2_ref[j * FF_TILE:(j + 1) * FF_TILE, :])
    return acc


def _post_even_kernel(x_ref, a_ref, b_ref, wout_ref, g_ref, w1_ref, w2_ref, o_ref):
    mix = (_dot(a_ref[...], wout_ref[0:ATTN_WIDTH, :])
           + _dot(b_ref[...].astype(BF16), wout_ref[ATTN_WIDTH:, :]))
    o_ref[...] = _mlp(x_ref[...] + mix, g_ref, w1_ref, w2_ref)


def _post_even(x2, a_out, b_out, w_out, g_mlp, w1, w2):
    n_tok = x2.shape[0]
    tm = TOKEN_TILE
    return pl.pallas_call(
        _post_even_kernel,
        grid=(n_tok // tm,),
        in_specs=[pl.BlockSpec((tm, D_MODEL), lambda i: (i, 0)),
                  pl.BlockSpec((tm, ATTN_WIDTH), lambda i: (i, 0)),
                  pl.BlockSpec((tm, S5_WIDTH), lambda i: (i, 0)),
                  _const_spec(w_out.shape), _const_spec(g_mlp.shape), _const_spec(w1.shape),
                  _const_spec(w2.shape)],
        out_specs=pl.BlockSpec((tm, D_MODEL), lambda i: (i, 0)),
        out_shape=jax.ShapeDtypeStruct((n_tok, D_MODEL), F32),
        compiler_params=_params(1),
        name="post_even",
    )(x2, a_out, b_out, w_out, g_mlp, w1, w2)


def _post_odd_kernel(x_ref, m_ref, wout_ref, g_ref, w1_ref, w2_ref, gf_ref, o_ref):
    x1 = x_ref[...] + _dot(m_ref[...], wout_ref[...])
    o_ref[...] = _rmsnorm_rows(_mlp(x1, g_ref, w1_ref, w2_ref), gf_ref[...])


def _post_odd(x2, mixed, w_out, g_mlp, w1, w2, g_final):
    n_tok = x2.shape[0]
    tm = TOKEN_TILE
    return pl.pallas_call(
        _post_odd_kernel,
        grid=(n_tok // tm,),
        in_specs=[pl.BlockSpec((tm, D_MODEL), lambda i: (i, 0)),
                  pl.BlockSpec((tm, D_MODEL), lambda i: (i, 0)),
                  _const_spec(w_out.shape), _const_spec(g_mlp.shape),
                  _const_spec(w1.shape), _const_spec(w2.shape), _const_spec(g_final.shape)],
        out_specs=pl.BlockSpec((tm, D_MODEL), lambda i: (i, 0)),
        out_shape=jax.ShapeDtypeStruct((n_tok, D_MODEL), F32),
        compiler_params=_params(1),
        name="post_odd",
    )(x2, mixed, w_out, g_mlp, w1, w2, g_final)


def _pre_odd_kernel(x_ref, g_ref, w_ref, q_ref, i_ref, ff_ref, fb_ref, gate_ref):
    h = _rmsnorm_rows(x_ref[...], g_ref[...]).astype(BF16)
    for s, out_ref in enumerate((q_ref, i_ref, ff_ref, fb_ref, gate_ref)):
        out_ref[...] = _dot(h, w_ref[:, s * D_MODEL:(s + 1) * D_MODEL]).astype(out_ref.dtype)


def _pre_odd(x2, g, w):
    n_tok = x2.shape[0]
    tm = TOKEN_TILE
    row_spec = pl.BlockSpec((tm, D_MODEL), lambda i: (i, 0))
    lo = jax.ShapeDtypeStruct((n_tok, D_MODEL), BF16)
    hi = jax.ShapeDtypeStruct((n_tok, D_MODEL), F32)
    return pl.pallas_call(
        _pre_odd_kernel,
        grid=(n_tok // tm,),
        in_specs=[row_spec, _const_spec((1, D_MODEL)), _const_spec(w.shape)],
        out_specs=[row_spec] * 5,
        out_shape=[lo, lo, hi, hi, lo],
        compiler_params=_params(1),
        name="pre_odd",
    )(x2, g, w)


def _chunk_cumprod(x, reverse):
    n = x.shape[0]
    pos = lax.broadcasted_iota(jnp.int32, x.shape, 0) % HGRN_CHUNK
    s = 1
    while s < HGRN_CHUNK:
        if reverse:
            shifted = pltpu.roll(x, n - s, 0)
            x = x * jnp.where(pos < HGRN_CHUNK - s, shifted, 1.0)
        else:
            shifted = pltpu.roll(x, s, 0)
            x = x * jnp.where(pos >= s, shifted, 1.0)
        s *= 2
    return x


def _hgrn_kernel(layer, q_ref, v_ref, ff_ref, fb_ref, gate_ref, lbl_ref, ng_ref, o_ref,
                 qd_ref, dec_ref, kv_ref, acc_ref):
    seq = q_ref.shape[1]
    ch = HGRN_CHUNK
    blk = HGRN_BLOCK
    n_chunks = seq // ch
    per_blk = blk // ch
    logits = lbl_ref[...]
    soft = jnp.exp(logits - jnp.max(logits, axis=0, keepdims=True))
    soft = soft / jnp.sum(soft, axis=0, keepdims=True)
    lb = jnp.sum(soft[0:layer + 1], axis=0, keepdims=True) - soft[0:1]

    c2 = 0.5 * (1.0 - lb)
    c1 = lb + c2
    row = lax.broadcasted_iota(jnp.int32, (blk, blk), 0)
    col = lax.broadcasted_iota(jnp.int32, (blk, blk), 1)
    u32 = lambda a: a.astype(jnp.uint32)
    keep = (u32(row - col) <= u32(row % ch), u32(col - row) <= u32(ch - 1 - row % ch))
    row_chunk = lax.broadcasted_iota(jnp.int32, (blk, 128), 0) // ch

    def intra(j, carry):
        sl = pl.ds(pl.multiple_of(j * blk, blk), blk)
        v = v_ref[0, sl, :]
        vt = v.astype(F32).T.astype(BF16)
        qf = q_ref[0, sl, :].astype(F32)
        for d, f_ref in enumerate((ff_ref, fb_ref)):
            ct = c2 * jnp.tanh(f_ref[0, sl, :])
            eb = _chunk_cumprod(c1 + ct, reverse=(d == 1))
            k_inv = (c2 - ct) / eb
            eb3 = eb.reshape(per_blk, ch, 128)
            e_end = eb3[:, ch - 1:ch, :] if d == 0 else eb3[:, 0:1, :]
            dec_ref[d, pl.ds(j * per_blk, per_blk)] = e_end
            q_dec = (qf * eb).astype(BF16)
            qd_ref[d, sl, :] = q_dec
            k_dec = (k_inv.reshape(per_blk, ch, 128) * e_end).reshape(blk, 128).astype(BF16)

            s = _dot_nt(q_dec, k_inv.astype(BF16))
            p = jnp.where(keep[d], s, 0.0).astype(BF16)
            acc_ref[d, sl, :] = _dot(p, v)
            rhs = jnp.concatenate([jnp.where(row_chunk == cc, k_dec, jnp.zeros_like(k_dec))
                                   for cc in range(per_blk)], axis=1)
            kvs = _dot(vt, rhs)
            for cc in range(per_blk):
                kv_ref[d, j * per_blk + cc] = kvs[:, 128 * cc:128 * (cc + 1)]
        return carry

    lax.fori_loop(0, seq // blk, intra, 0, unroll=4)

    def finish(rows):
        o = _rmsnorm_rows(acc_ref[0, rows, :] + acc_ref[1, rows, :], ng_ref[...])
        gate = 0.5 + 0.5 * jnp.tanh(gate_ref[0, rows, :].astype(F32))
        o_ref[0, rows, :] = (o * gate).astype(o_ref.dtype)

    states = [jnp.zeros((128, 128), F32)] * 2
    for n in range(n_chunks):
        for d in range(2):
            c = n if d == 0 else n_chunks - 1 - n
            rows = slice(c * ch, (c + 1) * ch)
            acc_ref[d, rows, :] += _dot_nt(qd_ref[d, rows, :], states[d].astype(BF16))
            states[d] = dec_ref[d, c] * states[d] + kv_ref[d, c]
        if 2 * n >= n_chunks:
            finish(slice(n * ch, (n + 1) * ch))
            finish(slice((n_chunks - 1 - n) * ch, (n_chunks - n) * ch))


def _hgrn(q, v, ff, fb, gate, lb_logits, norm_g, layer):
    bsz, seq, _ = q.shape
    n_chunks = seq // HGRN_CHUNK
    head_spec = pl.BlockSpec((1, seq, 128), lambda b, h: (b, 0, h))
    return pl.pallas_call(
        functools.partial(_hgrn_kernel, layer),
        grid=(bsz, HGRN_HEADS),
        in_specs=[head_spec] * 5 + [pl.BlockSpec((DEPTH, 128), lambda b, h: (0, h)),
                                    pl.BlockSpec((1, 128), lambda b, h: (0, h))],
        out_specs=head_spec,
        out_shape=jax.ShapeDtypeStruct((bsz, seq, D_MODEL), BF16),
        scratch_shapes=[
           pltpu.VMEM((2, seq, 128), BF16),
           pltpu.VMEM((2, n_chunks, 1, 128), F32),
           pltpu.VMEM((2, n_chunks, 128, 128), F32),
           pltpu.VMEM((2, seq, 128), F32)],
        compiler_params=_params(2),
        name="hgrn2",
    )(q, v, ff, fb, gate, lb_logits, norm_g)


def _rope_pair_tables(seq):
    inv = ROPE_THETA ** (-jnp.arange(0, ATTN_QK_DIM, 2, dtype=F32) / ATTN_QK_DIM)
    ang = jnp.arange(seq, dtype=F32)[:, None] * inv[None, :]
    return jnp.tile(jnp.cos(ang), (1, 4)), jnp.tile(jnp.sin(ang), (1, 4))


def _to_pair_layout(w):
    idx = jnp.arange(ATTN_QK_WIDTH).reshape(2, 2, 2, 2, 32)
    return w[:, idx.transpose(0, 3, 1, 2, 4).reshape(-1)]


def kernel(x, norm_mix_g, norm_mlp_g, final_norm_g, w_ff_in, w_ff_out, w_in_even, w_out_even, diff_lambda, diff_subln_g, s5_lam_re, s5_lam_im, s5_log_step, s5_b_re, s5_b_im, s5_c_re, s5_c_im, s5_d, s5_w_glu, s5_b_glu, w_in_odd, w_out_odd, hgrn_norm_g, hgrn_lb_logits):
    bsz, seq, _ = x.shape
    n_tok = bsz * seq
    n_chunks = seq // S5_CHUNK
    assert n_chunks & (n_chunks - 1) == 0 and n_chunks % 128 == 0
    x2 = x.reshape(n_tok, D_MODEL)
    cos, sin = _rope_pair_tables(seq)

    assert DEPTH == 2, "an even (attention + S5) layer followed by an odd (HGRN2) layer"
    for layer in range(DEPTH):
        g_mix = norm_mix_g[layer].reshape(1, D_MODEL)
        g_mlp = norm_mlp_g[layer].reshape(1, D_MODEL)
        if layer % 2 == 0:
            e = layer // 2
            w = w_in_even[e]
            w_qkv = jnp.concatenate([_to_pair_layout(w[:, :ATTN_QK_WIDTH]),
                                     _to_pair_layout(w[:, ATTN_QK_WIDTH:2 * ATTN_QK_WIDTH]),
                                     w[:, 2 * ATTN_QK_WIDTH:2 * ATTN_QK_WIDTH + ATTN_WIDTH]], axis=1).astype(BF16)
            wu = w[:, 2 * ATTN_QK_WIDTH + ATTN_WIDTH:].astype(BF16)
            q, k, v = _pre_even(x2, g_mix, w_qkv, cos, sin, seq)
            u_t = _s5_in(x2, g_mix, wu, bsz, seq)
            lambda_init = 0.8 - 0.6 * math.exp(-0.3 * layer)
            a_out, w16 = _attention(q.reshape(bsz, seq, -1), k.reshape(bsz, seq, -1), v.reshape(bsz, seq, -1),
                                    diff_lambda[e], diff_subln_g[e].reshape(1, ATTN_V_DIM), lambda_init,
                                    [w_out_even[e], w_ff_in[layer], w_ff_out[layer],
                                     w_in_odd[e], w_out_odd[e], w_ff_in[layer + 1], w_ff_out[layer + 1]],
                                    [None, None, None, 2 * D_MODEL, None, None, None])
            w_out_e, w1, w2, w_in_o, w_out_o, w1_next, w2_next = w16
            ops = _s5_operators(s5_lam_re[e], s5_lam_im[e], s5_log_step[e], s5_b_re[e], s5_b_im[e],
                                s5_c_re[e], s5_c_im[e], s5_d[e], n_chunks.bit_length() - 1)
            y_t = _s5_chunked(u_t.reshape(S5_GROUPS, S5_CW, bsz * n_chunks), *ops, n_chunks)
            b_out = _s5_out(y_t.reshape(S5_GROUPS, S5_CHUNK, S5_GROUP, bsz * n_chunks),
                            s5_w_glu[e].T.astype(BF16), s5_b_glu[e].reshape(S5_WIDTH, 1), bsz, seq)
            x2 = _post_even(x2, a_out.reshape(n_tok, ATTN_WIDTH), b_out, w_out_e, g_mlp, w1, w2)
        else:
            o_i = layer // 2
            q, v, ff, fb, gate = _pre_odd(x2, g_mix, w_in_o)
            shp = (bsz, seq, D_MODEL)
            mixed = _hgrn(q.reshape(shp), v.reshape(shp), ff.reshape(shp), fb.reshape(shp), gate.reshape(shp),
                          hgrn_lb_logits, hgrn_norm_g[o_i].reshape(1, D_MODEL), layer)
            x2 = _post_odd(x2, mixed.reshape(n_tok, D_MODEL), w_out_o, g_mlp, w1_next, w2_next,
                           final_norm_g.reshape(1, D_MODEL))
    return x2.reshape(bsz, seq, D_MODEL)
```

```python
import functools
import math

import jax
import jax.numpy as jnp
from jax import lax
from jax.experimental import pallas as pl
from jax.experimental.pallas import tpu as pltpu

D_MODEL = 1024
DEPTH = 2
ATTN_HEADS = 4
ATTN_QK_DIM = 64
ATTN_V_DIM = 128
ATTN_QK_WIDTH = 512
ATTN_WIDTH = 512
ROPE_THETA = 10000.0
S5_WIDTH = 512
S5_GROUP = 16
S5_GROUPS = 32
S5_STATE = 64
HGRN_HEADS = 8
HGRN_CHUNK = 64
HGRN_BLOCK = 256
D_FF = 4096
EPS = 1e-6

S5_CHUNK = 16
S5_CW = S5_CHUNK * S5_GROUP

TOKEN_TILE = 512
ATTN_Q_TILE = 512
ATTN_Q_SUB = 512
FF_TILE = 1024
VMEM_LIMIT = 56 * 1024 * 1024

BF16 = jnp.bfloat16
F32 = jnp.float32


def _const_spec(shape):
    nd = len(shape)
    return pl.BlockSpec(shape, lambda *_: (0,) * nd, pipeline_mode=pl.Buffered(1))


def _params(n_axes):
    return pltpu.CompilerParams(dimension_semantics=("arbitrary",) * n_axes,
                                vmem_limit_bytes=VMEM_LIMIT)


def _rmsnorm_rows(x, g):
    ms = jnp.mean(x * x, axis=-1, keepdims=True)
    return x * lax.rsqrt(ms + EPS) * g


def _gelu_tanh(x):
    c = math.sqrt(2.0 / math.pi)
    return 0.5 * x * (1.0 + jnp.tanh(c * (x + 0.044715 * (x * x * x))))


def _dot(a, b):
    return jnp.dot(a, b, preferred_element_type=F32)


def _dot_nt(a, b):
    return lax.dot_general(a, b, (((1,), (1,)), ((), ())), preferred_element_type=F32)


def _dot_tn(a, b):
    return lax.dot_general(a, b, (((0,), (0,)), ((), ())), preferred_element_type=F32)


def _pre_even_kernel(x_ref, g_ref, w_ref, cos_ref, sin_ref, q_ref, k_ref, v_ref):
    h = _rmsnorm_rows(x_ref[...], g_ref[...]).astype(BF16)
    cos = cos_ref[...]
    sin = sin_ref[...]
    for out_ref, base, scale in ((q_ref, 0, ATTN_QK_DIM ** -0.5 * math.log2(math.e)),
                                 (k_ref, ATTN_QK_WIDTH, 1.0)):
        p = _dot(h, w_ref[:, base:base + ATTN_QK_WIDTH])
        for pair in range(2):
            lo = p[:, 256 * pair:256 * pair + 128]
            hi = p[:, 256 * pair + 128:256 * pair + 256]
            out_ref[:, 256 * pair:256 * pair + 128] = ((lo * cos - hi * sin) * scale).astype(BF16)
            out_ref[:, 256 * pair + 128:256 * pair + 256] = ((hi * cos + lo * sin) * scale).astype(BF16)
    v_ref[...] = _dot(h, w_ref[:, 2 * ATTN_QK_WIDTH:]).astype(BF16)


def _pre_even(x2, g, w_qkv, cos, sin, seq):
    n_tok = x2.shape[0]
    tm = TOKEN_TILE
    n_pos_blocks = seq // tm
    out = jax.ShapeDtypeStruct((n_tok, 512), BF16)
    row_spec = pl.BlockSpec((tm, 512), lambda i: (i, 0))
    rope_spec = pl.BlockSpec((tm, 128), lambda i: (i % n_pos_blocks, 0))
    return pl.pallas_call(
        _pre_even_kernel,
        grid=(n_tok // tm,),
        in_specs=[pl.BlockSpec((tm, D_MODEL), lambda i: (i, 0)), _const_spec((1, D_MODEL)),
                  _const_spec(w_qkv.shape), rope_spec, rope_spec],
        out_specs=[row_spec] * 3,
        out_shape=[out] * 3,
        compiler_params=_params(1),
        name="pre_even",
    )(x2, g, w_qkv, cos, sin)


def _s5_in_kernel(x_ref, g_ref, wu_ref, ut_ref, u_scr):
    nc = x_ref.shape[0] // S5_CHUNK
    groups_per_slab = 128 // S5_GROUP
    h = _rmsnorm_rows(x_ref[...], g_ref[...]).astype(BF16)
    u = _dot(h, wu_ref[...])
    for j in range(S5_WIDTH // 128):
        u_scr[j] = u[:, 128 * j:128 * (j + 1)]
    for j in range(S5_WIDTH // 128):
        for ph in range(S5_CHUNK):
            t = u_scr[j, pl.ds(ph, nc, stride=S5_CHUNK), :]
            ut_ref[groups_per_slab * j:groups_per_slab * (j + 1), ph, :, :] = (
                t.T.astype(BF16).reshape(groups_per_slab, S5_GROUP, nc))


def _s5_in(x2, g, wu, bsz, seq):
    nc = seq // S5_CHUNK
    return pl.pallas_call(
        _s5_in_kernel,
        grid=(bsz,),
        in_specs=[pl.BlockSpec((seq, D_MODEL), lambda b: (b, 0)), _const_spec((1, D_MODEL)),
                  _const_spec(wu.shape)],
        out_specs=pl.BlockSpec((S5_GROUPS, S5_CHUNK, S5_GROUP, nc), lambda b: (0, 0, 0, b)),
        out_shape=jax.ShapeDtypeStruct((S5_GROUPS, S5_CHUNK, S5_GROUP, bsz * nc), BF16),
        scratch_shapes=[pltpu.VMEM((S5_WIDTH // 128, seq, 128), F32)],
        compiler_params=_params(1),
        name="s5_in",
    )(x2, g, wu)


def _s5_out_kernel(yt_ref, wglut_ref, bglu_ref, o_ref, b_scr):
    nc = yt_ref.shape[3]
    yt = jnp.concatenate([yt_ref[:, ph, :, :].reshape(S5_WIDTH, nc) for ph in range(S5_CHUNK)], axis=1)
    yt = _gelu_tanh(yt)
    z = _dot(wglut_ref[...], yt.astype(BF16)) + bglu_ref[...]
    bt = yt * (0.5 + 0.5 * jnp.tanh(0.5 * z))
    for j in range(S5_WIDTH // 128):
        for ph in range(S5_CHUNK):
            b_scr[j, pl.ds(ph, nc, stride=S5_CHUNK), :] = bt[128 * j:128 * (j + 1), ph * nc:(ph + 1) * nc].T
    for j in range(S5_WIDTH // 128):
        o_ref[:, 128 * j:128 * (j + 1)] = b_scr[j]


def _s5_out(y_t, w_glu_t, b_glu_col, bsz, seq):
    nc = seq // S5_CHUNK
    return pl.pallas_call(
        _s5_out_kernel,
        grid=(bsz,),
        in_specs=[pl.BlockSpec((S5_GROUPS, S5_CHUNK, S5_GROUP, nc), lambda b: (0, 0, 0, b)),
                  _const_spec(w_glu_t.shape), _const_spec(b_glu_col.shape)],
        out_specs=pl.BlockSpec((seq, S5_WIDTH), lambda b: (b, 0)),
        out_shape=jax.ShapeDtypeStruct((bsz * seq, S5_WIDTH), F32),
        scratch_shapes=[pltpu.VMEM((S5_WIDTH // 128, seq, 128), F32)],
        compiler_params=_params(1),
        name="s5_out",
    )(y_t, w_glu_t, b_glu_col)


def _attn_kernel(lambda_init, halve_from, q_ref, k_ref, v_ref, lam_ref, g_ref, *refs):
    n_weights = len(halve_from)
    o_ref = refs[n_weights]
    for w_ref, w16_ref, col0 in zip(refs[:n_weights], refs[n_weights + 1:], halve_from):
        w = w_ref[...]
        if col0 is not None:
            w = w * jnp.where(lax.broadcasted_iota(jnp.int32, w.shape, 1) >= col0, 0.5, 1.0)
        w16_ref[...] = w.astype(BF16)
    k = k_ref[0]
    lam = lam_ref[...]
    lam_val = (jnp.exp(jnp.sum(lam[0:1] * lam[1:2], axis=-1, keepdims=True))
               - jnp.exp(jnp.sum(lam[2:3] * lam[3:4], axis=-1, keepdims=True)) + lambda_init)
    tq = ATTN_Q_SUB
    lane_group = (lax.broadcasted_iota(jnp.int32, (tq, 256), 1) // 32) % 4
    ones = jnp.ones((k.shape[0], ATTN_V_DIM), BF16)
    v_ext = [jnp.concatenate([v_ref[0, :, 128 * hh:128 * hh + 128], ones], axis=1) for hh in range(2)]
    for t in range(q_ref.shape[1] // tq):
        q = q_ref[0, t * tq:(t + 1) * tq, :]
        zero = jnp.zeros_like(q)
        q_all = jnp.concatenate([jnp.where(lane_group == hc, q, zero) for hc in range(4)], axis=0)
        s = _dot_nt(q_all, k)
        e = jnp.exp2(s - jnp.max(s, axis=-1, keepdims=True)).astype(BF16)
        for hh in range(2):
            r = _dot(e[2 * hh * tq:2 * (hh + 1) * tq], v_ext[hh])
            o = (r[0:tq, 0:ATTN_V_DIM] / r[0:tq, ATTN_V_DIM:]
                 - lam_val * (r[tq:2 * tq, 0:ATTN_V_DIM] / r[tq:2 * tq, ATTN_V_DIM:]))
            o = _rmsnorm_rows(o, g_ref[...]) * (1.0 - lambda_init)
            o_ref[0, t * tq:(t + 1) * tq, 128 * hh:128 * hh + 128] = o.astype(o_ref.dtype)


def _attention(q, k, v, lam, subln_g, lambda_init, later_weights, halve_from):
    bsz, seq, _ = q.shape
    tq = ATTN_Q_TILE
    n_q = seq // tq
    n_steps = bsz * 2 * n_q
    kv_spec = pl.BlockSpec((1, seq, 256), lambda b, p, i: (b, 0, p))
    step = lambda b, p, i: (b * 2 + p) * n_q + i
    slab_in = [pl.BlockSpec((None, w.shape[1] // n_steps, w.shape[2]),
                            functools.partial(lambda idx, b, p, i: (idx, step(b, p, i), 0), idx))
               for w, idx in later_weights]
    slab_out = [pl.BlockSpec((w.shape[1] // n_steps, w.shape[2]), lambda b, p, i: (step(b, p, i), 0))
                for w, _ in later_weights]
    outs = pl.pallas_call(
        functools.partial(_attn_kernel, lambda_init, tuple(halve_from)),
        grid=(bsz, 2, n_q),
        in_specs=[pl.BlockSpec((1, tq, 256), lambda b, p, i: (b, i, p)),
                  kv_spec, kv_spec,
                  _const_spec(lam.shape),
                  _const_spec(subln_g.shape)] + slab_in,
        out_specs=[pl.BlockSpec((1, tq, 256), lambda b, p, i: (b, i, p))] + slab_out,
        out_shape=[jax.ShapeDtypeStruct((bsz, seq, ATTN_WIDTH), BF16)]
        + [jax.ShapeDtypeStruct(w.shape[1:], BF16) for w, _ in later_weights],
        compiler_params=_params(3),
        name="diff_attention",
    )(q, k, v, lam, subln_g, *[w for w, _ in later_weights])
    return outs[0], outs[1:]


def _s5_kernel(n_chunks, ut_ref, mi_ref, min_ref, mo_ref, ctab_ref, y_ref):
    n_rows = ut_ref.shape[2]
    n_steps = n_chunks.bit_length() - 1
    ut2 = ut_ref[...].reshape(2 * S5_CW, n_rows)
    st = _dot_tn(ut2, min_ref[0])
    ctab = ctab_ref[0]
    pos = lax.broadcasted_iota(jnp.int32, (n_rows, 128), 0) % n_chunks
    zs = []
    for d in range(2):
        xr = st[:, 256 * d:256 * d + 128]
        xi = st[:, 256 * d + 128:256 * d + 256]
        for k in range(n_steps):
            s = 1 << k
            if d == 0:
                shift, keep = s, pos >= s
            else:
                shift, keep = n_rows - s, pos < n_chunks - s
            pr = ctab[4 * k + 2 * d:4 * k + 2 * d + 1]
            pi = ctab[4 * k + 2 * d + 1:4 * k + 2 * d + 2]
            sr = jnp.where(keep, pltpu.roll(xr, shift, 0), 0.0)
            si = jnp.where(keep, pltpu.roll(xi, shift, 0), 0.0)
            xr, xi = xr + pr * sr - pi * si, xi + pr * si + pi * sr
        if d == 0:
            shift, keep = 1, pos >= 1
        else:
            shift, keep = n_rows - 1, pos < n_chunks - 1
        zs += [jnp.where(keep, pltpu.roll(xr, shift, 0), 0.0), jnp.where(keep, pltpu.roll(xi, shift, 0), 0.0)]
    z = jnp.concatenate(zs, axis=1).astype(BF16)
    for g in range(2):
        y_ref[g] = _dot(mi_ref[g], ut_ref[g]) + _dot_nt(mo_ref[g], z)


def _s5_chunked(u_t, m_intra_t, m_in_pair, m_out_ext, ctab, n_chunks):
    n_groups, _, n_lanes = u_t.shape
    return pl.pallas_call(
        functools.partial(_s5_kernel, n_chunks),
        grid=(n_groups // 2,),
        in_specs=[pl.BlockSpec((2, S5_CW, n_lanes), lambda i: (i, 0, 0)),
                  pl.BlockSpec((2, S5_CW, S5_CW), lambda i: (i, 0, 0)),
                  pl.BlockSpec((1, 2 * S5_CW, 512), lambda i: (i, 0, 0)),
                  pl.BlockSpec((2, S5_CW, 512), lambda i: (i, 0, 0)),
                  pl.BlockSpec((1, 32, 128), lambda i: (i, 0, 0))],
        out_specs=pl.BlockSpec((2, S5_CW, n_lanes), lambda i: (i, 0, 0)),
        out_shape=jax.ShapeDtypeStruct((n_groups, S5_CW, n_lanes), F32),
        compiler_params=_params(1),
        name="s5_chunked",
    )(u_t, m_intra_t, m_in_pair, m_out_ext, ctab)


def _dot_hp(a, b):
    return jnp.dot(a, b, precision=lax.Precision.HIGHEST, preferred_element_type=F32)


def _bf16_terms(x):
    hi = x.astype(BF16)
    r = x - hi.astype(F32)
    mid = r.astype(BF16)
    return hi, mid, (r - mid.astype(F32)).astype(BF16)


def _spread(values, onehot):
    return sum(_dot(term, onehot) for term in _bf16_terms(values))


def _spread_rows(onehot, values):
    return sum(_dot(onehot, term) for term in _bf16_terms(values))


def _discretise(lr, li, log_step):
    lr = jnp.minimum(lr, -1e-4)
    dt = jnp.exp(log_step)
    mag = jnp.exp(lr * dt)
    ar = mag * jnp.cos(li * dt)
    ai = mag * jnp.sin(li * dt)
    den = lr * lr + li * li
    cr = ((ar - 1.0) * lr + ai * li) / den
    ci = (ai * lr - (ar - 1.0) * li) / den
    return lr * dt, li * dt, cr, ci


def _s5_ops_kernel(n_steps, colp_ref, rowp_ref, bcol_ref, crow_ref, dcol_ref,
                   mi_ref, min_ref, mo_ref, ctab_ref):
    t = S5_CHUNK
    n = S5_GROUP
    onehot = lambda m: jnp.where(m, 1.0, 0.0).astype(BF16)
    i16 = lambda shape, dim: lax.broadcasted_iota(jnp.int32, shape, dim)
    rep_lanes = onehot(i16((t, S5_CW), 1) // n == i16((t, S5_CW), 0))
    tile_lanes = onehot(i16((n, S5_CW), 1) % n == i16((n, S5_CW), 0))
    rep_rows = onehot(i16((S5_CW, t), 0) // n == i16((S5_CW, t), 1))
    tile_rows = onehot(i16((S5_CW, n), 0) % n == i16((S5_CW, n), 1))
    row_group = i16((2 * S5_STATE, S5_CW), 0) // S5_STATE
    lane_group = i16((1, 2 * S5_STATE), 1) // S5_STATE
    lane16 = i16((1, t), 1).astype(F32)
    row16 = i16((t, 1), 0).astype(F32)
    colp = colp_ref[0]
    rowp = rowp_ref[0]

    taps = [[None, None], [None, None]]
    min_t_rows = []
    mo_kinds = []
    for d in range(2):
        lmag, ang, cr, ci = _discretise(colp[:, 3 * d:3 * d + 1], colp[:, 3 * d + 1:3 * d + 2],
                                        colp[:, 3 * d + 2:3 * d + 3])
        b_re, b_im = bcol_ref[0, 2 * d], bcol_ref[0, 2 * d + 1]
        bbr = _spread(cr * b_re - ci * b_im, tile_lanes)
        bbi = _spread(cr * b_im + ci * b_re, tile_lanes)
        lag = (t - 1.0) - lane16 if d == 0 else lane16
        pmag = jnp.exp(lmag * lag)
        pr = _spread(pmag * jnp.cos(ang * lag), rep_lanes)
        pi = _spread(pmag * jnp.sin(ang * lag), rep_lanes)
        rr = pr * bbr - pi * bbi
        ri = pr * bbi + pi * bbr
        for part in (rr, ri):
            min_t_rows.append(jnp.concatenate([jnp.where(row_group == gs, part, 0.0) for gs in range(2)], axis=1))
        c_re, c_im = crow_ref[0, 2 * d], crow_ref[0, 2 * d + 1]
        for gs in range(2):
            own = lane_group == gs
            taps[gs][d] = (_dot_hp(jnp.where(own, c_re, 0.0), rr) - _dot_hp(jnp.where(own, c_im, 0.0), ri))

        lmag_r, ang_r, _, _ = _discretise(rowp[3 * d:3 * d + 1], rowp[3 * d + 1:3 * d + 2],
                                          rowp[3 * d + 2:3 * d + 3])
        tau = row16 + 1.0 if d == 0 else float(t) - row16
        qmag = jnp.exp(tau * lmag_r)
        qr16 = qmag * jnp.cos(tau * ang_r)
        qi16 = qmag * jnp.sin(tau * ang_r)
        qr = _spread_rows(rep_rows, qr16)
        qi = _spread_rows(rep_rows, qi16)
        cre = _spread_rows(tile_rows, c_re)
        cim = _spread_rows(tile_rows, c_im)
        mo_kinds += [qr * cre - qi * cim, -(qr * cim + qi * cre)]

        far = t - 1 if d == 0 else 0
        pw_r, pw_i = qr16[far:far + 1], qi16[far:far + 1]
        for k in range(n_steps):
            ctab_ref[0, 4 * k + 2 * d:4 * k + 2 * d + 1, :] = pw_r
            ctab_ref[0, 4 * k + 2 * d + 1:4 * k + 2 * d + 2, :] = pw_i
            pw_r, pw_i = pw_r * pw_r - pw_i * pw_i, 2.0 * pw_r * pw_i
    ctab_ref[0, 4 * n_steps:, :] = jnp.zeros((32 - 4 * n_steps, 2 * S5_STATE), F32)

    min_ref[0] = jnp.concatenate(min_t_rows, axis=0).T.astype(BF16)
    lane_group_wide = i16((1, 2 * S5_STATE), 1) // S5_STATE
    for gs in range(2):
        mo_ref[gs] = jnp.concatenate([jnp.where(lane_group_wide == gs, kind, 0.0) for kind in mo_kinds],
                                     axis=1).astype(BF16)
        zeros = jnp.zeros((n, S5_CW), F32)
        edge = (t - 1) * n
        lane = i16((n, 2 * S5_CW), 1)
        skip = jnp.where(lane == edge + i16((n, 2 * S5_CW), 0), dcol_ref[0, gs], 0.0)
        kk = (jnp.concatenate([taps[gs][0], zeros], axis=1)
              + pltpu.roll(jnp.concatenate([taps[gs][1], zeros], axis=1), edge, 1) + skip)
        strips = []
        for j in range(t):
            shift = (t - 1 - j) * n
            moved = kk if shift == 0 else pltpu.roll(kk, 2 * S5_CW - shift, 1)
            strips.append(moved[:, 0:S5_CW])
        mi_ref[gs] = jnp.concatenate(strips, axis=0).astype(BF16)


def _s5_operators(lam_re, lam_im, log_step, b_re, b_im, c_re, c_im, d_skip, n_steps):
    g2 = S5_GROUPS // 2
    zero = jnp.zeros_like(lam_re[0])
    step = [jnp.broadcast_to(log_step[d][:, None], lam_re[d].shape) for d in range(2)]
    params = jnp.stack([lam_re[0], lam_im[0], step[0], lam_re[1], lam_im[1], step[1], zero, zero])
    colp = params.transpose(1, 2, 0).reshape(g2, 2 * S5_STATE, 8)
    rowp = params.reshape(8, g2, 2 * S5_STATE).transpose(1, 0, 2)
    bcol = (jnp.stack([b_re[0], b_im[0], b_re[1], b_im[1]])
            .reshape(4, g2, 2 * S5_STATE, S5_GROUP).transpose(1, 0, 2, 3))
    crow = (jnp.stack([c_re[0], c_im[0], c_re[1], c_im[1]])
            .reshape(4, g2, 2, S5_GROUP, S5_STATE).transpose(1, 0, 3, 2, 4).reshape(g2, 4, S5_GROUP, 2 * S5_STATE))
    dcol = d_skip.reshape(g2, 2, S5_GROUP, 1)
    pair = lambda *shape: pl.BlockSpec((1,) + shape, lambda i: (i,) + (0,) * len(shape))
    return pl.pallas_call(
        functools.partial(_s5_ops_kernel, n_steps),
        grid=(g2,),
        in_specs=[pair(2 * S5_STATE, 8), pair(8, 2 * S5_STATE), pair(4, 2 * S5_STATE, S5_GROUP),
                  pair(4, S5_GROUP, 2 * S5_STATE), pair(2, S5_GROUP, 1)],
        out_specs=[pl.BlockSpec((2, S5_CW, S5_CW), lambda i: (i, 0, 0)), pair(2 * S5_CW, 512),
                   pl.BlockSpec((2, S5_CW, 512), lambda i: (i, 0, 0)), pair(32, 2 * S5_STATE)],
        out_shape=[jax.ShapeDtypeStruct((S5_GROUPS, S5_CW, S5_CW), BF16),
                   jax.ShapeDtypeStruct((g2, 2 * S5_CW, 512), BF16),
                   jax.ShapeDtypeStruct((S5_GROUPS, S5_CW, 512), BF16),
                   jax.ShapeDtypeStruct((g2, 32, 2 * S5_STATE), F32)],
        compiler_params=_params(1),
        name="s5_operators",
    )(colp, rowp, bcol, crow, dcol)


def _mlp(x1, g_ref, w1_ref, w2_ref):
    h = _rmsnorm_rows(x1, g_ref[...]).astype(BF16)
    acc = x1
    for j in range(D_FF // FF_TILE):
        hid = _dot(h, w1_ref[:, j * FF_TILE:(j + 1) * FF_TILE])
        hid = jnp.square(jnp.maximum(hid, 0.0)).astype(BF16)
        acc = acc + _dot(hid, w2_ref[j * FF_TILE:(j + 1) * FF_TILE, :])
    return acc


def _post_even_kernel(x_ref, a_ref, b_ref, wout_ref, g_ref, w1_ref, w2_ref, o_ref):
    mix = (_dot(a_ref[...], wout_ref[0:ATTN_WIDTH, :])
           + _dot(b_ref[...].astype(BF16), wout_ref[ATTN_WIDTH:, :]))
    o_ref[...] = _mlp(x_ref[...] + mix, g_ref, w1_ref, w2_ref)


def _post_even(x2, a_out, b_out, w_out, g_mlp, w1, w2):
    n_tok = x2.shape[0]
    tm = TOKEN_TILE
    return pl.pallas_call(
        _post_even_kernel,
        grid=(n_tok // tm,),
        in_specs=[pl.BlockSpec((tm, D_MODEL), lambda i: (i, 0)),
                  pl.BlockSpec((tm, ATTN_WIDTH), lambda i: (i, 0)),
                  pl.BlockSpec((tm, S5_WIDTH), lambda i: (i, 0)),
                  _const_spec(w_out.shape), _const_spec(g_mlp.shape), _const_spec(w1.shape),
                  _const_spec(w2.shape)],
        out_specs=pl.BlockSpec((tm, D_MODEL), lambda i: (i, 0)),
        out_shape=jax.ShapeDtypeStruct((n_tok, D_MODEL), F32),
        compiler_params=_params(1),
        name="post_even",
    )(x2, a_out, b_out, w_out, g_mlp, w1, w2)


def _post_odd_kernel(x_ref, m_ref, wout_ref, g_ref, w1_ref, w2_ref, gf_ref, o_ref):
    x1 = x_ref[...] + _dot(m_ref[...], wout_ref[...])
    o_ref[...] = _rmsnorm_rows(_mlp(x1, g_ref, w1_ref, w2_ref), gf_ref[...])


def _post_odd(x2, mixed, w_out, g_mlp, w1, w2, g_final):
    n_tok = x2.shape[0]
    tm = TOKEN_TILE
    return pl.pallas_call(
        _post_odd_kernel,
        grid=(n_tok // tm,),
        in_specs=[pl.BlockSpec((tm, D_MODEL), lambda i: (i, 0)),
                  pl.BlockSpec((tm, D_MODEL), lambda i: (i, 0)),
                  _const_spec(w_out.shape), _const_spec(g_mlp.shape),
                  _const_spec(w1.shape), _const_spec(w2.shape), _const_spec(g_final.shape)],
        out_specs=pl.BlockSpec((tm, D_MODEL), lambda i: (i, 0)),
        out_shape=jax.ShapeDtypeStruct((n_tok, D_MODEL), F32),
        compiler_params=_params(1),
        name="post_odd",
    )(x2, mixed, w_out, g_mlp, w1, w2, g_final)


def _pre_odd_kernel(x_ref, g_ref, w_ref, q_ref, i_ref, ff_ref, fb_ref, gate_ref):
    h = _rmsnorm_rows(x_ref[...], g_ref[...]).astype(BF16)
    for s, out_ref in enumerate((q_ref, i_ref, ff_ref, fb_ref, gate_ref)):
        out_ref[...] = _dot(h, w_ref[:, s * D_MODEL:(s + 1) * D_MODEL]).astype(out_ref.dtype)


def _pre_odd(x2, g, w):
    n_tok = x2.shape[0]
    tm = TOKEN_TILE
    row_spec = pl.BlockSpec((tm, D_MODEL), lambda i: (i, 0))
    lo = jax.ShapeDtypeStruct((n_tok, D_MODEL), BF16)
    hi = jax.ShapeDtypeStruct((n_tok, D_MODEL), F32)
    return pl.pallas_call(
        _pre_odd_kernel,
        grid=(n_tok // tm,),
        in_specs=[row_spec, _const_spec((1, D_MODEL)), _const_spec(w.shape)],
        out_specs=[row_spec] * 5,
        out_shape=[lo, lo, hi, hi, lo],
        compiler_params=_params(1),
        name="pre_odd",
    )(x2, g, w)


def _chunk_cumprod(x, reverse):
    n = x.shape[0]
    pos = lax.broadcasted_iota(jnp.int32, x.shape, 0) % HGRN_CHUNK
    s = 1
    while s < HGRN_CHUNK:
        if reverse:
            shifted = pltpu.roll(x, n - s, 0)
            x = x * jnp.where(pos < HGRN_CHUNK - s, shifted, 1.0)
        else:
            shifted = pltpu.roll(x, s, 0)
            x = x * jnp.where(pos >= s, shifted, 1.0)
        s *= 2
    return x


def _hgrn_kernel(layer, q_ref, v_ref, ff_ref, fb_ref, gate_ref, lbl_ref, ng_ref, o_ref,
                 qd_ref, dec_ref, kv_ref, acc_ref):
    seq = q_ref.shape[1]
    ch = HGRN_CHUNK
    blk = HGRN_BLOCK
    n_chunks = seq // ch
    per_blk = blk // ch
    logits = lbl_ref[...]
    soft = jnp.exp(logits - jnp.max(logits, axis=0, keepdims=True))
    soft = soft / jnp.sum(soft, axis=0, keepdims=True)
    lb = jnp.sum(soft[0:layer + 1], axis=0, keepdims=True) - soft[0:1]

    c2 = 0.5 * (1.0 - lb)
    c1 = lb + c2
    row = lax.broadcasted_iota(jnp.int32, (blk, blk), 0)
    col = lax.broadcasted_iota(jnp.int32, (blk, blk), 1)
    u32 = lambda a: a.astype(jnp.uint32)
    keep = (u32(row - col) <= u32(row % ch), u32(col - row) <= u32(ch - 1 - row % ch))
    row_chunk = lax.broadcasted_iota(jnp.int32, (blk, 128), 0) // ch

    def intra(j, carry):
        sl = pl.ds(pl.multiple_of(j * blk, blk), blk)
        v = v_ref[0, sl, :]
        vt = v.astype(F32).T.astype(BF16)
        qf = q_ref[0, sl, :].astype(F32)
        for d, f_ref in enumerate((ff_ref, fb_ref)):
            ct = c2 * jnp.tanh(f_ref[0, sl, :])
            eb = _chunk_cumprod(c1 + ct, reverse=(d == 1))
            k_inv = (c2 - ct) / eb
            eb3 = eb.reshape(per_blk, ch, 128)
            e_end = eb3[:, ch - 1:ch, :] if d == 0 else eb3[:, 0:1, :]
            dec_ref[d, pl.ds(j * per_blk, per_blk)] = e_end
            q_dec = (qf * eb).astype(BF16)
            qd_ref[d, sl, :] = q_dec
            k_dec = (k_inv.reshape(per_blk, ch, 128) * e_end).reshape(blk, 128).astype(BF16)

            s = _dot_nt(q_dec, k_inv.astype(BF16))
            p = jnp.where(keep[d], s, 0.0).astype(BF16)
            acc_ref[d, sl, :] = _dot(p, v)
            rhs = jnp.concatenate([jnp.where(row_chunk == cc, k_dec, jnp.zeros_like(k_dec))
                                   for cc in range(per_blk)], axis=1)
            kvs = _dot(vt, rhs)
            for cc in range(per_blk):
                kv_ref[d, j * per_blk + cc] = kvs[:, 128 * cc:128 * (cc + 1)]
        return carry

    lax.fori_loop(0, seq // blk, intra, 0, unroll=4)

    def finish(rows):
        o = _rmsnorm_rows(acc_ref[0, rows, :] + acc_ref[1, rows, :], ng_ref[...])
        gate = 0.5 + 0.5 * jnp.tanh(gate_ref[0, rows, :].astype(F32))
        o_ref[0, rows, :] = (o * gate).astype(o_ref.dtype)

    states = [jnp.zeros((128, 128), F32)] * 2
    for n in range(n_chunks):
        for d in range(2):
            c = n if d == 0 else n_chunks - 1 - n
            rows = slice(c * ch, (c + 1) * ch)
            acc_ref[d, rows, :] += _dot_nt(qd_ref[d, rows, :], states[d].astype(BF16))
            states[d] = dec_ref[d, c] * states[d] + kv_ref[d, c]
        if 2 * n >= n_chunks:
            finish(slice(n * ch, (n + 1) * ch))
            finish(slice((n_chunks - 1 - n) * ch, (n_chunks - n) * ch))


def _hgrn(q, v, ff, fb, gate, lb_logits, norm_g, layer):
    bsz, seq, _ = q.shape
    n_chunks = seq // HGRN_CHUNK
    head_spec = pl.BlockSpec((1, seq, 128), lambda b, h: (b, 0, h))
    return pl.pallas_call(
        functools.partial(_hgrn_kernel, layer),
        grid=(bsz, HGRN_HEADS),
        in_specs=[head_spec] * 5 + [pl.BlockSpec((DEPTH, 128), lambda b, h: (0, h)),
                                    pl.BlockSpec((1, 128), lambda b, h: (0, h))],
        out_specs=head_spec,
        out_shape=jax.ShapeDtypeStruct((bsz, seq, D_MODEL), BF16),
        scratch_shapes=[
           pltpu.VMEM((2, seq, 128), BF16),
           pltpu.VMEM((2, n_chunks, 1, 128), F32),
           pltpu.VMEM((2, n_chunks, 128, 128), F32),
           pltpu.VMEM((2, seq, 128), F32)],
        compiler_params=_params(2),
        name="hgrn2",
    )(q, v, ff, fb, gate, lb_logits, norm_g)


def _rope_pair_tables(seq):
    inv = ROPE_THETA ** (-jnp.arange(0, ATTN_QK_DIM, 2, dtype=F32) / ATTN_QK_DIM)
    ang = jnp.arange(seq, dtype=F32)[:, None] * inv[None, :]
    return jnp.tile(jnp.cos(ang), (1, 4)), jnp.tile(jnp.sin(ang), (1, 4))


def _to_pair_layout(w):
    idx = jnp.arange(ATTN_QK_WIDTH).reshape(2, 2, 2, 2, 32)
    return w[:, idx.transpose(0, 3, 1, 2, 4).reshape(-1)]


def kernel(x, norm_mix_g, norm_mlp_g, final_norm_g, w_ff_in, w_ff_out, w_in_even, w_out_even, diff_lambda, diff_subln_g, s5_lam_re, s5_lam_im, s5_log_step, s5_b_re, s5_b_im, s5_c_re, s5_c_im, s5_d, s5_w_glu, s5_b_glu, w_in_odd, w_out_odd, hgrn_norm_g, hgrn_lb_logits):
    bsz, seq, _ = x.shape
    n_tok = bsz * seq
    n_chunks = seq // S5_CHUNK
    assert n_chunks & (n_chunks - 1) == 0 and n_chunks % 128 == 0
    x2 = x.reshape(n_tok, D_MODEL)
    cos, sin = _rope_pair_tables(seq)

    assert DEPTH == 2, "an even (attention + S5) layer followed by an odd (HGRN2) layer"
    for layer in range(DEPTH):
        g_mix = norm_mix_g[layer].reshape(1, D_MODEL)
        g_mlp = norm_mlp_g[layer].reshape(1, D_MODEL)
        if layer % 2 == 0:
            e = layer // 2
            w = w_in_even[e]
            w_qkv = jnp.concatenate([_to_pair_layout(w[:, :ATTN_QK_WIDTH]),
                                     _to_pair_layout(w[:, ATTN_QK_WIDTH:2 * ATTN_QK_WIDTH]),
                                     w[:, 2 * ATTN_QK_WIDTH:2 * ATTN_QK_WIDTH + ATTN_WIDTH]], axis=1).astype(BF16)
            wu = w[:, 2 * ATTN_QK_WIDTH + ATTN_WIDTH:].astype(BF16)
            q, k, v = _pre_even(x2, g_mix, w_qkv, cos, sin, seq)
            u_t = _s5_in(x2, g_mix, wu, bsz, seq)
            lambda_init = 0.8 - 0.6 * math.exp(-0.3 * layer)
            a_out, w16 = _attention(q.reshape(bsz, seq, -1), k.reshape(bsz, seq, -1), v.reshape(bsz, seq, -1),
                                    diff_lambda[e], diff_subln_g[e].reshape(1, ATTN_V_DIM), lambda_init,
                                    [(w_out_even, e), (w_ff_in, layer), (w_ff_out, layer),
                                     (w_in_odd, e), (w_out_odd, e), (w_ff_in, layer + 1), (w_ff_out, layer + 1)],
                                    [None, None, None, 2 * D_MODEL, None, None, None])
            w_out_e, w1, w2, w_in_o, w_out_o, w1_next, w2_next = w16
            ops = _s5_operators(s5_lam_re[e], s5_lam_im[e], s5_log_step[e], s5_b_re[e], s5_b_im[e],
                                s5_c_re[e], s5_c_im[e], s5_d[e], n_chunks.bit_length() - 1)
            y_t = _s5_chunked(u_t.reshape(S5_GROUPS, S5_CW, bsz * n_chunks), *ops, n_chunks)
            b_out = _s5_out(y_t.reshape(S5_GROUPS, S5_CHUNK, S5_GROUP, bsz * n_chunks),
                            s5_w_glu[e].T.astype(BF16), s5_b_glu[e].reshape(S5_WIDTH, 1), bsz, seq)
            x2 = _post_even(x2, a_out.reshape(n_tok, ATTN_WIDTH), b_out, w_out_e, g_mlp, w1, w2)
        else:
            o_i = layer // 2
            q, v, ff, fb, gate = _pre_odd(x2, g_mix, w_in_o)
            shp = (bsz, seq, D_MODEL)
            mixed = _hgrn(q.reshape(shp), v.reshape(shp), ff.reshape(shp), fb.reshape(shp), gate.reshape(shp),
                          hgrn_lb_logits, hgrn_norm_g[o_i].reshape(1, D_MODEL), layer)
            x2 = _post_odd(x2, mixed.reshape(n_tok, D_MODEL), w_out_o, g_mlp, w1_next, w2_next,
                           final_norm_g.reshape(1, D_MODEL))
    return x2.reshape(bsz, seq, D_MODEL)
```

```python
import functools
import math

import jax
import jax.numpy as jnp
from jax import lax
from jax.experimental import pallas as pl
from jax.experimental.pallas import tpu as pltpu

D_MODEL = 1024
DEPTH = 2
ATTN_HEADS = 4
ATTN_QK_DIM = 64
ATTN_V_DIM = 128
ATTN_QK_WIDTH = 512
ATTN_WIDTH = 512
ROPE_THETA = 10000.0
S5_WIDTH = 512
S5_GROUP = 16
S5_GROUPS = 32
S5_STATE = 64
HGRN_HEADS = 8
HGRN_CHUNK = 64
HGRN_BLOCK = 256
D_FF = 4096
EPS = 1e-6

S5_CHUNK = 16
S5_CW = S5_CHUNK * S5_GROUP

TOKEN_TILE = 1024
ATTN_Q_TILE = 512
ATTN_Q_SUB = 512
FF_TILE = 1024
VMEM_LIMIT = 56 * 1024 * 1024

BF16 = jnp.bfloat16
F32 = jnp.float32


def _const_spec(shape):
    nd = len(shape)
    return pl.BlockSpec(shape, lambda *_: (0,) * nd, pipeline_mode=pl.Buffered(1))


def _params(n_axes):
    return pltpu.CompilerParams(dimension_semantics=("arbitrary",) * n_axes,
                                vmem_limit_bytes=VMEM_LIMIT)


def _rmsnorm_rows(x, g):
    ms = jnp.mean(x * x, axis=-1, keepdims=True)
    return x * lax.rsqrt(ms + EPS) * g


def _gelu_tanh(x):
    c = math.sqrt(2.0 / math.pi)
    return 0.5 * x * (1.0 + jnp.tanh(c * (x + 0.044715 * (x * x * x))))


def _dot(a, b):
    return jnp.dot(a, b, preferred_element_type=F32)


def _dot_nt(a, b):
    return lax.dot_general(a, b, (((1,), (1,)), ((), ())), preferred_element_type=F32)


def _dot_tn(a, b):
    return lax.dot_general(a, b, (((0,), (0,)), ((), ())), preferred_element_type=F32)


def _pre_even_kernel(x_ref, g_ref, w_ref, cos_ref, sin_ref, q_ref, k_ref, v_ref):
    h = _rmsnorm_rows(x_ref[...], g_ref[...]).astype(BF16)
    cos = cos_ref[...]
    sin = sin_ref[...]
    for out_ref, base, scale in ((q_ref, 0, ATTN_QK_DIM ** -0.5 * math.log2(math.e)),
                                 (k_ref, ATTN_QK_WIDTH, 1.0)):
        p = _dot(h, w_ref[:, base:base + ATTN_QK_WIDTH])
        for pair in range(2):
            lo = p[:, 256 * pair:256 * pair + 128]
            hi = p[:, 256 * pair + 128:256 * pair + 256]
            out_ref[:, 256 * pair:256 * pair + 128] = ((lo * cos - hi * sin) * scale).astype(BF16)
            out_ref[:, 256 * pair + 128:256 * pair + 256] = ((hi * cos + lo * sin) * scale).astype(BF16)
    v_ref[...] = _dot(h, w_ref[:, 2 * ATTN_QK_WIDTH:]).astype(BF16)


def _pre_even(x2, g, w_qkv, cos, sin, seq):
    n_tok = x2.shape[0]
    tm = TOKEN_TILE
    n_pos_blocks = seq // tm
    out = jax.ShapeDtypeStruct((n_tok, 512), BF16)
    row_spec = pl.BlockSpec((tm, 512), lambda i: (i, 0))
    rope_spec = pl.BlockSpec((tm, 128), lambda i: (i % n_pos_blocks, 0))
    return pl.pallas_call(
        _pre_even_kernel,
        grid=(n_tok // tm,),
        in_specs=[pl.BlockSpec((tm, D_MODEL), lambda i: (i, 0)), _const_spec((1, D_MODEL)),
                  _const_spec(w_qkv.shape), rope_spec, rope_spec],
        out_specs=[row_spec] * 3,
        out_shape=[out] * 3,
        compiler_params=_params(1),
        name="pre_even",
    )(x2, g, w_qkv, cos, sin)


def _s5_in_kernel(x_ref, g_ref, wu_ref, ut_ref, u_scr):
    nc = x_ref.shape[0] // S5_CHUNK
    groups_per_slab = 128 // S5_GROUP
    h = _rmsnorm_rows(x_ref[...], g_ref[...]).astype(BF16)
    u = _dot(h, wu_ref[...])
    for j in range(S5_WIDTH // 128):
        u_scr[j] = u[:, 128 * j:128 * (j + 1)]
    for j in range(S5_WIDTH // 128):
        for ph in range(S5_CHUNK):
            t = u_scr[j, pl.ds(ph, nc, stride=S5_CHUNK), :]
            ut_ref[groups_per_slab * j:groups_per_slab * (j + 1), ph, :, :] = (
                t.T.astype(BF16).reshape(groups_per_slab, S5_GROUP, nc))


def _s5_in(x2, g, wu, bsz, seq):
    nc = seq // S5_CHUNK
    return pl.pallas_call(
        _s5_in_kernel,
        grid=(bsz,),
        in_specs=[pl.BlockSpec((seq, D_MODEL), lambda b: (b, 0)), _const_spec((1, D_MODEL)),
                  _const_spec(wu.shape)],
        out_specs=pl.BlockSpec((S5_GROUPS, S5_CHUNK, S5_GROUP, nc), lambda b: (0, 0, 0, b)),
        out_shape=jax.ShapeDtypeStruct((S5_GROUPS, S5_CHUNK, S5_GROUP, bsz * nc), BF16),
        scratch_shapes=[pltpu.VMEM((S5_WIDTH // 128, seq, 128), F32)],
        compiler_params=_params(1),
        name="s5_in",
    )(x2, g, wu)


def _s5_out_kernel(yt_ref, wglut_ref, bglu_ref, o_ref, b_scr):
    nc = yt_ref.shape[3]
    yt = jnp.concatenate([yt_ref[:, ph, :, :].reshape(S5_WIDTH, nc) for ph in range(S5_CHUNK)], axis=1)
    yt = _gelu_tanh(yt)
    z = _dot(wglut_ref[...], yt.astype(BF16)) + bglu_ref[...]
    bt = yt * (0.5 + 0.5 * jnp.tanh(0.5 * z))
    for j in range(S5_WIDTH // 128):
        for ph in range(S5_CHUNK):
            b_scr[j, pl.ds(ph, nc, stride=S5_CHUNK), :] = bt[128 * j:128 * (j + 1), ph * nc:(ph + 1) * nc].T
    for j in range(S5_WIDTH // 128):
        o_ref[:, 128 * j:128 * (j + 1)] = b_scr[j]


def _s5_out(y_t, w_glu_t, b_glu_col, bsz, seq):
    nc = seq // S5_CHUNK
    return pl.pallas_call(
        _s5_out_kernel,
        grid=(bsz,),
        in_specs=[pl.BlockSpec((S5_GROUPS, S5_CHUNK, S5_GROUP, nc), lambda b: (0, 0, 0, b)),
                  _const_spec(w_glu_t.shape), _const_spec(b_glu_col.shape)],
        out_specs=pl.BlockSpec((seq, S5_WIDTH), lambda b: (b, 0)),
        out_shape=jax.ShapeDtypeStruct((bsz * seq, S5_WIDTH), F32),
        scratch_shapes=[pltpu.VMEM((S5_WIDTH // 128, seq, 128), F32)],
        compiler_params=_params(1),
        name="s5_out",
    )(y_t, w_glu_t, b_glu_col)


def _attn_kernel(lambda_init, halve_from, q_ref, k_ref, v_ref, lam_ref, g_ref, *refs):
    n_weights = len(halve_from)
    o_ref = refs[n_weights]
    for w_ref, w16_ref, col0 in zip(refs[:n_weights], refs[n_weights + 1:], halve_from):
        w = w_ref[...]
        if col0 is not None:
            w = w * jnp.where(lax.broadcasted_iota(jnp.int32, w.shape, 1) >= col0, 0.5, 1.0)
        w16_ref[...] = w.astype(BF16)
    k = k_ref[0]
    lam = lam_ref[...]
    lam_val = (jnp.exp(jnp.sum(lam[0:1] * lam[1:2], axis=-1, keepdims=True))
               - jnp.exp(jnp.sum(lam[2:3] * lam[3:4], axis=-1, keepdims=True)) + lambda_init)
    tq = ATTN_Q_SUB
    lane_group = (lax.broadcasted_iota(jnp.int32, (tq, 256), 1) // 32) % 4
    ones = jnp.ones((k.shape[0], ATTN_V_DIM), BF16)
    v_ext = [jnp.concatenate([v_ref[0, :, 128 * hh:128 * hh + 128], ones], axis=1) for hh in range(2)]
    for t in range(q_ref.shape[1] // tq):
        q = q_ref[0, t * tq:(t + 1) * tq, :]
        zero = jnp.zeros_like(q)
        q_all = jnp.concatenate([jnp.where(lane_group == hc, q, zero) for hc in range(4)], axis=0)
        s = _dot_nt(q_all, k)
        e = jnp.exp2(s - jnp.max(s, axis=-1, keepdims=True)).astype(BF16)
        for hh in range(2):
            r = _dot(e[2 * hh * tq:2 * (hh + 1) * tq], v_ext[hh])
            o = (r[0:tq, 0:ATTN_V_DIM] / r[0:tq, ATTN_V_DIM:]
                 - lam_val * (r[tq:2 * tq, 0:ATTN_V_DIM] / r[tq:2 * tq, ATTN_V_DIM:]))
            o = _rmsnorm_rows(o, g_ref[...]) * (1.0 - lambda_init)
            o_ref[0, t * tq:(t + 1) * tq, 128 * hh:128 * hh + 128] = o.astype(o_ref.dtype)


def _attention(q, k, v, lam, subln_g, lambda_init, later_weights, halve_from):
    bsz, seq, _ = q.shape
    tq = ATTN_Q_TILE
    n_q = seq // tq
    n_steps = bsz * 2 * n_q
    kv_spec = pl.BlockSpec((1, seq, 256), lambda b, p, i: (b, 0, p))
    step = lambda b, p, i: (b * 2 + p) * n_q + i
    slab_in = [pl.BlockSpec((None, w.shape[1] // n_steps, w.shape[2]),
                            functools.partial(lambda idx, b, p, i: (idx, step(b, p, i), 0), idx))
               for w, idx in later_weights]
    slab_out = [pl.BlockSpec((w.shape[1] // n_steps, w.shape[2]), lambda b, p, i: (step(b, p, i), 0))
                for w, _ in later_weights]
    outs = pl.pallas_call(
        functools.partial(_attn_kernel, lambda_init, tuple(halve_from)),
        grid=(bsz, 2, n_q),
        in_specs=[pl.BlockSpec((1, tq, 256), lambda b, p, i: (b, i, p)),
                  kv_spec, kv_spec,
                  _const_spec(lam.shape),
                  _const_spec(subln_g.shape)] + slab_in,
        out_specs=[pl.BlockSpec((1, tq, 256), lambda b, p, i: (b, i, p))] + slab_out,
        out_shape=[jax.ShapeDtypeStruct((bsz, seq, ATTN_WIDTH), BF16)]
        + [jax.ShapeDtypeStruct(w.shape[1:], BF16) for w, _ in later_weights],
        compiler_params=_params(3),
        name="diff_attention",
    )(q, k, v, lam, subln_g, *[w for w, _ in later_weights])
    return outs[0], outs[1:]


def _s5_kernel(n_chunks, ut_ref, mi_ref, min_ref, mo_ref, ctab_ref, y_ref):
    n_rows = ut_ref.shape[2]
    n_steps = n_chunks.bit_length() - 1
    ut2 = ut_ref[...].reshape(2 * S5_CW, n_rows)
    st = _dot_tn(ut2, min_ref[0])
    ctab = ctab_ref[0]
    pos = lax.broadcasted_iota(jnp.int32, (n_rows, 128), 0) % n_chunks
    zs = []
    for d in range(2):
        xr = st[:, 256 * d:256 * d + 128]
        xi = st[:, 256 * d + 128:256 * d + 256]
        for k in range(n_steps):
            s = 1 << k
            if d == 0:
                shift, keep = s, pos >= s
            else:
                shift, keep = n_rows - s, pos < n_chunks - s
            pr = ctab[4 * k + 2 * d:4 * k + 2 * d + 1]
            pi = ctab[4 * k + 2 * d + 1:4 * k + 2 * d + 2]
            sr = jnp.where(keep, pltpu.roll(xr, shift, 0), 0.0)
            si = jnp.where(keep, pltpu.roll(xi, shift, 0), 0.0)
            xr, xi = xr + pr * sr - pi * si, xi + pr * si + pi * sr
        if d == 0:
            shift, keep = 1, pos >= 1
        else:
            shift, keep = n_rows - 1, pos < n_chunks - 1
        zs += [jnp.where(keep, pltpu.roll(xr, shift, 0), 0.0), jnp.where(keep, pltpu.roll(xi, shift, 0), 0.0)]
    z = jnp.concatenate(zs, axis=1).astype(BF16)
    for g in range(2):
        y_ref[g] = _dot(mi_ref[g], ut_ref[g]) + _dot_nt(mo_ref[g], z)


def _s5_chunked(u_t, m_intra_t, m_in_pair, m_out_ext, ctab, n_chunks):
    n_groups, _, n_lanes = u_t.shape
    return pl.pallas_call(
        functools.partial(_s5_kernel, n_chunks),
        grid=(n_groups // 2,),
        in_specs=[pl.BlockSpec((2, S5_CW, n_lanes), lambda i: (i, 0, 0)),
                  pl.BlockSpec((2, S5_CW, S5_CW), lambda i: (i, 0, 0)),
                  pl.BlockSpec((1, 2 * S5_CW, 512), lambda i: (i, 0, 0)),
                  pl.BlockSpec((2, S5_CW, 512), lambda i: (i, 0, 0)),
                  pl.BlockSpec((1, 32, 128), lambda i: (i, 0, 0))],
        out_specs=pl.BlockSpec((2, S5_CW, n_lanes), lambda i: (i, 0, 0)),
        out_shape=jax.ShapeDtypeStruct((n_groups, S5_CW, n_lanes), F32),
        compiler_params=_params(1),
        name="s5_chunked",
    )(u_t, m_intra_t, m_in_pair, m_out_ext, ctab)


def _dot_hp(a, b):
    return jnp.dot(a, b, precision=lax.Precision.HIGHEST, preferred_element_type=F32)


def _bf16_terms(x):
    hi = x.astype(BF16)
    r = x - hi.astype(F32)
    mid = r.astype(BF16)
    return hi, mid, (r - mid.astype(F32)).astype(BF16)


def _spread(values, onehot):
    return sum(_dot(term, onehot) for term in _bf16_terms(values))


def _spread_rows(onehot, values):
    return sum(_dot(onehot, term) for term in _bf16_terms(values))


def _discretise(lr, li, log_step):
    lr = jnp.minimum(lr, -1e-4)
    dt = jnp.exp(log_step)
    mag = jnp.exp(lr * dt)
    ar = mag * jnp.cos(li * dt)
    ai = mag * jnp.sin(li * dt)
    den = lr * lr + li * li
    cr = ((ar - 1.0) * lr + ai * li) / den
    ci = (ai * lr - (ar - 1.0) * li) / den
    return lr * dt, li * dt, cr, ci


def _s5_ops_kernel(n_steps, colp_ref, rowp_ref, bcol_ref, crow_ref, dcol_ref,
                   mi_ref, min_ref, mo_ref, ctab_ref):
    t = S5_CHUNK
    n = S5_GROUP
    onehot = lambda m: jnp.where(m, 1.0, 0.0).astype(BF16)
    i16 = lambda shape, dim: lax.broadcasted_iota(jnp.int32, shape, dim)
    rep_lanes = onehot(i16((t, S5_CW), 1) // n == i16((t, S5_CW), 0))
    tile_lanes = onehot(i16((n, S5_CW), 1) % n == i16((n, S5_CW), 0))
    rep_rows = onehot(i16((S5_CW, t), 0) // n == i16((S5_CW, t), 1))
    tile_rows = onehot(i16((S5_CW, n), 0) % n == i16((S5_CW, n), 1))
    row_group = i16((2 * S5_STATE, S5_CW), 0) // S5_STATE
    lane_group = i16((1, 2 * S5_STATE), 1) // S5_STATE
    lane16 = i16((1, t), 1).astype(F32)
    row16 = i16((t, 1), 0).astype(F32)
    colp = colp_ref[0]
    rowp = rowp_ref[0]

    taps = [[None, None], [None, None]]
    min_t_rows = []
    mo_kinds = []
    for d in range(2):
        lmag, ang, cr, ci = _discretise(colp[:, 3 * d:3 * d + 1], colp[:, 3 * d + 1:3 * d + 2],
                                        colp[:, 3 * d + 2:3 * d + 3])
        b_re, b_im = bcol_ref[0, 2 * d], bcol_ref[0, 2 * d + 1]
        bbr = _spread(cr * b_re - ci * b_im, tile_lanes)
        bbi = _spread(cr * b_im + ci * b_re, tile_lanes)
        lag = (t - 1.0) - lane16 if d == 0 else lane16
        pmag = jnp.exp(lmag * lag)
        pr = _spread(pmag * jnp.cos(ang * lag), rep_lanes)
        pi = _spread(pmag * jnp.sin(ang * lag), rep_lanes)
        rr = pr * bbr - pi * bbi
        ri = pr * bbi + pi * bbr
        for part in (rr, ri):
            min_t_rows.append(jnp.concatenate([jnp.where(row_group == gs, part, 0.0) for gs in range(2)], axis=1))
        c_re, c_im = crow_ref[0, 2 * d], crow_ref[0, 2 * d + 1]
        for gs in range(2):
            own = lane_group == gs
            taps[gs][d] = (_dot_hp(jnp.where(own, c_re, 0.0), rr) - _dot_hp(jnp.where(own, c_im, 0.0), ri))

        lmag_r, ang_r, _, _ = _discretise(rowp[3 * d:3 * d + 1], rowp[3 * d + 1:3 * d + 2],
                                          rowp[3 * d + 2:3 * d + 3])
        tau = row16 + 1.0 if d == 0 else float(t) - row16
        qmag = jnp.exp(tau * lmag_r)
        qr16 = qmag * jnp.cos(tau * ang_r)
        qi16 = qmag * jnp.sin(tau * ang_r)
        qr = _spread_rows(rep_rows, qr16)
        qi = _spread_rows(rep_rows, qi16)
        cre = _spread_rows(tile_rows, c_re)
        cim = _spread_rows(tile_rows, c_im)
        mo_kinds += [qr * cre - qi * cim, -(qr * cim + qi * cre)]

        far = t - 1 if d == 0 else 0
        pw_r, pw_i = qr16[far:far + 1], qi16[far:far + 1]
        for k in range(n_steps):
            ctab_ref[0, 4 * k + 2 * d:4 * k + 2 * d + 1, :] = pw_r
            ctab_ref[0, 4 * k + 2 * d + 1:4 * k + 2 * d + 2, :] = pw_i
            pw_r, pw_i = pw_r * pw_r - pw_i * pw_i, 2.0 * pw_r * pw_i
    ctab_ref[0, 4 * n_steps:, :] = jnp.zeros((32 - 4 * n_steps, 2 * S5_STATE), F32)

    min_ref[0] = jnp.concatenate(min_t_rows, axis=0).T.astype(BF16)
    lane_group_wide = i16((1, 2 * S5_STATE), 1) // S5_STATE
    for gs in range(2):
        mo_ref[gs] = jnp.concatenate([jnp.where(lane_group_wide == gs, kind, 0.0) for kind in mo_kinds],
                                     axis=1).astype(BF16)
        zeros = jnp.zeros((n, S5_CW), F32)
        edge = (t - 1) * n
        lane = i16((n, 2 * S5_CW), 1)
        skip = jnp.where(lane == edge + i16((n, 2 * S5_CW), 0), dcol_ref[0, gs], 0.0)
        kk = (jnp.concatenate([taps[gs][0], zeros], axis=1)
              + pltpu.roll(jnp.concatenate([taps[gs][1], zeros], axis=1), edge, 1) + skip)
        strips = []
        for j in range(t):
            shift = (t - 1 - j) * n
            moved = kk if shift == 0 else pltpu.roll(kk, 2 * S5_CW - shift, 1)
            strips.append(moved[:, 0:S5_CW])
        mi_ref[gs] = jnp.concatenate(strips, axis=0).astype(BF16)


def _s5_operators(lam_re, lam_im, log_step, b_re, b_im, c_re, c_im, d_skip, n_steps):
    g2 = S5_GROUPS // 2
    zero = jnp.zeros_like(lam_re[0])
    step = [jnp.broadcast_to(log_step[d][:, None], lam_re[d].shape) for d in range(2)]
    params = jnp.stack([lam_re[0], lam_im[0], step[0], lam_re[1], lam_im[1], step[1], zero, zero])
    colp = params.transpose(1, 2, 0).reshape(g2, 2 * S5_STATE, 8)
    rowp = params.reshape(8, g2, 2 * S5_STATE).transpose(1, 0, 2)
    bcol = (jnp.stack([b_re[0], b_im[0], b_re[1], b_im[1]])
            .reshape(4, g2, 2 * S5_STATE, S5_GROUP).transpose(1, 0, 2, 3))
    crow = (jnp.stack([c_re[0], c_im[0], c_re[1], c_im[1]])
            .reshape(4, g2, 2, S5_GROUP, S5_STATE).transpose(1, 0, 3, 2, 4).reshape(g2, 4, S5_GROUP, 2 * S5_STATE))
    dcol = d_skip.reshape(g2, 2, S5_GROUP, 1)
    pair = lambda *shape: pl.BlockSpec((1,) + shape, lambda i: (i,) + (0,) * len(shape))
    return pl.pallas_call(
        functools.partial(_s5_ops_kernel, n_steps),
        grid=(g2,),
        in_specs=[pair(2 * S5_STATE, 8), pair(8, 2 * S5_STATE), pair(4, 2 * S5_STATE, S5_GROUP),
                  pair(4, S5_GROUP, 2 * S5_STATE), pair(2, S5_GROUP, 1)],
        out_specs=[pl.BlockSpec((2, S5_CW, S5_CW), lambda i: (i, 0, 0)), pair(2 * S5_CW, 512),
                   pl.BlockSpec((2, S5_CW, 512), lambda i: (i, 0, 0)), pair(32, 2 * S5_STATE)],
        out_shape=[jax.ShapeDtypeStruct((S5_GROUPS, S5_CW, S5_CW), BF16),
                   jax.ShapeDtypeStruct((g2, 2 * S5_CW, 512), BF16),
                   jax.ShapeDtypeStruct((S5_GROUPS, S5_CW, 512), BF16),
                   jax.ShapeDtypeStruct((g2, 32, 2 * S5_STATE), F32)],
        compiler_params=_params(1),
        name="s5_operators",
    )(colp, rowp, bcol, crow, dcol)


def _mlp(x1, g_ref, w1_ref, w2_ref):
    h = _rmsnorm_rows(x1, g_ref[...]).astype(BF16)
    acc = x1
    for j in range(D_FF // FF_TILE):
        hid = _dot(h, w1_ref[:, j * FF_TILE:(j + 1) * FF_TILE])
        hid = jnp.square(jnp.maximum(hid, 0.0)).astype(BF16)
        acc = acc + _dot(hid, w2_ref[j * FF_TILE:(j + 1) * FF_TILE, :])
    return acc


def _post_even_kernel(x_ref, a_ref, b_ref, wout_ref, g_ref, w1_ref, w2_ref, o_ref):
    mix = (_dot(a_ref[...], wout_ref[0:ATTN_WIDTH, :])
           + _dot(b_ref[...].astype(BF16), wout_ref[ATTN_WIDTH:, :]))
    o_ref[...] = _mlp(x_ref[...] + mix, g_ref, w1_ref, w2_ref)


def _post_even(x2, a_out, b_out, w_out, g_mlp, w1, w2):
    n_tok = x2.shape[0]
    tm = TOKEN_TILE
    return pl.pallas_call(
        _post_even_kernel,
        grid=(n_tok // tm,),
        in_specs=[pl.BlockSpec((tm, D_MODEL), lambda i: (i, 0)),
                  pl.BlockSpec((tm, ATTN_WIDTH), lambda i: (i, 0)),
                  pl.BlockSpec((tm, S5_WIDTH), lambda i: (i, 0)),
                  _const_spec(w_out.shape), _const_spec(g_mlp.shape), _const_spec(w1.shape),
                  _const_spec(w2.shape)],
        out_specs=pl.BlockSpec((tm, D_MODEL), lambda i: (i, 0)),
        out_shape=jax.ShapeDtypeStruct((n_tok, D_MODEL), F32),
        compiler_params=_params(1),
        name="post_even",
    )(x2, a_out, b_out, w_out, g_mlp, w1, w2)


def _post_odd_kernel(x_ref, m_ref, wout_ref, g_ref, w1_ref, w2_ref, gf_ref, o_ref):
    x1 = x_ref[...] + _dot(m_ref[...], wout_ref[...])
    o_ref[...] = _rmsnorm_rows(_mlp(x1, g_ref, w1_ref, w2_ref), gf_ref[...])


def _post_odd(x2, mixed, w_out, g_mlp, w1, w2, g_final):
    n_tok = x2.shape[0]
    tm = TOKEN_TILE
    return pl.pallas_call(
        _post_odd_kernel,
        grid=(n_tok // tm,),
        in_specs=[pl.BlockSpec((tm, D_MODEL), lambda i: (i, 0)),
                  pl.BlockSpec((tm, D_MODEL), lambda i: (i, 0)),
                  _const_spec(w_out.shape), _const_spec(g_mlp.shape),
                  _const_spec(w1.shape), _const_spec(w2.shape), _const_spec(g_final.shape)],
        out_specs=pl.BlockSpec((tm, D_MODEL), lambda i: (i, 0)),
        out_shape=jax.ShapeDtypeStruct((n_tok, D_MODEL), F32),
        compiler_params=_params(1),
        name="post_odd",
    )(x2, mixed, w_out, g_mlp, w1, w2, g_final)


def _pre_odd_kernel(x_ref, g_ref, w_ref, q_ref, i_ref, ff_ref, fb_ref, gate_ref):
    h = _rmsnorm_rows(x_ref[...], g_ref[...]).astype(BF16)
    for s, out_ref in enumerate((q_ref, i_ref, ff_ref, fb_ref, gate_ref)):
        out_ref[...] = _dot(h, w_ref[:, s * D_MODEL:(s + 1) * D_MODEL]).astype(out_ref.dtype)


def _pre_odd(x2, g, w):
    n_tok = x2.shape[0]
    tm = TOKEN_TILE
    row_spec = pl.BlockSpec((tm, D_MODEL), lambda i: (i, 0))
    lo = jax.ShapeDtypeStruct((n_tok, D_MODEL), BF16)
    hi = jax.ShapeDtypeStruct((n_tok, D_MODEL), F32)
    return pl.pallas_call(
        _pre_odd_kernel,
        grid=(n_tok // tm,),
        in_specs=[row_spec, _const_spec((1, D_MODEL)), _const_spec(w.shape)],
        out_specs=[row_spec] * 5,
        out_shape=[lo, lo, hi, hi, lo],
        compiler_params=_params(1),
        name="pre_odd",
    )(x2, g, w)


def _chunk_cumprod(x, reverse):
    n = x.shape[0]
    pos = lax.broadcasted_iota(jnp.int32, x.shape, 0) % HGRN_CHUNK
    s = 1
    while s < HGRN_CHUNK:
        if reverse:
            shifted = pltpu.roll(x, n - s, 0)
            x = x * jnp.where(pos < HGRN_CHUNK - s, shifted, 1.0)
        else:
            shifted = pltpu.roll(x, s, 0)
            x = x * jnp.where(pos >= s, shifted, 1.0)
        s *= 2
    return x


def _hgrn_kernel(layer, q_ref, v_ref, ff_ref, fb_ref, gate_ref, lbl_ref, ng_ref, o_ref,
                 qd_ref, dec_ref, kv_ref, acc_ref):
    seq = q_ref.shape[1]
    ch = HGRN_CHUNK
    blk = HGRN_BLOCK
    n_chunks = seq // ch
    per_blk = blk // ch
    logits = lbl_ref[...]
    soft = jnp.exp(logits - jnp.max(logits, axis=0, keepdims=True))
    soft = soft / jnp.sum(soft, axis=0, keepdims=True)
    lb = jnp.sum(soft[0:layer + 1], axis=0, keepdims=True) - soft[0:1]

    c2 = 0.5 * (1.0 - lb)
    c1 = lb + c2
    row = lax.broadcasted_iota(jnp.int32, (blk, blk), 0)
    col = lax.broadcasted_iota(jnp.int32, (blk, blk), 1)
    u32 = lambda a: a.astype(jnp.uint32)
    keep = (u32(row - col) <= u32(row % ch), u32(col - row) <= u32(ch - 1 - row % ch))
    row_chunk = lax.broadcasted_iota(jnp.int32, (blk, 128), 0) // ch

    def intra(j, carry):
        sl = pl.ds(pl.multiple_of(j * blk, blk), blk)
        v = v_ref[0, sl, :]
        vt = v.astype(F32).T.astype(BF16)
        qf = q_ref[0, sl, :].astype(F32)
        for d, f_ref in enumerate((ff_ref, fb_ref)):
            ct = c2 * jnp.tanh(f_ref[0, sl, :])
            eb = _chunk_cumprod(c1 + ct, reverse=(d == 1))
            k_inv = (c2 - ct) / eb
            eb3 = eb.reshape(per_blk, ch, 128)
            e_end = eb3[:, ch - 1:ch, :] if d == 0 else eb3[:, 0:1, :]
            dec_ref[d, pl.ds(j * per_blk, per_blk)] = e_end
            q_dec = (qf * eb).astype(BF16)
            qd_ref[d, sl, :] = q_dec
            k_dec = (k_inv.reshape(per_blk, ch, 128) * e_end).reshape(blk, 128).astype(BF16)

            s = _dot_nt(q_dec, k_inv.astype(BF16))
            p = jnp.where(keep[d], s, 0.0).astype(BF16)
            acc_ref[d, sl, :] = _dot(p, v)
            rhs = jnp.concatenate([jnp.where(row_chunk == cc, k_dec, jnp.zeros_like(k_dec))
                                   for cc in range(per_blk)], axis=1)
            kvs = _dot(vt, rhs)
            for cc in range(per_blk):
                kv_ref[d, j * per_blk + cc] = kvs[:, 128 * cc:128 * (cc + 1)]
        return carry

    lax.fori_loop(0, seq // blk, intra, 0, unroll=4)

    def finish(rows):
        o = _rmsnorm_rows(acc_ref[0, rows, :] + acc_ref[1, rows, :], ng_ref[...])
        gate = 0.5 + 0.5 * jnp.tanh(gate_ref[0, rows, :].astype(F32))
        o_ref[0, rows, :] = (o * gate).astype(o_ref.dtype)

    states = [jnp.zeros((128, 128), F32)] * 2
    for n in range(n_chunks):
        for d in range(2):
            c = n if d == 0 else n_chunks - 1 - n
            rows = slice(c * ch, (c + 1) * ch)
            acc_ref[d, rows, :] += _dot_nt(qd_ref[d, rows, :], states[d].astype(BF16))
            states[d] = dec_ref[d, c] * states[d] + kv_ref[d, c]
        if 2 * n >= n_chunks:
            finish(slice(n * ch, (n + 1) * ch))
            finish(slice((n_chunks - 1 - n) * ch, (n_chunks - n) * ch))


def _hgrn(q, v, ff, fb, gate, lb_logits, norm_g, layer):
    bsz, seq, _ = q.shape
    n_chunks = seq // HGRN_CHUNK
    head_spec = pl.BlockSpec((1, seq, 128), lambda b, h: (b, 0, h))
    return pl.pallas_call(
        functools.partial(_hgrn_kernel, layer),
        grid=(bsz, HGRN_HEADS),
        in_specs=[head_spec] * 5 + [pl.BlockSpec((DEPTH, 128), lambda b, h: (0, h)),
                                    pl.BlockSpec((1, 128), lambda b, h: (0, h))],
        out_specs=head_spec,
        out_shape=jax.ShapeDtypeStruct((bsz, seq, D_MODEL), BF16),
        scratch_shapes=[
           pltpu.VMEM((2, seq, 128), BF16),
           pltpu.VMEM((2, n_chunks, 1, 128), F32),
           pltpu.VMEM((2, n_chunks, 128, 128), F32),
           pltpu.VMEM((2, seq, 128), F32)],
        compiler_params=_params(2),
        name="hgrn2",
    )(q, v, ff, fb, gate, lb_logits, norm_g)


def _rope_pair_tables(seq):
    inv = ROPE_THETA ** (-jnp.arange(0, ATTN_QK_DIM, 2, dtype=F32) / ATTN_QK_DIM)
    ang = jnp.arange(seq, dtype=F32)[:, None] * inv[None, :]
    return jnp.tile(jnp.cos(ang), (1, 4)), jnp.tile(jnp.sin(ang), (1, 4))


def _to_pair_layout(w):
    idx = jnp.arange(ATTN_QK_WIDTH).reshape(2, 2, 2, 2, 32)
    return w[:, idx.transpose(0, 3, 1, 2, 4).reshape(-1)]


def kernel(x, norm_mix_g, norm_mlp_g, final_norm_g, w_ff_in, w_ff_out, w_in_even, w_out_even, diff_lambda, diff_subln_g, s5_lam_re, s5_lam_im, s5_log_step, s5_b_re, s5_b_im, s5_c_re, s5_c_im, s5_d, s5_w_glu, s5_b_glu, w_in_odd, w_out_odd, hgrn_norm_g, hgrn_lb_logits):
    bsz, seq, _ = x.shape
    n_tok = bsz * seq
    n_chunks = seq // S5_CHUNK
    assert n_chunks & (n_chunks - 1) == 0 and n_chunks % 128 == 0
    x2 = x.reshape(n_tok, D_MODEL)
    cos, sin = _rope_pair_tables(seq)

    assert DEPTH == 2, "an even (attention + S5) layer followed by an odd (HGRN2) layer"
    for layer in range(DEPTH):
        g_mix = norm_mix_g[layer].reshape(1, D_MODEL)
        g_mlp = norm_mlp_g[layer].reshape(1, D_MODEL)
        if layer % 2 == 0:
            e = layer // 2
            w = w_in_even[e]
            w_qkv = jnp.concatenate([_to_pair_layout(w[:, :ATTN_QK_WIDTH]),
                                     _to_pair_layout(w[:, ATTN_QK_WIDTH:2 * ATTN_QK_WIDTH]),
                                     w[:, 2 * ATTN_QK_WIDTH:2 * ATTN_QK_WIDTH + ATTN_WIDTH]], axis=1).astype(BF16)
            wu = w[:, 2 * ATTN_QK_WIDTH + ATTN_WIDTH:].astype(BF16)
            q, k, v = _pre_even(x2, g_mix, w_qkv, cos, sin, seq)
            u_t = _s5_in(x2, g_mix, wu, bsz, seq)
            lambda_init = 0.8 - 0.6 * math.exp(-0.3 * layer)
            a_out, w16 = _attention(q.reshape(bsz, seq, -1), k.reshape(bsz, seq, -1), v.reshape(bsz, seq, -1),
                                    diff_lambda[e], diff_subln_g[e].reshape(1, ATTN_V_DIM), lambda_init,
                                    [(w_out_even, e), (w_ff_in, layer), (w_ff_out, layer),
                                     (w_in_odd, e), (w_out_odd, e), (w_ff_in, layer + 1), (w_ff_out, layer + 1)],
                                    [None, None, None, 2 * D_MODEL, None, None, None])
            w_out_e, w1, w2, w_in_o, w_out_o, w1_next, w2_next = w16
            ops = _s5_operators(s5_lam_re[e], s5_lam_im[e], s5_log_step[e], s5_b_re[e], s5_b_im[e],
                                s5_c_re[e], s5_c_im[e], s5_d[e], n_chunks.bit_length() - 1)
            y_t = _s5_chunked(u_t.reshape(S5_GROUPS, S5_CW, bsz * n_chunks), *ops, n_chunks)
            b_out = _s5_out(y_t.reshape(S5_GROUPS, S5_CHUNK, S5_GROUP, bsz * n_chunks),
                            s5_w_glu[e].T.astype(BF16), s5_b_glu[e].reshape(S5_WIDTH, 1), bsz, seq)
            x2 = _post_even(x2, a_out.reshape(n_tok, ATTN_WIDTH), b_out, w_out_e, g_mlp, w1, w2)
        else:
            o_i = layer // 2
            q, v, ff, fb, gate = _pre_odd(x2, g_mix, w_in_o)
            shp = (bsz, seq, D_MODEL)
            mixed = _hgrn(q.reshape(shp), v.reshape(shp), ff.reshape(shp), fb.reshape(shp), gate.reshape(shp),
                          hgrn_lb_logits, hgrn_norm_g[o_i].reshape(1, D_MODEL), layer)
            x2 = _post_odd(x2, mixed.reshape(n_tok, D_MODEL), w_out_o, g_mlp, w1_next, w2_next,
                           final_norm_g.reshape(1, D_MODEL))
    return x2.reshape(bsz, seq, D_MODEL)
```

```python
import functools
import math

import jax
import jax.numpy as jnp
from jax import lax
from jax.experimental import pallas as pl
from jax.experimental.pallas import tpu as pltpu

D_MODEL = 1024
DEPTH = 2
ATTN_HEADS = 4
ATTN_QK_DIM = 64
ATTN_V_DIM = 128
ATTN_QK_WIDTH = 512
ATTN_WIDTH = 512
ROPE_THETA = 10000.0
S5_WIDTH = 512
S5_GROUP = 16
S5_GROUPS = 32
S5_STATE = 64
HGRN_HEADS = 8
HGRN_CHUNK = 64
HGRN_BLOCK = 256
D_FF = 4096
EPS = 1e-6

S5_CHUNK = 16
S5_CW = S5_CHUNK * S5_GROUP

TOKEN_TILE = 1024
ATTN_Q_TILE = 1024
ATTN_Q_SUB = 512
FF_TILE = 1024
VMEM_LIMIT = 56 * 1024 * 1024

BF16 = jnp.bfloat16
F32 = jnp.float32


def _const_spec(shape):
    nd = len(shape)
    return pl.BlockSpec(shape, lambda *_: (0,) * nd, pipeline_mode=pl.Buffered(1))


def _params(n_axes):
    return pltpu.CompilerParams(dimension_semantics=("arbitrary",) * n_axes,
                                vmem_limit_bytes=VMEM_LIMIT)


def _rmsnorm_rows(x, g):
    ms = jnp.mean(x * x, axis=-1, keepdims=True)
    return x * lax.rsqrt(ms + EPS) * g


def _gelu_tanh(x):
    c = math.sqrt(2.0 / math.pi)
    return 0.5 * x * (1.0 + jnp.tanh(c * (x + 0.044715 * (x * x * x))))


def _dot(a, b):
    return jnp.dot(a, b, preferred_element_type=F32)


def _dot_nt(a, b):
    return lax.dot_general(a, b, (((1,), (1,)), ((), ())), preferred_element_type=F32)


def _dot_tn(a, b):
    return lax.dot_general(a, b, (((0,), (0,)), ((), ())), preferred_element_type=F32)


def _pre_even_kernel(x_ref, g_ref, w_ref, cos_ref, sin_ref, q_ref, k_ref, v_ref):
    h = _rmsnorm_rows(x_ref[...], g_ref[...]).astype(BF16)
    cos = cos_ref[...]
    sin = sin_ref[...]
    for out_ref, base, scale in ((q_ref, 0, ATTN_QK_DIM ** -0.5 * math.log2(math.e)),
                                 (k_ref, ATTN_QK_WIDTH, 1.0)):
        p = _dot(h, w_ref[:, base:base + ATTN_QK_WIDTH])
        for pair in range(2):
            lo = p[:, 256 * pair:256 * pair + 128]
            hi = p[:, 256 * pair + 128:256 * pair + 256]
            out_ref[:, 256 * pair:256 * pair + 128] = ((lo * cos - hi * sin) * scale).astype(BF16)
            out_ref[:, 256 * pair + 128:256 * pair + 256] = ((hi * cos + lo * sin) * scale).astype(BF16)
    v_ref[...] = _dot(h, w_ref[:, 2 * ATTN_QK_WIDTH:]).astype(BF16)


def _pre_even(x2, g, w_qkv, cos, sin, seq):
    n_tok = x2.shape[0]
    tm = TOKEN_TILE
    n_pos_blocks = seq // tm
    out = jax.ShapeDtypeStruct((n_tok, 512), BF16)
    row_spec = pl.BlockSpec((tm, 512), lambda i: (i, 0))
    rope_spec = pl.BlockSpec((tm, 128), lambda i: (i % n_pos_blocks, 0))
    return pl.pallas_call(
        _pre_even_kernel,
        grid=(n_tok // tm,),
        in_specs=[pl.BlockSpec((tm, D_MODEL), lambda i: (i, 0)), _const_spec((1, D_MODEL)),
                  _const_spec(w_qkv.shape), rope_spec, rope_spec],
        out_specs=[row_spec] * 3,
        out_shape=[out] * 3,
        compiler_params=_params(1),
        name="pre_even",
    )(x2, g, w_qkv, cos, sin)


def _s5_in_kernel(x_ref, g_ref, wu_ref, ut_ref, u_scr):
    nc = x_ref.shape[0] // S5_CHUNK
    groups_per_slab = 128 // S5_GROUP
    h = _rmsnorm_rows(x_ref[...], g_ref[...]).astype(BF16)
    u = _dot(h, wu_ref[...])
    for j in range(S5_WIDTH // 128):
        u_scr[j] = u[:, 128 * j:128 * (j + 1)]
    for j in range(S5_WIDTH // 128):
        for ph in range(S5_CHUNK):
            t = u_scr[j, pl.ds(ph, nc, stride=S5_CHUNK), :]
            ut_ref[groups_per_slab * j:groups_per_slab * (j + 1), ph, :, :] = (
                t.T.astype(BF16).reshape(groups_per_slab, S5_GROUP, nc))


def _s5_in(x2, g, wu, bsz, seq):
    nc = seq // S5_CHUNK
    return pl.pallas_call(
        _s5_in_kernel,
        grid=(bsz,),
        in_specs=[pl.BlockSpec((seq, D_MODEL), lambda b: (b, 0)), _const_spec((1, D_MODEL)),
                  _const_spec(wu.shape)],
        out_specs=pl.BlockSpec((S5_GROUPS, S5_CHUNK, S5_GROUP, nc), lambda b: (0, 0, 0, b)),
        out_shape=jax.ShapeDtypeStruct((S5_GROUPS, S5_CHUNK, S5_GROUP, bsz * nc), BF16),
        scratch_shapes=[pltpu.VMEM((S5_WIDTH // 128, seq, 128), F32)],
        compiler_params=_params(1),
        name="s5_in",
    )(x2, g, wu)


def _s5_out_kernel(yt_ref, wglut_ref, bglu_ref, o_ref, b_scr):
    nc = yt_ref.shape[3]
    yt = jnp.concatenate([yt_ref[:, ph, :, :].reshape(S5_WIDTH, nc) for ph in range(S5_CHUNK)], axis=1)
    yt = _gelu_tanh(yt)
    z = _dot(wglut_ref[...], yt.astype(BF16)) + bglu_ref[...]
    bt = yt * (0.5 + 0.5 * jnp.tanh(0.5 * z))
    for j in range(S5_WIDTH // 128):
        for ph in range(S5_CHUNK):
            b_scr[j, pl.ds(ph, nc, stride=S5_CHUNK), :] = bt[128 * j:128 * (j + 1), ph * nc:(ph + 1) * nc].T
    for j in range(S5_WIDTH // 128):
        o_ref[:, 128 * j:128 * (j + 1)] = b_scr[j]


def _s5_out(y_t, w_glu_t, b_glu_col, bsz, seq):
    nc = seq // S5_CHUNK
    return pl.pallas_call(
        _s5_out_kernel,
        grid=(bsz,),
        in_specs=[pl.BlockSpec((S5_GROUPS, S5_CHUNK, S5_GROUP, nc), lambda b: (0, 0, 0, b)),
                  _const_spec(w_glu_t.shape), _const_spec(b_glu_col.shape)],
        out_specs=pl.BlockSpec((seq, S5_WIDTH), lambda b: (b, 0)),
        out_shape=jax.ShapeDtypeStruct((bsz * seq, S5_WIDTH), F32),
        scratch_shapes=[pltpu.VMEM((S5_WIDTH // 128, seq, 128), F32)],
        compiler_params=_params(1),
        name="s5_out",
    )(y_t, w_glu_t, b_glu_col)


def _attn_kernel(lambda_init, halve_from, q_ref, k_ref, v_ref, lam_ref, g_ref, *refs):
    n_weights = len(halve_from)
    o_ref = refs[n_weights]
    for w_ref, w16_ref, col0 in zip(refs[:n_weights], refs[n_weights + 1:], halve_from):
        w = w_ref[...]
        if col0 is not None:
            w = w * jnp.where(lax.broadcasted_iota(jnp.int32, w.shape, 1) >= col0, 0.5, 1.0)
        w16_ref[...] = w.astype(BF16)
    k = k_ref[0]
    lam = lam_ref[...]
    lam_val = (jnp.exp(jnp.sum(lam[0:1] * lam[1:2], axis=-1, keepdims=True))
               - jnp.exp(jnp.sum(lam[2:3] * lam[3:4], axis=-1, keepdims=True)) + lambda_init)
    tq = ATTN_Q_SUB
    lane_group = (lax.broadcasted_iota(jnp.int32, (tq, 256), 1) // 32) % 4
    ones = jnp.ones((k.shape[0], ATTN_V_DIM), BF16)
    v_ext = [jnp.concatenate([v_ref[0, :, 128 * hh:128 * hh + 128], ones], axis=1) for hh in range(2)]
    for t in range(q_ref.shape[1] // tq):
        q = q_ref[0, t * tq:(t + 1) * tq, :]
        zero = jnp.zeros_like(q)
        q_all = jnp.concatenate([jnp.where(lane_group == hc, q, zero) for hc in range(4)], axis=0)
        s = _dot_nt(q_all, k)
        e = jnp.exp2(s - jnp.max(s, axis=-1, keepdims=True)).astype(BF16)
        for hh in range(2):
            r = _dot(e[2 * hh * tq:2 * (hh + 1) * tq], v_ext[hh])
            o = (r[0:tq, 0:ATTN_V_DIM] / r[0:tq, ATTN_V_DIM:]
                 - lam_val * (r[tq:2 * tq, 0:ATTN_V_DIM] / r[tq:2 * tq, ATTN_V_DIM:]))
            o = _rmsnorm_rows(o, g_ref[...]) * (1.0 - lambda_init)
            o_ref[0, t * tq:(t + 1) * tq, 128 * hh:128 * hh + 128] = o.astype(o_ref.dtype)


def _attention(q, k, v, lam, subln_g, lambda_init, later_weights, halve_from):
    bsz, seq, _ = q.shape
    tq = ATTN_Q_TILE
    n_q = seq // tq
    n_steps = bsz * 2 * n_q
    kv_spec = pl.BlockSpec((1, seq, 256), lambda b, p, i: (b, 0, p))
    step = lambda b, p, i: (b * 2 + p) * n_q + i
    slab_in = [pl.BlockSpec((None, w.shape[1] // n_steps, w.shape[2]),
                            functools.partial(lambda idx, b, p, i: (idx, step(b, p, i), 0), idx))
               for w, idx in later_weights]
    slab_out = [pl.BlockSpec((w.shape[1] // n_steps, w.shape[2]), lambda b, p, i: (step(b, p, i), 0))
                for w, _ in later_weights]
    outs = pl.pallas_call(
        functools.partial(_attn_kernel, lambda_init, tuple(halve_from)),
        grid=(bsz, 2, n_q),
        in_specs=[pl.BlockSpec((1, tq, 256), lambda b, p, i: (b, i, p)),
                  kv_spec, kv_spec,
                  _const_spec(lam.shape),
                  _const_spec(subln_g.shape)] + slab_in,
        out_specs=[pl.BlockSpec((1, tq, 256), lambda b, p, i: (b, i, p))] + slab_out,
        out_shape=[jax.ShapeDtypeStruct((bsz, seq, ATTN_WIDTH), BF16)]
        + [jax.ShapeDtypeStruct(w.shape[1:], BF16) for w, _ in later_weights],
        compiler_params=_params(3),
        name="diff_attention",
    )(q, k, v, lam, subln_g, *[w for w, _ in later_weights])
    return outs[0], outs[1:]


def _s5_kernel(n_chunks, ut_ref, mi_ref, min_ref, mo_ref, ctab_ref, y_ref):
    n_rows = ut_ref.shape[2]
    n_steps = n_chunks.bit_length() - 1
    ut2 = ut_ref[...].reshape(2 * S5_CW, n_rows)
    st = _dot_tn(ut2, min_ref[0])
    ctab = ctab_ref[0]
    pos = lax.broadcasted_iota(jnp.int32, (n_rows, 128), 0) % n_chunks
    zs = []
    for d in range(2):
        xr = st[:, 256 * d:256 * d + 128]
        xi = st[:, 256 * d + 128:256 * d + 256]
        for k in range(n_steps):
            s = 1 << k
            if d == 0:
                shift, keep = s, pos >= s
            else:
                shift, keep = n_rows - s, pos < n_chunks - s
            pr = ctab[4 * k + 2 * d:4 * k + 2 * d + 1]
            pi = ctab[4 * k + 2 * d + 1:4 * k + 2 * d + 2]
            sr = jnp.where(keep, pltpu.roll(xr, shift, 0), 0.0)
            si = jnp.where(keep, pltpu.roll(xi, shift, 0), 0.0)
            xr, xi = xr + pr * sr - pi * si, xi + pr * si + pi * sr
        if d == 0:
            shift, keep = 1, pos >= 1
        else:
            shift, keep = n_rows - 1, pos < n_chunks - 1
        zs += [jnp.where(keep, pltpu.roll(xr, shift, 0), 0.0), jnp.where(keep, pltpu.roll(xi, shift, 0), 0.0)]
    z = jnp.concatenate(zs, axis=1).astype(BF16)
    for g in range(2):
        y_ref[g] = _dot(mi_ref[g], ut_ref[g]) + _dot_nt(mo_ref[g], z)


def _s5_chunked(u_t, m_intra_t, m_in_pair, m_out_ext, ctab, n_chunks):
    n_groups, _, n_lanes = u_t.shape
    return pl.pallas_call(
        functools.partial(_s5_kernel, n_chunks),
        grid=(n_groups // 2,),
        in_specs=[pl.BlockSpec((2, S5_CW, n_lanes), lambda i: (i, 0, 0)),
                  pl.BlockSpec((2, S5_CW, S5_CW), lambda i: (i, 0, 0)),
                  pl.BlockSpec((1, 2 * S5_CW, 512), lambda i: (i, 0, 0)),
                  pl.BlockSpec((2, S5_CW, 512), lambda i: (i, 0, 0)),
                  pl.BlockSpec((1, 32, 128), lambda i: (i, 0, 0))],
        out_specs=pl.BlockSpec((2, S5_CW, n_lanes), lambda i: (i, 0, 0)),
        out_shape=jax.ShapeDtypeStruct((n_groups, S5_CW, n_lanes), F32),
        compiler_params=_params(1),
        name="s5_chunked",
    )(u_t, m_intra_t, m_in_pair, m_out_ext, ctab)


def _dot_hp(a, b):
    return jnp.dot(a, b, precision=lax.Precision.HIGHEST, preferred_element_type=F32)


def _bf16_terms(x):
    hi = x.astype(BF16)
    r = x - hi.astype(F32)
    mid = r.astype(BF16)
    return hi, mid, (r - mid.astype(F32)).astype(BF16)


def _spread(values, onehot):
    return sum(_dot(term, onehot) for term in _bf16_terms(values))


def _spread_rows(onehot, values):
    return sum(_dot(onehot, term) for term in _bf16_terms(values))


def _discretise(lr, li, log_step):
    lr = jnp.minimum(lr, -1e-4)
    dt = jnp.exp(log_step)
    mag = jnp.exp(lr * dt)
    ar = mag * jnp.cos(li * dt)
    ai = mag * jnp.sin(li * dt)
    den = lr * lr + li * li
    cr = ((ar - 1.0) * lr + ai * li) / den
    ci = (ai * lr - (ar - 1.0) * li) / den
    return lr * dt, li * dt, cr, ci


def _s5_ops_kernel(n_steps, colp_ref, rowp_ref, bcol_ref, crow_ref, dcol_ref,
                   mi_ref, min_ref, mo_ref, ctab_ref):
    t = S5_CHUNK
    n = S5_GROUP
    onehot = lambda m: jnp.where(m, 1.0, 0.0).astype(BF16)
    i16 = lambda shape, dim: lax.broadcasted_iota(jnp.int32, shape, dim)
    rep_lanes = onehot(i16((t, S5_CW), 1) // n == i16((t, S5_CW), 0))
    tile_lanes = onehot(i16((n, S5_CW), 1) % n == i16((n, S5_CW), 0))
    rep_rows = onehot(i16((S5_CW, t), 0) // n == i16((S5_CW, t), 1))
    tile_rows = onehot(i16((S5_CW, n), 0) % n == i16((S5_CW, n), 1))
    row_group = i16((2 * S5_STATE, S5_CW), 0) // S5_STATE
    lane_group = i16((1, 2 * S5_STATE), 1) // S5_STATE
    lane16 = i16((1, t), 1).astype(F32)
    row16 = i16((t, 1), 0).astype(F32)
    colp = colp_ref[0]
    rowp = rowp_ref[0]

    taps = [[None, None], [None, None]]
    min_t_rows = []
    mo_kinds = []
    for d in range(2):
        lmag, ang, cr, ci = _discretise(colp[:, 3 * d:3 * d + 1], colp[:, 3 * d + 1:3 * d + 2],
                                        colp[:, 3 * d + 2:3 * d + 3])
        b_re, b_im = bcol_ref[0, 2 * d], bcol_ref[0, 2 * d + 1]
        bbr = _spread(cr * b_re - ci * b_im, tile_lanes)
        bbi = _spread(cr * b_im + ci * b_re, tile_lanes)
        lag = (t - 1.0) - lane16 if d == 0 else lane16
        pmag = jnp.exp(lmag * lag)
        pr = _spread(pmag * jnp.cos(ang * lag), rep_lanes)
        pi = _spread(pmag * jnp.sin(ang * lag), rep_lanes)
        rr = pr * bbr - pi * bbi
        ri = pr * bbi + pi * bbr
        for part in (rr, ri):
            min_t_rows.append(jnp.concatenate([jnp.where(row_group == gs, part, 0.0) for gs in range(2)], axis=1))
        c_re, c_im = crow_ref[0, 2 * d], crow_ref[0, 2 * d + 1]
        for gs in range(2):
            own = lane_group == gs
            taps[gs][d] = (_dot_hp(jnp.where(own, c_re, 0.0), rr) - _dot_hp(jnp.where(own, c_im, 0.0), ri))

        lmag_r, ang_r, _, _ = _discretise(rowp[3 * d:3 * d + 1], rowp[3 * d + 1:3 * d + 2],
                                          rowp[3 * d + 2:3 * d + 3])
        tau = row16 + 1.0 if d == 0 else float(t) - row16
        qmag = jnp.exp(tau * lmag_r)
        qr16 = qmag * jnp.cos(tau * ang_r)
        qi16 = qmag * jnp.sin(tau * ang_r)
        qr = _spread_rows(rep_rows, qr16)
        qi = _spread_rows(rep_rows, qi16)
        cre = _spread_rows(tile_rows, c_re)
        cim = _spread_rows(tile_rows, c_im)
        mo_kinds += [qr * cre - qi * cim, -(qr * cim + qi * cre)]

        far = t - 1 if d == 0 else 0
        pw_r, pw_i = qr16[far:far + 1], qi16[far:far + 1]
        for k in range(n_steps):
            ctab_ref[0, 4 * k + 2 * d:4 * k + 2 * d + 1, :] = pw_r
            ctab_ref[0, 4 * k + 2 * d + 1:4 * k + 2 * d + 2, :] = pw_i
            pw_r, pw_i = pw_r * pw_r - pw_i * pw_i, 2.0 * pw_r * pw_i
    ctab_ref[0, 4 * n_steps:, :] = jnp.zeros((32 - 4 * n_steps, 2 * S5_STATE), F32)

    min_ref[0] = jnp.concatenate(min_t_rows, axis=0).T.astype(BF16)
    lane_group_wide = i16((1, 2 * S5_STATE), 1) // S5_STATE
    for gs in range(2):
        mo_ref[gs] = jnp.concatenate([jnp.where(lane_group_wide == gs, kind, 0.0) for kind in mo_kinds],
                                     axis=1).astype(BF16)
        zeros = jnp.zeros((n, S5_CW), F32)
        edge = (t - 1) * n
        lane = i16((n, 2 * S5_CW), 1)
        skip = jnp.where(lane == edge + i16((n, 2 * S5_CW), 0), dcol_ref[0, gs], 0.0)
        kk = (jnp.concatenate([taps[gs][0], zeros], axis=1)
              + pltpu.roll(jnp.concatenate([taps[gs][1], zeros], axis=1), edge, 1) + skip)
        strips = []
        for j in range(t):
            shift = (t - 1 - j) * n
            moved = kk if shift == 0 else pltpu.roll(kk, 2 * S5_CW - shift, 1)
            strips.append(moved[:, 0:S5_CW])
        mi_ref[gs] = jnp.concatenate(strips, axis=0).astype(BF16)


def _s5_operators(lam_re, lam_im, log_step, b_re, b_im, c_re, c_im, d_skip, n_steps):
    g2 = S5_GROUPS // 2
    zero = jnp.zeros_like(lam_re[0])
    step = [jnp.broadcast_to(log_step[d][:, None], lam_re[d].shape) for d in range(2)]
    params = jnp.stack([lam_re[0], lam_im[0], step[0], lam_re[1], lam_im[1], step[1], zero, zero])
    colp = params.transpose(1, 2, 0).reshape(g2, 2 * S5_STATE, 8)
    rowp = params.reshape(8, g2, 2 * S5_STATE).transpose(1, 0, 2)
    bcol = (jnp.stack([b_re[0], b_im[0], b_re[1], b_im[1]])
            .reshape(4, g2, 2 * S5_STATE, S5_GROUP).transpose(1, 0, 2, 3))
    crow = (jnp.stack([c_re[0], c_im[0], c_re[1], c_im[1]])
            .reshape(4, g2, 2, S5_GROUP, S5_STATE).transpose(1, 0, 3, 2, 4).reshape(g2, 4, S5_GROUP, 2 * S5_STATE))
    dcol = d_skip.reshape(g2, 2, S5_GROUP, 1)
    pair = lambda *shape: pl.BlockSpec((1,) + shape, lambda i: (i,) + (0,) * len(shape))
    return pl.pallas_call(
        functools.partial(_s5_ops_kernel, n_steps),
        grid=(g2,),
        in_specs=[pair(2 * S5_STATE, 8), pair(8, 2 * S5_STATE), pair(4, 2 * S5_STATE, S5_GROUP),
                  pair(4, S5_GROUP, 2 * S5_STATE), pair(2, S5_GROUP, 1)],
        out_specs=[pl.BlockSpec((2, S5_CW, S5_CW), lambda i: (i, 0, 0)), pair(2 * S5_CW, 512),
                   pl.BlockSpec((2, S5_CW, 512), lambda i: (i, 0, 0)), pair(32, 2 * S5_STATE)],
        out_shape=[jax.ShapeDtypeStruct((S5_GROUPS, S5_CW, S5_CW), BF16),
                   jax.ShapeDtypeStruct((g2, 2 * S5_CW, 512), BF16),
                   jax.ShapeDtypeStruct((S5_GROUPS, S5_CW, 512), BF16),
                   jax.ShapeDtypeStruct((g2, 32, 2 * S5_STATE), F32)],
        compiler_params=_params(1),
        name="s5_operators",
    )(colp, rowp, bcol, crow, dcol)


def _mlp(x1, g_ref, w1_ref, w2_ref):
    h = _rmsnorm_rows(x1, g_ref[...]).astype(BF16)
    acc = x1
    for j in range(D_FF // FF_TILE):
        hid = _dot(h, w1_ref[:, j * FF_TILE:(j + 1) * FF_TILE])
        hid = jnp.square(jnp.maximum(hid, 0.0)).astype(BF16)
        acc = acc + _dot(hid, w2_ref[j * FF_TILE:(j + 1) * FF_TILE, :])
    return acc


def _post_even_kernel(x_ref, a_ref, b_ref, wout_ref, g_ref, w1_ref, w2_ref, o_ref):
    mix = (_dot(a_ref[...], wout_ref[0:ATTN_WIDTH, :])
           + _dot(b_ref[...].astype(BF16), wout_ref[ATTN_WIDTH:, :]))
    o_ref[...] = _mlp(x_ref[...] + mix, g_ref, w1_ref, w2_ref)


def _post_even(x2, a_out, b_out, w_out, g_mlp, w1, w2):
    n_tok = x2.shape[0]
    tm = TOKEN_TILE
    return pl.pallas_call(
        _post_even_kernel,
        grid=(n_tok // tm,),
        in_specs=[pl.BlockSpec((tm, D_MODEL), lambda i: (i, 0)),
                  pl.BlockSpec((tm, ATTN_WIDTH), lambda i: (i, 0)),
                  pl.BlockSpec((tm, S5_WIDTH), lambda i: (i, 0)),
                  _const_spec(w_out.shape), _const_spec(g_mlp.shape), _const_spec(w1.shape),
                  _const_spec(w2.shape)],
        out_specs=pl.BlockSpec((tm, D_MODEL), lambda i: (i, 0)),
        out_shape=jax.ShapeDtypeStruct((n_tok, D_MODEL), F32),
        compiler_params=_params(1),
        name="post_even",
    )(x2, a_out, b_out, w_out, g_mlp, w1, w2)


def _post_odd_kernel(x_ref, m_ref, wout_ref, g_ref, w1_ref, w2_ref, gf_ref, o_ref):
    x1 = x_ref[...] + _dot(m_ref[...], wout_ref[...])
    o_ref[...] = _rmsnorm_rows(_mlp(x1, g_ref, w1_ref, w2_ref), gf_ref[...])


def _post_odd(x2, mixed, w_out, g_mlp, w1, w2, g_final):
    n_tok = x2.shape[0]
    tm = TOKEN_TILE
    return pl.pallas_call(
        _post_odd_kernel,
        grid=(n_tok // tm,),
        in_specs=[pl.BlockSpec((tm, D_MODEL), lambda i: (i, 0)),
                  pl.BlockSpec((tm, D_MODEL), lambda i: (i, 0)),
                  _const_spec(w_out.shape), _const_spec(g_mlp.shape),
                  _const_spec(w1.shape), _const_spec(w2.shape), _const_spec(g_final.shape)],
        out_specs=pl.BlockSpec((tm, D_MODEL), lambda i: (i, 0)),
        out_shape=jax.ShapeDtypeStruct((n_tok, D_MODEL), F32),
        compiler_params=_params(1),
        name="post_odd",
    )(x2, mixed, w_out, g_mlp, w1, w2, g_final)


def _pre_odd_kernel(x_ref, g_ref, w_ref, q_ref, i_ref, ff_ref, fb_ref, gate_ref):
    h = _rmsnorm_rows(x_ref[...], g_ref[...]).astype(BF16)
    for s, out_ref in enumerate((q_ref, i_ref, ff_ref, fb_ref, gate_ref)):
        out_ref[...] = _dot(h, w_ref[:, s * D_MODEL:(s + 1) * D_MODEL]).astype(out_ref.dtype)


def _pre_odd(x2, g, w):
    n_tok = x2.shape[0]
    tm = TOKEN_TILE
    row_spec = pl.BlockSpec((tm, D_MODEL), lambda i: (i, 0))
    lo = jax.ShapeDtypeStruct((n_tok, D_MODEL), BF16)
    hi = jax.ShapeDtypeStruct((n_tok, D_MODEL), F32)
    return pl.pallas_call(
        _pre_odd_kernel,
        grid=(n_tok // tm,),
        in_specs=[row_spec, _const_spec((1, D_MODEL)), _const_spec(w.shape)],
        out_specs=[row_spec] * 5,
        out_shape=[lo, lo, hi, hi, lo],
        compiler_params=_params(1),
        name="pre_odd",
    )(x2, g, w)


def _chunk_cumprod(x, reverse):
    n = x.shape[0]
    pos = lax.broadcasted_iota(jnp.int32, x.shape, 0) % HGRN_CHUNK
    s = 1
    while s < HGRN_CHUNK:
        if reverse:
            shifted = pltpu.roll(x, n - s, 0)
            x = x * jnp.where(pos < HGRN_CHUNK - s, shifted, 1.0)
        else:
            shifted = pltpu.roll(x, s, 0)
            x = x * jnp.where(pos >= s, shifted, 1.0)
        s *= 2
    return x


def _hgrn_kernel(layer, q_ref, v_ref, ff_ref, fb_ref, gate_ref, lbl_ref, ng_ref, o_ref,
                 qd_ref, dec_ref, kv_ref, acc_ref):
    seq = q_ref.shape[1]
    ch = HGRN_CHUNK
    blk = HGRN_BLOCK
    n_chunks = seq // ch
    per_blk = blk // ch
    logits = lbl_ref[...]
    soft = jnp.exp(logits - jnp.max(logits, axis=0, keepdims=True))
    soft = soft / jnp.sum(soft, axis=0, keepdims=True)
    lb = jnp.sum(soft[0:layer + 1], axis=0, keepdims=True) - soft[0:1]

    c2 = 0.5 * (1.0 - lb)
    c1 = lb + c2
    row = lax.broadcasted_iota(jnp.int32, (blk, blk), 0)
    col = lax.broadcasted_iota(jnp.int32, (blk, blk), 1)
    u32 = lambda a: a.astype(jnp.uint32)
    keep = (u32(row - col) <= u32(row % ch), u32(col - row) <= u32(ch - 1 - row % ch))
    row_chunk = lax.broadcasted_iota(jnp.int32, (blk, 128), 0) // ch

    def intra(j, carry):
        sl = pl.ds(pl.multiple_of(j * blk, blk), blk)
        v = v_ref[0, sl, :]
        vt = v.astype(F32).T.astype(BF16)
        qf = q_ref[0, sl, :].astype(F32)
        for d, f_ref in enumerate((ff_ref, fb_ref)):
            ct = c2 * jnp.tanh(f_ref[0, sl, :])
            eb = _chunk_cumprod(c1 + ct, reverse=(d == 1))
            k_inv = (c2 - ct) / eb
            eb3 = eb.reshape(per_blk, ch, 128)
            e_end = eb3[:, ch - 1:ch, :] if d == 0 else eb3[:, 0:1, :]
            dec_ref[d, pl.ds(j * per_blk, per_blk)] = e_end
            q_dec = (qf * eb).astype(BF16)
            qd_ref[d, sl, :] = q_dec
            k_dec = (k_inv.reshape(per_blk, ch, 128) * e_end).reshape(blk, 128).astype(BF16)

            s = _dot_nt(q_dec, k_inv.astype(BF16))
            p = jnp.where(keep[d], s, 0.0).astype(BF16)
            acc_ref[d, sl, :] = _dot(p, v)
            rhs = jnp.concatenate([jnp.where(row_chunk == cc, k_dec, jnp.zeros_like(k_dec))
                                   for cc in range(per_blk)], axis=1)
            kvs = _dot(vt, rhs)
            for cc in range(per_blk):
                kv_ref[d, j * per_blk + cc] = kvs[:, 128 * cc:128 * (cc + 1)]
        return carry

    lax.fori_loop(0, seq // blk, intra, 0, unroll=4)

    def finish(rows):
        o = _rmsnorm_rows(acc_ref[0, rows, :] + acc_ref[1, rows, :], ng_ref[...])
        gate = 0.5 + 0.5 * jnp.tanh(gate_ref[0, rows, :].astype(F32))
        o_ref[0, rows, :] = (o * gate).astype(o_ref.dtype)

    states = [jnp.zeros((128, 128), F32)] * 2
    for n in range(n_chunks):
        for d in range(2):
            c = n if d == 0 else n_chunks - 1 - n
            rows = slice(c * ch, (c + 1) * ch)
            acc_ref[d, rows, :] += _dot_nt(qd_ref[d, rows, :], states[d].astype(BF16))
            states[d] = dec_ref[d, c] * states[d] + kv_ref[d, c]
        if 2 * n >= n_chunks:
            finish(slice(n * ch, (n + 1) * ch))
            finish(slice((n_chunks - 1 - n) * ch, (n_chunks - n) * ch))


def _hgrn(q, v, ff, fb, gate, lb_logits, norm_g, layer):
    bsz, seq, _ = q.shape
    n_chunks = seq // HGRN_CHUNK
    head_spec = pl.BlockSpec((1, seq, 128), lambda b, h: (b, 0, h))
    return pl.pallas_call(
        functools.partial(_hgrn_kernel, layer),
        grid=(bsz, HGRN_HEADS),
        in_specs=[head_spec] * 5 + [pl.BlockSpec((DEPTH, 128), lambda b, h: (0, h)),
                                    pl.BlockSpec((1, 128), lambda b, h: (0, h))],
        out_specs=head_spec,
        out_shape=jax.ShapeDtypeStruct((bsz, seq, D_MODEL), BF16),
        scratch_shapes=[
           pltpu.VMEM((2, seq, 128), BF16),
           pltpu.VMEM((2, n_chunks, 1, 128), F32),
           pltpu.VMEM((2, n_chunks, 128, 128), F32),
           pltpu.VMEM((2, seq, 128), F32)],
        compiler_params=_params(2),
        name="hgrn2",
    )(q, v, ff, fb, gate, lb_logits, norm_g)


def _rope_pair_tables(seq):
    inv = ROPE_THETA ** (-jnp.arange(0, ATTN_QK_DIM, 2, dtype=F32) / ATTN_QK_DIM)
    ang = jnp.arange(seq, dtype=F32)[:, None] * inv[None, :]
    return jnp.tile(jnp.cos(ang), (1, 4)), jnp.tile(jnp.sin(ang), (1, 4))


def _to_pair_layout(w):
    idx = jnp.arange(ATTN_QK_WIDTH).reshape(2, 2, 2, 2, 32)
    return w[:, idx.transpose(0, 3, 1, 2, 4).reshape(-1)]


def kernel(x, norm_mix_g, norm_mlp_g, final_norm_g, w_ff_in, w_ff_out, w_in_even, w_out_even, diff_lambda, diff_subln_g, s5_lam_re, s5_lam_im, s5_log_step, s5_b_re, s5_b_im, s5_c_re, s5_c_im, s5_d, s5_w_glu, s5_b_glu, w_in_odd, w_out_odd, hgrn_norm_g, hgrn_lb_logits):
    bsz, seq, _ = x.shape
    n_tok = bsz * seq
    n_chunks = seq // S5_CHUNK
    assert n_chunks & (n_chunks - 1) == 0 and n_chunks % 128 == 0
    x2 = x.reshape(n_tok, D_MODEL)
    cos, sin = _rope_pair_tables(seq)

    assert DEPTH == 2, "an even (attention + S5) layer followed by an odd (HGRN2) layer"
    for layer in range(DEPTH):
        g_mix = norm_mix_g[layer].reshape(1, D_MODEL)
        g_mlp = norm_mlp_g[layer].reshape(1, D_MODEL)
        if layer % 2 == 0:
            e = layer // 2
            w = w_in_even[e]
            w_qkv = jnp.concatenate([_to_pair_layout(w[:, :ATTN_QK_WIDTH]),
                                     _to_pair_layout(w[:, ATTN_QK_WIDTH:2 * ATTN_QK_WIDTH]),
                                     w[:, 2 * ATTN_QK_WIDTH:2 * ATTN_QK_WIDTH + ATTN_WIDTH]], axis=1).astype(BF16)
            wu = w[:, 2 * ATTN_QK_WIDTH + ATTN_WIDTH:].astype(BF16)
            q, k, v = _pre_even(x2, g_mix, w_qkv, cos, sin, seq)
            u_t = _s5_in(x2, g_mix, wu, bsz, seq)
            lambda_init = 0.8 - 0.6 * math.exp(-0.3 * layer)
            a_out, w16 = _attention(q.reshape(bsz, seq, -1), k.reshape(bsz, seq, -1), v.reshape(bsz, seq, -1),
                                    diff_lambda[e], diff_subln_g[e].reshape(1, ATTN_V_DIM), lambda_init,
                                    [(w_out_even, e), (w_ff_in, layer), (w_ff_out, layer),
                                     (w_in_odd, e), (w_out_odd, e), (w_ff_in, layer + 1), (w_ff_out, layer + 1)],
                                    [None, None, None, 2 * D_MODEL, None, None, None])
            w_out_e, w1, w2, w_in_o, w_out_o, w1_next, w2_next = w16
            ops = _s5_operators(s5_lam_re[e], s5_lam_im[e], s5_log_step[e], s5_b_re[e], s5_b_im[e],
                                s5_c_re[e], s5_c_im[e], s5_d[e], n_chunks.bit_length() - 1)
            y_t = _s5_chunked(u_t.reshape(S5_GROUPS, S5_CW, bsz * n_chunks), *ops, n_chunks)
            b_out = _s5_out(y_t.reshape(S5_GROUPS, S5_CHUNK, S5_GROUP, bsz * n_chunks),
                            s5_w_glu[e].T.astype(BF16), s5_b_glu[e].reshape(S5_WIDTH, 1), bsz, seq)
            x2 = _post_even(x2, a_out.reshape(n_tok, ATTN_WIDTH), b_out, w_out_e, g_mlp, w1, w2)
        else:
            o_i = layer // 2
            q, v, ff, fb, gate = _pre_odd(x2, g_mix, w_in_o)
            shp = (bsz, seq, D_MODEL)
            mixed = _hgrn(q.reshape(shp), v.reshape(shp), ff.reshape(shp), fb.reshape(shp), gate.reshape(shp),
                          hgrn_lb_logits, hgrn_norm_g[o_i].reshape(1, D_MODEL), layer)
            x2 = _post_odd(x2, mixed.reshape(n_tok, D_MODEL), w_out_o, g_mlp, w1_next, w2_next,
                           final_norm_g.reshape(1, D_MODEL))
    return x2.reshape(bsz, seq, D_MODEL)
```

```python
import functools
import math

import jax
import jax.numpy as jnp
from jax import lax
from jax.experimental import pallas as pl
from jax.experimental.pallas import tpu as pltpu

D_MODEL = 1024
DEPTH = 2
ATTN_HEADS = 4
ATTN_QK_DIM = 64
ATTN_V_DIM = 128
ATTN_QK_WIDTH = 512
ATTN_WIDTH = 512
ROPE_THETA = 10000.0
S5_WIDTH = 512
S5_GROUP = 16
S5_GROUPS = 32
S5_STATE = 64
HGRN_HEADS = 8
HGRN_CHUNK = 64
HGRN_BLOCK = 256
D_FF = 4096
EPS = 1e-6

S5_CHUNK = 16
S5_CW = S5_CHUNK * S5_GROUP

TOKEN_TILE = 1024
ATTN_Q_TILE = 1024
ATTN_Q_SUB = 512
FF_TILE = 1024
VMEM_LIMIT = 56 * 1024 * 1024

BF16 = jnp.bfloat16
F32 = jnp.float32


def _const_spec(shape):
    nd = len(shape)
    return pl.BlockSpec(shape, lambda *_: (0,) * nd, pipeline_mode=pl.Buffered(1))


def _params(n_axes):
    return pltpu.CompilerParams(dimension_semantics=("arbitrary",) * n_axes,
                                vmem_limit_bytes=VMEM_LIMIT)


def _rmsnorm_rows(x, g):
    ms = jnp.mean(x * x, axis=-1, keepdims=True)
    return x * lax.rsqrt(ms + EPS) * g


def _gelu_tanh(x):
    c = math.sqrt(2.0 / math.pi)
    return 0.5 * x * (1.0 + jnp.tanh(c * (x + 0.044715 * (x * x * x))))


def _dot(a, b):
    return jnp.dot(a, b, preferred_element_type=F32)


def _dot_nt(a, b):
    return lax.dot_general(a, b, (((1,), (1,)), ((), ())), preferred_element_type=F32)


def _dot_tn(a, b):
    return lax.dot_general(a, b, (((0,), (0,)), ((), ())), preferred_element_type=F32)


def _pre_even_kernel(x_ref, g_ref, w_ref, cos_ref, sin_ref, q_ref, k_ref, v_ref):
    h = _rmsnorm_rows(x_ref[...], g_ref[...]).astype(BF16)
    cos = cos_ref[...]
    sin = sin_ref[...]
    for out_ref, base, scale in ((q_ref, 0, ATTN_QK_DIM ** -0.5 * math.log2(math.e)),
                                 (k_ref, ATTN_QK_WIDTH, 1.0)):
        p = _dot(h, w_ref[:, base:base + ATTN_QK_WIDTH])
        for pair in range(2):
            lo = p[:, 256 * pair:256 * pair + 128]
            hi = p[:, 256 * pair + 128:256 * pair + 256]
            out_ref[:, 256 * pair:256 * pair + 128] = ((lo * cos - hi * sin) * scale).astype(BF16)
            out_ref[:, 256 * pair + 128:256 * pair + 256] = ((hi * cos + lo * sin) * scale).astype(BF16)
    v_ref[...] = _dot(h, w_ref[:, 2 * ATTN_QK_WIDTH:]).astype(BF16)


def _pre_even(x2, g, w_qkv, cos, sin, seq):
    n_tok = x2.shape[0]
    tm = TOKEN_TILE
    n_pos_blocks = seq // tm
    out = jax.ShapeDtypeStruct((n_tok, 512), BF16)
    row_spec = pl.BlockSpec((tm, 512), lambda i: (i, 0))
    rope_spec = pl.BlockSpec((tm, 128), lambda i: (i % n_pos_blocks, 0))
    return pl.pallas_call(
        _pre_even_kernel,
        grid=(n_tok // tm,),
        in_specs=[pl.BlockSpec((tm, D_MODEL), lambda i: (i, 0)), _const_spec((1, D_MODEL)),
                  _const_spec(w_qkv.shape), rope_spec, rope_spec],
        out_specs=[row_spec] * 3,
        out_shape=[out] * 3,
        compiler_params=_params(1),
        name="pre_even",
    )(x2, g, w_qkv, cos, sin)


def _s5_in_kernel(x_ref, g_ref, wu_ref, ut_ref, u_scr):
    nc = x_ref.shape[0] // S5_CHUNK
    groups_per_slab = 128 // S5_GROUP
    h = _rmsnorm_rows(x_ref[...], g_ref[...]).astype(BF16)
    u = _dot(h, wu_ref[...])
    for j in range(S5_WIDTH // 128):
        u_scr[j] = u[:, 128 * j:128 * (j + 1)]
    for j in range(S5_WIDTH // 128):
        for ph in range(S5_CHUNK):
            t = u_scr[j, pl.ds(ph, nc, stride=S5_CHUNK), :]
            ut_ref[groups_per_slab * j:groups_per_slab * (j + 1), ph, :, :] = (
                t.T.astype(BF16).reshape(groups_per_slab, S5_GROUP, nc))


def _s5_in(x2, g, wu, bsz, seq):
    nc = seq // S5_CHUNK
    return pl.pallas_call(
        _s5_in_kernel,
        grid=(bsz,),
        in_specs=[pl.BlockSpec((seq, D_MODEL), lambda b: (b, 0)), _const_spec((1, D_MODEL)),
                  _const_spec(wu.shape)],
        out_specs=pl.BlockSpec((S5_GROUPS, S5_CHUNK, S5_GROUP, nc), lambda b: (0, 0, 0, b)),
        out_shape=jax.ShapeDtypeStruct((S5_GROUPS, S5_CHUNK, S5_GROUP, bsz * nc), BF16),
        scratch_shapes=[pltpu.VMEM((S5_WIDTH // 128, seq, 128), F32)],
        compiler_params=_params(1),
        name="s5_in",
    )(x2, g, wu)


def _s5_out_kernel(yt_ref, wglut_ref, bglu_ref, o_ref, b_scr):
    nc = yt_ref.shape[3]
    yt = jnp.concatenate([yt_ref[:, ph, :, :].reshape(S5_WIDTH, nc) for ph in range(S5_CHUNK)], axis=1)
    yt = _gelu_tanh(yt)
    z = _dot(wglut_ref[...], yt.astype(BF16)) + bglu_ref[...]
    bt = yt * (0.5 + 0.5 * jnp.tanh(0.5 * z))
    for j in range(S5_WIDTH // 128):
        for ph in range(S5_CHUNK):
            b_scr[j, pl.ds(ph, nc, stride=S5_CHUNK), :] = bt[128 * j:128 * (j + 1), ph * nc:(ph + 1) * nc].T
    for j in range(S5_WIDTH // 128):
        o_ref[:, 128 * j:128 * (j + 1)] = b_scr[j]


def _s5_out(y_t, w_glu_t, b_glu_col, bsz, seq):
    nc = seq // S5_CHUNK
    return pl.pallas_call(
        _s5_out_kernel,
        grid=(bsz,),
        in_specs=[pl.BlockSpec((S5_GROUPS, S5_CHUNK, S5_GROUP, nc), lambda b: (0, 0, 0, b)),
                  _const_spec(w_glu_t.shape), _const_spec(b_glu_col.shape)],
        out_specs=pl.BlockSpec((seq, S5_WIDTH), lambda b: (b, 0)),
        out_shape=jax.ShapeDtypeStruct((bsz * seq, S5_WIDTH), F32),
        scratch_shapes=[pltpu.VMEM((S5_WIDTH // 128, seq, 128), F32)],
        compiler_params=_params(1),
        name="s5_out",
    )(y_t, w_glu_t, b_glu_col)


def _attn_kernel(lambda_init, halve_from, q_ref, k_ref, v_ref, lam_ref, g_ref, *refs):
    n_weights = len(halve_from)
    o_ref = refs[n_weights]
    for w_ref, w16_ref, col0 in zip(refs[:n_weights], refs[n_weights + 1:], halve_from):
        w = w_ref[...]
        if col0 is not None:
            w = w * jnp.where(lax.broadcasted_iota(jnp.int32, w.shape, 1) >= col0, 0.5, 1.0)
        w16_ref[...] = w.astype(BF16)
    k = k_ref[0]
    lam = lam_ref[...]
    lam_val = (jnp.exp(jnp.sum(lam[0:1] * lam[1:2], axis=-1, keepdims=True))
               - jnp.exp(jnp.sum(lam[2:3] * lam[3:4], axis=-1, keepdims=True)) + lambda_init)
    tq = ATTN_Q_SUB
    lane_group = (lax.broadcasted_iota(jnp.int32, (tq, 256), 1) // 32) % 4
    ones = jnp.ones((k.shape[0], ATTN_V_DIM), BF16)
    v_ext = [jnp.concatenate([v_ref[0, :, 128 * hh:128 * hh + 128], ones], axis=1) for hh in range(2)]
    for t in range(q_ref.shape[1] // tq):
        q = q_ref[0, t * tq:(t + 1) * tq, :]
        zero = jnp.zeros_like(q)
        q_all = jnp.concatenate([jnp.where(lane_group == hc, q, zero) for hc in range(4)], axis=0)
        s = _dot_nt(q_all, k)
        e = jnp.exp2(s - jnp.max(s, axis=-1, keepdims=True)).astype(BF16)
        for hh in range(2):
            r = _dot(e[2 * hh * tq:2 * (hh + 1) * tq], v_ext[hh])
            o = (r[0:tq, 0:ATTN_V_DIM] / r[0:tq, ATTN_V_DIM:]
                 - lam_val * (r[tq:2 * tq, 0:ATTN_V_DIM] / r[tq:2 * tq, ATTN_V_DIM:]))
            o = _rmsnorm_rows(o, g_ref[...]) * (1.0 - lambda_init)
            o_ref[0, t * tq:(t + 1) * tq, 128 * hh:128 * hh + 128] = o.astype(o_ref.dtype)


def _attention(q, k, v, lam, subln_g, lambda_init, later_weights, halve_from):
    bsz, seq, _ = q.shape
    tq = ATTN_Q_TILE
    n_q = seq // tq
    n_steps = bsz * 2 * n_q
    kv_spec = pl.BlockSpec((1, seq, 256), lambda b, p, i: (b, 0, p))
    step = lambda b, p, i: (b * 2 + p) * n_q + i
    slab_in = [pl.BlockSpec((None, w.shape[1] // n_steps, w.shape[2]),
                            functools.partial(lambda idx, b, p, i: (idx, step(b, p, i), 0), idx))
               for w, idx in later_weights]
    slab_out = [pl.BlockSpec((w.shape[1] // n_steps, w.shape[2]), lambda b, p, i: (step(b, p, i), 0))
                for w, _ in later_weights]
    outs = pl.pallas_call(
        functools.partial(_attn_kernel, lambda_init, tuple(halve_from)),
        grid=(bsz, 2, n_q),
        in_specs=[pl.BlockSpec((1, tq, 256), lambda b, p, i: (b, i, p)),
                  kv_spec, kv_spec,
                  _const_spec(lam.shape),
                  _const_spec(subln_g.shape)] + slab_in,
        out_specs=[pl.BlockSpec((1, tq, 256), lambda b, p, i: (b, i, p))] + slab_out,
        out_shape=[jax.ShapeDtypeStruct((bsz, seq, ATTN_WIDTH), BF16)]
        + [jax.ShapeDtypeStruct(w.shape[1:], BF16) for w, _ in later_weights],
        compiler_params=_params(3),
        name="diff_attention",
    )(q, k, v, lam, subln_g, *[w for w, _ in later_weights])
    return outs[0], outs[1:]


def _s5_kernel(n_chunks, ut_ref, mi_ref, min_ref, mo_ref, ctab_ref, y_ref):
    n_rows = ut_ref.shape[2]
    n_steps = n_chunks.bit_length() - 1
    ut2 = ut_ref[...].reshape(2 * S5_CW, n_rows)
    st = _dot_tn(ut2, min_ref[0])
    ctab = ctab_ref[0]
    pos = lax.broadcasted_iota(jnp.int32, (n_rows, 128), 0) % n_chunks
    zs = []
    for d in range(2):
        xr = st[:, 256 * d:256 * d + 128]
        xi = st[:, 256 * d + 128:256 * d + 256]
        for k in range(n_steps):
            s = 1 << k
            if d == 0:
                shift, keep = s, pos >= s
            else:
                shift, keep = n_rows - s, pos < n_chunks - s
            pr = ctab[4 * k + 2 * d:4 * k + 2 * d + 1]
            pi = ctab[4 * k + 2 * d + 1:4 * k + 2 * d + 2]
            sr = jnp.where(keep, pltpu.roll(xr, shift, 0), 0.0)
            si = jnp.where(keep, pltpu.roll(xi, shift, 0), 0.0)
            xr, xi = xr + pr * sr - pi * si, xi + pr * si + pi * sr
        if d == 0:
            shift, keep = 1, pos >= 1
        else:
            shift, keep = n_rows - 1, pos < n_chunks - 1
        zs += [jnp.where(keep, pltpu.roll(xr, shift, 0), 0.0), jnp.where(keep, pltpu.roll(xi, shift, 0), 0.0)]
    z = jnp.concatenate(zs, axis=1).astype(BF16)
    for g in range(2):
        y_ref[g] = _dot(mi_ref[g], ut_ref[g]) + _dot_nt(mo_ref[g], z)


def _s5_chunked(u_t, m_intra_t, m_in_pair, m_out_ext, ctab, n_chunks):
    n_groups, _, n_lanes = u_t.shape
    return pl.pallas_call(
        functools.partial(_s5_kernel, n_chunks),
        grid=(n_groups // 2,),
        in_specs=[pl.BlockSpec((2, S5_CW, n_lanes), lambda i: (i, 0, 0)),
                  pl.BlockSpec((2, S5_CW, S5_CW), lambda i: (i, 0, 0)),
                  pl.BlockSpec((1, 2 * S5_CW, 512), lambda i: (i, 0, 0)),
                  pl.BlockSpec((2, S5_CW, 512), lambda i: (i, 0, 0)),
                  pl.BlockSpec((1, 32, 128), lambda i: (i, 0, 0))],
        out_specs=pl.BlockSpec((2, S5_CW, n_lanes), lambda i: (i, 0, 0)),
        out_shape=jax.ShapeDtypeStruct((n_groups, S5_CW, n_lanes), F32),
        compiler_params=_params(1),
        name="s5_chunked",
    )(u_t, m_intra_t, m_in_pair, m_out_ext, ctab)


def _dot_hp(a, b):
    return jnp.dot(a, b, precision=lax.Precision.HIGHEST, preferred_element_type=F32)


def _bf16_terms(x):
    hi = x.astype(BF16)
    r = x - hi.astype(F32)
    mid = r.astype(BF16)
    return hi, mid, (r - mid.astype(F32)).astype(BF16)


def _spread(values, onehot):
    return sum(_dot(term, onehot) for term in _bf16_terms(values))


def _spread_rows(onehot, values):
    return sum(_dot(onehot, term) for term in _bf16_terms(values))


def _discretise(lr, li, log_step):
    lr = jnp.minimum(lr, -1e-4)
    dt = jnp.exp(log_step)
    mag = jnp.exp(lr * dt)
    ar = mag * jnp.cos(li * dt)
    ai = mag * jnp.sin(li * dt)
    den = lr * lr + li * li
    cr = ((ar - 1.0) * lr + ai * li) / den
    ci = (ai * lr - (ar - 1.0) * li) / den
    return lr * dt, li * dt, cr, ci


def _s5_ops_kernel(n_steps, colp_ref, rowp_ref, bcol_ref, crow_ref, dcol_ref,
                   mi_ref, min_ref, mo_ref, ctab_ref):
    t = S5_CHUNK
    n = S5_GROUP
    onehot = lambda m: jnp.where(m, 1.0, 0.0).astype(BF16)
    i16 = lambda shape, dim: lax.broadcasted_iota(jnp.int32, shape, dim)
    rep_lanes = onehot(i16((t, S5_CW), 1) // n == i16((t, S5_CW), 0))
    tile_lanes = onehot(i16((n, S5_CW), 1) % n == i16((n, S5_CW), 0))
    rep_rows = onehot(i16((S5_CW, t), 0) // n == i16((S5_CW, t), 1))
    tile_rows = onehot(i16((S5_CW, n), 0) % n == i16((S5_CW, n), 1))
    row_group = i16((2 * S5_STATE, S5_CW), 0) // S5_STATE
    lane_group = i16((1, 2 * S5_STATE), 1) // S5_STATE
    lane16 = i16((1, t), 1).astype(F32)
    row16 = i16((t, 1), 0).astype(F32)
    colp = colp_ref[0]
    rowp = rowp_ref[0]

    taps = [[None, None], [None, None]]
    min_t_rows = []
    mo_kinds = []
    for d in range(2):
        lmag, ang, cr, ci = _discretise(colp[:, 3 * d:3 * d + 1], colp[:, 3 * d + 1:3 * d + 2],
                                        colp[:, 3 * d + 2:3 * d + 3])
        b_re, b_im = bcol_ref[0, 2 * d], bcol_ref[0, 2 * d + 1]
        bbr = _spread(cr * b_re - ci * b_im, tile_lanes)
        bbi = _spread(cr * b_im + ci * b_re, tile_lanes)
        lag = (t - 1.0) - lane16 if d == 0 else lane16
        pmag = jnp.exp(lmag * lag)
        pr = _spread(pmag * jnp.cos(ang * lag), rep_lanes)
        pi = _spread(pmag * jnp.sin(ang * lag), rep_lanes)
        rr = pr * bbr - pi * bbi
        ri = pr * bbi + pi * bbr
        for part in (rr, ri):
            min_t_rows.append(jnp.concatenate([jnp.where(row_group == gs, part, 0.0) for gs in range(2)], axis=1))
        c_re, c_im = crow_ref[0, 2 * d], crow_ref[0, 2 * d + 1]
        for gs in range(2):
            own = lane_group == gs
            taps[gs][d] = (_dot_hp(jnp.where(own, c_re, 0.0), rr) - _dot_hp(jnp.where(own, c_im, 0.0), ri))

        lmag_r, ang_r, _, _ = _discretise(rowp[3 * d:3 * d + 1], rowp[3 * d + 1:3 * d + 2],
                                          rowp[3 * d + 2:3 * d + 3])
        tau = row16 + 1.0 if d == 0 else float(t) - row16
        qmag = jnp.exp(tau * lmag_r)
        qr16 = qmag * jnp.cos(tau * ang_r)
        qi16 = qmag * jnp.sin(tau * ang_r)
        qr = _spread_rows(rep_rows, qr16)
        qi = _spread_rows(rep_rows, qi16)
        cre = _spread_rows(tile_rows, c_re)
        cim = _spread_rows(tile_rows, c_im)
        mo_kinds += [qr * cre - qi * cim, -(qr * cim + qi * cre)]

        far = t - 1 if d == 0 else 0
        pw_r, pw_i = qr16[far:far + 1], qi16[far:far + 1]
        for k in range(n_steps):
            ctab_ref[0, 4 * k + 2 * d:4 * k + 2 * d + 1, :] = pw_r
            ctab_ref[0, 4 * k + 2 * d + 1:4 * k + 2 * d + 2, :] = pw_i
            pw_r, pw_i = pw_r * pw_r - pw_i * pw_i, 2.0 * pw_r * pw_i
    ctab_ref[0, 4 * n_steps:, :] = jnp.zeros((32 - 4 * n_steps, 2 * S5_STATE), F32)

    min_ref[0] = jnp.concatenate(min_t_rows, axis=0).T.astype(BF16)
    lane_group_wide = i16((1, 2 * S5_STATE), 1) // S5_STATE
    for gs in range(2):
        mo_ref[gs] = jnp.concatenate([jnp.where(lane_group_wide == gs, kind, 0.0) for kind in mo_kinds],
                                     axis=1).astype(BF16)
        zeros = jnp.zeros((n, S5_CW), F32)
        edge = (t - 1) * n
        lane = i16((n, 2 * S5_CW), 1)
        skip = jnp.where(lane == edge + i16((n, 2 * S5_CW), 0), dcol_ref[0, gs], 0.0)
        kk = (jnp.concatenate([taps[gs][0], zeros], axis=1)
              + pltpu.roll(jnp.concatenate([taps[gs][1], zeros], axis=1), edge, 1) + skip)
        strips = []
        for j in range(t):
            shift = (t - 1 - j) * n
            moved = kk if shift == 0 else pltpu.roll(kk, 2 * S5_CW - shift, 1)
            strips.append(moved[:, 0:S5_CW])
        mi_ref[gs] = jnp.concatenate(strips, axis=0).astype(BF16)


def _s5_operators(lam_re, lam_im, log_step, b_re, b_im, c_re, c_im, d_skip, n_steps):
    g2 = S5_GROUPS // 2
    zero = jnp.zeros_like(lam_re[0])
    step = [jnp.broadcast_to(log_step[d][:, None], lam_re[d].shape) for d in range(2)]
    params = jnp.stack([lam_re[0], lam_im[0], step[0], lam_re[1], lam_im[1], step[1], zero, zero])
    colp = params.transpose(1, 2, 0).reshape(g2, 2 * S5_STATE, 8)
    rowp = params.reshape(8, g2, 2 * S5_STATE).transpose(1, 0, 2)
    bcol = (jnp.stack([b_re[0], b_im[0], b_re[1], b_im[1]])
            .reshape(4, g2, 2 * S5_STATE, S5_GROUP).transpose(1, 0, 2, 3))
    crow = (jnp.stack([c_re[0], c_im[0], c_re[1], c_im[1]])
            .reshape(4, g2, 2, S5_GROUP, S5_STATE).transpose(1, 0, 3, 2, 4).reshape(g2, 4, S5_GROUP, 2 * S5_STATE))
    dcol = d_skip.reshape(g2, 2, S5_GROUP, 1)
    pair = lambda *shape: pl.BlockSpec((1,) + shape, lambda i: (i,) + (0,) * len(shape))
    return pl.pallas_call(
        functools.partial(_s5_ops_kernel, n_steps),
        grid=(g2,),
        in_specs=[pair(2 * S5_STATE, 8), pair(8, 2 * S5_STATE), pair(4, 2 * S5_STATE, S5_GROUP),
                  pair(4, S5_GROUP, 2 * S5_STATE), pair(2, S5_GROUP, 1)],
        out_specs=[pl.BlockSpec((2, S5_CW, S5_CW), lambda i: (i, 0, 0)), pair(2 * S5_CW, 512),
                   pl.BlockSpec((2, S5_CW, 512), lambda i: (i, 0, 0)), pair(32, 2 * S5_STATE)],
        out_shape=[jax.ShapeDtypeStruct((S5_GROUPS, S5_CW, S5_CW), BF16),
                   jax.ShapeDtypeStruct((g2, 2 * S5_CW, 512), BF16),
                   jax.ShapeDtypeStruct((S5_GROUPS, S5_CW, 512), BF16),
                   jax.ShapeDtypeStruct((g2, 32, 2 * S5_STATE), F32)],
        compiler_params=_params(1),
        name="s5_operators",
    )(colp, rowp, bcol, crow, dcol)


def _mlp(x1, g_ref, w1_ref, w2_ref):
    h = _rmsnorm_rows(x1, g_ref[...]).astype(BF16)
    acc = x1
    for j in range(D_FF // FF_TILE):
        hid = _dot(h, w1_ref[:, j * FF_TILE:(j + 1) * FF_TILE])
        hid = jnp.square(jnp.maximum(hid, 0.0)).astype(BF16)
        acc = acc + _dot(hid, w2_ref[j * FF_TILE:(j + 1) * FF_TILE, :])
    return acc


def _post_even_kernel(x_ref, a_ref, b_ref, wout_ref, g_ref, w1_ref, w2_ref, o_ref):
    mix = (_dot(a_ref[...], wout_ref[0:ATTN_WIDTH, :])
           + _dot(b_ref[...].astype(BF16), wout_ref[ATTN_WIDTH:, :]))
    o_ref[...] = _mlp(x_ref[...] + mix, g_ref, w1_ref, w2_ref)


def _post_even(x2, a_out, b_out, w_out, g_mlp, w1, w2):
    n_tok = x2.shape[0]
    tm = TOKEN_TILE
    return pl.pallas_call(
        _post_even_kernel,
        grid=(n_tok // tm,),
        in_specs=[pl.BlockSpec((tm, D_MODEL), lambda i: (i, 0)),
                  pl.BlockSpec((tm, ATTN_WIDTH), lambda i: (i, 0)),
                  pl.BlockSpec((tm, S5_WIDTH), lambda i: (i, 0)),
                  _const_spec(w_out.shape), _const_spec(g_mlp.shape), _const_spec(w1.shape),
                  _const_spec(w2.shape)],
        out_specs=pl.BlockSpec((tm, D_MODEL), lambda i: (i, 0)),
        out_shape=jax.ShapeDtypeStruct((n_tok, D_MODEL), F32),
        compiler_params=_params(1),
        name="post_even",
    )(x2, a_out, b_out, w_out, g_mlp, w1, w2)


def _post_odd_kernel(x_ref, m_ref, wout_ref, g_ref, w1_ref, w2_ref, gf_ref, o_ref):
    x1 = x_ref[...] + _dot(m_ref[...], wout_ref[...])
    o_ref[...] = _rmsnorm_rows(_mlp(x1, g_ref, w1_ref, w2_ref), gf_ref[...])


def _post_odd(x2, mixed, w_out, g_mlp, w1, w2, g_final):
    n_tok = x2.shape[0]
    tm = TOKEN_TILE
    return pl.pallas_call(
        _post_odd_kernel,
        grid=(n_tok // tm,),
        in_specs=[pl.BlockSpec((tm, D_MODEL), lambda i: (i, 0)),
                  pl.BlockSpec((tm, D_MODEL), lambda i: (i, 0)),
                  _const_spec(w_out.shape), _const_spec(g_mlp.shape),
                  _const_spec(w1.shape), _const_spec(w2.shape), _const_spec(g_final.shape)],
        out_specs=pl.BlockSpec((tm, D_MODEL), lambda i: (i, 0)),
        out_shape=jax.ShapeDtypeStruct((n_tok, D_MODEL), F32),
        compiler_params=_params(1),
        name="post_odd",
    )(x2, mixed, w_out, g_mlp, w1, w2, g_final)


def _pre_odd_kernel(x_ref, g_ref, w_ref, q_ref, i_ref, ff_ref, fb_ref, gate_ref):
    h = _rmsnorm_rows(x_ref[...], g_ref[...]).astype(BF16)
    for s, out_ref in enumerate((q_ref, i_ref, ff_ref, fb_ref, gate_ref)):
        out_ref[...] = _dot(h, w_ref[:, s * D_MODEL:(s + 1) * D_MODEL]).astype(out_ref.dtype)


def _pre_odd(x2, g, w):
    n_tok = x2.shape[0]
    tm = TOKEN_TILE
    row_spec = pl.BlockSpec((tm, D_MODEL), lambda i: (i, 0))
    lo = jax.ShapeDtypeStruct((n_tok, D_MODEL), BF16)
    hi = jax.ShapeDtypeStruct((n_tok, D_MODEL), F32)
    return pl.pallas_call(
        _pre_odd_kernel,
        grid=(n_tok // tm,),
        in_specs=[row_spec, _const_spec((1, D_MODEL)), _const_spec(w.shape)],
        out_specs=[row_spec] * 5,
        out_shape=[lo, lo, hi, hi, lo],
        compiler_params=_params(1),
        name="pre_odd",
    )(x2, g, w)


def _chunk_cumprod(x, reverse):
    n = x.shape[0]
    pos = lax.broadcasted_iota(jnp.int32, x.shape, 0) % HGRN_CHUNK
    s = 1
    while s < HGRN_CHUNK:
        if reverse:
            shifted = pltpu.roll(x, n - s, 0)
            x = x * jnp.where(pos < HGRN_CHUNK - s, shifted, 1.0)
        else:
            shifted = pltpu.roll(x, s, 0)
            x = x * jnp.where(pos >= s, shifted, 1.0)
        s *= 2
    return x


def _hgrn_kernel(layer, q_ref, v_ref, ff_ref, fb_ref, gate_ref, lbl_ref, ng_ref, o_ref,
                 qd_ref, dec_ref, kv_ref, acc_ref):
    seq = q_ref.shape[1]
    ch = HGRN_CHUNK
    blk = HGRN_BLOCK
    n_chunks = seq // ch
    per_blk = blk // ch
    logits = lbl_ref[...]
    soft = jnp.exp(logits - jnp.max(logits, axis=0, keepdims=True))
    soft = soft / jnp.sum(soft, axis=0, keepdims=True)
    lb = jnp.sum(soft[0:layer + 1], axis=0, keepdims=True) - soft[0:1]

    c2 = 0.5 * (1.0 - lb)
    c1 = lb + c2
    row = lax.broadcasted_iota(jnp.int32, (blk, blk), 0)
    col = lax.broadcasted_iota(jnp.int32, (blk, blk), 1)
    u32 = lambda a: a.astype(jnp.uint32)
    keep = (u32(row - col) <= u32(row % ch), u32(col - row) <= u32(ch - 1 - row % ch))
    row_chunk = lax.broadcasted_iota(jnp.int32, (blk, 128), 0) // ch

    def intra(j, carry):
        sl = pl.ds(pl.multiple_of(j * blk, blk), blk)
        v = v_ref[0, sl, :]
        vt = v.astype(F32).T.astype(BF16)
        qf = q_ref[0, sl, :].astype(F32)
        for d, f_ref in enumerate((ff_ref, fb_ref)):
            ct = c2 * jnp.tanh(f_ref[0, sl, :])
            eb = _chunk_cumprod(c1 + ct, reverse=(d == 1))
            k_inv = (c2 - ct) / eb
            eb3 = eb.reshape(per_blk, ch, 128)
            e_end = eb3[:, ch - 1:ch, :] if d == 0 else eb3[:, 0:1, :]
            dec_ref[d, pl.ds(j * per_blk, per_blk)] = e_end
            q_dec = (qf * eb).astype(BF16)
            qd_ref[d, sl, :] = q_dec
            k_dec = (k_inv.reshape(per_blk, ch, 128) * e_end).reshape(blk, 128).astype(BF16)

            s = _dot_nt(q_dec, k_inv.astype(BF16))
            p = jnp.where(keep[d], s, 0.0).astype(BF16)
            acc_ref[d, sl, :] = _dot(p, v)
            rhs = jnp.concatenate([jnp.where(row_chunk == cc, k_dec, jnp.zeros_like(k_dec))
                                   for cc in range(per_blk)], axis=1)
            kvs = _dot(vt, rhs)
            for cc in range(per_blk):
                kv_ref[d, j * per_blk + cc] = kvs[:, 128 * cc:128 * (cc + 1)]
        return carry

    lax.fori_loop(0, seq // blk, intra, 0, unroll=8)

    def finish(rows):
        o = _rmsnorm_rows(acc_ref[0, rows, :] + acc_ref[1, rows, :], ng_ref[...])
        gate = 0.5 + 0.5 * jnp.tanh(gate_ref[0, rows, :].astype(F32))
        o_ref[0, rows, :] = (o * gate).astype(o_ref.dtype)

    states = [jnp.zeros((128, 128), F32)] * 2
    for n in range(n_chunks):
        for d in range(2):
            c = n if d == 0 else n_chunks - 1 - n
            rows = slice(c * ch, (c + 1) * ch)
            acc_ref[d, rows, :] += _dot_nt(qd_ref[d, rows, :], states[d].astype(BF16))
            states[d] = dec_ref[d, c] * states[d] + kv_ref[d, c]
        if 2 * n >= n_chunks:
            finish(slice(n * ch, (n + 1) * ch))
            finish(slice((n_chunks - 1 - n) * ch, (n_chunks - n) * ch))


def _hgrn(q, v, ff, fb, gate, lb_logits, norm_g, layer):
    bsz, seq, _ = q.shape
    n_chunks = seq // HGRN_CHUNK
    head_spec = pl.BlockSpec((1, seq, 128), lambda b, h: (b, 0, h))
    return pl.pallas_call(
        functools.partial(_hgrn_kernel, layer),
        grid=(bsz, HGRN_HEADS),
        in_specs=[head_spec] * 5 + [pl.BlockSpec((DEPTH, 128), lambda b, h: (0, h)),
                                    pl.BlockSpec((1, 128), lambda b, h: (0, h))],
        out_specs=head_spec,
        out_shape=jax.ShapeDtypeStruct((bsz, seq, D_MODEL), BF16),
        scratch_shapes=[
           pltpu.VMEM((2, seq, 128), BF16),
           pltpu.VMEM((2, n_chunks, 1, 128), F32),
           pltpu.VMEM((2, n_chunks, 128, 128), F32),
           pltpu.VMEM((2, seq, 128), F32)],
        compiler_params=_params(2),
        name="hgrn2",
    )(q, v, ff, fb, gate, lb_logits, norm_g)


def _rope_pair_tables(seq):
    inv = ROPE_THETA ** (-jnp.arange(0, ATTN_QK_DIM, 2, dtype=F32) / ATTN_QK_DIM)
    ang = jnp.arange(seq, dtype=F32)[:, None] * inv[None, :]
    return jnp.tile(jnp.cos(ang), (1, 4)), jnp.tile(jnp.sin(ang), (1, 4))


def _to_pair_layout(w):
    idx = jnp.arange(ATTN_QK_WIDTH).reshape(2, 2, 2, 2, 32)
    return w[:, idx.transpose(0, 3, 1, 2, 4).reshape(-1)]


def kernel(x, norm_mix_g, norm_mlp_g, final_norm_g, w_ff_in, w_ff_out, w_in_even, w_out_even, diff_lambda, diff_subln_g, s5_lam_re, s5_lam_im, s5_log_step, s5_b_re, s5_b_im, s5_c_re, s5_c_im, s5_d, s5_w_glu, s5_b_glu, w_in_odd, w_out_odd, hgrn_norm_g, hgrn_lb_logits):
    bsz, seq, _ = x.shape
    n_tok = bsz * seq
    n_chunks = seq // S5_CHUNK
    assert n_chunks & (n_chunks - 1) == 0 and n_chunks % 128 == 0
    x2 = x.reshape(n_tok, D_MODEL)
    cos, sin = _rope_pair_tables(seq)

    assert DEPTH == 2, "an even (attention + S5) layer followed by an odd (HGRN2) layer"
    for layer in range(DEPTH):
        g_mix = norm_mix_g[layer].reshape(1, D_MODEL)
        g_mlp = norm_mlp_g[layer].reshape(1, D_MODEL)
        if layer % 2 == 0:
            e = layer // 2
            w = w_in_even[e]
            w_qkv = jnp.concatenate([_to_pair_layout(w[:, :ATTN_QK_WIDTH]),
                                     _to_pair_layout(w[:, ATTN_QK_WIDTH:2 * ATTN_QK_WIDTH]),
                                     w[:, 2 * ATTN_QK_WIDTH:2 * ATTN_QK_WIDTH + ATTN_WIDTH]], axis=1).astype(BF16)
            wu = w[:, 2 * ATTN_QK_WIDTH + ATTN_WIDTH:].astype(BF16)
            q, k, v = _pre_even(x2, g_mix, w_qkv, cos, sin, seq)
            u_t = _s5_in(x2, g_mix, wu, bsz, seq)
            lambda_init = 0.8 - 0.6 * math.exp(-0.3 * layer)
            a_out, w16 = _attention(q.reshape(bsz, seq, -1), k.reshape(bsz, seq, -1), v.reshape(bsz, seq, -1),
                                    diff_lambda[e], diff_subln_g[e].reshape(1, ATTN_V_DIM), lambda_init,
                                    [(w_out_even, e), (w_ff_in, layer), (w_ff_out, layer),
                                     (w_in_odd, e), (w_out_odd, e), (w_ff_in, layer + 1), (w_ff_out, layer + 1)],
                                    [None, None, None, 2 * D_MODEL, None, None, None])
            w_out_e, w1, w2, w_in_o, w_out_o, w1_next, w2_next = w16
            ops = _s5_operators(s5_lam_re[e], s5_lam_im[e], s5_log_step[e], s5_b_re[e], s5_b_im[e],
                                s5_c_re[e], s5_c_im[e], s5_d[e], n_chunks.bit_length() - 1)
            y_t = _s5_chunked(u_t.reshape(S5_GROUPS, S5_CW, bsz * n_chunks), *ops, n_chunks)
            b_out = _s5_out(y_t.reshape(S5_GROUPS, S5_CHUNK, S5_GROUP, bsz * n_chunks),
                            s5_w_glu[e].T.astype(BF16), s5_b_glu[e].reshape(S5_WIDTH, 1), bsz, seq)
            x2 = _post_even(x2, a_out.reshape(n_tok, ATTN_WIDTH), b_out, w_out_e, g_mlp, w1, w2)
        else:
            o_i = layer // 2
            q, v, ff, fb, gate = _pre_odd(x2, g_mix, w_in_o)
            shp = (bsz, seq, D_MODEL)
            mixed = _hgrn(q.reshape(shp), v.reshape(shp), ff.reshape(shp), fb.reshape(shp), gate.reshape(shp),
                          hgrn_lb_logits, hgrn_norm_g[o_i].reshape(1, D_MODEL), layer)
            x2 = _post_odd(x2, mixed.reshape(n_tok, D_MODEL), w_out_o, g_mlp, w1_next, w2_next,
                           final_norm_g.reshape(1, D_MODEL))
    return x2.reshape(bsz, seq, D_MODEL)
```

```python
import functools
import math

import jax
import jax.numpy as jnp
from jax import lax
from jax.experimental import pallas as pl
from jax.experimental.pallas import tpu as pltpu

D_MODEL = 1024
DEPTH = 2
ATTN_HEADS = 4
ATTN_QK_DIM = 64
ATTN_V_DIM = 128
ATTN_QK_WIDTH = 512
ATTN_WIDTH = 512
ROPE_THETA = 10000.0
S5_WIDTH = 512
S5_GROUP = 16
S5_GROUPS = 32
S5_STATE = 64
HGRN_HEADS = 8
HGRN_CHUNK = 64
HGRN_BLOCK = 256
HGRN_HEADS_PER_STEP = 4
D_FF = 4096
EPS = 1e-6

S5_CHUNK = 16
S5_CW = S5_CHUNK * S5_GROUP

TOKEN_TILE = 1024
ATTN_Q_TILE = 1024
ATTN_Q_SUB = 512
FF_TILE = 1024
VMEM_LIMIT = 56 * 1024 * 1024

BF16 = jnp.bfloat16
F32 = jnp.float32


def _const_spec(shape):
    nd = len(shape)
    return pl.BlockSpec(shape, lambda *_: (0,) * nd, pipeline_mode=pl.Buffered(1))


def _params(n_axes):
    return pltpu.CompilerParams(dimension_semantics=("arbitrary",) * n_axes,
                                vmem_limit_bytes=VMEM_LIMIT)


def _rmsnorm_rows(x, g):
    ms = jnp.mean(x * x, axis=-1, keepdims=True)
    return x * lax.rsqrt(ms + EPS) * g


def _gelu_tanh(x):
    c = math.sqrt(2.0 / math.pi)
    return 0.5 * x * (1.0 + jnp.tanh(c * (x + 0.044715 * (x * x * x))))


def _dot(a, b):
    return jnp.dot(a, b, preferred_element_type=F32)


def _dot_nt(a, b):
    return lax.dot_general(a, b, (((1,), (1,)), ((), ())), preferred_element_type=F32)


def _dot_tn(a, b):
    return lax.dot_general(a, b, (((0,), (0,)), ((), ())), preferred_element_type=F32)


def _pre_even_kernel(x_ref, g_ref, w_ref, cos_ref, sin_ref, q_ref, k_ref, v_ref):
    h = _rmsnorm_rows(x_ref[...], g_ref[...]).astype(BF16)
    cos = cos_ref[...]
    sin = sin_ref[...]
    for out_ref, base, scale in ((q_ref, 0, ATTN_QK_DIM ** -0.5 * math.log2(math.e)),
                                 (k_ref, ATTN_QK_WIDTH, 1.0)):
        p = _dot(h, w_ref[:, base:base + ATTN_QK_WIDTH])
        for pair in range(2):
            lo = p[:, 256 * pair:256 * pair + 128]
            hi = p[:, 256 * pair + 128:256 * pair + 256]
            out_ref[:, 256 * pair:256 * pair + 128] = ((lo * cos - hi * sin) * scale).astype(BF16)
            out_ref[:, 256 * pair + 128:256 * pair + 256] = ((hi * cos + lo * sin) * scale).astype(BF16)
    v_ref[...] = _dot(h, w_ref[:, 2 * ATTN_QK_WIDTH:]).astype(BF16)


def _pre_even(x2, g, w_qkv, cos, sin, seq):
    n_tok = x2.shape[0]
    tm = TOKEN_TILE
    n_pos_blocks = seq // tm
    out = jax.ShapeDtypeStruct((n_tok, 512), BF16)
    row_spec = pl.BlockSpec((tm, 512), lambda i: (i, 0))
    rope_spec = pl.BlockSpec((tm, 128), lambda i: (i % n_pos_blocks, 0))
    return pl.pallas_call(
        _pre_even_kernel,
        grid=(n_tok // tm,),
        in_specs=[pl.BlockSpec((tm, D_MODEL), lambda i: (i, 0)), _const_spec((1, D_MODEL)),
                  _const_spec(w_qkv.shape), rope_spec, rope_spec],
        out_specs=[row_spec] * 3,
        out_shape=[out] * 3,
        compiler_params=_params(1),
        name="pre_even",
    )(x2, g, w_qkv, cos, sin)


def _s5_in_kernel(x_ref, g_ref, wu_ref, ut_ref, u_scr):
    nc = x_ref.shape[0] // S5_CHUNK
    groups_per_slab = 128 // S5_GROUP
    h = _rmsnorm_rows(x_ref[...], g_ref[...]).astype(BF16)
    u = _dot(h, wu_ref[...])
    for j in range(S5_WIDTH // 128):
        u_scr[j] = u[:, 128 * j:128 * (j + 1)]
    for j in range(S5_WIDTH // 128):
        for ph in range(S5_CHUNK):
            t = u_scr[j, pl.ds(ph, nc, stride=S5_CHUNK), :]
            ut_ref[groups_per_slab * j:groups_per_slab * (j + 1), ph, :, :] = (
                t.T.astype(BF16).reshape(groups_per_slab, S5_GROUP, nc))


def _s5_in(x2, g, wu, bsz, seq):
    nc = seq // S5_CHUNK
    return pl.pallas_call(
        _s5_in_kernel,
        grid=(bsz,),
        in_specs=[pl.BlockSpec((seq, D_MODEL), lambda b: (b, 0)), _const_spec((1, D_MODEL)),
                  _const_spec(wu.shape)],
        out_specs=pl.BlockSpec((S5_GROUPS, S5_CHUNK, S5_GROUP, nc), lambda b: (0, 0, 0, b)),
        out_shape=jax.ShapeDtypeStruct((S5_GROUPS, S5_CHUNK, S5_GROUP, bsz * nc), BF16),
        scratch_shapes=[pltpu.VMEM((S5_WIDTH // 128, seq, 128), F32)],
        compiler_params=_params(1),
        name="s5_in",
    )(x2, g, wu)


def _s5_out_kernel(yt_ref, wglut_ref, bglu_ref, o_ref, b_scr):
    nc = yt_ref.shape[3]
    yt = jnp.concatenate([yt_ref[:, ph, :, :].reshape(S5_WIDTH, nc) for ph in range(S5_CHUNK)], axis=1)
    yt = _gelu_tanh(yt)
    z = _dot(wglut_ref[...], yt.astype(BF16)) + bglu_ref[...]
    bt = yt * (0.5 + 0.5 * jnp.tanh(0.5 * z))
    for j in range(S5_WIDTH // 128):
        for ph in range(S5_CHUNK):
            b_scr[j, pl.ds(ph, nc, stride=S5_CHUNK), :] = bt[128 * j:128 * (j + 1), ph * nc:(ph + 1) * nc].T
    for j in range(S5_WIDTH // 128):
        o_ref[:, 128 * j:128 * (j + 1)] = b_scr[j]


def _s5_out(y_t, w_glu_t, b_glu_col, bsz, seq):
    nc = seq // S5_CHUNK
    return pl.pallas_call(
        _s5_out_kernel,
        grid=(bsz,),
        in_specs=[pl.BlockSpec((S5_GROUPS, S5_CHUNK, S5_GROUP, nc), lambda b: (0, 0, 0, b)),
                  _const_spec(w_glu_t.shape), _const_spec(b_glu_col.shape)],
        out_specs=pl.BlockSpec((seq, S5_WIDTH), lambda b: (b, 0)),
        out_shape=jax.ShapeDtypeStruct((bsz * seq, S5_WIDTH), F32),
        scratch_shapes=[pltpu.VMEM((S5_WIDTH // 128, seq, 128), F32)],
        compiler_params=_params(1),
        name="s5_out",
    )(y_t, w_glu_t, b_glu_col)


def _attn_kernel(lambda_init, halve_from, q_ref, k_ref, v_ref, lam_ref, g_ref, *refs):
    n_weights = len(halve_from)
    o_ref = refs[n_weights]
    for w_ref, w16_ref, col0 in zip(refs[:n_weights], refs[n_weights + 1:], halve_from):
        w = w_ref[...]
        if col0 is not None:
            w = w * jnp.where(lax.broadcasted_iota(jnp.int32, w.shape, 1) >= col0, 0.5, 1.0)
        w16_ref[...] = w.astype(BF16)
    k = k_ref[0]
    lam = lam_ref[...]
    lam_val = (jnp.exp(jnp.sum(lam[0:1] * lam[1:2], axis=-1, keepdims=True))
               - jnp.exp(jnp.sum(lam[2:3] * lam[3:4], axis=-1, keepdims=True)) + lambda_init)
    tq = ATTN_Q_SUB
    lane_group = (lax.broadcasted_iota(jnp.int32, (tq, 256), 1) // 32) % 4
    ones = jnp.ones((k.shape[0], ATTN_V_DIM), BF16)
    v_ext = [jnp.concatenate([v_ref[0, :, 128 * hh:128 * hh + 128], ones], axis=1) for hh in range(2)]
    for t in range(q_ref.shape[1] // tq):
        q = q_ref[0, t * tq:(t + 1) * tq, :]
        zero = jnp.zeros_like(q)
        q_all = jnp.concatenate([jnp.where(lane_group == hc, q, zero) for hc in range(4)], axis=0)
        s = _dot_nt(q_all, k)
        e = jnp.exp2(s - jnp.max(s, axis=-1, keepdims=True)).astype(BF16)
        for hh in range(2):
            r = _dot(e[2 * hh * tq:2 * (hh + 1) * tq], v_ext[hh])
            o = (r[0:tq, 0:ATTN_V_DIM] / r[0:tq, ATTN_V_DIM:]
                 - lam_val * (r[tq:2 * tq, 0:ATTN_V_DIM] / r[tq:2 * tq, ATTN_V_DIM:]))
            o = _rmsnorm_rows(o, g_ref[...]) * (1.0 - lambda_init)
            o_ref[0, t * tq:(t + 1) * tq, 128 * hh:128 * hh + 128] = o.astype(o_ref.dtype)


def _attention(q, k, v, lam, subln_g, lambda_init, later_weights, halve_from):
    bsz, seq, _ = q.shape
    tq = ATTN_Q_TILE
    n_q = seq // tq
    n_steps = bsz * 2 * n_q
    kv_spec = pl.BlockSpec((1, seq, 256), lambda b, p, i: (b, 0, p))
    step = lambda b, p, i: (b * 2 + p) * n_q + i
    slab_in = [pl.BlockSpec((None, w.shape[1] // n_steps, w.shape[2]),
                            functools.partial(lambda idx, b, p, i: (idx, step(b, p, i), 0), idx))
               for w, idx in later_weights]
    slab_out = [pl.BlockSpec((w.shape[1] // n_steps, w.shape[2]), lambda b, p, i: (step(b, p, i), 0))
                for w, _ in later_weights]
    outs = pl.pallas_call(
        functools.partial(_attn_kernel, lambda_init, tuple(halve_from)),
        grid=(bsz, 2, n_q),
        in_specs=[pl.BlockSpec((1, tq, 256), lambda b, p, i: (b, i, p)),
                  kv_spec, kv_spec,
                  _const_spec(lam.shape),
                  _const_spec(subln_g.shape)] + slab_in,
        out_specs=[pl.BlockSpec((1, tq, 256), lambda b, p, i: (b, i, p))] + slab_out,
        out_shape=[jax.ShapeDtypeStruct((bsz, seq, ATTN_WIDTH), BF16)]
        + [jax.ShapeDtypeStruct(w.shape[1:], BF16) for w, _ in later_weights],
        compiler_params=_params(3),
        name="diff_attention",
    )(q, k, v, lam, subln_g, *[w for w, _ in later_weights])
    return outs[0], outs[1:]


def _s5_kernel(n_chunks, ut_ref, mi_ref, min_ref, mo_ref, ctab_ref, y_ref):
    n_rows = ut_ref.shape[2]
    n_steps = n_chunks.bit_length() - 1
    ut2 = ut_ref[...].reshape(2 * S5_CW, n_rows)
    st = _dot_tn(ut2, min_ref[0])
    ctab = ctab_ref[0]
    pos = lax.broadcasted_iota(jnp.int32, (n_rows, 128), 0) % n_chunks
    zs = []
    for d in range(2):
        xr = st[:, 256 * d:256 * d + 128]
        xi = st[:, 256 * d + 128:256 * d + 256]
        for k in range(n_steps):
            s = 1 << k
            if d == 0:
                shift, keep = s, pos >= s
            else:
                shift, keep = n_rows - s, pos < n_chunks - s
            pr = ctab[4 * k + 2 * d:4 * k + 2 * d + 1]
            pi = ctab[4 * k + 2 * d + 1:4 * k + 2 * d + 2]
            sr = jnp.where(keep, pltpu.roll(xr, shift, 0), 0.0)
            si = jnp.where(keep, pltpu.roll(xi, shift, 0), 0.0)
            xr, xi = xr + pr * sr - pi * si, xi + pr * si + pi * sr
        if d == 0:
            shift, keep = 1, pos >= 1
        else:
            shift, keep = n_rows - 1, pos < n_chunks - 1
        zs += [jnp.where(keep, pltpu.roll(xr, shift, 0), 0.0), jnp.where(keep, pltpu.roll(xi, shift, 0), 0.0)]
    z = jnp.concatenate(zs, axis=1).astype(BF16)
    for g in range(2):
        y_ref[g] = _dot(mi_ref[g], ut_ref[g]) + _dot_nt(mo_ref[g], z)


def _s5_chunked(u_t, m_intra_t, m_in_pair, m_out_ext, ctab, n_chunks):
    n_groups, _, n_lanes = u_t.shape
    return pl.pallas_call(
        functools.partial(_s5_kernel, n_chunks),
        grid=(n_groups // 2,),
        in_specs=[pl.BlockSpec((2, S5_CW, n_lanes), lambda i: (i, 0, 0)),
                  pl.BlockSpec((2, S5_CW, S5_CW), lambda i: (i, 0, 0)),
                  pl.BlockSpec((1, 2 * S5_CW, 512), lambda i: (i, 0, 0)),
                  pl.BlockSpec((2, S5_CW, 512), lambda i: (i, 0, 0)),
                  pl.BlockSpec((1, 32, 128), lambda i: (i, 0, 0))],
        out_specs=pl.BlockSpec((2, S5_CW, n_lanes), lambda i: (i, 0, 0)),
        out_shape=jax.ShapeDtypeStruct((n_groups, S5_CW, n_lanes), F32),
        compiler_params=_params(1),
        name="s5_chunked",
    )(u_t, m_intra_t, m_in_pair, m_out_ext, ctab)


def _dot_hp(a, b):
    return jnp.dot(a, b, precision=lax.Precision.HIGHEST, preferred_element_type=F32)


def _bf16_terms(x):
    hi = x.astype(BF16)
    r = x - hi.astype(F32)
    mid = r.astype(BF16)
    return hi, mid, (r - mid.astype(F32)).astype(BF16)


def _spread(values, onehot):
    return sum(_dot(term, onehot) for term in _bf16_terms(values))


def _spread_rows(onehot, values):
    return sum(_dot(onehot, term) for term in _bf16_terms(values))


def _discretise(lr, li, log_step):
    lr = jnp.minimum(lr, -1e-4)
    dt = jnp.exp(log_step)
    mag = jnp.exp(lr * dt)
    ar = mag * jnp.cos(li * dt)
    ai = mag * jnp.sin(li * dt)
    den = lr * lr + li * li
    cr = ((ar - 1.0) * lr + ai * li) / den
    ci = (ai * lr - (ar - 1.0) * li) / den
    return lr * dt, li * dt, cr, ci


def _s5_ops_kernel(n_steps, colp_ref, rowp_ref, bcol_ref, crow_ref, dcol_ref,
                   mi_ref, min_ref, mo_ref, ctab_ref):
    t = S5_CHUNK
    n = S5_GROUP
    onehot = lambda m: jnp.where(m, 1.0, 0.0).astype(BF16)
    i16 = lambda shape, dim: lax.broadcasted_iota(jnp.int32, shape, dim)
    rep_lanes = onehot(i16((t, S5_CW), 1) // n == i16((t, S5_CW), 0))
    tile_lanes = onehot(i16((n, S5_CW), 1) % n == i16((n, S5_CW), 0))
    rep_rows = onehot(i16((S5_CW, t), 0) // n == i16((S5_CW, t), 1))
    tile_rows = onehot(i16((S5_CW, n), 0) % n == i16((S5_CW, n), 1))
    row_group = i16((2 * S5_STATE, S5_CW), 0) // S5_STATE
    lane_group = i16((1, 2 * S5_STATE), 1) // S5_STATE
    lane16 = i16((1, t), 1).astype(F32)
    row16 = i16((t, 1), 0).astype(F32)
    colp = colp_ref[0]
    rowp = rowp_ref[0]

    taps = [[None, None], [None, None]]
    min_t_rows = []
    mo_kinds = []
    for d in range(2):
        lmag, ang, cr, ci = _discretise(colp[:, 3 * d:3 * d + 1], colp[:, 3 * d + 1:3 * d + 2],
                                        colp[:, 3 * d + 2:3 * d + 3])
        b_re, b_im = bcol_ref[0, 2 * d], bcol_ref[0, 2 * d + 1]
        bbr = _spread(cr * b_re - ci * b_im, tile_lanes)
        bbi = _spread(cr * b_im + ci * b_re, tile_lanes)
        lag = (t - 1.0) - lane16 if d == 0 else lane16
        pmag = jnp.exp(lmag * lag)
        pr = _spread(pmag * jnp.cos(ang * lag), rep_lanes)
        pi = _spread(pmag * jnp.sin(ang * lag), rep_lanes)
        rr = pr * bbr - pi * bbi
        ri = pr * bbi + pi * bbr
        for part in (rr, ri):
            min_t_rows.append(jnp.concatenate([jnp.where(row_group == gs, part, 0.0) for gs in range(2)], axis=1))
        c_re, c_im = crow_ref[0, 2 * d], crow_ref[0, 2 * d + 1]
        for gs in range(2):
            own = lane_group == gs
            taps[gs][d] = (_dot_hp(jnp.where(own, c_re, 0.0), rr) - _dot_hp(jnp.where(own, c_im, 0.0), ri))

        lmag_r, ang_r, _, _ = _discretise(rowp[3 * d:3 * d + 1], rowp[3 * d + 1:3 * d + 2],
                                          rowp[3 * d + 2:3 * d + 3])
        tau = row16 + 1.0 if d == 0 else float(t) - row16
        qmag = jnp.exp(tau * lmag_r)
        qr16 = qmag * jnp.cos(tau * ang_r)
        qi16 = qmag * jnp.sin(tau * ang_r)
        qr = _spread_rows(rep_rows, qr16)
        qi = _spread_rows(rep_rows, qi16)
        cre = _spread_rows(tile_rows, c_re)
        cim = _spread_rows(tile_rows, c_im)
        mo_kinds += [qr * cre - qi * cim, -(qr * cim + qi * cre)]

        far = t - 1 if d == 0 else 0
        pw_r, pw_i = qr16[far:far + 1], qi16[far:far + 1]
        for k in range(n_steps):
            ctab_ref[0, 4 * k + 2 * d:4 * k + 2 * d + 1, :] = pw_r
            ctab_ref[0, 4 * k + 2 * d + 1:4 * k + 2 * d + 2, :] = pw_i
            pw_r, pw_i = pw_r * pw_r - pw_i * pw_i, 2.0 * pw_r * pw_i
    ctab_ref[0, 4 * n_steps:, :] = jnp.zeros((32 - 4 * n_steps, 2 * S5_STATE), F32)

    min_ref[0] = jnp.concatenate(min_t_rows, axis=0).T.astype(BF16)
    lane_group_wide = i16((1, 2 * S5_STATE), 1) // S5_STATE
    for gs in range(2):
        mo_ref[gs] = jnp.concatenate([jnp.where(lane_group_wide == gs, kind, 0.0) for kind in mo_kinds],
                                     axis=1).astype(BF16)
        zeros = jnp.zeros((n, S5_CW), F32)
        edge = (t - 1) * n
        lane = i16((n, 2 * S5_CW), 1)
        skip = jnp.where(lane == edge + i16((n, 2 * S5_CW), 0), dcol_ref[0, gs], 0.0)
        kk = (jnp.concatenate([taps[gs][0], zeros], axis=1)
              + pltpu.roll(jnp.concatenate([taps[gs][1], zeros], axis=1), edge, 1) + skip)
        strips = []
        for j in range(t):
            shift = (t - 1 - j) * n
            moved = kk if shift == 0 else pltpu.roll(kk, 2 * S5_CW - shift, 1)
            strips.append(moved[:, 0:S5_CW])
        mi_ref[gs] = jnp.concatenate(strips, axis=0).astype(BF16)


def _s5_operators(lam_re, lam_im, log_step, b_re, b_im, c_re, c_im, d_skip, n_steps):
    g2 = S5_GROUPS // 2
    zero = jnp.zeros_like(lam_re[0])
    step = [jnp.broadcast_to(log_step[d][:, None], lam_re[d].shape) for d in range(2)]
    params = jnp.stack([lam_re[0], lam_im[0], step[0], lam_re[1], lam_im[1], step[1], zero, zero])
    colp = params.transpose(1, 2, 0).reshape(g2, 2 * S5_STATE, 8)
    rowp = params.reshape(8, g2, 2 * S5_STATE).transpose(1, 0, 2)
    bcol = (jnp.stack([b_re[0], b_im[0], b_re[1], b_im[1]])
            .reshape(4, g2, 2 * S5_STATE, S5_GROUP).transpose(1, 0, 2, 3))
    crow = (jnp.stack([c_re[0], c_im[0], c_re[1], c_im[1]])
            .reshape(4, g2, 2, S5_GROUP, S5_STATE).transpose(1, 0, 3, 2, 4).reshape(g2, 4, S5_GROUP, 2 * S5_STATE))
    dcol = d_skip.reshape(g2, 2, S5_GROUP, 1)
    pair = lambda *shape: pl.BlockSpec((1,) + shape, lambda i: (i,) + (0,) * len(shape))
    return pl.pallas_call(
        functools.partial(_s5_ops_kernel, n_steps),
        grid=(g2,),
        in_specs=[pair(2 * S5_STATE, 8), pair(8, 2 * S5_STATE), pair(4, 2 * S5_STATE, S5_GROUP),
                  pair(4, S5_GROUP, 2 * S5_STATE), pair(2, S5_GROUP, 1)],
        out_specs=[pl.BlockSpec((2, S5_CW, S5_CW), lambda i: (i, 0, 0)), pair(2 * S5_CW, 512),
                   pl.BlockSpec((2, S5_CW, 512), lambda i: (i, 0, 0)), pair(32, 2 * S5_STATE)],
        out_shape=[jax.ShapeDtypeStruct((S5_GROUPS, S5_CW, S5_CW), BF16),
                   jax.ShapeDtypeStruct((g2, 2 * S5_CW, 512), BF16),
                   jax.ShapeDtypeStruct((S5_GROUPS, S5_CW, 512), BF16),
                   jax.ShapeDtypeStruct((g2, 32, 2 * S5_STATE), F32)],
        compiler_params=_params(1),
        name="s5_operators",
    )(colp, rowp, bcol, crow, dcol)


def _mlp(x1, g_ref, w1_ref, w2_ref):
    h = _rmsnorm_rows(x1, g_ref[...]).astype(BF16)
    acc = x1
    for j in range(D_FF // FF_TILE):
        hid = _dot(h, w1_ref[:, j * FF_TILE:(j + 1) * FF_TILE])
        hid = jnp.square(jnp.maximum(hid, 0.0)).astype(BF16)
        acc = acc + _dot(hid, w2_ref[j * FF_TILE:(j + 1) * FF_TILE, :])
    return acc


def _post_even_kernel(x_ref, a_ref, b_ref, wout_ref, g_ref, w1_ref, w2_ref, o_ref):
    mix = (_dot(a_ref[...], wout_ref[0:ATTN_WIDTH, :])
           + _dot(b_ref[...].astype(BF16), wout_ref[ATTN_WIDTH:, :]))
    o_ref[...] = _mlp(x_ref[...] + mix, g_ref, w1_ref, w2_ref)


def _post_even(x2, a_out, b_out, w_out, g_mlp, w1, w2):
    n_tok = x2.shape[0]
    tm = TOKEN_TILE
    return pl.pallas_call(
        _post_even_kernel,
        grid=(n_tok // tm,),
        in_specs=[pl.BlockSpec((tm, D_MODEL), lambda i: (i, 0)),
                  pl.BlockSpec((tm, ATTN_WIDTH), lambda i: (i, 0)),
                  pl.BlockSpec((tm, S5_WIDTH), lambda i: (i, 0)),
                  _const_spec(w_out.shape), _const_spec(g_mlp.shape), _const_spec(w1.shape),
                  _const_spec(w2.shape)],
        out_specs=pl.BlockSpec((tm, D_MODEL), lambda i: (i, 0)),
        out_shape=jax.ShapeDtypeStruct((n_tok, D_MODEL), F32),
        compiler_params=_params(1),
        name="post_even",
    )(x2, a_out, b_out, w_out, g_mlp, w1, w2)


def _post_odd_kernel(x_ref, m_ref, wout_ref, g_ref, w1_ref, w2_ref, gf_ref, o_ref):
    x1 = x_ref[...] + _dot(m_ref[...], wout_ref[...])
    o_ref[...] = _rmsnorm_rows(_mlp(x1, g_ref, w1_ref, w2_ref), gf_ref[...])


def _post_odd(x2, mixed, w_out, g_mlp, w1, w2, g_final):
    n_tok = x2.shape[0]
    tm = TOKEN_TILE
    return pl.pallas_call(
        _post_odd_kernel,
        grid=(n_tok // tm,),
        in_specs=[pl.BlockSpec((tm, D_MODEL), lambda i: (i, 0)),
                  pl.BlockSpec((tm, D_MODEL), lambda i: (i, 0)),
                  _const_spec(w_out.shape), _const_spec(g_mlp.shape),
                  _const_spec(w1.shape), _const_spec(w2.shape), _const_spec(g_final.shape)],
        out_specs=pl.BlockSpec((tm, D_MODEL), lambda i: (i, 0)),
        out_shape=jax.ShapeDtypeStruct((n_tok, D_MODEL), F32),
        compiler_params=_params(1),
        name="post_odd",
    )(x2, mixed, w_out, g_mlp, w1, w2, g_final)


def _pre_odd_kernel(x_ref, g_ref, w_ref, q_ref, i_ref, ff_ref, fb_ref, gate_ref):
    h = _rmsnorm_rows(x_ref[...], g_ref[...]).astype(BF16)
    for s, out_ref in enumerate((q_ref, i_ref, ff_ref, fb_ref, gate_ref)):
        out_ref[...] = _dot(h, w_ref[:, s * D_MODEL:(s + 1) * D_MODEL]).astype(out_ref.dtype)


def _pre_odd(x2, g, w):
    n_tok = x2.shape[0]
    tm = TOKEN_TILE
    row_spec = pl.BlockSpec((tm, D_MODEL), lambda i: (i, 0))
    lo = jax.ShapeDtypeStruct((n_tok, D_MODEL), BF16)
    hi = jax.ShapeDtypeStruct((n_tok, D_MODEL), F32)
    return pl.pallas_call(
        _pre_odd_kernel,
        grid=(n_tok // tm,),
        in_specs=[row_spec, _const_spec((1, D_MODEL)), _const_spec(w.shape)],
        out_specs=[row_spec] * 5,
        out_shape=[lo, lo, hi, hi, lo],
        compiler_params=_params(1),
        name="pre_odd",
    )(x2, g, w)


def _chunk_cumprod(x, reverse):
    n = x.shape[0]
    pos = lax.broadcasted_iota(jnp.int32, x.shape, 0) % HGRN_CHUNK
    s = 1
    while s < HGRN_CHUNK:
        if reverse:
            shifted = pltpu.roll(x, n - s, 0)
            x = x * jnp.where(pos < HGRN_CHUNK - s, shifted, 1.0)
        else:
            shifted = pltpu.roll(x, s, 0)
            x = x * jnp.where(pos >= s, shifted, 1.0)
        s *= 2
    return x


def _hgrn_kernel(layer, *refs):
    io, scratch = refs[:8], refs[8:]
    for hd in range(HGRN_HEADS_PER_STEP):
        lanes = pl.ds(128 * hd, 128)
        views = [r.at[:, :, lanes] for r in io[:5]] + [r.at[:, lanes] for r in io[5:7]] + [io[7].at[:, :, lanes]]
        _hgrn_head(layer, *views, *[s.at[hd] for s in scratch])


def _hgrn_head(layer, q_ref, v_ref, ff_ref, fb_ref, gate_ref, lbl_ref, ng_ref, o_ref,
               qd_ref, dec_ref, kv_ref, acc_ref):
    seq = q_ref.shape[1]
    ch = HGRN_CHUNK
    blk = HGRN_BLOCK
    n_chunks = seq // ch
    per_blk = blk // ch
    logits = lbl_ref[...]
    soft = jnp.exp(logits - jnp.max(logits, axis=0, keepdims=True))
    soft = soft / jnp.sum(soft, axis=0, keepdims=True)
    lb = jnp.sum(soft[0:layer + 1], axis=0, keepdims=True) - soft[0:1]

    c2 = 0.5 * (1.0 - lb)
    c1 = lb + c2
    row = lax.broadcasted_iota(jnp.int32, (blk, blk), 0)
    col = lax.broadcasted_iota(jnp.int32, (blk, blk), 1)
    u32 = lambda a: a.astype(jnp.uint32)
    keep = (u32(row - col) <= u32(row % ch), u32(col - row) <= u32(ch - 1 - row % ch))
    row_chunk = lax.broadcasted_iota(jnp.int32, (blk, 128), 0) // ch

    for j in range(seq // blk):
        sl = slice(j * blk, (j + 1) * blk)
        v = v_ref[0, sl, :]
        vt = v.astype(F32).T.astype(BF16)
        qf = q_ref[0, sl, :].astype(F32)
        for d, f_ref in enumerate((ff_ref, fb_ref)):
            ct = c2 * jnp.tanh(f_ref[0, sl, :])
            eb = _chunk_cumprod(c1 + ct, reverse=(d == 1))
            k_inv = (c2 - ct) / eb
            eb3 = eb.reshape(per_blk, ch, 128)
            e_end = eb3[:, ch - 1:ch, :] if d == 0 else eb3[:, 0:1, :]
            dec_ref[d, j * per_blk:(j + 1) * per_blk] = e_end
            q_dec = (qf * eb).astype(BF16)
            qd_ref[d, sl, :] = q_dec
            k_dec = (k_inv.reshape(per_blk, ch, 128) * e_end).reshape(blk, 128).astype(BF16)

            s = _dot_nt(q_dec, k_inv.astype(BF16))
            p = jnp.where(keep[d], s, 0.0).astype(BF16)
            acc_ref[d, sl, :] = _dot(p, v)
            rhs = jnp.concatenate([jnp.where(row_chunk == cc, k_dec, jnp.zeros_like(k_dec))
                                   for cc in range(per_blk)], axis=1)
            kvs = _dot(vt, rhs)
            for cc in range(per_blk):
                kv_ref[d, j * per_blk + cc] = kvs[:, 128 * cc:128 * (cc + 1)]

    def finish(rows):
        o = _rmsnorm_rows(acc_ref[0, rows, :] + acc_ref[1, rows, :], ng_ref[...])
        gate = 0.5 + 0.5 * jnp.tanh(gate_ref[0, rows, :].astype(F32))
        o_ref[0, rows, :] = (o * gate).astype(o_ref.dtype)

    states = [jnp.zeros((128, 128), F32)] * 2
    for n in range(n_chunks):
        for d in range(2):
            c = n if d == 0 else n_chunks - 1 - n
            rows = slice(c * ch, (c + 1) * ch)
            acc_ref[d, rows, :] += _dot_nt(qd_ref[d, rows, :], states[d].astype(BF16))
            states[d] = dec_ref[d, c] * states[d] + kv_ref[d, c]
        if 2 * n >= n_chunks:
            finish(slice(n * ch, (n + 1) * ch))
            finish(slice((n_chunks - 1 - n) * ch, (n_chunks - n) * ch))


def _hgrn(q, v, ff, fb, gate, lb_logits, norm_g, layer):
    bsz, seq, _ = q.shape
    n_chunks = seq // HGRN_CHUNK
    hps = HGRN_HEADS_PER_STEP
    head_spec = pl.BlockSpec((1, seq, 128 * hps), lambda b, h: (b, 0, h))
    return pl.pallas_call(
        functools.partial(_hgrn_kernel, layer),
        grid=(bsz, HGRN_HEADS // hps),
        in_specs=[head_spec] * 5 + [pl.BlockSpec((DEPTH, 128 * hps), lambda b, h: (0, h)),
                                    pl.BlockSpec((1, 128 * hps), lambda b, h: (0, h))],
        out_specs=head_spec,
        out_shape=jax.ShapeDtypeStruct((bsz, seq, D_MODEL), BF16),
        scratch_shapes=[
           pltpu.VMEM((hps, 2, seq, 128), BF16),
           pltpu.VMEM((hps, 2, n_chunks, 1, 128), F32),
           pltpu.VMEM((hps, 2, n_chunks, 128, 128), F32),
           pltpu.VMEM((hps, 2, seq, 128), F32)],
        compiler_params=_params(2),
        name="hgrn2",
    )(q, v, ff, fb, gate, lb_logits, norm_g)


def _rope_pair_tables(seq):
    inv = ROPE_THETA ** (-jnp.arange(0, ATTN_QK_DIM, 2, dtype=F32) / ATTN_QK_DIM)
    ang = jnp.arange(seq, dtype=F32)[:, None] * inv[None, :]
    return jnp.tile(jnp.cos(ang), (1, 4)), jnp.tile(jnp.sin(ang), (1, 4))


def _to_pair_layout(w):
    idx = jnp.arange(ATTN_QK_WIDTH).reshape(2, 2, 2, 2, 32)
    return w[:, idx.transpose(0, 3, 1, 2, 4).reshape(-1)]


def kernel(x, norm_mix_g, norm_mlp_g, final_norm_g, w_ff_in, w_ff_out, w_in_even, w_out_even, diff_lambda, diff_subln_g, s5_lam_re, s5_lam_im, s5_log_step, s5_b_re, s5_b_im, s5_c_re, s5_c_im, s5_d, s5_w_glu, s5_b_glu, w_in_odd, w_out_odd, hgrn_norm_g, hgrn_lb_logits):
    bsz, seq, _ = x.shape
    n_tok = bsz * seq
    n_chunks = seq // S5_CHUNK
    assert n_chunks & (n_chunks - 1) == 0 and n_chunks % 128 == 0
    x2 = x.reshape(n_tok, D_MODEL)
    cos, sin = _rope_pair_tables(seq)

    assert DEPTH == 2, "an even (attention + S5) layer followed by an odd (HGRN2) layer"
    for layer in range(DEPTH):
        g_mix = norm_mix_g[layer].reshape(1, D_MODEL)
        g_mlp = norm_mlp_g[layer].reshape(1, D_MODEL)
        if layer % 2 == 0:
            e = layer // 2
            w = w_in_even[e]
            w_qkv = jnp.concatenate([_to_pair_layout(w[:, :ATTN_QK_WIDTH]),
                                     _to_pair_layout(w[:, ATTN_QK_WIDTH:2 * ATTN_QK_WIDTH]),
                                     w[:, 2 * ATTN_QK_WIDTH:2 * ATTN_QK_WIDTH + ATTN_WIDTH]], axis=1).astype(BF16)
            wu = w[:, 2 * ATTN_QK_WIDTH + ATTN_WIDTH:].astype(BF16)
            q, k, v = _pre_even(x2, g_mix, w_qkv, cos, sin, seq)
            u_t = _s5_in(x2, g_mix, wu, bsz, seq)
            lambda_init = 0.8 - 0.6 * math.exp(-0.3 * layer)
            a_out, w16 = _attention(q.reshape(bsz, seq, -1), k.reshape(bsz, seq, -1), v.reshape(bsz, seq, -1),
                                    diff_lambda[e], diff_subln_g[e].reshape(1, ATTN_V_DIM), lambda_init,
                                    [(w_out_even, e), (w_ff_in, layer), (w_ff_out, layer),
                                     (w_in_odd, e), (w_out_odd, e), (w_ff_in, layer + 1), (w_ff_out, layer + 1)],
                                    [None, None, None, 2 * D_MODEL, None, None, None])
            w_out_e, w1, w2, w_in_o, w_out_o, w1_next, w2_next = w16
            ops = _s5_operators(s5_lam_re[e], s5_lam_im[e], s5_log_step[e], s5_b_re[e], s5_b_im[e],
                                s5_c_re[e], s5_c_im[e], s5_d[e], n_chunks.bit_length() - 1)
            y_t = _s5_chunked(u_t.reshape(S5_GROUPS, S5_CW, bsz * n_chunks), *ops, n_chunks)
            b_out = _s5_out(y_t.reshape(S5_GROUPS, S5_CHUNK, S5_GROUP, bsz * n_chunks),
                            s5_w_glu[e].T.astype(BF16), s5_b_glu[e].reshape(S5_WIDTH, 1), bsz, seq)
            x2 = _post_even(x2, a_out.reshape(n_tok, ATTN_WIDTH), b_out, w_out_e, g_mlp, w1, w2)
        else:
            o_i = layer // 2
            q, v, ff, fb, gate = _pre_odd(x2, g_mix, w_in_o)
            shp = (bsz, seq, D_MODEL)
            mixed = _hgrn(q.reshape(shp), v.reshape(shp), ff.reshape(shp), fb.reshape(shp), gate.reshape(shp),
                          hgrn_lb_logits, hgrn_norm_g[o_i].reshape(1, D_MODEL), layer)
            x2 = _post_odd(x2, mixed.reshape(n_tok, D_MODEL), w_out_o, g_mlp, w1_next, w2_next,
                           final_norm_g.reshape(1, D_MODEL))
    return x2.reshape(bsz, seq, D_MODEL)
```

```python
import functools
import math

import jax
import jax.numpy as jnp
from jax import lax
from jax.experimental import pallas as pl
from jax.experimental.pallas import tpu as pltpu

D_MODEL = 1024
DEPTH = 2
ATTN_HEADS = 4
ATTN_QK_DIM = 64
ATTN_V_DIM = 128
ATTN_QK_WIDTH = 512
ATTN_WIDTH = 512
ROPE_THETA = 10000.0
S5_WIDTH = 512
S5_GROUP = 16
S5_GROUPS = 32
S5_STATE = 64
HGRN_HEADS = 8
HGRN_CHUNK = 64
HGRN_BLOCK = 256
HGRN_HEADS_PER_STEP = 4
D_FF = 4096
EPS = 1e-6

S5_CHUNK = 16
S5_CW = S5_CHUNK * S5_GROUP
S5_PAIRS_PER_STEP = 4

TOKEN_TILE = 1024
ATTN_Q_TILE = 1024
ATTN_Q_SUB = 512
FF_TILE = 1024
VMEM_LIMIT = 56 * 1024 * 1024

BF16 = jnp.bfloat16
F32 = jnp.float32


def _const_spec(shape):
    nd = len(shape)
    return pl.BlockSpec(shape, lambda *_: (0,) * nd, pipeline_mode=pl.Buffered(1))


def _params(n_axes):
    return pltpu.CompilerParams(dimension_semantics=("arbitrary",) * n_axes,
                                vmem_limit_bytes=VMEM_LIMIT)


def _rmsnorm_rows(x, g):
    ms = jnp.mean(x * x, axis=-1, keepdims=True)
    return x * lax.rsqrt(ms + EPS) * g


def _gelu_tanh(x):
    c = math.sqrt(2.0 / math.pi)
    return 0.5 * x * (1.0 + jnp.tanh(c * (x + 0.044715 * (x * x * x))))


def _dot(a, b):
    return jnp.dot(a, b, preferred_element_type=F32)


def _dot_nt(a, b):
    return lax.dot_general(a, b, (((1,), (1,)), ((), ())), preferred_element_type=F32)


def _dot_tn(a, b):
    return lax.dot_general(a, b, (((0,), (0,)), ((), ())), preferred_element_type=F32)


def _pre_even_kernel(x_ref, g_ref, w_ref, cos_ref, sin_ref, q_ref, k_ref, v_ref):
    h = _rmsnorm_rows(x_ref[...], g_ref[...]).astype(BF16)
    cos = cos_ref[...]
    sin = sin_ref[...]
    for out_ref, base, scale in ((q_ref, 0, ATTN_QK_DIM ** -0.5 * math.log2(math.e)),
                                 (k_ref, ATTN_QK_WIDTH, 1.0)):
        p = _dot(h, w_ref[:, base:base + ATTN_QK_WIDTH])
        for pair in range(2):
            lo = p[:, 256 * pair:256 * pair + 128]
            hi = p[:, 256 * pair + 128:256 * pair + 256]
            out_ref[:, 256 * pair:256 * pair + 128] = ((lo * cos - hi * sin) * scale).astype(BF16)
            out_ref[:, 256 * pair + 128:256 * pair + 256] = ((hi * cos + lo * sin) * scale).astype(BF16)
    v_ref[...] = _dot(h, w_ref[:, 2 * ATTN_QK_WIDTH:]).astype(BF16)


def _pre_even(x2, g, w_qkv, cos, sin, seq):
    n_tok = x2.shape[0]
    tm = TOKEN_TILE
    n_pos_blocks = seq // tm
    out = jax.ShapeDtypeStruct((n_tok, 512), BF16)
    row_spec = pl.BlockSpec((tm, 512), lambda i: (i, 0))
    rope_spec = pl.BlockSpec((tm, 128), lambda i: (i % n_pos_blocks, 0))
    return pl.pallas_call(
        _pre_even_kernel,
        grid=(n_tok // tm,),
        in_specs=[pl.BlockSpec((tm, D_MODEL), lambda i: (i, 0)), _const_spec((1, D_MODEL)),
                  _const_spec(w_qkv.shape), rope_spec, rope_spec],
        out_specs=[row_spec] * 3,
        out_shape=[out] * 3,
        compiler_params=_params(1),
        name="pre_even",
    )(x2, g, w_qkv, cos, sin)


def _s5_in_kernel(x_ref, g_ref, wu_ref, ut_ref, u_scr):
    nc = x_ref.shape[0] // S5_CHUNK
    groups_per_slab = 128 // S5_GROUP
    h = _rmsnorm_rows(x_ref[...], g_ref[...]).astype(BF16)
    u = _dot(h, wu_ref[...])
    for j in range(S5_WIDTH // 128):
        u_scr[j] = u[:, 128 * j:128 * (j + 1)]
    for j in range(S5_WIDTH // 128):
        for ph in range(S5_CHUNK):
            t = u_scr[j, pl.ds(ph, nc, stride=S5_CHUNK), :]
            ut_ref[groups_per_slab * j:groups_per_slab * (j + 1), ph, :, :] = (
                t.T.astype(BF16).reshape(groups_per_slab, S5_GROUP, nc))


def _s5_in(x2, g, wu, bsz, seq):
    nc = seq // S5_CHUNK
    return pl.pallas_call(
        _s5_in_kernel,
        grid=(bsz,),
        in_specs=[pl.BlockSpec((seq, D_MODEL), lambda b: (b, 0)), _const_spec((1, D_MODEL)),
                  _const_spec(wu.shape)],
        out_specs=pl.BlockSpec((S5_GROUPS, S5_CHUNK, S5_GROUP, nc), lambda b: (0, 0, 0, b)),
        out_shape=jax.ShapeDtypeStruct((S5_GROUPS, S5_CHUNK, S5_GROUP, bsz * nc), BF16),
        scratch_shapes=[pltpu.VMEM((S5_WIDTH // 128, seq, 128), F32)],
        compiler_params=_params(1),
        name="s5_in",
    )(x2, g, wu)


def _s5_out_kernel(yt_ref, wglut_ref, bglu_ref, o_ref, b_scr):
    nc = yt_ref.shape[3]
    yt = jnp.concatenate([yt_ref[:, ph, :, :].reshape(S5_WIDTH, nc) for ph in range(S5_CHUNK)], axis=1)
    yt = _gelu_tanh(yt)
    z = _dot(wglut_ref[...], yt.astype(BF16)) + bglu_ref[...]
    bt = yt * (0.5 + 0.5 * jnp.tanh(0.5 * z))
    for j in range(S5_WIDTH // 128):
        for ph in range(S5_CHUNK):
            b_scr[j, pl.ds(ph, nc, stride=S5_CHUNK), :] = bt[128 * j:128 * (j + 1), ph * nc:(ph + 1) * nc].T
    for j in range(S5_WIDTH // 128):
        o_ref[:, 128 * j:128 * (j + 1)] = b_scr[j]


def _s5_out(y_t, w_glu_t, b_glu_col, bsz, seq):
    nc = seq // S5_CHUNK
    return pl.pallas_call(
        _s5_out_kernel,
        grid=(bsz,),
        in_specs=[pl.BlockSpec((S5_GROUPS, S5_CHUNK, S5_GROUP, nc), lambda b: (0, 0, 0, b)),
                  _const_spec(w_glu_t.shape), _const_spec(b_glu_col.shape)],
        out_specs=pl.BlockSpec((seq, S5_WIDTH), lambda b: (b, 0)),
        out_shape=jax.ShapeDtypeStruct((bsz * seq, S5_WIDTH), F32),
        scratch_shapes=[pltpu.VMEM((S5_WIDTH // 128, seq, 128), F32)],
        compiler_params=_params(1),
        name="s5_out",
    )(y_t, w_glu_t, b_glu_col)


def _attn_kernel(lambda_init, halve_from, q_ref, k_ref, v_ref, lam_ref, g_ref, *refs):
    n_weights = len(halve_from)
    o_ref = refs[n_weights]
    for w_ref, w16_ref, col0 in zip(refs[:n_weights], refs[n_weights + 1:], halve_from):
        w = w_ref[...]
        if col0 is not None:
            w = w * jnp.where(lax.broadcasted_iota(jnp.int32, w.shape, 1) >= col0, 0.5, 1.0)
        w16_ref[...] = w.astype(BF16)
    k = k_ref[0]
    lam = lam_ref[...]
    lam_val = (jnp.exp(jnp.sum(lam[0:1] * lam[1:2], axis=-1, keepdims=True))
               - jnp.exp(jnp.sum(lam[2:3] * lam[3:4], axis=-1, keepdims=True)) + lambda_init)
    tq = ATTN_Q_SUB
    lane_group = (lax.broadcasted_iota(jnp.int32, (tq, 256), 1) // 32) % 4
    ones = jnp.ones((k.shape[0], ATTN_V_DIM), BF16)
    v_ext = [jnp.concatenate([v_ref[0, :, 128 * hh:128 * hh + 128], ones], axis=1) for hh in range(2)]
    for t in range(q_ref.shape[1] // tq):
        q = q_ref[0, t * tq:(t + 1) * tq, :]
        zero = jnp.zeros_like(q)
        q_all = jnp.concatenate([jnp.where(lane_group == hc, q, zero) for hc in range(4)], axis=0)
        s = _dot_nt(q_all, k)
        e = jnp.exp2(s - jnp.max(s, axis=-1, keepdims=True)).astype(BF16)
        for hh in range(2):
            r = _dot(e[2 * hh * tq:2 * (hh + 1) * tq], v_ext[hh])
            o = (r[0:tq, 0:ATTN_V_DIM] / r[0:tq, ATTN_V_DIM:]
                 - lam_val * (r[tq:2 * tq, 0:ATTN_V_DIM] / r[tq:2 * tq, ATTN_V_DIM:]))
            o = _rmsnorm_rows(o, g_ref[...]) * (1.0 - lambda_init)
            o_ref[0, t * tq:(t + 1) * tq, 128 * hh:128 * hh + 128] = o.astype(o_ref.dtype)


def _attention(q, k, v, lam, subln_g, lambda_init, later_weights, halve_from):
    bsz, seq, _ = q.shape
    tq = ATTN_Q_TILE
    n_q = seq // tq
    n_steps = bsz * 2 * n_q
    kv_spec = pl.BlockSpec((1, seq, 256), lambda b, p, i: (b, 0, p))
    step = lambda b, p, i: (b * 2 + p) * n_q + i
    slab_in = [pl.BlockSpec((None, w.shape[1] // n_steps, w.shape[2]),
                            functools.partial(lambda idx, b, p, i: (idx, step(b, p, i), 0), idx))
               for w, idx in later_weights]
    slab_out = [pl.BlockSpec((w.shape[1] // n_steps, w.shape[2]), lambda b, p, i: (step(b, p, i), 0))
                for w, _ in later_weights]
    outs = pl.pallas_call(
        functools.partial(_attn_kernel, lambda_init, tuple(halve_from)),
        grid=(bsz, 2, n_q),
        in_specs=[pl.BlockSpec((1, tq, 256), lambda b, p, i: (b, i, p)),
                  kv_spec, kv_spec,
                  _const_spec(lam.shape),
                  _const_spec(subln_g.shape)] + slab_in,
        out_specs=[pl.BlockSpec((1, tq, 256), lambda b, p, i: (b, i, p))] + slab_out,
        out_shape=[jax.ShapeDtypeStruct((bsz, seq, ATTN_WIDTH), BF16)]
        + [jax.ShapeDtypeStruct(w.shape[1:], BF16) for w, _ in later_weights],
        compiler_params=_params(3),
        name="diff_attention",
    )(q, k, v, lam, subln_g, *[w for w, _ in later_weights])
    return outs[0], outs[1:]


def _s5_kernel(n_chunks, ut_ref, mi_ref, min_ref, mo_ref, ctab_ref, y_ref):
    for p in range(S5_PAIRS_PER_STEP):
        two = pl.ds(2 * p, 2)
        _s5_pair(n_chunks, ut_ref.at[two], mi_ref.at[two], min_ref.at[p], mo_ref.at[two], ctab_ref.at[p],
                 y_ref.at[two])


def _s5_pair(n_chunks, ut_ref, mi_ref, min_ref, mo_ref, ctab_ref, y_ref):
    n_rows = ut_ref.shape[2]
    n_steps = n_chunks.bit_length() - 1
    ut2 = ut_ref[...].reshape(2 * S5_CW, n_rows)
    st = _dot_tn(ut2, min_ref[...])
    ctab = ctab_ref[...]
    pos = lax.broadcasted_iota(jnp.int32, (n_rows, 128), 0) % n_chunks
    zs = []
    for d in range(2):
        xr = st[:, 256 * d:256 * d + 128]
        xi = st[:, 256 * d + 128:256 * d + 256]
        for k in range(n_steps):
            s = 1 << k
            if d == 0:
                shift, keep = s, pos >= s
            else:
                shift, keep = n_rows - s, pos < n_chunks - s
            pr = ctab[4 * k + 2 * d:4 * k + 2 * d + 1]
            pi = ctab[4 * k + 2 * d + 1:4 * k + 2 * d + 2]
            sr = jnp.where(keep, pltpu.roll(xr, shift, 0), 0.0)
            si = jnp.where(keep, pltpu.roll(xi, shift, 0), 0.0)
            xr, xi = xr + pr * sr - pi * si, xi + pr * si + pi * sr
        if d == 0:
            shift, keep = 1, pos >= 1
        else:
            shift, keep = n_rows - 1, pos < n_chunks - 1
        zs += [jnp.where(keep, pltpu.roll(xr, shift, 0), 0.0), jnp.where(keep, pltpu.roll(xi, shift, 0), 0.0)]
    z = jnp.concatenate(zs, axis=1).astype(BF16)
    for g in range(2):
        y_ref[g] = _dot(mi_ref[g], ut_ref[g]) + _dot_nt(mo_ref[g], z)


def _s5_chunked(u_t, m_intra_t, m_in_pair, m_out_ext, ctab, n_chunks):
    n_groups, _, n_lanes = u_t.shape
    pp = S5_PAIRS_PER_STEP
    return pl.pallas_call(
        functools.partial(_s5_kernel, n_chunks),
        grid=(n_groups // (2 * pp),),
        in_specs=[pl.BlockSpec((2 * pp, S5_CW, n_lanes), lambda i: (i, 0, 0)),
                  pl.BlockSpec((2 * pp, S5_CW, S5_CW), lambda i: (i, 0, 0)),
                  pl.BlockSpec((pp, 2 * S5_CW, 512), lambda i: (i, 0, 0)),
                  pl.BlockSpec((2 * pp, S5_CW, 512), lambda i: (i, 0, 0)),
                  pl.BlockSpec((pp, 32, 128), lambda i: (i, 0, 0))],
        out_specs=pl.BlockSpec((2 * pp, S5_CW, n_lanes), lambda i: (i, 0, 0)),
        out_shape=jax.ShapeDtypeStruct((n_groups, S5_CW, n_lanes), F32),
        compiler_params=_params(1),
        name="s5_chunked",
    )(u_t, m_intra_t, m_in_pair, m_out_ext, ctab)


def _dot_hp(a, b):
    return jnp.dot(a, b, precision=lax.Precision.HIGHEST, preferred_element_type=F32)


def _bf16_terms(x):
    hi = x.astype(BF16)
    r = x - hi.astype(F32)
    mid = r.astype(BF16)
    return hi, mid, (r - mid.astype(F32)).astype(BF16)


def _spread(values, onehot):
    return sum(_dot(term, onehot) for term in _bf16_terms(values))


def _spread_rows(onehot, values):
    return sum(_dot(onehot, term) for term in _bf16_terms(values))


def _discretise(lr, li, log_step):
    lr = jnp.minimum(lr, -1e-4)
    dt = jnp.exp(log_step)
    mag = jnp.exp(lr * dt)
    ar = mag * jnp.cos(li * dt)
    ai = mag * jnp.sin(li * dt)
    den = lr * lr + li * li
    cr = ((ar - 1.0) * lr + ai * li) / den
    ci = (ai * lr - (ar - 1.0) * li) / den
    return lr * dt, li * dt, cr, ci


def _s5_ops_kernel(n_steps, colp_ref, rowp_ref, bcol_ref, crow_ref, dcol_ref,
                   mi_ref, min_ref, mo_ref, ctab_ref):
    t = S5_CHUNK
    n = S5_GROUP
    onehot = lambda m: jnp.where(m, 1.0, 0.0).astype(BF16)
    i16 = lambda shape, dim: lax.broadcasted_iota(jnp.int32, shape, dim)
    rep_lanes = onehot(i16((t, S5_CW), 1) // n == i16((t, S5_CW), 0))
    tile_lanes = onehot(i16((n, S5_CW), 1) % n == i16((n, S5_CW), 0))
    rep_rows = onehot(i16((S5_CW, t), 0) // n == i16((S5_CW, t), 1))
    tile_rows = onehot(i16((S5_CW, n), 0) % n == i16((S5_CW, n), 1))
    row_group = i16((2 * S5_STATE, S5_CW), 0) // S5_STATE
    lane_group = i16((1, 2 * S5_STATE), 1) // S5_STATE
    lane16 = i16((1, t), 1).astype(F32)
    row16 = i16((t, 1), 0).astype(F32)
    colp = colp_ref[0]
    rowp = rowp_ref[0]

    taps = [[None, None], [None, None]]
    min_t_rows = []
    mo_kinds = []
    for d in range(2):
        lmag, ang, cr, ci = _discretise(colp[:, 3 * d:3 * d + 1], colp[:, 3 * d + 1:3 * d + 2],
                                        colp[:, 3 * d + 2:3 * d + 3])
        b_re, b_im = bcol_ref[0, 2 * d], bcol_ref[0, 2 * d + 1]
        bbr = _spread(cr * b_re - ci * b_im, tile_lanes)
        bbi = _spread(cr * b_im + ci * b_re, tile_lanes)
        lag = (t - 1.0) - lane16 if d == 0 else lane16
        pmag = jnp.exp(lmag * lag)
        pr = _spread(pmag * jnp.cos(ang * lag), rep_lanes)
        pi = _spread(pmag * jnp.sin(ang * lag), rep_lanes)
        rr = pr * bbr - pi * bbi
        ri = pr * bbi + pi * bbr
        for part in (rr, ri):
            min_t_rows.append(jnp.concatenate([jnp.where(row_group == gs, part, 0.0) for gs in range(2)], axis=1))
        c_re, c_im = crow_ref[0, 2 * d], crow_ref[0, 2 * d + 1]
        for gs in range(2):
            own = lane_group == gs
            taps[gs][d] = (_dot_hp(jnp.where(own, c_re, 0.0), rr) - _dot_hp(jnp.where(own, c_im, 0.0), ri))

        lmag_r, ang_r, _, _ = _discretise(rowp[3 * d:3 * d + 1], rowp[3 * d + 1:3 * d + 2],
                                          rowp[3 * d + 2:3 * d + 3])
        tau = row16 + 1.0 if d == 0 else float(t) - row16
        qmag = jnp.exp(tau * lmag_r)
        qr16 = qmag * jnp.cos(tau * ang_r)
        qi16 = qmag * jnp.sin(tau * ang_r)
        qr = _spread_rows(rep_rows, qr16)
        qi = _spread_rows(rep_rows, qi16)
        cre = _spread_rows(tile_rows, c_re)
        cim = _spread_rows(tile_rows, c_im)
        mo_kinds += [qr * cre - qi * cim, -(qr * cim + qi * cre)]

        far = t - 1 if d == 0 else 0
        pw_r, pw_i = qr16[far:far + 1], qi16[far:far + 1]
        for k in range(n_steps):
            ctab_ref[0, 4 * k + 2 * d:4 * k + 2 * d + 1, :] = pw_r
            ctab_ref[0, 4 * k + 2 * d + 1:4 * k + 2 * d + 2, :] = pw_i
            pw_r, pw_i = pw_r * pw_r - pw_i * pw_i, 2.0 * pw_r * pw_i
    ctab_ref[0, 4 * n_steps:, :] = jnp.zeros((32 - 4 * n_steps, 2 * S5_STATE), F32)

    min_ref[0] = jnp.concatenate(min_t_rows, axis=0).T.astype(BF16)
    lane_group_wide = i16((1, 2 * S5_STATE), 1) // S5_STATE
    for gs in range(2):
        mo_ref[gs] = jnp.concatenate([jnp.where(lane_group_wide == gs, kind, 0.0) for kind in mo_kinds],
                                     axis=1).astype(BF16)
        zeros = jnp.zeros((n, S5_CW), F32)
        edge = (t - 1) * n
        lane = i16((n, 2 * S5_CW), 1)
        skip = jnp.where(lane == edge + i16((n, 2 * S5_CW), 0), dcol_ref[0, gs], 0.0)
        kk = (jnp.concatenate([taps[gs][0], zeros], axis=1)
              + pltpu.roll(jnp.concatenate([taps[gs][1], zeros], axis=1), edge, 1) + skip)
        strips = []
        for j in range(t):
            shift = (t - 1 - j) * n
            moved = kk if shift == 0 else pltpu.roll(kk, 2 * S5_CW - shift, 1)
            strips.append(moved[:, 0:S5_CW])
        mi_ref[gs] = jnp.concatenate(strips, axis=0).astype(BF16)


def _s5_operators(lam_re, lam_im, log_step, b_re, b_im, c_re, c_im, d_skip, n_steps):
    g2 = S5_GROUPS // 2
    zero = jnp.zeros_like(lam_re[0])
    step = [jnp.broadcast_to(log_step[d][:, None], lam_re[d].shape) for d in range(2)]
    params = jnp.stack([lam_re[0], lam_im[0], step[0], lam_re[1], lam_im[1], step[1], zero, zero])
    colp = params.transpose(1, 2, 0).reshape(g2, 2 * S5_STATE, 8)
    rowp = params.reshape(8, g2, 2 * S5_STATE).transpose(1, 0, 2)
    bcol = (jnp.stack([b_re[0], b_im[0], b_re[1], b_im[1]])
            .reshape(4, g2, 2 * S5_STATE, S5_GROUP).transpose(1, 0, 2, 3))
    crow = (jnp.stack([c_re[0], c_im[0], c_re[1], c_im[1]])
            .reshape(4, g2, 2, S5_GROUP, S5_STATE).transpose(1, 0, 3, 2, 4).reshape(g2, 4, S5_GROUP, 2 * S5_STATE))
    dcol = d_skip.reshape(g2, 2, S5_GROUP, 1)
    pair = lambda *shape: pl.BlockSpec((1,) + shape, lambda i: (i,) + (0,) * len(shape))
    return pl.pallas_call(
        functools.partial(_s5_ops_kernel, n_steps),
        grid=(g2,),
        in_specs=[pair(2 * S5_STATE, 8), pair(8, 2 * S5_STATE), pair(4, 2 * S5_STATE, S5_GROUP),
                  pair(4, S5_GROUP, 2 * S5_STATE), pair(2, S5_GROUP, 1)],
        out_specs=[pl.BlockSpec((2, S5_CW, S5_CW), lambda i: (i, 0, 0)), pair(2 * S5_CW, 512),
                   pl.BlockSpec((2, S5_CW, 512), lambda i: (i, 0, 0)), pair(32, 2 * S5_STATE)],
        out_shape=[jax.ShapeDtypeStruct((S5_GROUPS, S5_CW, S5_CW), BF16),
                   jax.ShapeDtypeStruct((g2, 2 * S5_CW, 512), BF16),
                   jax.ShapeDtypeStruct((S5_GROUPS, S5_CW, 512), BF16),
                   jax.ShapeDtypeStruct((g2, 32, 2 * S5_STATE), F32)],
        compiler_params=_params(1),
        name="s5_operators",
    )(colp, rowp, bcol, crow, dcol)


def _mlp(x1, g_ref, w1_ref, w2_ref):
    h = _rmsnorm_rows(x1, g_ref[...]).astype(BF16)
    acc = x1
    for j in range(D_FF // FF_TILE):
        hid = _dot(h, w1_ref[:, j * FF_TILE:(j + 1) * FF_TILE])
        hid = jnp.square(jnp.maximum(hid, 0.0)).astype(BF16)
        acc = acc + _dot(hid, w2_ref[j * FF_TILE:(j + 1) * FF_TILE, :])
    return acc


def _post_even_kernel(x_ref, a_ref, b_ref, wout_ref, g_ref, w1_ref, w2_ref, o_ref):
    mix = (_dot(a_ref[...], wout_ref[0:ATTN_WIDTH, :])
           + _dot(b_ref[...].astype(BF16), wout_ref[ATTN_WIDTH:, :]))
    o_ref[...] = _mlp(x_ref[...] + mix, g_ref, w1_ref, w2_ref)


def _post_even(x2, a_out, b_out, w_out, g_mlp, w1, w2):
    n_tok = x2.shape[0]
    tm = TOKEN_TILE
    return pl.pallas_call(
        _post_even_kernel,
        grid=(n_tok // tm,),
        in_specs=[pl.BlockSpec((tm, D_MODEL), lambda i: (i, 0)),
                  pl.BlockSpec((tm, ATTN_WIDTH), lambda i: (i, 0)),
                  pl.BlockSpec((tm, S5_WIDTH), lambda i: (i, 0)),
                  _const_spec(w_out.shape), _const_spec(g_mlp.shape), _const_spec(w1.shape),
                  _const_spec(w2.shape)],
        out_specs=pl.BlockSpec((tm, D_MODEL), lambda i: (i, 0)),
        out_shape=jax.ShapeDtypeStruct((n_tok, D_MODEL), F32),
        compiler_params=_params(1),
        name="post_even",
    )(x2, a_out, b_out, w_out, g_mlp, w1, w2)


def _post_odd_kernel(x_ref, m_ref, wout_ref, g_ref, w1_ref, w2_ref, gf_ref, o_ref):
    x1 = x_ref[...] + _dot(m_ref[...], wout_ref[...])
    o_ref[...] = _rmsnorm_rows(_mlp(x1, g_ref, w1_ref, w2_ref), gf_ref[...])


def _post_odd(x2, mixed, w_out, g_mlp, w1, w2, g_final):
    n_tok = x2.shape[0]
    tm = TOKEN_TILE
    return pl.pallas_call(
        _post_odd_kernel,
        grid=(n_tok // tm,),
        in_specs=[pl.BlockSpec((tm, D_MODEL), lambda i: (i, 0)),
                  pl.BlockSpec((tm, D_MODEL), lambda i: (i, 0)),
                  _const_spec(w_out.shape), _const_spec(g_mlp.shape),
                  _const_spec(w1.shape), _const_spec(w2.shape), _const_spec(g_final.shape)],
        out_specs=pl.BlockSpec((tm, D_MODEL), lambda i: (i, 0)),
        out_shape=jax.ShapeDtypeStruct((n_tok, D_MODEL), F32),
        compiler_params=_params(1),
        name="post_odd",
    )(x2, mixed, w_out, g_mlp, w1, w2, g_final)


def _pre_odd_kernel(x_ref, g_ref, w_ref, q_ref, i_ref, ff_ref, fb_ref, gate_ref):
    h = _rmsnorm_rows(x_ref[...], g_ref[...]).astype(BF16)
    for s, out_ref in enumerate((q_ref, i_ref, ff_ref, fb_ref, gate_ref)):
        out_ref[...] = _dot(h, w_ref[:, s * D_MODEL:(s + 1) * D_MODEL]).astype(out_ref.dtype)


def _pre_odd(x2, g, w):
    n_tok = x2.shape[0]
    tm = TOKEN_TILE
    row_spec = pl.BlockSpec((tm, D_MODEL), lambda i: (i, 0))
    lo = jax.ShapeDtypeStruct((n_tok, D_MODEL), BF16)
    hi = jax.ShapeDtypeStruct((n_tok, D_MODEL), F32)
    return pl.pallas_call(
        _pre_odd_kernel,
        grid=(n_tok // tm,),
        in_specs=[row_spec, _const_spec((1, D_MODEL)), _const_spec(w.shape)],
        out_specs=[row_spec] * 5,
        out_shape=[lo, lo, hi, hi, lo],
        compiler_params=_params(1),
        name="pre_odd",
    )(x2, g, w)


def _chunk_cumprod(x, reverse):
    n = x.shape[0]
    pos = lax.broadcasted_iota(jnp.int32, x.shape, 0) % HGRN_CHUNK
    s = 1
    while s < HGRN_CHUNK:
        if reverse:
            shifted = pltpu.roll(x, n - s, 0)
            x = x * jnp.where(pos < HGRN_CHUNK - s, shifted, 1.0)
        else:
            shifted = pltpu.roll(x, s, 0)
            x = x * jnp.where(pos >= s, shifted, 1.0)
        s *= 2
    return x


def _hgrn_kernel(layer, *refs):
    io, scratch = refs[:8], refs[8:]
    for hd in range(HGRN_HEADS_PER_STEP):
        lanes = pl.ds(128 * hd, 128)
        views = [r.at[:, :, lanes] for r in io[:5]] + [r.at[:, lanes] for r in io[5:7]] + [io[7].at[:, :, lanes]]
        _hgrn_head(layer, *views, *[s.at[hd] for s in scratch])


def _hgrn_head(layer, q_ref, v_ref, ff_ref, fb_ref, gate_ref, lbl_ref, ng_ref, o_ref,
               qd_ref, dec_ref, kv_ref, acc_ref):
    seq = q_ref.shape[1]
    ch = HGRN_CHUNK
    blk = HGRN_BLOCK
    n_chunks = seq // ch
    per_blk = blk // ch
    logits = lbl_ref[...]
    soft = jnp.exp(logits - jnp.max(logits, axis=0, keepdims=True))
    soft = soft / jnp.sum(soft, axis=0, keepdims=True)
    lb = jnp.sum(soft[0:layer + 1], axis=0, keepdims=True) - soft[0:1]

    c2 = 0.5 * (1.0 - lb)
    c1 = lb + c2
    row = lax.broadcasted_iota(jnp.int32, (blk, blk), 0)
    col = lax.broadcasted_iota(jnp.int32, (blk, blk), 1)
    u32 = lambda a: a.astype(jnp.uint32)
    keep = (u32(row - col) <= u32(row % ch), u32(col - row) <= u32(ch - 1 - row % ch))
    row_chunk = lax.broadcasted_iota(jnp.int32, (blk, 128), 0) // ch

    for j in range(seq // blk):
        sl = slice(j * blk, (j + 1) * blk)
        v = v_ref[0, sl, :]
        vt = v.astype(F32).T.astype(BF16)
        qf = q_ref[0, sl, :].astype(F32)
        for d, f_ref in enumerate((ff_ref, fb_ref)):
            ct = c2 * jnp.tanh(f_ref[0, sl, :])
            eb = _chunk_cumprod(c1 + ct, reverse=(d == 1))
            k_inv = (c2 - ct) / eb
            eb3 = eb.reshape(per_blk, ch, 128)
            e_end = eb3[:, ch - 1:ch, :] if d == 0 else eb3[:, 0:1, :]
            dec_ref[d, j * per_blk:(j + 1) * per_blk] = e_end
            q_dec = (qf * eb).astype(BF16)
            qd_ref[d, sl, :] = q_dec
            k_dec = (k_inv.reshape(per_blk, ch, 128) * e_end).reshape(blk, 128).astype(BF16)

            s = _dot_nt(q_dec, k_inv.astype(BF16))
            p = jnp.where(keep[d], s, 0.0).astype(BF16)
            acc_ref[d, sl, :] = _dot(p, v)
            rhs = jnp.concatenate([jnp.where(row_chunk == cc, k_dec, jnp.zeros_like(k_dec))
                                   for cc in range(per_blk)], axis=1)
            kvs = _dot(vt, rhs)
            for cc in range(per_blk):
                kv_ref[d, j * per_blk + cc] = kvs[:, 128 * cc:128 * (cc + 1)]

    def finish(rows):
        o = _rmsnorm_rows(acc_ref[0, rows, :] + acc_ref[1, rows, :], ng_ref[...])
        gate = 0.5 + 0.5 * jnp.tanh(gate_ref[0, rows, :].astype(F32))
        o_ref[0, rows, :] = (o * gate).astype(o_ref.dtype)

    states = [jnp.zeros((128, 128), F32)] * 2
    for n in range(n_chunks):
        for d in range(2):
            c = n if d == 0 else n_chunks - 1 - n
            rows = slice(c * ch, (c + 1) * ch)
            acc_ref[d, rows, :] += _dot_nt(qd_ref[d, rows, :], states[d].astype(BF16))
            states[d] = dec_ref[d, c] * states[d] + kv_ref[d, c]
        if 2 * n >= n_chunks:
            finish(slice(n * ch, (n + 1) * ch))
            finish(slice((n_chunks - 1 - n) * ch, (n_chunks - n) * ch))


def _hgrn(q, v, ff, fb, gate, lb_logits, norm_g, layer):
    bsz, seq, _ = q.shape
    n_chunks = seq // HGRN_CHUNK
    hps = HGRN_HEADS_PER_STEP
    head_spec = pl.BlockSpec((1, seq, 128 * hps), lambda b, h: (b, 0, h))
    return pl.pallas_call(
        functools.partial(_hgrn_kernel, layer),
        grid=(bsz, HGRN_HEADS // hps),
        in_specs=[head_spec] * 5 + [pl.BlockSpec((DEPTH, 128 * hps), lambda b, h: (0, h)),
                                    pl.BlockSpec((1, 128 * hps), lambda b, h: (0, h))],
        out_specs=head_spec,
        out_shape=jax.ShapeDtypeStruct((bsz, seq, D_MODEL), BF16),
        scratch_shapes=[
           pltpu.VMEM((hps, 2, seq, 128), BF16),
           pltpu.VMEM((hps, 2, n_chunks, 1, 128), F32),
           pltpu.VMEM((hps, 2, n_chunks, 128, 128), F32),
           pltpu.VMEM((hps, 2, seq, 128), F32)],
        compiler_params=_params(2),
        name="hgrn2",
    )(q, v, ff, fb, gate, lb_logits, norm_g)


def _rope_pair_tables(seq):
    inv = ROPE_THETA ** (-jnp.arange(0, ATTN_QK_DIM, 2, dtype=F32) / ATTN_QK_DIM)
    ang = jnp.arange(seq, dtype=F32)[:, None] * inv[None, :]
    return jnp.tile(jnp.cos(ang), (1, 4)), jnp.tile(jnp.sin(ang), (1, 4))


def _to_pair_layout(w):
    idx = jnp.arange(ATTN_QK_WIDTH).reshape(2, 2, 2, 2, 32)
    return w[:, idx.transpose(0, 3, 1, 2, 4).reshape(-1)]


def kernel(x, norm_mix_g, norm_mlp_g, final_norm_g, w_ff_in, w_ff_out, w_in_even, w_out_even, diff_lambda, diff_subln_g, s5_lam_re, s5_lam_im, s5_log_step, s5_b_re, s5_b_im, s5_c_re, s5_c_im, s5_d, s5_w_glu, s5_b_glu, w_in_odd, w_out_odd, hgrn_norm_g, hgrn_lb_logits):
    bsz, seq, _ = x.shape
    n_tok = bsz * seq
    n_chunks = seq // S5_CHUNK
    assert n_chunks & (n_chunks - 1) == 0 and n_chunks % 128 == 0
    x2 = x.reshape(n_tok, D_MODEL)
    cos, sin = _rope_pair_tables(seq)

    assert DEPTH == 2, "an even (attention + S5) layer followed by an odd (HGRN2) layer"
    for layer in range(DEPTH):
        g_mix = norm_mix_g[layer].reshape(1, D_MODEL)
        g_mlp = norm_mlp_g[layer].reshape(1, D_MODEL)
        if layer % 2 == 0:
            e = layer // 2
            w = w_in_even[e]
            w_qkv = jnp.concatenate([_to_pair_layout(w[:, :ATTN_QK_WIDTH]),
                                     _to_pair_layout(w[:, ATTN_QK_WIDTH:2 * ATTN_QK_WIDTH]),
                                     w[:, 2 * ATTN_QK_WIDTH:2 * ATTN_QK_WIDTH + ATTN_WIDTH]], axis=1).astype(BF16)
            wu = w[:, 2 * ATTN_QK_WIDTH + ATTN_WIDTH:].astype(BF16)
            q, k, v = _pre_even(x2, g_mix, w_qkv, cos, sin, seq)
            u_t = _s5_in(x2, g_mix, wu, bsz, seq)
            lambda_init = 0.8 - 0.6 * math.exp(-0.3 * layer)
            a_out, w16 = _attention(q.reshape(bsz, seq, -1), k.reshape(bsz, seq, -1), v.reshape(bsz, seq, -1),
                                    diff_lambda[e], diff_subln_g[e].reshape(1, ATTN_V_DIM), lambda_init,
                                    [(w_out_even, e), (w_ff_in, layer), (w_ff_out, layer),
                                     (w_in_odd, e), (w_out_odd, e), (w_ff_in, layer + 1), (w_ff_out, layer + 1)],
                                    [None, None, None, 2 * D_MODEL, None, None, None])
            w_out_e, w1, w2, w_in_o, w_out_o, w1_next, w2_next = w16
            ops = _s5_operators(s5_lam_re[e], s5_lam_im[e], s5_log_step[e], s5_b_re[e], s5_b_im[e],
                                s5_c_re[e], s5_c_im[e], s5_d[e], n_chunks.bit_length() - 1)
            y_t = _s5_chunked(u_t.reshape(S5_GROUPS, S5_CW, bsz * n_chunks), *ops, n_chunks)
            b_out = _s5_out(y_t.reshape(S5_GROUPS, S5_CHUNK, S5_GROUP, bsz * n_chunks),
                            s5_w_glu[e].T.astype(BF16), s5_b_glu[e].reshape(S5_WIDTH, 1), bsz, seq)
            x2 = _post_even(x2, a_out.reshape(n_tok, ATTN_WIDTH), b_out, w_out_e, g_mlp, w1, w2)
        else:
            o_i = layer // 2
            q, v, ff, fb, gate = _pre_odd(x2, g_mix, w_in_o)
            shp = (bsz, seq, D_MODEL)
            mixed = _hgrn(q.reshape(shp), v.reshape(shp), ff.reshape(shp), fb.reshape(shp), gate.reshape(shp),
                          hgrn_lb_logits, hgrn_norm_g[o_i].reshape(1, D_MODEL), layer)
            x2 = _post_odd(x2, mixed.reshape(n_tok, D_MODEL), w_out_o, g_mlp, w1_next, w2_next,
                           final_norm_g.reshape(1, D_MODEL))
    return x2.reshape(bsz, seq, D_MODEL)
```

```python
import functools
import math

import jax
import jax.numpy as jnp
from jax import lax
from jax.experimental import pallas as pl
from jax.experimental.pallas import tpu as pltpu

D_MODEL = 1024
DEPTH = 2
ATTN_HEADS = 4
ATTN_QK_DIM = 64
ATTN_V_DIM = 128
ATTN_QK_WIDTH = 512
ATTN_WIDTH = 512
ROPE_THETA = 10000.0
S5_WIDTH = 512
S5_GROUP = 16
S5_GROUPS = 32
S5_STATE = 64
HGRN_HEADS = 8
HGRN_CHUNK = 64
HGRN_BLOCK = 256
HGRN_HEADS_PER_STEP = 4
D_FF = 4096
EPS = 1e-6

S5_CHUNK = 16
S5_CW = S5_CHUNK * S5_GROUP

TOKEN_TILE = 1024
ATTN_Q_TILE = 1024
ATTN_Q_SUB = 512
FF_TILE = 1024
VMEM_LIMIT = 56 * 1024 * 1024

BF16 = jnp.bfloat16
F32 = jnp.float32


def _const_spec(shape):
    nd = len(shape)
    return pl.BlockSpec(shape, lambda *_: (0,) * nd, pipeline_mode=pl.Buffered(1))


def _params(n_axes):
    return pltpu.CompilerParams(dimension_semantics=("arbitrary",) * n_axes,
                                vmem_limit_bytes=VMEM_LIMIT)


def _rmsnorm_rows(x, g):
    ms = jnp.mean(x * x, axis=-1, keepdims=True)
    return x * lax.rsqrt(ms + EPS) * g


def _gelu_tanh(x):
    c = math.sqrt(2.0 / math.pi)
    return 0.5 * x * (1.0 + jnp.tanh(c * (x + 0.044715 * (x * x * x))))


def _dot(a, b):
    return jnp.dot(a, b, preferred_element_type=F32)


def _dot_nt(a, b):
    return lax.dot_general(a, b, (((1,), (1,)), ((), ())), preferred_element_type=F32)


def _dot_tn(a, b):
    return lax.dot_general(a, b, (((0,), (0,)), ((), ())), preferred_element_type=F32)


def _pre_even_kernel(x_ref, g_ref, w_ref, cos_ref, sin_ref, q_ref, k_ref, v_ref):
    h = _rmsnorm_rows(x_ref[...], g_ref[...]).astype(BF16)
    cos = cos_ref[...]
    sin = sin_ref[...]
    for out_ref, base, scale in ((q_ref, 0, ATTN_QK_DIM ** -0.5 * math.log2(math.e)),
                                 (k_ref, ATTN_QK_WIDTH, 1.0)):
        p = _dot(h, w_ref[:, base:base + ATTN_QK_WIDTH])
        for pair in range(2):
            lo = p[:, 256 * pair:256 * pair + 128]
            hi = p[:, 256 * pair + 128:256 * pair + 256]
            out_ref[:, 256 * pair:256 * pair + 128] = ((lo * cos - hi * sin) * scale).astype(BF16)
            out_ref[:, 256 * pair + 128:256 * pair + 256] = ((hi * cos + lo * sin) * scale).astype(BF16)
    v_ref[...] = _dot(h, w_ref[:, 2 * ATTN_QK_WIDTH:]).astype(BF16)


def _pre_even(x2, g, w_qkv, cos, sin, seq):
    n_tok = x2.shape[0]
    tm = TOKEN_TILE
    n_pos_blocks = seq // tm
    out = jax.ShapeDtypeStruct((n_tok, 512), BF16)
    row_spec = pl.BlockSpec((tm, 512), lambda i: (i, 0))
    rope_spec = pl.BlockSpec((tm, 128), lambda i: (i % n_pos_blocks, 0))
    return pl.pallas_call(
        _pre_even_kernel,
        grid=(n_tok // tm,),
        in_specs=[pl.BlockSpec((tm, D_MODEL), lambda i: (i, 0)), _const_spec((1, D_MODEL)),
                  _const_spec(w_qkv.shape), rope_spec, rope_spec],
        out_specs=[row_spec] * 3,
        out_shape=[out] * 3,
        compiler_params=_params(1),
        name="pre_even",
    )(x2, g, w_qkv, cos, sin)


def _s5_in_kernel(x_ref, g_ref, wu_ref, ut_ref, u_scr):
    nc = x_ref.shape[0] // S5_CHUNK
    groups_per_slab = 128 // S5_GROUP
    h = _rmsnorm_rows(x_ref[...], g_ref[...]).astype(BF16)
    u = _dot(h, wu_ref[...])
    for j in range(S5_WIDTH // 128):
        u_scr[j] = u[:, 128 * j:128 * (j + 1)]
    for j in range(S5_WIDTH // 128):
        for ph in range(S5_CHUNK):
            t = u_scr[j, pl.ds(ph, nc, stride=S5_CHUNK), :]
            ut_ref[groups_per_slab * j:groups_per_slab * (j + 1), ph, :, :] = (
                t.T.astype(BF16).reshape(groups_per_slab, S5_GROUP, nc))


def _s5_in(x2, g, wu, bsz, seq):
    nc = seq // S5_CHUNK
    return pl.pallas_call(
        _s5_in_kernel,
        grid=(bsz,),
        in_specs=[pl.BlockSpec((seq, D_MODEL), lambda b: (b, 0)), _const_spec((1, D_MODEL)),
                  _const_spec(wu.shape)],
        out_specs=pl.BlockSpec((S5_GROUPS, S5_CHUNK, S5_GROUP, nc), lambda b: (0, 0, 0, b)),
        out_shape=jax.ShapeDtypeStruct((S5_GROUPS, S5_CHUNK, S5_GROUP, bsz * nc), BF16),
        scratch_shapes=[pltpu.VMEM((S5_WIDTH // 128, seq, 128), F32)],
        compiler_params=_params(1),
        name="s5_in",
    )(x2, g, wu)


def _s5_out_kernel(yt_ref, wglut_ref, bglu_ref, o_ref, b_scr):
    nc = yt_ref.shape[3]
    yt = jnp.concatenate([yt_ref[:, ph, :, :].reshape(S5_WIDTH, nc) for ph in range(S5_CHUNK)], axis=1)
    yt = _gelu_tanh(yt)
    z = _dot(wglut_ref[...], yt.astype(BF16)) + bglu_ref[...]
    bt = yt * (0.5 + 0.5 * jnp.tanh(0.5 * z))
    for j in range(S5_WIDTH // 128):
        for ph in range(S5_CHUNK):
            b_scr[j, pl.ds(ph, nc, stride=S5_CHUNK), :] = bt[128 * j:128 * (j + 1), ph * nc:(ph + 1) * nc].T
    for j in range(S5_WIDTH // 128):
        o_ref[:, 128 * j:128 * (j + 1)] = b_scr[j]


def _s5_out(y_t, w_glu_t, b_glu_col, bsz, seq):
    nc = seq // S5_CHUNK
    return pl.pallas_call(
        _s5_out_kernel,
        grid=(bsz,),
        in_specs=[pl.BlockSpec((S5_GROUPS, S5_CHUNK, S5_GROUP, nc), lambda b: (0, 0, 0, b)),
                  _const_spec(w_glu_t.shape), _const_spec(b_glu_col.shape)],
        out_specs=pl.BlockSpec((seq, S5_WIDTH), lambda b: (b, 0)),
        out_shape=jax.ShapeDtypeStruct((bsz * seq, S5_WIDTH), F32),
        scratch_shapes=[pltpu.VMEM((S5_WIDTH // 128, seq, 128), F32)],
        compiler_params=_params(1),
        name="s5_out",
    )(y_t, w_glu_t, b_glu_col)


def _attn_kernel(lambda_init, halve_from, q_ref, k_ref, v_ref, lam_ref, g_ref, *refs):
    n_weights = len(halve_from)
    o_ref = refs[n_weights]
    for w_ref, w16_ref, col0 in zip(refs[:n_weights], refs[n_weights + 1:], halve_from):
        w = w_ref[...]
        if col0 is not None:
            w = w * jnp.where(lax.broadcasted_iota(jnp.int32, w.shape, 1) >= col0, 0.5, 1.0)
        w16_ref[...] = w.astype(BF16)
    k = k_ref[0]
    lam = lam_ref[...]
    lam_val = (jnp.exp(jnp.sum(lam[0:1] * lam[1:2], axis=-1, keepdims=True))
               - jnp.exp(jnp.sum(lam[2:3] * lam[3:4], axis=-1, keepdims=True)) + lambda_init)
    tq = ATTN_Q_SUB
    lane_group = (lax.broadcasted_iota(jnp.int32, (tq, 256), 1) // 32) % 4
    ones = jnp.ones((k.shape[0], ATTN_V_DIM), BF16)
    v_ext = [jnp.concatenate([v_ref[0, :, 128 * hh:128 * hh + 128], ones], axis=1) for hh in range(2)]
    for t in range(q_ref.shape[1] // tq):
        q = q_ref[0, t * tq:(t + 1) * tq, :]
        zero = jnp.zeros_like(q)
        q_all = jnp.concatenate([jnp.where(lane_group == hc, q, zero) for hc in range(4)], axis=0)
        s = _dot_nt(q_all, k)
        e = jnp.exp2(s - jnp.max(s, axis=-1, keepdims=True)).astype(BF16)
        for hh in range(2):
            r = _dot(e[2 * hh * tq:2 * (hh + 1) * tq], v_ext[hh])
            o = (r[0:tq, 0:ATTN_V_DIM] / r[0:tq, ATTN_V_DIM:]
                 - lam_val * (r[tq:2 * tq, 0:ATTN_V_DIM] / r[tq:2 * tq, ATTN_V_DIM:]))
            o = _rmsnorm_rows(o, g_ref[...]) * (1.0 - lambda_init)
            o_ref[0, t * tq:(t + 1) * tq, 128 * hh:128 * hh + 128] = o.astype(o_ref.dtype)


def _attention(q, k, v, lam, subln_g, lambda_init, later_weights, halve_from):
    bsz, seq, _ = q.shape
    tq = ATTN_Q_TILE
    n_q = seq // tq
    n_steps = bsz * 2 * n_q
    kv_spec = pl.BlockSpec((1, seq, 256), lambda b, p, i: (b, 0, p))
    step = lambda b, p, i: (b * 2 + p) * n_q + i
    slab_in = [pl.BlockSpec((None, w.shape[1] // n_steps, w.shape[2]),
                            functools.partial(lambda idx, b, p, i: (idx, step(b, p, i), 0), idx))
               for w, idx in later_weights]
    slab_out = [pl.BlockSpec((w.shape[1] // n_steps, w.shape[2]), lambda b, p, i: (step(b, p, i), 0))
                for w, _ in later_weights]
    outs = pl.pallas_call(
        functools.partial(_attn_kernel, lambda_init, tuple(halve_from)),
        grid=(bsz, 2, n_q),
        in_specs=[pl.BlockSpec((1, tq, 256), lambda b, p, i: (b, i, p)),
                  kv_spec, kv_spec,
                  _const_spec(lam.shape),
                  _const_spec(subln_g.shape)] + slab_in,
        out_specs=[pl.BlockSpec((1, tq, 256), lambda b, p, i: (b, i, p))] + slab_out,
        out_shape=[jax.ShapeDtypeStruct((bsz, seq, ATTN_WIDTH), BF16)]
        + [jax.ShapeDtypeStruct(w.shape[1:], BF16) for w, _ in later_weights],
        compiler_params=_params(3),
        name="diff_attention",
    )(q, k, v, lam, subln_g, *[w for w, _ in later_weights])
    return outs[0], outs[1:]


def _s5_kernel(n_chunks, ut_ref, mi_ref, min_ref, mo_ref, ctab_ref, y_ref):
    n_rows = ut_ref.shape[2]
    n_steps = n_chunks.bit_length() - 1
    ut2 = ut_ref[...].reshape(2 * S5_CW, n_rows)
    st = _dot_tn(ut2, min_ref[0])
    ctab = ctab_ref[0]
    pos = lax.broadcasted_iota(jnp.int32, (n_rows, 128), 0) % n_chunks
    zs = []
    for d in range(2):
        xr = st[:, 256 * d:256 * d + 128]
        xi = st[:, 256 * d + 128:256 * d + 256]
        for k in range(n_steps):
            s = 1 << k
            if d == 0:
                shift, keep = s, pos >= s
            else:
                shift, keep = n_rows - s, pos < n_chunks - s
            pr = ctab[4 * k + 2 * d:4 * k + 2 * d + 1]
            pi = ctab[4 * k + 2 * d + 1:4 * k + 2 * d + 2]
            sr = jnp.where(keep, pltpu.roll(xr, shift, 0), 0.0)
            si = jnp.where(keep, pltpu.roll(xi, shift, 0), 0.0)
            xr, xi = xr + pr * sr - pi * si, xi + pr * si + pi * sr
        if d == 0:
            shift, keep = 1, pos >= 1
        else:
            shift, keep = n_rows - 1, pos < n_chunks - 1
        zs += [jnp.where(keep, pltpu.roll(xr, shift, 0), 0.0), jnp.where(keep, pltpu.roll(xi, shift, 0), 0.0)]
    z = jnp.concatenate(zs, axis=1).astype(BF16)
    for g in range(2):
        y_ref[g] = _dot(mi_ref[g], ut_ref[g]) + _dot_nt(mo_ref[g], z)


def _s5_chunked(u_t, m_intra_t, m_in_pair, m_out_ext, ctab, n_chunks):
    n_groups, _, n_lanes = u_t.shape
    return pl.pallas_call(
        functools.partial(_s5_kernel, n_chunks),
        grid=(n_groups // 2,),
        in_specs=[pl.BlockSpec((2, S5_CW, n_lanes), lambda i: (i, 0, 0)),
                  pl.BlockSpec((2, S5_CW, S5_CW), lambda i: (i, 0, 0)),
                  pl.BlockSpec((1, 2 * S5_CW, 512), lambda i: (i, 0, 0)),
                  pl.BlockSpec((2, S5_CW, 512), lambda i: (i, 0, 0)),
                  pl.BlockSpec((1, 32, 128), lambda i: (i, 0, 0))],
        out_specs=pl.BlockSpec((2, S5_CW, n_lanes), lambda i: (i, 0, 0)),
        out_shape=jax.ShapeDtypeStruct((n_groups, S5_CW, n_lanes), F32),
        compiler_params=_params(1),
        name="s5_chunked",
    )(u_t, m_intra_t, m_in_pair, m_out_ext, ctab)


def _dot_hp(a, b):
    return jnp.dot(a, b, precision=lax.Precision.HIGHEST, preferred_element_type=F32)


def _bf16_terms(x):
    hi = x.astype(BF16)
    r = x - hi.astype(F32)
    mid = r.astype(BF16)
    return hi, mid, (r - mid.astype(F32)).astype(BF16)


def _spread(values, onehot):
    return sum(_dot(term, onehot) for term in _bf16_terms(values))


def _spread_rows(onehot, values):
    return sum(_dot(onehot, term) for term in _bf16_terms(values))


def _discretise(lr, li, log_step):
    lr = jnp.minimum(lr, -1e-4)
    dt = jnp.exp(log_step)
    mag = jnp.exp(lr * dt)
    ar = mag * jnp.cos(li * dt)
    ai = mag * jnp.sin(li * dt)
    den = lr * lr + li * li
    cr = ((ar - 1.0) * lr + ai * li) / den
    ci = (ai * lr - (ar - 1.0) * li) / den
    return lr * dt, li * dt, cr, ci


def _s5_ops_kernel(n_steps, rowp_ref, bcol_ref, crow_ref, dcol_ref,
                   mi_ref, min_ref, mo_ref, ctab_ref):
    t = S5_CHUNK
    n = S5_GROUP
    onehot = lambda m: jnp.where(m, 1.0, 0.0).astype(BF16)
    i16 = lambda shape, dim: lax.broadcasted_iota(jnp.int32, shape, dim)
    rep_lanes = onehot(i16((t, S5_CW), 1) // n == i16((t, S5_CW), 0))
    tile_lanes = onehot(i16((n, S5_CW), 1) % n == i16((n, S5_CW), 0))
    rep_rows = onehot(i16((S5_CW, t), 0) // n == i16((S5_CW, t), 1))
    tile_rows = onehot(i16((S5_CW, n), 0) % n == i16((S5_CW, n), 1))
    row_group = i16((2 * S5_STATE, S5_CW), 0) // S5_STATE
    lane_group = i16((1, 2 * S5_STATE), 1) // S5_STATE
    row16 = i16((t, 1), 0).astype(F32)
    rowp = rowp_ref[0]

    taps = [[None, None], [None, None]]
    min_t_rows = []
    mo_kinds = []
    for d in range(2):
        lmag_r, ang_r, cr_r, ci_r = _discretise(rowp[3 * d:3 * d + 1], rowp[3 * d + 1:3 * d + 2],
                                                rowp[3 * d + 2:3 * d + 3])
        lag = (t - 1.0) - row16 if d == 0 else row16
        pmag = jnp.exp(lag * lmag_r)
        by_state = jnp.concatenate([pmag * jnp.cos(lag * ang_r), pmag * jnp.sin(lag * ang_r), cr_r, ci_r,
                                    jnp.zeros((128 - 2 * t - 2, 2 * S5_STATE), F32)], axis=0).T
        cr, ci = by_state[:, 2 * t:2 * t + 1], by_state[:, 2 * t + 1:2 * t + 2]
        b_re, b_im = bcol_ref[0, 2 * d], bcol_ref[0, 2 * d + 1]
        bbr = _spread(cr * b_re - ci * b_im, tile_lanes)
        bbi = _spread(cr * b_im + ci * b_re, tile_lanes)
        pr = _spread(by_state[:, 0:t], rep_lanes)
        pi = _spread(by_state[:, t:2 * t], rep_lanes)
        rr = pr * bbr - pi * bbi
        ri = pr * bbi + pi * bbr
        for part in (rr, ri):
            min_t_rows.append(jnp.concatenate([jnp.where(row_group == gs, part, 0.0) for gs in range(2)], axis=1))
        c_re, c_im = crow_ref[0, 2 * d], crow_ref[0, 2 * d + 1]
        for gs in range(2):
            own = lane_group == gs
            taps[gs][d] = (_dot_hp(jnp.where(own, c_re, 0.0), rr) - _dot_hp(jnp.where(own, c_im, 0.0), ri))

        tau = row16 + 1.0 if d == 0 else float(t) - row16
        qmag = jnp.exp(tau * lmag_r)
        qr16 = qmag * jnp.cos(tau * ang_r)
        qi16 = qmag * jnp.sin(tau * ang_r)
        qr = _spread_rows(rep_rows, qr16)
        qi = _spread_rows(rep_rows, qi16)
        cre = _spread_rows(tile_rows, c_re)
        cim = _spread_rows(tile_rows, c_im)
        mo_kinds += [qr * cre - qi * cim, -(qr * cim + qi * cre)]

        far = t - 1 if d == 0 else 0
        pw_r, pw_i = qr16[far:far + 1], qi16[far:far + 1]
        for k in range(n_steps):
            ctab_ref[0, 4 * k + 2 * d:4 * k + 2 * d + 1, :] = pw_r
            ctab_ref[0, 4 * k + 2 * d + 1:4 * k + 2 * d + 2, :] = pw_i
            pw_r, pw_i = pw_r * pw_r - pw_i * pw_i, 2.0 * pw_r * pw_i
    ctab_ref[0, 4 * n_steps:, :] = jnp.zeros((32 - 4 * n_steps, 2 * S5_STATE), F32)

    min_ref[0] = jnp.concatenate(min_t_rows, axis=0).T.astype(BF16)
    lane_group_wide = i16((1, 2 * S5_STATE), 1) // S5_STATE
    for gs in range(2):
        mo_ref[gs] = jnp.concatenate([jnp.where(lane_group_wide == gs, kind, 0.0) for kind in mo_kinds],
                                     axis=1).astype(BF16)
        zeros = jnp.zeros((n, S5_CW), F32)
        edge = (t - 1) * n
        lane = i16((n, 2 * S5_CW), 1)
        skip = jnp.where(lane == edge + i16((n, 2 * S5_CW), 0), dcol_ref[0, gs], 0.0)
        kk = (jnp.concatenate([taps[gs][0], zeros], axis=1)
              + pltpu.roll(jnp.concatenate([taps[gs][1], zeros], axis=1), edge, 1) + skip)
        strips = []
        for j in range(t):
            shift = (t - 1 - j) * n
            moved = kk if shift == 0 else pltpu.roll(kk, 2 * S5_CW - shift, 1)
            strips.append(moved[:, 0:S5_CW])
        mi_ref[gs] = jnp.concatenate(strips, axis=0).astype(BF16)


def _s5_operators(lam_re, lam_im, log_step, b_re, b_im, c_re, c_im, d_skip, n_steps):
    g2 = S5_GROUPS // 2
    zero = jnp.zeros_like(lam_re[0])
    step = [jnp.broadcast_to(log_step[d][:, None], lam_re[d].shape) for d in range(2)]
    params = jnp.stack([lam_re[0], lam_im[0], step[0], lam_re[1], lam_im[1], step[1], zero, zero])
    rowp = params.reshape(8, g2, 2 * S5_STATE).transpose(1, 0, 2)
    bcol = (jnp.stack([b_re[0], b_im[0], b_re[1], b_im[1]])
            .reshape(4, g2, 2 * S5_STATE, S5_GROUP).transpose(1, 0, 2, 3))
    crow = (jnp.stack([c_re[0], c_im[0], c_re[1], c_im[1]])
            .reshape(4, g2, 2, S5_GROUP, S5_STATE).transpose(1, 0, 3, 2, 4).reshape(g2, 4, S5_GROUP, 2 * S5_STATE))
    dcol = d_skip.reshape(g2, 2, S5_GROUP, 1)
    pair = lambda *shape: pl.BlockSpec((1,) + shape, lambda i: (i,) + (0,) * len(shape))
    return pl.pallas_call(
        functools.partial(_s5_ops_kernel, n_steps),
        grid=(g2,),
        in_specs=[pair(8, 2 * S5_STATE), pair(4, 2 * S5_STATE, S5_GROUP),
                  pair(4, S5_GROUP, 2 * S5_STATE), pair(2, S5_GROUP, 1)],
        out_specs=[pl.BlockSpec((2, S5_CW, S5_CW), lambda i: (i, 0, 0)), pair(2 * S5_CW, 512),
                   pl.BlockSpec((2, S5_CW, 512), lambda i: (i, 0, 0)), pair(32, 2 * S5_STATE)],
        out_shape=[jax.ShapeDtypeStruct((S5_GROUPS, S5_CW, S5_CW), BF16),
                   jax.ShapeDtypeStruct((g2, 2 * S5_CW, 512), BF16),
                   jax.ShapeDtypeStruct((S5_GROUPS, S5_CW, 512), BF16),
                   jax.ShapeDtypeStruct((g2, 32, 2 * S5_STATE), F32)],
        compiler_params=_params(1),
        name="s5_operators",
    )(rowp, bcol, crow, dcol)


def _mlp(x1, g_ref, w1_ref, w2_ref):
    h = _rmsnorm_rows(x1, g_ref[...]).astype(BF16)
    acc = x1
    for j in range(D_FF // FF_TILE):
        hid = _dot(h, w1_ref[:, j * FF_TILE:(j + 1) * FF_TILE])
        hid = jnp.square(jnp.maximum(hid, 0.0)).astype(BF16)
        acc = acc + _dot(hid, w2_ref[j * FF_TILE:(j + 1) * FF_TILE, :])
    return acc


def _post_even_kernel(x_ref, a_ref, b_ref, wout_ref, g_ref, w1_ref, w2_ref, o_ref):
    mix = (_dot(a_ref[...], wout_ref[0:ATTN_WIDTH, :])
           + _dot(b_ref[...].astype(BF16), wout_ref[ATTN_WIDTH:, :]))
    o_ref[...] = _mlp(x_ref[...] + mix, g_ref, w1_ref, w2_ref)


def _post_even(x2, a_out, b_out, w_out, g_mlp, w1, w2):
    n_tok = x2.shape[0]
    tm = TOKEN_TILE
    return pl.pallas_call(
        _post_even_kernel,
        grid=(n_tok // tm,),
        in_specs=[pl.BlockSpec((tm, D_MODEL), lambda i: (i, 0)),
                  pl.BlockSpec((tm, ATTN_WIDTH), lambda i: (i, 0)),
                  pl.BlockSpec((tm, S5_WIDTH), lambda i: (i, 0)),
                  _const_spec(w_out.shape), _const_spec(g_mlp.shape), _const_spec(w1.shape),
                  _const_spec(w2.shape)],
        out_specs=pl.BlockSpec((tm, D_MODEL), lambda i: (i, 0)),
        out_shape=jax.ShapeDtypeStruct((n_tok, D_MODEL), F32),
        compiler_params=_params(1),
        name="post_even",
    )(x2, a_out, b_out, w_out, g_mlp, w1, w2)


def _post_odd_kernel(x_ref, m_ref, wout_ref, g_ref, w1_ref, w2_ref, gf_ref, o_ref):
    x1 = x_ref[...] + _dot(m_ref[...], wout_ref[...])
    o_ref[...] = _rmsnorm_rows(_mlp(x1, g_ref, w1_ref, w2_ref), gf_ref[...])


def _post_odd(x2, mixed, w_out, g_mlp, w1, w2, g_final):
    n_tok = x2.shape[0]
    tm = TOKEN_TILE
    return pl.pallas_call(
        _post_odd_kernel,
        grid=(n_tok // tm,),
        in_specs=[pl.BlockSpec((tm, D_MODEL), lambda i: (i, 0)),
                  pl.BlockSpec((tm, D_MODEL), lambda i: (i, 0)),
                  _const_spec(w_out.shape), _const_spec(g_mlp.shape),
                  _const_spec(w1.shape), _const_spec(w2.shape), _const_spec(g_final.shape)],
        out_specs=pl.BlockSpec((tm, D_MODEL), lambda i: (i, 0)),
        out_shape=jax.ShapeDtypeStruct((n_tok, D_MODEL), F32),
        compiler_params=_params(1),
        name="post_odd",
    )(x2, mixed, w_out, g_mlp, w1, w2, g_final)


def _pre_odd_kernel(x_ref, g_ref, w_ref, q_ref, i_ref, ff_ref, fb_ref, gate_ref):
    h = _rmsnorm_rows(x_ref[...], g_ref[...]).astype(BF16)
    for s, out_ref in enumerate((q_ref, i_ref, ff_ref, fb_ref, gate_ref)):
        out_ref[...] = _dot(h, w_ref[:, s * D_MODEL:(s + 1) * D_MODEL]).astype(out_ref.dtype)


def _pre_odd(x2, g, w):
    n_tok = x2.shape[0]
    tm = TOKEN_TILE
    row_spec = pl.BlockSpec((tm, D_MODEL), lambda i: (i, 0))
    lo = jax.ShapeDtypeStruct((n_tok, D_MODEL), BF16)
    hi = jax.ShapeDtypeStruct((n_tok, D_MODEL), F32)
    return pl.pallas_call(
        _pre_odd_kernel,
        grid=(n_tok // tm,),
        in_specs=[row_spec, _const_spec((1, D_MODEL)), _const_spec(w.shape)],
        out_specs=[row_spec] * 5,
        out_shape=[lo, lo, hi, hi, lo],
        compiler_params=_params(1),
        name="pre_odd",
    )(x2, g, w)


def _chunk_cumprod(x, reverse):
    n = x.shape[0]
    pos = lax.broadcasted_iota(jnp.int32, x.shape, 0) % HGRN_CHUNK
    s = 1
    while s < HGRN_CHUNK:
        if reverse:
            shifted = pltpu.roll(x, n - s, 0)
            x = x * jnp.where(pos < HGRN_CHUNK - s, shifted, 1.0)
        else:
            shifted = pltpu.roll(x, s, 0)
            x = x * jnp.where(pos >= s, shifted, 1.0)
        s *= 2
    return x


def _hgrn_kernel(layer, *refs):
    io, scratch = refs[:8], refs[8:]
    for hd in range(HGRN_HEADS_PER_STEP):
        lanes = pl.ds(128 * hd, 128)
        views = [r.at[:, :, lanes] for r in io[:5]] + [r.at[:, lanes] for r in io[5:7]] + [io[7].at[:, :, lanes]]
        _hgrn_head(layer, *views, *[s.at[hd] for s in scratch])


def _hgrn_head(layer, q_ref, v_ref, ff_ref, fb_ref, gate_ref, lbl_ref, ng_ref, o_ref,
               qd_ref, dec_ref, kv_ref, acc_ref):
    seq = q_ref.shape[1]
    ch = HGRN_CHUNK
    blk = HGRN_BLOCK
    n_chunks = seq // ch
    per_blk = blk // ch
    logits = lbl_ref[...]
    soft = jnp.exp(logits - jnp.max(logits, axis=0, keepdims=True))
    soft = soft / jnp.sum(soft, axis=0, keepdims=True)
    lb = jnp.sum(soft[0:layer + 1], axis=0, keepdims=True) - soft[0:1]

    c2 = 0.5 * (1.0 - lb)
    c1 = lb + c2
    row = lax.broadcasted_iota(jnp.int32, (blk, blk), 0)
    col = lax.broadcasted_iota(jnp.int32, (blk, blk), 1)
    u32 = lambda a: a.astype(jnp.uint32)
    keep = (u32(row - col) <= u32(row % ch), u32(col - row) <= u32(ch - 1 - row % ch))
    row_chunk = lax.broadcasted_iota(jnp.int32, (blk, 128), 0) // ch

    for j in range(seq // blk):
        sl = slice(j * blk, (j + 1) * blk)
        v = v_ref[0, sl, :]
        vt = v.astype(F32).T.astype(BF16)
        qf = q_ref[0, sl, :].astype(F32)
        for d, f_ref in enumerate((ff_ref, fb_ref)):
            ct = c2 * jnp.tanh(f_ref[0, sl, :])
            eb = _chunk_cumprod(c1 + ct, reverse=(d == 1))
            k_inv = (c2 - ct) / eb
            eb3 = eb.reshape(per_blk, ch, 128)
            e_end = eb3[:, ch - 1:ch, :] if d == 0 else eb3[:, 0:1, :]
            dec_ref[d, j * per_blk:(j + 1) * per_blk] = e_end
            q_dec = (qf * eb).astype(BF16)
            qd_ref[d, sl, :] = q_dec
            k_dec = (k_inv.reshape(per_blk, ch, 128) * e_end).reshape(blk, 128).astype(BF16)

            s = _dot_nt(q_dec, k_inv.astype(BF16))
            p = jnp.where(keep[d], s, 0.0).astype(BF16)
            acc_ref[d, sl, :] = _dot(p, v)
            rhs = jnp.concatenate([jnp.where(row_chunk == cc, k_dec, jnp.zeros_like(k_dec))
                                   for cc in range(per_blk)], axis=1)
            kvs = _dot(vt, rhs)
            for cc in range(per_blk):
                kv_ref[d, j * per_blk + cc] = kvs[:, 128 * cc:128 * (cc + 1)]

    def finish(rows):
        o = _rmsnorm_rows(acc_ref[0, rows, :] + acc_ref[1, rows, :], ng_ref[...])
        gate = 0.5 + 0.5 * jnp.tanh(gate_ref[0, rows, :].astype(F32))
        o_ref[0, rows, :] = (o * gate).astype(o_ref.dtype)

    states = [jnp.zeros((128, 128), F32)] * 2
    for n in range(n_chunks):
        for d in range(2):
            c = n if d == 0 else n_chunks - 1 - n
            rows = slice(c * ch, (c + 1) * ch)
            acc_ref[d, rows, :] += _dot_nt(qd_ref[d, rows, :], states[d].astype(BF16))
            states[d] = dec_ref[d, c] * states[d] + kv_ref[d, c]
        if 2 * n >= n_chunks:
            finish(slice(n * ch, (n + 1) * ch))
            finish(slice((n_chunks - 1 - n) * ch, (n_chunks - n) * ch))


def _hgrn(q, v, ff, fb, gate, lb_logits, norm_g, layer):
    bsz, seq, _ = q.shape
    n_chunks = seq // HGRN_CHUNK
    hps = HGRN_HEADS_PER_STEP
    head_spec = pl.BlockSpec((1, seq, 128 * hps), lambda b, h: (b, 0, h))
    return pl.pallas_call(
        functools.partial(_hgrn_kernel, layer),
        grid=(bsz, HGRN_HEADS // hps),
        in_specs=[head_spec] * 5 + [pl.BlockSpec((DEPTH, 128 * hps), lambda b, h: (0, h)),
                                    pl.BlockSpec((1, 128 * hps), lambda b, h: (0, h))],
        out_specs=head_spec,
        out_shape=jax.ShapeDtypeStruct((bsz, seq, D_MODEL), BF16),
        scratch_shapes=[
           pltpu.VMEM((hps, 2, seq, 128), BF16),
           pltpu.VMEM((hps, 2, n_chunks, 1, 128), F32),
           pltpu.VMEM((hps, 2, n_chunks, 128, 128), F32),
           pltpu.VMEM((hps, 2, seq, 128), F32)],
        compiler_params=_params(2),
        name="hgrn2",
    )(q, v, ff, fb, gate, lb_logits, norm_g)


def _rope_pair_tables(seq):
    inv = ROPE_THETA ** (-jnp.arange(0, ATTN_QK_DIM, 2, dtype=F32) / ATTN_QK_DIM)
    ang = jnp.arange(seq, dtype=F32)[:, None] * inv[None, :]
    return jnp.tile(jnp.cos(ang), (1, 4)), jnp.tile(jnp.sin(ang), (1, 4))


def _to_pair_layout(w):
    idx = jnp.arange(ATTN_QK_WIDTH).reshape(2, 2, 2, 2, 32)
    return w[:, idx.transpose(0, 3, 1, 2, 4).reshape(-1)]


def kernel(x, norm_mix_g, norm_mlp_g, final_norm_g, w_ff_in, w_ff_out, w_in_even, w_out_even, diff_lambda, diff_subln_g, s5_lam_re, s5_lam_im, s5_log_step, s5_b_re, s5_b_im, s5_c_re, s5_c_im, s5_d, s5_w_glu, s5_b_glu, w_in_odd, w_out_odd, hgrn_norm_g, hgrn_lb_logits):
    bsz, seq, _ = x.shape
    n_tok = bsz * seq
    n_chunks = seq // S5_CHUNK
    assert n_chunks & (n_chunks - 1) == 0 and n_chunks % 128 == 0
    x2 = x.reshape(n_tok, D_MODEL)
    cos, sin = _rope_pair_tables(seq)

    assert DEPTH == 2, "an even (attention + S5) layer followed by an odd (HGRN2) layer"
    for layer in range(DEPTH):
        g_mix = norm_mix_g[layer].reshape(1, D_MODEL)
        g_mlp = norm_mlp_g[layer].reshape(1, D_MODEL)
        if layer % 2 == 0:
            e = layer // 2
            w = w_in_even[e]
            w_qkv = jnp.concatenate([_to_pair_layout(w[:, :ATTN_QK_WIDTH]),
                                     _to_pair_layout(w[:, ATTN_QK_WIDTH:2 * ATTN_QK_WIDTH]),
                                     w[:, 2 * ATTN_QK_WIDTH:2 * ATTN_QK_WIDTH + ATTN_WIDTH]], axis=1).astype(BF16)
            wu = w[:, 2 * ATTN_QK_WIDTH + ATTN_WIDTH:].astype(BF16)
            q, k, v = _pre_even(x2, g_mix, w_qkv, cos, sin, seq)
            u_t = _s5_in(x2, g_mix, wu, bsz, seq)
            lambda_init = 0.8 - 0.6 * math.exp(-0.3 * layer)
            a_out, w16 = _attention(q.reshape(bsz, seq, -1), k.reshape(bsz, seq, -1), v.reshape(bsz, seq, -1),
                                    diff_lambda[e], diff_subln_g[e].reshape(1, ATTN_V_DIM), lambda_init,
                                    [(w_out_even, e), (w_ff_in, layer), (w_ff_out, layer),
                                     (w_in_odd, e), (w_out_odd, e), (w_ff_in, layer + 1), (w_ff_out, layer + 1)],
                                    [None, None, None, 2 * D_MODEL, None, None, None])
            w_out_e, w1, w2, w_in_o, w_out_o, w1_next, w2_next = w16
            ops = _s5_operators(s5_lam_re[e], s5_lam_im[e], s5_log_step[e], s5_b_re[e], s5_b_im[e],
                                s5_c_re[e], s5_c_im[e], s5_d[e], n_chunks.bit_length() - 1)
            y_t = _s5_chunked(u_t.reshape(S5_GROUPS, S5_CW, bsz * n_chunks), *ops, n_chunks)
            b_out = _s5_out(y_t.reshape(S5_GROUPS, S5_CHUNK, S5_GROUP, bsz * n_chunks),
                            s5_w_glu[e].T.astype(BF16), s5_b_glu[e].reshape(S5_WIDTH, 1), bsz, seq)
            x2 = _post_even(x2, a_out.reshape(n_tok, ATTN_WIDTH), b_out, w_out_e, g_mlp, w1, w2)
        else:
            o_i = layer // 2
            q, v, ff, fb, gate = _pre_odd(x2, g_mix, w_in_o)
            shp = (bsz, seq, D_MODEL)
            mixed = _hgrn(q.reshape(shp), v.reshape(shp), ff.reshape(shp), fb.reshape(shp), gate.reshape(shp),
                          hgrn_lb_logits, hgrn_norm_g[o_i].reshape(1, D_MODEL), layer)
            x2 = _post_odd(x2, mixed.reshape(n_tok, D_MODEL), w_out_o, g_mlp, w1_next, w2_next,
                           final_norm_g.reshape(1, D_MODEL))
    return x2.reshape(bsz, seq, D_MODEL)
```

```python
import functools
import math

import jax
import jax.numpy as jnp
from jax import lax
from jax.experimental import pallas as pl
from jax.experimental.pallas import tpu as pltpu

D_MODEL = 1024
DEPTH = 2
ATTN_HEADS = 4
ATTN_QK_DIM = 64
ATTN_V_DIM = 128
ATTN_QK_WIDTH = 512
ATTN_WIDTH = 512
ROPE_THETA = 10000.0
S5_WIDTH = 512
S5_GROUP = 16
S5_GROUPS = 32
S5_STATE = 64
HGRN_HEADS = 8
HGRN_CHUNK = 64
HGRN_BLOCK = 256
HGRN_HEADS_PER_STEP = 4
D_FF = 4096
EPS = 1e-6

S5_CHUNK = 16
S5_CW = S5_CHUNK * S5_GROUP

TOKEN_TILE = 1024
ATTN_Q_TILE = 1024
ATTN_Q_SUB = 512
FF_TILE = 1024
VMEM_LIMIT = 56 * 1024 * 1024

BF16 = jnp.bfloat16
F32 = jnp.float32


def _const_spec(shape):
    nd = len(shape)
    return pl.BlockSpec(shape, lambda *_: (0,) * nd, pipeline_mode=pl.Buffered(1))


def _params(n_axes):
    return pltpu.CompilerParams(dimension_semantics=("arbitrary",) * n_axes,
                                vmem_limit_bytes=VMEM_LIMIT)


def _rmsnorm_rows(x, g):
    ms = jnp.mean(x * x, axis=-1, keepdims=True)
    return x * lax.rsqrt(ms + EPS) * g


def _gelu_tanh(x):
    c = math.sqrt(2.0 / math.pi)
    return 0.5 * x * (1.0 + jnp.tanh(c * (x + 0.044715 * (x * x * x))))


def _dot(a, b):
    return jnp.dot(a, b, preferred_element_type=F32)


def _dot_nt(a, b):
    return lax.dot_general(a, b, (((1,), (1,)), ((), ())), preferred_element_type=F32)


def _dot_tn(a, b):
    return lax.dot_general(a, b, (((0,), (0,)), ((), ())), preferred_element_type=F32)


def _pre_even_kernel(x_ref, g_ref, w_ref, cos_ref, sin_ref, q_ref, k_ref, v_ref):
    h = _rmsnorm_rows(x_ref[...], g_ref[...]).astype(BF16)
    cos = cos_ref[...]
    sin = sin_ref[...]
    for out_ref, base, scale in ((q_ref, 0, ATTN_QK_DIM ** -0.5 * math.log2(math.e)),
                                 (k_ref, ATTN_QK_WIDTH, 1.0)):
        p = _dot(h, w_ref[:, base:base + ATTN_QK_WIDTH])
        for pair in range(2):
            lo = p[:, 256 * pair:256 * pair + 128]
            hi = p[:, 256 * pair + 128:256 * pair + 256]
            out_ref[:, 256 * pair:256 * pair + 128] = ((lo * cos - hi * sin) * scale).astype(BF16)
            out_ref[:, 256 * pair + 128:256 * pair + 256] = ((hi * cos + lo * sin) * scale).astype(BF16)
    v_ref[...] = _dot(h, w_ref[:, 2 * ATTN_QK_WIDTH:]).astype(BF16)


def _pre_even(x2, g, w_qkv, cos, sin, seq):
    n_tok = x2.shape[0]
    tm = TOKEN_TILE
    n_pos_blocks = seq // tm
    out = jax.ShapeDtypeStruct((n_tok, 512), BF16)
    row_spec = pl.BlockSpec((tm, 512), lambda i: (i, 0))
    rope_spec = pl.BlockSpec((tm, 128), lambda i: (i % n_pos_blocks, 0))
    return pl.pallas_call(
        _pre_even_kernel,
        grid=(n_tok // tm,),
        in_specs=[pl.BlockSpec((tm, D_MODEL), lambda i: (i, 0)), _const_spec((1, D_MODEL)),
                  _const_spec(w_qkv.shape), rope_spec, rope_spec],
        out_specs=[row_spec] * 3,
        out_shape=[out] * 3,
        compiler_params=_params(1),
        name="pre_even",
    )(x2, g, w_qkv, cos, sin)


def _s5_in_kernel(x_ref, g_ref, wu_ref, ut_ref, u_scr):
    nc = x_ref.shape[0] // S5_CHUNK
    groups_per_slab = 128 // S5_GROUP
    h = _rmsnorm_rows(x_ref[...], g_ref[...]).astype(BF16)
    u = _dot(h, wu_ref[...])
    for j in range(S5_WIDTH // 128):
        u_scr[j] = u[:, 128 * j:128 * (j + 1)]
    for j in range(S5_WIDTH // 128):
        for ph in range(S5_CHUNK):
            t = u_scr[j, pl.ds(ph, nc, stride=S5_CHUNK), :]
            ut_ref[groups_per_slab * j:groups_per_slab * (j + 1), ph, :, :] = (
                t.T.astype(BF16).reshape(groups_per_slab, S5_GROUP, nc))


def _s5_in(x2, g, wu, bsz, seq):
    nc = seq // S5_CHUNK
    return pl.pallas_call(
        _s5_in_kernel,
        grid=(bsz,),
        in_specs=[pl.BlockSpec((seq, D_MODEL), lambda b: (b, 0)), _const_spec((1, D_MODEL)),
                  _const_spec(wu.shape)],
        out_specs=pl.BlockSpec((S5_GROUPS, S5_CHUNK, S5_GROUP, nc), lambda b: (0, 0, 0, b)),
        out_shape=jax.ShapeDtypeStruct((S5_GROUPS, S5_CHUNK, S5_GROUP, bsz * nc), BF16),
        scratch_shapes=[pltpu.VMEM((S5_WIDTH // 128, seq, 128), F32)],
        compiler_params=_params(1),
        name="s5_in",
    )(x2, g, wu)


def _s5_out_kernel(yt_ref, wglut_ref, bglu_ref, o_ref, b_scr):
    nc = yt_ref.shape[3]
    yt = jnp.concatenate([yt_ref[:, ph, :, :].reshape(S5_WIDTH, nc) for ph in range(S5_CHUNK)], axis=1)
    yt = _gelu_tanh(yt)
    z = _dot(wglut_ref[...], yt.astype(BF16)) + bglu_ref[...]
    bt = yt * (0.5 + 0.5 * jnp.tanh(0.5 * z))
    for j in range(S5_WIDTH // 128):
        for ph in range(S5_CHUNK):
            b_scr[j, pl.ds(ph, nc, stride=S5_CHUNK), :] = bt[128 * j:128 * (j + 1), ph * nc:(ph + 1) * nc].T
    for j in range(S5_WIDTH // 128):
        o_ref[:, 128 * j:128 * (j + 1)] = b_scr[j]


def _s5_out(y_t, w_glu_t, b_glu_col, bsz, seq):
    nc = seq // S5_CHUNK
    return pl.pallas_call(
        _s5_out_kernel,
        grid=(bsz,),
        in_specs=[pl.BlockSpec((S5_GROUPS, S5_CHUNK, S5_GROUP, nc), lambda b: (0, 0, 0, b)),
                  _const_spec(w_glu_t.shape), _const_spec(b_glu_col.shape)],
        out_specs=pl.BlockSpec((seq, S5_WIDTH), lambda b: (b, 0)),
        out_shape=jax.ShapeDtypeStruct((bsz * seq, S5_WIDTH), F32),
        scratch_shapes=[pltpu.VMEM((S5_WIDTH // 128, seq, 128), F32)],
        compiler_params=_params(1),
        name="s5_out",
    )(y_t, w_glu_t, b_glu_col)


def _attn_kernel(lambda_init, halve_from, q_ref, k_ref, v_ref, lam_ref, g_ref, *refs):
    n_weights = len(halve_from)
    o_ref = refs[n_weights]
    for w_ref, w16_ref, col0 in zip(refs[:n_weights], refs[n_weights + 1:], halve_from):
        w = w_ref[...]
        if col0 is not None:
            w = w * jnp.where(lax.broadcasted_iota(jnp.int32, w.shape, 1) >= col0, 0.5, 1.0)
        w16_ref[...] = w.astype(BF16)
    k = k_ref[0]
    lam = lam_ref[...]
    lam_val = (jnp.exp(jnp.sum(lam[0:1] * lam[1:2], axis=-1, keepdims=True))
               - jnp.exp(jnp.sum(lam[2:3] * lam[3:4], axis=-1, keepdims=True)) + lambda_init)
    tq = ATTN_Q_SUB
    lane_group = (lax.broadcasted_iota(jnp.int32, (tq, 256), 1) // 32) % 4
    ones = jnp.ones((k.shape[0], ATTN_V_DIM), BF16)
    v_ext = [jnp.concatenate([v_ref[0, :, 128 * hh:128 * hh + 128], ones], axis=1) for hh in range(2)]
    for t in range(q_ref.shape[1] // tq):
        q = q_ref[0, t * tq:(t + 1) * tq, :]
        zero = jnp.zeros_like(q)
        q_all = jnp.concatenate([jnp.where(lane_group == hc, q, zero) for hc in range(4)], axis=0)
        s = _dot_nt(q_all, k)
        e = jnp.exp2(s - jnp.max(s, axis=-1, keepdims=True)).astype(BF16)
        for hh in range(2):
            r = _dot(e[2 * hh * tq:2 * (hh + 1) * tq], v_ext[hh])
            o = (r[0:tq, 0:ATTN_V_DIM] / r[0:tq, ATTN_V_DIM:]
                 - lam_val * (r[tq:2 * tq, 0:ATTN_V_DIM] / r[tq:2 * tq, ATTN_V_DIM:]))
            o = _rmsnorm_rows(o, g_ref[...]) * (1.0 - lambda_init)
            o_ref[0, t * tq:(t + 1) * tq, 128 * hh:128 * hh + 128] = o.astype(o_ref.dtype)


def _attention(q, k, v, lam, subln_g, lambda_init, later_weights, halve_from):
    bsz, seq, _ = q.shape
    tq = ATTN_Q_TILE
    n_q = seq // tq
    n_steps = bsz * 2 * n_q
    kv_spec = pl.BlockSpec((1, seq, 256), lambda b, p, i: (b, 0, p))
    step = lambda b, p, i: (b * 2 + p) * n_q + i
    slab_in = [pl.BlockSpec((None, w.shape[1] // n_steps, w.shape[2]),
                            functools.partial(lambda idx, b, p, i: (idx, step(b, p, i), 0), idx))
               for w, idx in later_weights]
    slab_out = [pl.BlockSpec((w.shape[1] // n_steps, w.shape[2]), lambda b, p, i: (step(b, p, i), 0))
                for w, _ in later_weights]
    outs = pl.pallas_call(
        functools.partial(_attn_kernel, lambda_init, tuple(halve_from)),
        grid=(bsz, 2, n_q),
        in_specs=[pl.BlockSpec((1, tq, 256), lambda b, p, i: (b, i, p)),
                  kv_spec, kv_spec,
                  _const_spec(lam.shape),
                  _const_spec(subln_g.shape)] + slab_in,
        out_specs=[pl.BlockSpec((1, tq, 256), lambda b, p, i: (b, i, p))] + slab_out,
        out_shape=[jax.ShapeDtypeStruct((bsz, seq, ATTN_WIDTH), BF16)]
        + [jax.ShapeDtypeStruct(w.shape[1:], BF16) for w, _ in later_weights],
        compiler_params=_params(3),
        name="diff_attention",
    )(q, k, v, lam, subln_g, *[w for w, _ in later_weights])
    return outs[0], outs[1:]


def _s5_kernel(n_chunks, ut_ref, mi_ref, min_ref, mo_ref, ctab_ref, y_ref):
    n_rows = ut_ref.shape[2]
    n_steps = n_chunks.bit_length() - 1
    ut2 = ut_ref[...].reshape(2 * S5_CW, n_rows)
    st = _dot_tn(ut2, min_ref[0])
    ctab = ctab_ref[0]
    pos = lax.broadcasted_iota(jnp.int32, (n_rows, 128), 0) % n_chunks
    zs = []
    for d in range(2):
        xr = st[:, 256 * d:256 * d + 128]
        xi = st[:, 256 * d + 128:256 * d + 256]
        for k in range(n_steps):
            s = 1 << k
            if d == 0:
                shift, keep = s, pos >= s
            else:
                shift, keep = n_rows - s, pos < n_chunks - s
            pr = ctab[4 * k + 2 * d:4 * k + 2 * d + 1]
            pi = ctab[4 * k + 2 * d + 1:4 * k + 2 * d + 2]
            sr = jnp.where(keep, pltpu.roll(xr, shift, 0), 0.0)
            si = jnp.where(keep, pltpu.roll(xi, shift, 0), 0.0)
            xr, xi = xr + pr * sr - pi * si, xi + pr * si + pi * sr
        if d == 0:
            shift, keep = 1, pos >= 1
        else:
            shift, keep = n_rows - 1, pos < n_chunks - 1
        zs += [jnp.where(keep, pltpu.roll(xr, shift, 0), 0.0), jnp.where(keep, pltpu.roll(xi, shift, 0), 0.0)]
    z = jnp.concatenate(zs, axis=1).astype(BF16)
    for g in range(2):
        y_ref[g] = _dot(mi_ref[g], ut_ref[g]) + _dot_nt(mo_ref[g], z)


def _s5_chunked(u_t, m_intra_t, m_in_pair, m_out_ext, ctab, n_chunks):
    n_groups, _, n_lanes = u_t.shape
    return pl.pallas_call(
        functools.partial(_s5_kernel, n_chunks),
        grid=(n_groups // 2,),
        in_specs=[pl.BlockSpec((2, S5_CW, n_lanes), lambda i: (i, 0, 0)),
                  pl.BlockSpec((2, S5_CW, S5_CW), lambda i: (i, 0, 0)),
                  pl.BlockSpec((1, 2 * S5_CW, 512), lambda i: (i, 0, 0)),
                  pl.BlockSpec((2, S5_CW, 512), lambda i: (i, 0, 0)),
                  pl.BlockSpec((1, 32, 128), lambda i: (i, 0, 0))],
        out_specs=pl.BlockSpec((2, S5_CW, n_lanes), lambda i: (i, 0, 0)),
        out_shape=jax.ShapeDtypeStruct((n_groups, S5_CW, n_lanes), F32),
        compiler_params=_params(1),
        name="s5_chunked",
    )(u_t, m_intra_t, m_in_pair, m_out_ext, ctab)


def _dot_hp(a, b):
    return jnp.dot(a, b, precision=lax.Precision.HIGHEST, preferred_element_type=F32)


def _bf16_terms(x):
    hi = x.astype(BF16)
    r = x - hi.astype(F32)
    mid = r.astype(BF16)
    return hi, mid, (r - mid.astype(F32)).astype(BF16)


def _spread(values, onehot):
    return sum(_dot(term, onehot) for term in _bf16_terms(values))


def _spread_rows(onehot, values):
    return sum(_dot(onehot, term) for term in _bf16_terms(values))


def _discretise(lr, li, log_step):
    lr = jnp.minimum(lr, -1e-4)
    dt = jnp.exp(log_step)
    mag = jnp.exp(lr * dt)
    ar = mag * jnp.cos(li * dt)
    ai = mag * jnp.sin(li * dt)
    den = lr * lr + li * li
    cr = ((ar - 1.0) * lr + ai * li) / den
    ci = (ai * lr - (ar - 1.0) * li) / den
    return lr * dt, li * dt, cr, ci


def _s5_ops_kernel(n_steps, rowp_ref, bcol_ref, crow_ref, dcol_ref,
                   mi_ref, min_ref, mo_ref, ctab_ref):
    t = S5_CHUNK
    n = S5_GROUP
    onehot = lambda m: jnp.where(m, 1.0, 0.0).astype(BF16)
    i16 = lambda shape, dim: lax.broadcasted_iota(jnp.int32, shape, dim)
    rep_lanes = onehot(i16((t, S5_CW), 1) // n == i16((t, S5_CW), 0))
    tile_lanes = onehot(i16((n, S5_CW), 1) % n == i16((n, S5_CW), 0))
    rep_rows = onehot(i16((S5_CW, t), 0) // n == i16((S5_CW, t), 1))
    tile_rows = onehot(i16((S5_CW, n), 0) % n == i16((S5_CW, n), 1))
    row_group = i16((2 * S5_STATE, S5_CW), 0) // S5_STATE
    lane_group = i16((1, 2 * S5_STATE), 1) // S5_STATE
    row16 = i16((t, 1), 0).astype(F32)
    rowp = rowp_ref[0]

    taps = [[None, None], [None, None]]
    min_t_rows = []
    mo_kinds = []
    for d in range(2):
        lmag_r, ang_r, cr_r, ci_r = _discretise(rowp[3 * d:3 * d + 1], rowp[3 * d + 1:3 * d + 2],
                                                rowp[3 * d + 2:3 * d + 3])
        lag = (t - 1.0) - row16 if d == 0 else row16
        pmag = jnp.exp(lag * lmag_r)
        by_state = jnp.concatenate([pmag * jnp.cos(lag * ang_r), pmag * jnp.sin(lag * ang_r), cr_r, ci_r,
                                    jnp.zeros((128 - 2 * t - 2, 2 * S5_STATE), F32)], axis=0).T
        cr, ci = by_state[:, 2 * t:2 * t + 1], by_state[:, 2 * t + 1:2 * t + 2]
        b_re, b_im = bcol_ref[0, 2 * d], bcol_ref[0, 2 * d + 1]
        bbr = _spread(cr * b_re - ci * b_im, tile_lanes)
        bbi = _spread(cr * b_im + ci * b_re, tile_lanes)
        pr = _spread(by_state[:, 0:t], rep_lanes)
        pi = _spread(by_state[:, t:2 * t], rep_lanes)
        rr = pr * bbr - pi * bbi
        ri = pr * bbi + pi * bbr
        for part in (rr, ri):
            min_t_rows.append(jnp.concatenate([jnp.where(row_group == gs, part, 0.0) for gs in range(2)], axis=1))
        c_re, c_im = crow_ref[0, 2 * d], crow_ref[0, 2 * d + 1]
        for gs in range(2):
            own = lane_group == gs
            taps[gs][d] = (_dot_hp(jnp.where(own, c_re, 0.0), rr) - _dot_hp(jnp.where(own, c_im, 0.0), ri))

        tau = row16 + 1.0 if d == 0 else float(t) - row16
        qmag = jnp.exp(tau * lmag_r)
        qr16 = qmag * jnp.cos(tau * ang_r)
        qi16 = qmag * jnp.sin(tau * ang_r)
        qr = _spread_rows(rep_rows, qr16)
        qi = _spread_rows(rep_rows, qi16)
        cre = _spread_rows(tile_rows, c_re)
        cim = _spread_rows(tile_rows, c_im)
        mo_kinds += [qr * cre - qi * cim, -(qr * cim + qi * cre)]

        far = t - 1 if d == 0 else 0
        pw_r, pw_i = qr16[far:far + 1], qi16[far:far + 1]
        for k in range(n_steps):
            ctab_ref[0, 4 * k + 2 * d:4 * k + 2 * d + 1, :] = pw_r
            ctab_ref[0, 4 * k + 2 * d + 1:4 * k + 2 * d + 2, :] = pw_i
            pw_r, pw_i = pw_r * pw_r - pw_i * pw_i, 2.0 * pw_r * pw_i
    ctab_ref[0, 4 * n_steps:, :] = jnp.zeros((32 - 4 * n_steps, 2 * S5_STATE), F32)

    min_ref[0] = jnp.concatenate(min_t_rows, axis=0).T.astype(BF16)
    lane_group_wide = i16((1, 2 * S5_STATE), 1) // S5_STATE
    for gs in range(2):
        mo_ref[gs] = jnp.concatenate([jnp.where(lane_group_wide == gs, kind, 0.0) for kind in mo_kinds],
                                     axis=1).astype(BF16)
        zeros = jnp.zeros((n, S5_CW), F32)
        edge = (t - 1) * n
        lane = i16((n, 2 * S5_CW), 1)
        skip = jnp.where(lane == edge + i16((n, 2 * S5_CW), 0), dcol_ref[0, gs], 0.0)
        kk = (jnp.concatenate([taps[gs][0], zeros], axis=1)
              + pltpu.roll(jnp.concatenate([taps[gs][1], zeros], axis=1), edge, 1) + skip)
        strips = []
        for j in range(t):
            shift = (t - 1 - j) * n
            moved = kk if shift == 0 else pltpu.roll(kk, 2 * S5_CW - shift, 1)
            strips.append(moved[:, 0:S5_CW])
        mi_ref[gs] = jnp.concatenate(strips, axis=0).astype(BF16)


def _s5_operators(lam_re, lam_im, log_step, b_re, b_im, c_re, c_im, d_skip, n_steps):
    g2 = S5_GROUPS // 2
    step = jnp.broadcast_to(log_step[:, :, None], lam_re.shape)
    params = jnp.stack([lam_re, lam_im, step], axis=1).reshape(6, g2, 2 * S5_STATE)
    rowp = jnp.pad(params, ((0, 2), (0, 0), (0, 0))).transpose(1, 0, 2)
    bcol = jnp.stack([b_re, b_im], axis=1).reshape(4, g2, 2 * S5_STATE, S5_GROUP).transpose(1, 0, 2, 3)
    crow = (jnp.stack([c_re, c_im], axis=1).reshape(4, g2, 2, S5_GROUP, S5_STATE)
            .transpose(1, 0, 3, 2, 4).reshape(g2, 4, S5_GROUP, 2 * S5_STATE))
    dcol = d_skip.reshape(g2, 2, S5_GROUP, 1)
    pair = lambda *shape: pl.BlockSpec((1,) + shape, lambda i: (i,) + (0,) * len(shape))
    return pl.pallas_call(
        functools.partial(_s5_ops_kernel, n_steps),
        grid=(g2,),
        in_specs=[pair(8, 2 * S5_STATE), pair(4, 2 * S5_STATE, S5_GROUP),
                  pair(4, S5_GROUP, 2 * S5_STATE), pair(2, S5_GROUP, 1)],
        out_specs=[pl.BlockSpec((2, S5_CW, S5_CW), lambda i: (i, 0, 0)), pair(2 * S5_CW, 512),
                   pl.BlockSpec((2, S5_CW, 512), lambda i: (i, 0, 0)), pair(32, 2 * S5_STATE)],
        out_shape=[jax.ShapeDtypeStruct((S5_GROUPS, S5_CW, S5_CW), BF16),
                   jax.ShapeDtypeStruct((g2, 2 * S5_CW, 512), BF16),
                   jax.ShapeDtypeStruct((S5_GROUPS, S5_CW, 512), BF16),
                   jax.ShapeDtypeStruct((g2, 32, 2 * S5_STATE), F32)],
        compiler_params=_params(1),
        name="s5_operators",
    )(rowp, bcol, crow, dcol)


def _mlp(x1, g_ref, w1_ref, w2_ref):
    h = _rmsnorm_rows(x1, g_ref[...]).astype(BF16)
    acc = x1
    for j in range(D_FF // FF_TILE):
        hid = _dot(h, w1_ref[:, j * FF_TILE:(j + 1) * FF_TILE])
        hid = jnp.square(jnp.maximum(hid, 0.0)).astype(BF16)
        acc = acc + _dot(hid, w2_ref[j * FF_TILE:(j + 1) * FF_TILE, :])
    return acc


def _post_even_kernel(x_ref, a_ref, b_ref, wout_ref, g_ref, w1_ref, w2_ref, o_ref):
    mix = (_dot(a_ref[...], wout_ref[0:ATTN_WIDTH, :])
           + _dot(b_ref[...].astype(BF16), wout_ref[ATTN_WIDTH:, :]))
    o_ref[...] = _mlp(x_ref[...] + mix, g_ref, w1_ref, w2_ref)


def _post_even(x2, a_out, b_out, w_out, g_mlp, w1, w2):
    n_tok = x2.shape[0]
    tm = TOKEN_TILE
    return pl.pallas_call(
        _post_even_kernel,
        grid=(n_tok // tm,),
        in_specs=[pl.BlockSpec((tm, D_MODEL), lambda i: (i, 0)),
                  pl.BlockSpec((tm, ATTN_WIDTH), lambda i: (i, 0)),
                  pl.BlockSpec((tm, S5_WIDTH), lambda i: (i, 0)),
                  _const_spec(w_out.shape), _const_spec(g_mlp.shape), _const_spec(w1.shape),
                  _const_spec(w2.shape)],
        out_specs=pl.BlockSpec((tm, D_MODEL), lambda i: (i, 0)),
        out_shape=jax.ShapeDtypeStruct((n_tok, D_MODEL), F32),
        compiler_params=_params(1),
        name="post_even",
    )(x2, a_out, b_out, w_out, g_mlp, w1, w2)


def _post_odd_kernel(x_ref, m_ref, wout_ref, g_ref, w1_ref, w2_ref, gf_ref, o_ref):
    x1 = x_ref[...] + _dot(m_ref[...], wout_ref[...])
    o_ref[...] = _rmsnorm_rows(_mlp(x1, g_ref, w1_ref, w2_ref), gf_ref[...])


def _post_odd(x2, mixed, w_out, g_mlp, w1, w2, g_final):
    n_tok = x2.shape[0]
    tm = TOKEN_TILE
    return pl.pallas_call(
        _post_odd_kernel,
        grid=(n_tok // tm,),
        in_specs=[pl.BlockSpec((tm, D_MODEL), lambda i: (i, 0)),
                  pl.BlockSpec((tm, D_MODEL), lambda i: (i, 0)),
                  _const_spec(w_out.shape), _const_spec(g_mlp.shape),
                  _const_spec(w1.shape), _const_spec(w2.shape), _const_spec(g_final.shape)],
        out_specs=pl.BlockSpec((tm, D_MODEL), lambda i: (i, 0)),
        out_shape=jax.ShapeDtypeStruct((n_tok, D_MODEL), F32),
        compiler_params=_params(1),
        name="post_odd",
    )(x2, mixed, w_out, g_mlp, w1, w2, g_final)


def _pre_odd_kernel(x_ref, g_ref, w_ref, q_ref, i_ref, ff_ref, fb_ref, gate_ref):
    h = _rmsnorm_rows(x_ref[...], g_ref[...]).astype(BF16)
    for s, out_ref in enumerate((q_ref, i_ref, ff_ref, fb_ref, gate_ref)):
        out_ref[...] = _dot(h, w_ref[:, s * D_MODEL:(s + 1) * D_MODEL]).astype(out_ref.dtype)


def _pre_odd(x2, g, w):
    n_tok = x2.shape[0]
    tm = TOKEN_TILE
    row_spec = pl.BlockSpec((tm, D_MODEL), lambda i: (i, 0))
    lo = jax.ShapeDtypeStruct((n_tok, D_MODEL), BF16)
    hi = jax.ShapeDtypeStruct((n_tok, D_MODEL), F32)
    return pl.pallas_call(
        _pre_odd_kernel,
        grid=(n_tok // tm,),
        in_specs=[row_spec, _const_spec((1, D_MODEL)), _const_spec(w.shape)],
        out_specs=[row_spec] * 5,
        out_shape=[lo, lo, hi, hi, lo],
        compiler_params=_params(1),
        name="pre_odd",
    )(x2, g, w)


def _chunk_cumprod(x, reverse):
    n = x.shape[0]
    pos = lax.broadcasted_iota(jnp.int32, x.shape, 0) % HGRN_CHUNK
    s = 1
    while s < HGRN_CHUNK:
        if reverse:
            shifted = pltpu.roll(x, n - s, 0)
            x = x * jnp.where(pos < HGRN_CHUNK - s, shifted, 1.0)
        else:
            shifted = pltpu.roll(x, s, 0)
            x = x * jnp.where(pos >= s, shifted, 1.0)
        s *= 2
    return x


def _hgrn_kernel(layer, *refs):
    io, scratch = refs[:8], refs[8:]
    for hd in range(HGRN_HEADS_PER_STEP):
        lanes = pl.ds(128 * hd, 128)
        views = [r.at[:, :, lanes] for r in io[:5]] + [r.at[:, lanes] for r in io[5:7]] + [io[7].at[:, :, lanes]]
        _hgrn_head(layer, *views, *[s.at[hd] for s in scratch])


def _hgrn_head(layer, q_ref, v_ref, ff_ref, fb_ref, gate_ref, lbl_ref, ng_ref, o_ref,
               qd_ref, dec_ref, kv_ref, acc_ref):
    seq = q_ref.shape[1]
    ch = HGRN_CHUNK
    blk = HGRN_BLOCK
    n_chunks = seq // ch
    per_blk = blk // ch
    logits = lbl_ref[...]
    soft = jnp.exp(logits - jnp.max(logits, axis=0, keepdims=True))
    soft = soft / jnp.sum(soft, axis=0, keepdims=True)
    lb = jnp.sum(soft[0:layer + 1], axis=0, keepdims=True) - soft[0:1]

    c2 = 0.5 * (1.0 - lb)
    c1 = lb + c2
    row = lax.broadcasted_iota(jnp.int32, (blk, blk), 0)
    col = lax.broadcasted_iota(jnp.int32, (blk, blk), 1)
    u32 = lambda a: a.astype(jnp.uint32)
    keep = (u32(row - col) <= u32(row % ch), u32(col - row) <= u32(ch - 1 - row % ch))
    row_chunk = lax.broadcasted_iota(jnp.int32, (blk, 128), 0) // ch

    for j in range(seq // blk):
        sl = slice(j * blk, (j + 1) * blk)
        v = v_ref[0, sl, :]
        vt = v.astype(F32).T.astype(BF16)
        qf = q_ref[0, sl, :].astype(F32)
        for d, f_ref in enumerate((ff_ref, fb_ref)):
            ct = c2 * jnp.tanh(f_ref[0, sl, :])
            eb = _chunk_cumprod(c1 + ct, reverse=(d == 1))
            k_inv = (c2 - ct) / eb
            eb3 = eb.reshape(per_blk, ch, 128)
            e_end = eb3[:, ch - 1:ch, :] if d == 0 else eb3[:, 0:1, :]
            dec_ref[d, j * per_blk:(j + 1) * per_blk] = e_end
            q_dec = (qf * eb).astype(BF16)
            qd_ref[d, sl, :] = q_dec
            k_dec = (k_inv.reshape(per_blk, ch, 128) * e_end).reshape(blk, 128).astype(BF16)

            s = _dot_nt(q_dec, k_inv.astype(BF16))
            p = jnp.where(keep[d], s, 0.0).astype(BF16)
            acc_ref[d, sl, :] = _dot(p, v)
            rhs = jnp.concatenate([jnp.where(row_chunk == cc, k_dec, jnp.zeros_like(k_dec))
                                   for cc in range(per_blk)], axis=1)
            kvs = _dot(vt, rhs)
            for cc in range(per_blk):
                kv_ref[d, j * per_blk + cc] = kvs[:, 128 * cc:128 * (cc + 1)]

    def finish(rows):
        o = _rmsnorm_rows(acc_ref[0, rows, :] + acc_ref[1, rows, :], ng_ref[...])
        gate = 0.5 + 0.5 * jnp.tanh(gate_ref[0, rows, :].astype(F32))
        o_ref[0, rows, :] = (o * gate).astype(o_ref.dtype)

    states = [jnp.zeros((128, 128), F32)] * 2
    for n in range(n_chunks):
        for d in range(2):
            c = n if d == 0 else n_chunks - 1 - n
            rows = slice(c * ch, (c + 1) * ch)
            acc_ref[d, rows, :] += _dot_nt(qd_ref[d, rows, :], states[d].astype(BF16))
            states[d] = dec_ref[d, c] * states[d] + kv_ref[d, c]
        if 2 * n >= n_chunks:
            finish(slice(n * ch, (n + 1) * ch))
            finish(slice((n_chunks - 1 - n) * ch, (n_chunks - n) * ch))


def _hgrn(q, v, ff, fb, gate, lb_logits, norm_g, layer):
    bsz, seq, _ = q.shape
    n_chunks = seq // HGRN_CHUNK
    hps = HGRN_HEADS_PER_STEP
    head_spec = pl.BlockSpec((1, seq, 128 * hps), lambda b, h: (b, 0, h))
    return pl.pallas_call(
        functools.partial(_hgrn_kernel, layer),
        grid=(bsz, HGRN_HEADS // hps),
        in_specs=[head_spec] * 5 + [pl.BlockSpec((DEPTH, 128 * hps), lambda b, h: (0, h)),
                                    pl.BlockSpec((1, 128 * hps), lambda b, h: (0, h))],
        out_specs=head_spec,
        out_shape=jax.ShapeDtypeStruct((bsz, seq, D_MODEL), BF16),
        scratch_shapes=[
           pltpu.VMEM((hps, 2, seq, 128), BF16),
           pltpu.VMEM((hps, 2, n_chunks, 1, 128), F32),
           pltpu.VMEM((hps, 2, n_chunks, 128, 128), F32),
           pltpu.VMEM((hps, 2, seq, 128), F32)],
        compiler_params=_params(2),
        name="hgrn2",
    )(q, v, ff, fb, gate, lb_logits, norm_g)


def _rope_pair_tables(seq):
    inv = ROPE_THETA ** (-jnp.arange(0, ATTN_QK_DIM, 2, dtype=F32) / ATTN_QK_DIM)
    ang = jnp.arange(seq, dtype=F32)[:, None] * inv[None, :]
    return jnp.tile(jnp.cos(ang), (1, 4)), jnp.tile(jnp.sin(ang), (1, 4))


def _to_pair_layout(w):
    idx = jnp.arange(ATTN_QK_WIDTH).reshape(2, 2, 2, 2, 32)
    return w[:, idx.transpose(0, 3, 1, 2, 4).reshape(-1)]


def kernel(x, norm_mix_g, norm_mlp_g, final_norm_g, w_ff_in, w_ff_out, w_in_even, w_out_even, diff_lambda, diff_subln_g, s5_lam_re, s5_lam_im, s5_log_step, s5_b_re, s5_b_im, s5_c_re, s5_c_im, s5_d, s5_w_glu, s5_b_glu, w_in_odd, w_out_odd, hgrn_norm_g, hgrn_lb_logits):
    bsz, seq, _ = x.shape
    n_tok = bsz * seq
    n_chunks = seq // S5_CHUNK
    assert n_chunks & (n_chunks - 1) == 0 and n_chunks % 128 == 0
    x2 = x.reshape(n_tok, D_MODEL)
    cos, sin = _rope_pair_tables(seq)

    assert DEPTH == 2, "an even (attention + S5) layer followed by an odd (HGRN2) layer"
    for layer in range(DEPTH):
        g_mix = norm_mix_g[layer].reshape(1, D_MODEL)
        g_mlp = norm_mlp_g[layer].reshape(1, D_MODEL)
        if layer % 2 == 0:
            e = layer // 2
            w = w_in_even[e]
            w_qkv = jnp.concatenate([_to_pair_layout(w[:, :ATTN_QK_WIDTH]),
                                     _to_pair_layout(w[:, ATTN_QK_WIDTH:2 * ATTN_QK_WIDTH]),
                                     w[:, 2 * ATTN_QK_WIDTH:2 * ATTN_QK_WIDTH + ATTN_WIDTH]], axis=1).astype(BF16)
            wu = w[:, 2 * ATTN_QK_WIDTH + ATTN_WIDTH:].astype(BF16)
            q, k, v = _pre_even(x2, g_mix, w_qkv, cos, sin, seq)
            u_t = _s5_in(x2, g_mix, wu, bsz, seq)
            lambda_init = 0.8 - 0.6 * math.exp(-0.3 * layer)
            a_out, w16 = _attention(q.reshape(bsz, seq, -1), k.reshape(bsz, seq, -1), v.reshape(bsz, seq, -1),
                                    diff_lambda[e], diff_subln_g[e].reshape(1, ATTN_V_DIM), lambda_init,
                                    [(w_out_even, e), (w_ff_in, layer), (w_ff_out, layer),
                                     (w_in_odd, e), (w_out_odd, e), (w_ff_in, layer + 1), (w_ff_out, layer + 1)],
                                    [None, None, None, 2 * D_MODEL, None, None, None])
            w_out_e, w1, w2, w_in_o, w_out_o, w1_next, w2_next = w16
            ops = _s5_operators(s5_lam_re[e], s5_lam_im[e], s5_log_step[e], s5_b_re[e], s5_b_im[e],
                                s5_c_re[e], s5_c_im[e], s5_d[e], n_chunks.bit_length() - 1)
            y_t = _s5_chunked(u_t.reshape(S5_GROUPS, S5_CW, bsz * n_chunks), *ops, n_chunks)
            b_out = _s5_out(y_t.reshape(S5_GROUPS, S5_CHUNK, S5_GROUP, bsz * n_chunks),
                            s5_w_glu[e].T.astype(BF16), s5_b_glu[e].reshape(S5_WIDTH, 1), bsz, seq)
            x2 = _post_even(x2, a_out.reshape(n_tok, ATTN_WIDTH), b_out, w_out_e, g_mlp, w1, w2)
        else:
            o_i = layer // 2
            q, v, ff, fb, gate = _pre_odd(x2, g_mix, w_in_o)
            shp = (bsz, seq, D_MODEL)
            mixed = _hgrn(q.reshape(shp), v.reshape(shp), ff.reshape(shp), fb.reshape(shp), gate.reshape(shp),
                          hgrn_lb_logits, hgrn_norm_g[o_i].reshape(1, D_MODEL), layer)
            x2 = _post_odd(x2, mixed.reshape(n_tok, D_MODEL), w_out_o, g_mlp, w1_next, w2_next,
                           final_norm_g.reshape(1, D_MODEL))
    return x2.reshape(bsz, seq, D_MODEL)
```
